```python
import jax
import jax.numpy as jnp
from jax import lax
import numpy as np

D_MODEL = 1024
BATCH = 2
SEQ = 8192
DEPTH = 1
DEC_BATCH = 128
DEC_SEQ = 8
PAST_LEN = 8192
PAGE_SIZE = 128

HEAD_DIM = 64
NSA_HEADS = 8
NSA_KV_HEADS = 2
NSA_GROUP = NSA_HEADS // NSA_KV_HEADS
NSA_W = NSA_HEADS * HEAD_DIM
NSA_KV_W = NSA_KV_HEADS * HEAD_DIM
CMP_LEN = 32
CMP_STRIDE = 16
CMP_HID = 128
SLC_BLOCK = 64
SLC_TOPK = 16
WINDOW = 512
RET_HEADS = 4
RET_W = RET_HEADS * HEAD_DIM
RET_CHUNK = 64
MEM_HEADS = 4
MEM_W = MEM_HEADS * HEAD_DIM
N_MEM = 256
MIX_W = NSA_W + RET_W + MEM_W
IN_SIZES = (NSA_W, 2 * NSA_KV_W, 2 * NSA_KV_W, 2 * NSA_KV_W, 3 * NSA_HEADS, NSA_W, RET_W, RET_W, RET_W, RET_W, MEM_W, MEM_W)
IN_W = 2 * NSA_W + 6 * NSA_KV_W + 3 * NSA_HEADS + 4 * RET_W + 2 * MEM_W
Q_BLOCK = 128
ROPE_BASE = 10000.0
EPS = 1e-6
NEG_INF = -1e30
SCALE = HEAD_DIM ** -0.5

kernel_name = 'hymba_nsa_retention_decoder_step'


def rmsnorm(x, g):
    xf = x.astype(jnp.float32)
    y = xf * lax.rsqrt(jnp.mean(xf * xf, axis=-1, keepdims=True) + EPS)
    return (y * g.astype(jnp.float32)).astype(x.dtype)


def masked_softmax(s, mask):
    s = jnp.where(mask, s.astype(jnp.float32), NEG_INF)
    m = jnp.max(s, axis=-1, keepdims=True)
    e = jnp.where(mask, jnp.exp(s - m), 0.0)
    return e / jnp.maximum(jnp.sum(e, axis=-1, keepdims=True), 1e-30)


def rope(x, pos):
    half = HEAD_DIM // 2
    inv = ROPE_BASE ** (-jnp.arange(half, dtype=jnp.float32) / half)
    ang = pos.astype(jnp.float32)[:, None] * inv[None, :]
    cos = jnp.cos(ang)[None, :, None, :]
    sin = jnp.sin(ang)[None, :, None, :]
    xf = x.astype(jnp.float32)
    x1, x2 = xf[..., :half], xf[..., half:]
    return jnp.concatenate([x1 * cos - x2 * sin, x1 * sin + x2 * cos], axis=-1).astype(x.dtype)


def split_cols(h):
    cuts = np.cumsum(np.array(IN_SIZES))[:-1].tolist()
    return jnp.split(h, cuts, axis=-1)


def mixer_inputs(x, ln_g, w_in):
    B, T = x.shape[:2]
    h = rmsnorm(x, ln_g) @ w_in
    q_n, kv_c, kv_s, kv_w, g_n, z_n, q_r, k_r, v_r, z_r, q_m, z_m = split_cols(h)
    kvs = lambda a: a.reshape(B, T, 2, NSA_KV_HEADS, HEAD_DIM)
    heads = lambda a, n: a.reshape(B, T, n, HEAD_DIM)
    gates = jax.nn.sigmoid(g_n.astype(jnp.float32)).reshape(B, T, 3, NSA_KV_HEADS, NSA_GROUP, 1)
    return (heads(q_n, NSA_HEADS), kvs(kv_c), kvs(kv_s), kvs(kv_w), gates, z_n,
            heads(q_r, RET_HEADS), heads(k_r, RET_HEADS), heads(v_r, RET_HEADS), z_r,
            heads(q_m, MEM_HEADS), z_m)


def nsa_compress(kv, comp):
    pe, w1, b1, w2 = comp
    B, L = kv.shape[:2]
    nch = L // CMP_STRIDE
    ch = kv[:, :nch * CMP_STRIDE].reshape(B, nch, CMP_STRIDE, 2, NSA_KV_HEADS, HEAD_DIM)
    pe_r = pe.reshape(2, 2, CMP_STRIDE, HEAD_DIM)
    w1_r = w1.reshape(2, 2, CMP_STRIDE, HEAD_DIM, CMP_HID)
    pe_lo = pe_r[:, 0].transpose(1, 0, 2)[:, :, None, :]
    pe_hi = pe_r[:, 1].transpose(1, 0, 2)[:, :, None, :]
    lo = jnp.einsum('bcpvkd,vpdh->bcvkh', ch[:, :-1] + pe_lo, w1_r[:, 0])
    hi = jnp.einsum('bcpvkd,vpdh->bcvkh', ch[:, 1:] + pe_hi, w1_r[:, 1])
    h = jax.nn.gelu(lo + hi + b1[None, None, :, None, :])
    return jnp.einsum('bcvkh,vhd->bcvkd', h, w2)


def nsa_cmp_attend(q, t, kc, vc):
    nc = kc.shape[1]
    end = jnp.arange(nc) * CMP_STRIDE + CMP_LEN - 1
    s = jnp.einsum('bqkgd,bckd->bkgqc', q, kc) * SCALE
    p = masked_softmax(s, end[None, :] <= t[:, None])
    o = jnp.einsum('bkgqc,bckd->bqkgd', p.astype(vc.dtype), vc)
    return o, p.sum(axis=2)


def slc_importance(imp, nb):
    r = SLC_BLOCK // CMP_STRIDE
    need = r * nb + 1
    lead = [(0, 0)] * (imp.ndim - 1)
    pp = jnp.pad(imp, lead + [(1, 0)])
    cur = pp.shape[-1]
    if cur >= need:
        pp = pp[..., :need]
    else:
        pp = jnp.pad(pp, lead + [(0, need - cur)])
    return pp[..., :r * nb].reshape(pp.shape[:-1] + (nb, r)).sum(-1) + pp[..., r::r]


def gathered_attend(qb, kg, vg, mask):
    s = jnp.einsum('bqkgd,bkqsd->bkgqs', qb, kg) * SCALE
    p = masked_softmax(s, mask[:, :, None])
    return jnp.einsum('bkgqs,bkqsd->bqkgd', p.astype(vg.dtype), vg)


def window_attend(qb, t, kw, vw, kpos):
    s = jnp.einsum('bqkgd,bskd->bkgqs', qb, kw) * SCALE
    mask = (kpos[None, :] <= t[:, None]) & (kpos[None, :] > t[:, None] - WINDOW) & (kpos[None, :] >= 0)
    p = masked_softmax(s, mask)
    return jnp.einsum('bkgqs,bskd->bqkgd', p.astype(vw.dtype), vw)


def nsa_prompt(q, kv_c, kv_s, kv_w, gates, comp):
    B, T = q.shape[:2]
    nb = T // SLC_BLOCK
    k_top = min(SLC_TOPK, nb)
    kvc = nsa_compress(kv_c, comp)
    kc, vc = kvc[:, :, 0], kvc[:, :, 1]
    sel = kv_s.reshape(B, nb, SLC_BLOCK, 2, NSA_KV_HEADS, HEAD_DIM).transpose(3, 0, 4, 1, 2, 5)
    ks, vs = sel[0], sel[1]
    kw = jnp.pad(kv_w, ((0, 0), (WINDOW, 0), (0, 0), (0, 0), (0, 0)))
    qg = q.reshape(B, T, NSA_KV_HEADS, NSA_GROUP, HEAD_DIM)
    blk = jnp.arange(nb)
    bi = jnp.arange(B)[:, None, None]
    hi = jnp.arange(NSA_KV_HEADS)[None, :, None]

    def query_block(q0):
        qb = lax.dynamic_slice_in_dim(qg, q0, Q_BLOCK, axis=1)
        gb = lax.dynamic_slice_in_dim(gates, q0, Q_BLOCK, axis=1)
        t = q0 + jnp.arange(Q_BLOCK)
        o_c, imp = nsa_cmp_attend(qb, t, kc, vc)
        score = slc_importance(imp, nb)
        cur = (t // SLC_BLOCK)[:, None]
        forced = (blk[None, :] == 0) | (blk[None, :] == cur) | (blk[None, :] == cur - 1)
        valid = blk[None, :] * SLC_BLOCK <= t[:, None]
        score = jnp.where(forced, jnp.inf, jnp.where(valid, score, -jnp.inf))
        top_s, idx = lax.top_k(score, k_top)
        flat = idx.reshape(B, NSA_KV_HEADS, Q_BLOCK * k_top)
        kg = ks[bi, hi, flat].reshape(B, NSA_KV_HEADS, Q_BLOCK, k_top * SLC_BLOCK, HEAD_DIM)
        vg = vs[bi, hi, flat].reshape(B, NSA_KV_HEADS, Q_BLOCK, k_top * SLC_BLOCK, HEAD_DIM)
        kpos = (idx[..., None] * SLC_BLOCK + jnp.arange(SLC_BLOCK)).reshape(B, NSA_KV_HEADS, Q_BLOCK, k_top * SLC_BLOCK)
        ok = jnp.broadcast_to((top_s > -jnp.inf)[..., None], idx.shape + (SLC_BLOCK,)).reshape(kpos.shape)
        o_s = gathered_attend(qb, kg, vg, (kpos <= t[None, None, :, None]) & ok)
        wb = lax.dynamic_slice_in_dim(kw, q0, WINDOW + Q_BLOCK, axis=1)
        kpos_w = q0 - WINDOW + jnp.arange(WINDOW + Q_BLOCK)
        o_w = window_attend(qb, t, wb[:, :, 0], wb[:, :, 1], kpos_w)
        return gb[:, :, 0] * o_c + gb[:, :, 1] * o_s + gb[:, :, 2] * o_w

    o = lax.map(query_block, jnp.arange(0, T, Q_BLOCK))
    return o.transpose(1, 0, 2, 3, 4, 5).reshape(B, T, NSA_W)


def nsa_sample(q, kv_c, kv_s, kv_w, gates, comp, pool_cmp, pool_slc, win_buf, page_table):
    DB, S = q.shape[:2]
    t = PAST_LEN + jnp.arange(S)
    qg = q.reshape(DB, S, NSA_KV_HEADS, NSA_GROUP, HEAD_DIM)
    past_c = pool_cmp[page_table].reshape(DB, PAST_LEN, 2, NSA_KV_HEADS, HEAD_DIM)
    kvc = nsa_compress(jnp.concatenate([past_c, kv_c], axis=1), comp)
    o_c, imp = nsa_cmp_attend(qg, t, kvc[:, :, 0], kvc[:, :, 1])
    nbp = PAST_LEN // SLC_BLOCK
    score = slc_importance(imp, nbp)
    blk = jnp.arange(nbp)
    score = jnp.where((blk == 0) | (blk == nbp - 1), jnp.inf, score)
    k_top = min(SLC_TOPK - 1, nbp)
    _, idx = lax.top_k(score, k_top)
    bpp = PAGE_SIZE // SLC_BLOCK
    phys = page_table[jnp.arange(DB)[:, None, None, None], idx // bpp]
    rows = (idx % bpp)[..., None] * SLC_BLOCK + jnp.arange(SLC_BLOCK)
    hidx = jnp.arange(NSA_KV_HEADS)[None, :, None, None, None]
    g = pool_slc[phys[..., None], rows, :, hidx]
    nk = k_top * SLC_BLOCK
    kg = g[..., 0, :].reshape(DB, NSA_KV_HEADS, S, nk, HEAD_DIM)
    vg = g[..., 1, :].reshape(DB, NSA_KV_HEADS, S, nk, HEAD_DIM)
    k_new, v_new = kv_s[:, :, 0], kv_s[:, :, 1]
    s_past = jnp.einsum('bqkgd,bkqsd->bkgqs', qg, kg)
    s_new = jnp.einsum('bqkgd,bjkd->bkgqj', qg, k_new)
    s = jnp.concatenate([s_past, s_new], axis=-1) * SCALE
    causal = jnp.arange(S)[None, :] <= jnp.arange(S)[:, None]
    mask = jnp.concatenate([jnp.ones((S, nk), dtype=bool), causal], axis=-1)
    p = masked_softmax(s, mask)
    o_s = (jnp.einsum('bkgqs,bkqsd->bqkgd', p[..., :nk].astype(vg.dtype), vg)
           + jnp.einsum('bkgqj,bjkd->bqkgd', p[..., nk:].astype(v_new.dtype), v_new))
    buf = jnp.concatenate([win_buf, kv_w], axis=1)
    wb = win_buf.shape[1]
    kpos_w = PAST_LEN - wb + jnp.arange(wb + S)
    o_w = window_attend(qg, t, buf[:, :, 0], buf[:, :, 1], kpos_w)
    o = gates[:, :, 0] * o_c + gates[:, :, 1] * o_s + gates[:, :, 2] * o_w
    return o.reshape(DB, S, NSA_W), buf[:, S:]


def ret_log_decay():
    return jnp.log1p(-jnp.exp2(-5.0 - jnp.arange(RET_HEADS, dtype=jnp.float32)))


def retention_chunk(state, q, k, v, log_g):
    C = q.shape[1]
    n = jnp.arange(C, dtype=jnp.float32)
    diff = n[:, None] - n[None, :]
    dmat = jnp.where(diff >= 0, jnp.exp(log_g[:, None, None] * jnp.maximum(diff, 0.0)), 0.0)
    q = q * SCALE
    s = jnp.einsum('bnhd,bmhd->bhnm', q, k) * dmat[None]
    inner = jnp.einsum('bhnm,bmhe->bnhe', s, v)
    xi = jnp.exp(log_g[None, :] * (n[:, None] + 1.0))
    cross = jnp.einsum('bnhd,bhde->bnhe', q * xi[None, :, :, None], state)
    zeta = jnp.exp(log_g[None, :] * (C - 1.0 - n[:, None]))
    new_state = (jnp.exp(log_g * C)[None, :, None, None] * state
                 + jnp.einsum('bmhd,bmhe->bhde', k * zeta[None, :, :, None], v))
    return inner + cross, new_state


def head_groupnorm(o, g):
    mu = jnp.mean(o, axis=-1, keepdims=True)
    var = jnp.mean(jnp.square(o - mu), axis=-1, keepdims=True)
    y = (o - mu) * lax.rsqrt(var + EPS)
    return y.reshape(o.shape[:2] + (RET_W,)) * g.astype(jnp.float32)


def retention_prompt(q, k, v, pos, gn):
    B, T = q.shape[:2]
    q, k = rope(q, pos), rope(k, pos)
    log_g = ret_log_decay()
    nch = T // RET_CHUNK
    to_chunks = lambda a: a.astype(jnp.float32).reshape(B, nch, RET_CHUNK, RET_HEADS, HEAD_DIM).transpose(1, 0, 2, 3, 4)

    def step(st, qkv):
        o, st = retention_chunk(st, qkv[0], qkv[1], qkv[2], log_g)
        return st, o

    st0 = jnp.zeros((B, RET_HEADS, HEAD_DIM, HEAD_DIM), jnp.float32)
    st, o = lax.scan(step, st0, (to_chunks(q), to_chunks(k), to_chunks(v)))
    o = o.transpose(1, 0, 2, 3, 4).reshape(B, T, RET_HEADS, HEAD_DIM)
    return head_groupnorm(o, gn), st


def retention_sample(q, k, v, pos, state, gn):
    q, k = rope(q, pos), rope(k, pos)
    o, st = retention_chunk(state.astype(jnp.float32), q.astype(jnp.float32), k.astype(jnp.float32),
                            v.astype(jnp.float32), ret_log_decay())
    return head_groupnorm(o, gn), st


def mem_kv(mem, g, w):
    B = mem.shape[0]
    return (rmsnorm(mem, g) @ w).reshape(B, N_MEM, 2, MEM_HEADS, HEAD_DIM)


def mem_attend(q, mkv):
    s = jnp.einsum('bqhd,bmhd->bhqm', q, mkv[:, :, 0]) * SCALE
    p = jax.nn.softmax(s.astype(jnp.float32), axis=-1)
    return jnp.einsum('bhqm,bmhd->bqhd', p.astype(mkv.dtype), mkv[:, :, 1])


def finish(x, o_n, z_n, o_r, z_r, o_m, z_m, w_out):
    B, T = x.shape[:2]
    f32 = jnp.float32
    mix = jnp.concatenate([
        o_n.astype(f32) * jax.nn.silu(z_n.astype(f32)),
        o_r.astype(f32) * jax.nn.silu(z_r.astype(f32)),
        o_m.reshape(B, T, MEM_W).astype(f32) * jax.nn.silu(z_m.astype(f32)),
    ], axis=-1).astype(x.dtype)
    return x + mix @ w_out


def setup_inputs(seed: int = 0) -> dict:
    key = jax.random.key(seed)
    ks = jax.random.split(key, 24)
    f32 = jnp.float32
    nrm = lambda k, s, sc: jax.random.normal(k, s, f32) * sc
    n_pages = PAST_LEN // PAGE_SIZE
    n_used = DEC_BATCH * n_pages
    n_pool = n_used + n_used // 4
    wb = min(WINDOW, PAST_LEN)
    page_table = jax.random.permutation(ks[8], n_pool)[:n_used].reshape(DEC_BATCH, n_pages).astype(jnp.int32)
    return {
        'x_prompt': nrm(ks[0], (BATCH, SEQ, D_MODEL), 1.0),
        'mem_prompt': nrm(ks[1], (BATCH, N_MEM, D_MODEL), 1.0),
        'x_sample': nrm(ks[2], (DEC_BATCH, DEC_SEQ, D_MODEL), 1.0),
        'cache_nsa_cmp': nrm(ks[3], (DEPTH, n_pool, PAGE_SIZE, 2, NSA_KV_HEADS, HEAD_DIM), 1.0),
        'cache_nsa_slc': nrm(ks[4], (DEPTH, n_pool, PAGE_SIZE, 2, NSA_KV_HEADS, HEAD_DIM), 1.0),
        'cache_nsa_win': nrm(ks[5], (DEPTH, DEC_BATCH, wb, 2, NSA_KV_HEADS, HEAD_DIM), 1.0),
        'state_ret': nrm(ks[6], (DEPTH, DEC_BATCH, RET_HEADS, HEAD_DIM, HEAD_DIM), 1.0),
        'cache_mem': nrm(ks[7], (DEPTH, DEC_BATCH, N_MEM, 2, MEM_HEADS, HEAD_DIM), 1.0),
        'page_table': page_table,
        'ln_mix': 1.0 + nrm(ks[9], (DEPTH, D_MODEL), 0.05),
        'w_in': nrm(ks[10], (DEPTH, D_MODEL, IN_W), D_MODEL ** -0.5),
        'cmp_pe': nrm(ks[11], (DEPTH, 2, CMP_LEN, HEAD_DIM), 0.1),
        'cmp_w1': nrm(ks[12], (DEPTH, 2, CMP_LEN * HEAD_DIM, CMP_HID), (CMP_LEN * HEAD_DIM) ** -0.5),
        'cmp_b1': nrm(ks[13], (DEPTH, 2, CMP_HID), 0.02),
        'cmp_w2': nrm(ks[14], (DEPTH, 2, CMP_HID, HEAD_DIM), CMP_HID ** -0.5),
        'ret_gn': 1.0 + nrm(ks[15], (DEPTH, RET_W), 0.05),
        'ln_mem': 1.0 + nrm(ks[16], (DEPTH, D_MODEL), 0.05),
        'w_mem_kv': nrm(ks[17], (DEPTH, D_MODEL, 2 * MEM_W), D_MODEL ** -0.5),
        'w_out': nrm(ks[18], (DEPTH, MIX_W, D_MODEL), MIX_W ** -0.5),
        'ln_final': 1.0 + nrm(ks[19], (D_MODEL,), 0.05),
    }


def reference(x_prompt, mem_prompt, x_sample, cache_nsa_cmp, cache_nsa_slc, cache_nsa_win, state_ret, cache_mem,
              page_table, ln_mix, w_in, cmp_pe, cmp_w1, cmp_b1, cmp_w2, ret_gn, ln_mem, w_mem_kv, w_out, ln_final):
    xp, xs = x_prompt, x_sample
    T = xp.shape[1]
    S = xs.shape[1]
    pos_p = jnp.arange(T)
    pos_s = PAST_LEN + jnp.arange(S)
    cmp_p, cmp_s, slc_p, slc_s, win_p, win_s, ret_p, ret_s, mem_p = [], [], [], [], [], [], [], [], []
    for l in range(DEPTH):
        comp = (cmp_pe[l], cmp_w1[l], cmp_b1[l], cmp_w2[l])
        q_n, kv_c, kv_s, kv_w, g_n, z_n, q_r, k_r, v_r, z_r, q_m, z_m = mixer_inputs(xp, ln_mix[l], w_in[l])
        o_n = nsa_prompt(q_n, kv_c, kv_s, kv_w, g_n, comp)
        o_r, st_r = retention_prompt(q_r, k_r, v_r, pos_p, ret_gn[l])
        mkv = mem_kv(mem_prompt, ln_mem[l], w_mem_kv[l])
        o_m = mem_attend(q_m, mkv)
        xp = finish(xp, o_n, z_n, o_r, z_r, o_m, z_m, w_out[l])
        cmp_p.append(kv_c)
        slc_p.append(kv_s)
        win_p.append(kv_w[:, T - min(WINDOW, T):])
        ret_p.append(st_r)
        mem_p.append(mkv)
        q_n, kv_c, kv_s, kv_w, g_n, z_n, q_r, k_r, v_r, z_r, q_m, z_m = mixer_inputs(xs, ln_mix[l], w_in[l])
        o_n, new_win = nsa_sample(q_n, kv_c, kv_s, kv_w, g_n, comp, cache_nsa_cmp[l], cache_nsa_slc[l],
                                  cache_nsa_win[l], page_table)
        o_r, st_r = retention_sample(q_r, k_r, v_r, pos_s, state_ret[l], ret_gn[l])
        o_m = mem_attend(q_m, cache_mem[l])
        xs = finish(xs, o_n, z_n, o_r, z_r, o_m, z_m, w_out[l])
        cmp_s.append(kv_c)
        slc_s.append(kv_s)
        win_s.append(new_win)
        ret_s.append(st_r)
    y_prompt = rmsnorm(xp, ln_final)
    y_sample = rmsnorm(xs, ln_final)
    return (y_prompt, y_sample, jnp.stack(cmp_p), jnp.stack(cmp_s), jnp.stack(slc_p), jnp.stack(slc_s),
            jnp.stack(win_p), jnp.stack(win_s), jnp.stack(ret_p), jnp.stack(ret_s), jnp.stack(mem_p))
```

```python
import functools

import numpy as np
import jax
import jax.numpy as jnp
from jax import lax
from jax.experimental import pallas as pl
from jax.experimental.pallas import tpu as pltpu

D_MODEL = 1024
SEQ = 8192
DEC_SEQ = 8
PAST_LEN = 8192
PAGE_SIZE = 128
HEAD_DIM = 64
NSA_HEADS = 8
NSA_KV_HEADS = 2
NSA_GROUP = 4
CMP_LEN = 32
CMP_STRIDE = 16
CMP_HID = 128
SLC_BLOCK = 64
SLC_TOPK = 16
WINDOW = 512
RET_HEADS = 4
N_MEM = 256
Q_BLOCK = 128
ROPE_BASE = 10000.0
EPS = 1e-6
NEG_INF = -1e30
SCALE = HEAD_DIM ** -0.5

N_CMP = 512
N_SLC = 128
LANES = 128
VMEM_LIMIT = 56 * 1024 * 1024

_W_Q, _W_KVC, _W_KVS, _W_KVW, _W_QKVR, _W_QM, _W_Z, _W_G = 0, 512, 768, 1024, 1280, 2048, 2304, 3328
IN_WP = 3456

F32 = jnp.float32
BF16 = jnp.bfloat16

_LOG_G = [float(np.log1p(-(2.0 ** (-5.0 - h)))) for h in range(RET_HEADS)]


def _dot(a, b):
    return jnp.dot(a, b, preferred_element_type=F32)


def _dot_nt(a, b):
    return lax.dot_general(a, b, (((1,), (1,)), ((), ())), preferred_element_type=F32)


def _split3(x):
    hi = x.astype(BF16)
    r1 = x - hi.astype(F32)
    mid = r1.astype(BF16)
    lo = (r1 - mid.astype(F32)).astype(BF16)
    return hi, mid, lo


def _dot_exact_rhs(a_bf16, x):
    hi, mid, lo = _split3(x)
    return _dot(a_bf16, hi) + _dot(a_bf16, mid) + _dot(a_bf16, lo)


def _dot_exact_lhs(x, b_bf16):
    hi, mid, lo = _split3(x)
    return _dot(hi, b_bf16) + _dot(mid, b_bf16) + _dot(lo, b_bf16)


def _iota(shape, dim):
    return lax.broadcasted_iota(jnp.int32, shape, dim)


def _params(sem):
    return pltpu.CompilerParams(dimension_semantics=sem, vmem_limit_bytes=VMEM_LIMIT)


def _inproj_body(x_ref, g_ref, w_ref, qn_ref, kvc_ref, kvs_ref, kvw_ref, qkvr_ref, qm_ref, z_ref, gt_ref,
                 ksk_ref, ksvt_ref, kwk_ref, kwvt_ref):
    x = x_ref[...]
    xn = x * lax.rsqrt(jnp.mean(x * x, axis=-1, keepdims=True) + EPS) * g_ref[...]
    xb = xn.astype(BF16)

    def proj(a, b):
        return _dot(xb, w_ref[:, a:b])

    qn_ref[...] = proj(_W_Q, _W_KVC).astype(BF16)
    kvc_ref[...] = proj(_W_KVC, _W_KVS)
    kvs = proj(_W_KVS, _W_KVW)
    kvs_ref[...] = kvs
    ksk_ref[...] = kvs[:, :LANES].astype(BF16)
    ksvt_ref[0] = kvs[:, LANES:].T.astype(BF16)
    kvw = proj(_W_KVW, _W_QKVR)
    kvw_ref[...] = kvw
    kwk_ref[...] = kvw[:, :LANES].astype(BF16)
    kwvt_ref[0] = kvw[:, LANES:].T.astype(BF16)
    qkvr_ref[...] = proj(_W_QKVR, _W_QM)
    qm_ref[...] = proj(_W_QM, _W_Z)
    z_ref[...] = proj(_W_Z, _W_G)
    gt_ref[...] = jax.nn.sigmoid(proj(_W_G, IN_WP)).T


def _inproj(x2d, ln_g, w_perm):
    n = x2d.shape[0]
    tm = 512
    nt = n // tm
    row = lambda w: pl.BlockSpec((tm, w), lambda i: (i, 0))
    out_shape = (
        jax.ShapeDtypeStruct((n, 512), BF16),
        jax.ShapeDtypeStruct((n, 256), F32),
        jax.ShapeDtypeStruct((n, 256), F32),
        jax.ShapeDtypeStruct((n, 256), F32),
        jax.ShapeDtypeStruct((n, 768), F32),
        jax.ShapeDtypeStruct((n, 256), F32),
        jax.ShapeDtypeStruct((n, 1024), F32),
        jax.ShapeDtypeStruct((LANES, n), F32),
        jax.ShapeDtypeStruct((n, LANES), BF16),
        jax.ShapeDtypeStruct((nt, LANES, tm), BF16),
        jax.ShapeDtypeStruct((n, LANES), BF16),
        jax.ShapeDtypeStruct((nt, LANES, tm), BF16),
    )
    out_specs = (row(512), row(256), row(256), row(256), row(768), row(256), row(1024),
                 pl.BlockSpec((LANES, tm), lambda i: (0, i)),
                 row(LANES), pl.BlockSpec((1, LANES, tm), lambda i: (i, 0, 0)),
                 row(LANES), pl.BlockSpec((1, LANES, tm), lambda i: (i, 0, 0)))
    return pl.pallas_call(
        _inproj_body,
        grid=(nt,),
        in_specs=[row(D_MODEL),
                  pl.BlockSpec((1, D_MODEL), lambda i: (0, 0)),
                  pl.BlockSpec((D_MODEL, IN_WP), lambda i: (0, 0))],
        out_specs=out_specs,
        out_shape=out_shape,
        compiler_params=_params(("arbitrary",)),
        name="inproj",
    )(x2d, ln_g, w_perm)


def _rms_matmul_body(x_ref, g_ref, w_ref, o_ref):
    x = x_ref[...]
    xn = x * lax.rsqrt(jnp.mean(x * x, axis=-1, keepdims=True) + EPS) * g_ref[...]
    o_ref[...] = _dot(xn.astype(BF16), w_ref[...])


def _rms_matmul(x2d, ln_g, w_bf16):
    n, nw = x2d.shape[0], w_bf16.shape[1]
    tm = 256
    return pl.pallas_call(
        _rms_matmul_body,
        grid=(n // tm,),
        in_specs=[pl.BlockSpec((tm, D_MODEL), lambda i: (i, 0)),
                  pl.BlockSpec((1, D_MODEL), lambda i: (0, 0)),
                  pl.BlockSpec((D_MODEL, nw), lambda i: (0, 0))],
        out_specs=pl.BlockSpec((tm, nw), lambda i: (i, 0)),
        out_shape=jax.ShapeDtypeStruct((n, nw), F32),
        compiler_params=_params(("arbitrary",)),
        name="mem_kv",
    )(x2d, ln_g, w_bf16)


def _compress_slab(slabs, pe_ref, w1_ref, b1_ref, w2_ref, out_refs):
    rows = _iota((N_CMP, LANES), 0)
    for v in range(2):
        acc = [jnp.zeros((N_CMP, 2 * CMP_HID), F32) for _ in range(2)]
        for p in range(CMP_STRIDE):
            xp = slabs[v][pl.ds(p, N_CMP, stride=CMP_STRIDE), :]
            for half in range(2):
                xs = (xp + pe_ref[v, half, p]).astype(BF16)
                acc[half] = acc[half] + _dot(xs, w1_ref[v, half, p])
        nxt = pltpu.roll(acc[1], N_CMP - 1, 0)
        hid = jax.nn.gelu(acc[0] + nxt + b1_ref[v])
        out = _dot(hid.astype(BF16), w2_ref[v])
        out_refs[v][...] = jnp.where(rows < N_CMP - 1, out, 0.0)


def _compress_prompt_body(k_ref, v_ref, pe_ref, w1_ref, b1_ref, w2_ref, kc_ref, vc_ref):
    _compress_slab((k_ref.at[0], v_ref.at[0]), pe_ref, w1_ref, b1_ref, w2_ref, (kc_ref.at[0], vc_ref.at[0]))


def _comp_weight_specs():
    z = lambda *_: (0, 0, 0)
    z5 = lambda *_: (0, 0, 0, 0, 0)
    return [pl.BlockSpec((2, 2, CMP_STRIDE, 1, LANES), z5),
            pl.BlockSpec((2, 2, CMP_STRIDE, LANES, 2 * CMP_HID), z5),
            pl.BlockSpec((2, 1, 2 * CMP_HID), z),
            pl.BlockSpec((2, 2 * CMP_HID, LANES), z)]


def _compress_prompt(kvc3d, comp_w):
    b = kvc3d.shape[0]
    out = jax.ShapeDtypeStruct((b, N_CMP, LANES), F32)
    return pl.pallas_call(
        _compress_prompt_body,
        grid=(b,),
        in_specs=[pl.BlockSpec((1, SEQ, LANES), lambda i: (i, 0, 0)),
                  pl.BlockSpec((1, SEQ, LANES), lambda i: (i, 0, 1))] + _comp_weight_specs(),
        out_specs=(pl.BlockSpec((1, N_CMP, LANES), lambda i: (i, 0, 0)),) * 2,
        out_shape=(out, out),
        compiler_params=_params(("arbitrary",)),
        name="compress_prompt",
    )(kvc3d, kvc3d, *comp_w)


N_PAGES = PAST_LEN // PAGE_SIZE


def _page_copy(pt_ref, pool_ref, slab_ref, sem_ref, b, slot, j, v):
    return pltpu.make_async_copy(pool_ref.at[pt_ref[b, j], :, pl.ds(v * LANES, LANES)],
                                 slab_ref.at[slot, v, pl.ds(j * PAGE_SIZE, PAGE_SIZE)],
                                 sem_ref.at[slot])


def _start_pages(pt_ref, pool_ref, slab_ref, sem_ref, b, slot):
    def body(j, c):
        for v in range(2):
            _page_copy(pt_ref, pool_ref, slab_ref, sem_ref, b, slot, j, v).start()
        return c
    lax.fori_loop(0, N_PAGES, body, 0)


def _wait_pages(pt_ref, pool_ref, slab_ref, sem_ref, b, slot):
    def body(j, c):
        for v in range(2):
            _page_copy(pt_ref, pool_ref, slab_ref, sem_ref, b, slot, j, v).wait()
        return c
    lax.fori_loop(0, N_PAGES, body, 0)


def _stream_slab(pt_ref, pool_ref, slab_ref, sem_ref):
    b = pl.program_id(0)
    nb = pl.num_programs(0)
    slot = b % 2

    @pl.when(b == 0)
    def _():
        _start_pages(pt_ref, pool_ref, slab_ref, sem_ref, 0, 0)

    @pl.when(b + 1 < nb)
    def _():
        _start_pages(pt_ref, pool_ref, slab_ref, sem_ref, b + 1, 1 - slot)

    _wait_pages(pt_ref, pool_ref, slab_ref, sem_ref, b, slot)
    return slot


def _compress_sample_body(pt_ref, pool_ref, pe_ref, w1_ref, b1_ref, w2_ref, kc_ref, vc_ref, slab_ref, sem_ref):
    slot = _stream_slab(pt_ref, pool_ref, slab_ref, sem_ref)
    _compress_slab((slab_ref.at[slot, 0], slab_ref.at[slot, 1]), pe_ref, w1_ref, b1_ref, w2_ref,
                   (kc_ref.at[0], vc_ref.at[0]))


def _compress_sample(page_table, pool, comp_w):
    nb = page_table.shape[0]
    out = jax.ShapeDtypeStruct((nb, N_CMP, LANES), F32)
    grid_spec = pltpu.PrefetchScalarGridSpec(
        num_scalar_prefetch=1,
        grid=(nb,),
        in_specs=[pl.BlockSpec(memory_space=pl.ANY)] + _comp_weight_specs(),
        out_specs=(pl.BlockSpec((1, N_CMP, LANES), lambda i, pt: (i, 0, 0)),) * 2,
        scratch_shapes=[pltpu.VMEM((2, 2, PAST_LEN, LANES), F32), pltpu.SemaphoreType.DMA((2,))],
    )
    return pl.pallas_call(
        _compress_sample_body,
        grid_spec=grid_spec,
        out_shape=(out, out),
        compiler_params=_params(("arbitrary",)),
        name="compress_sample",
    )(page_table, pool, *comp_w)


def _top_k_rows(score, blk, k):
    sel = jnp.zeros(score.shape, F32)
    nblk = score.shape[0]
    for _ in range(k):
        m = jnp.max(score, axis=0, keepdims=True)
        idx = jnp.min(jnp.where(score == m, blk, nblk), axis=0, keepdims=True)
        hit = blk == idx
        sel = jnp.where(hit & (m > -jnp.inf), 1.0, sel)
        score = jnp.where(hit, -jnp.inf, score)
    return sel


def _tile4(row):
    return jnp.concatenate([row, row, row, row], axis=1)


KT_SLC = 512
WIN_KEYS = WINDOW + Q_BLOCK


def _nsa_prompt_body(qn_ref, gt_ref, ksk_ref, ksvt_ref, kwk_ref, kwvt_ref, kc_ref, vc_ref, o_ref, sel_ref):
    i = pl.program_id(1)
    q0 = i * Q_BLOCK
    qt = (qn_ref[...].astype(F32) * SCALE).T
    t_row = q0 + _iota((1, Q_BLOCK), 1)
    t4 = _tile4(t_row)
    kc = kc_ref[0].astype(BF16)
    vct = vc_ref[0].T.astype(BF16)
    gt = gt_ref[...]
    zeros_q = jnp.zeros((HEAD_DIM, NSA_GROUP * Q_BLOCK), F32)

    mi = _iota((N_SLC, N_CMP), 0)
    mj = _iota((N_SLC, N_CMP), 1)
    msel = ((mj >= 4 * mi - 1) & (mj <= 4 * mi + 3) & (mj < N_CMP - 1)).astype(BF16)
    blk = _iota((N_SLC, Q_BLOCK), 0)
    cur = t_row // SLC_BLOCK
    forced = (blk == 0) | (blk == cur) | (blk == cur - 1)
    valid = blk * SLC_BLOCK <= t_row
    cend = _iota((N_CMP, 1), 0) * CMP_STRIDE + (CMP_LEN - 1)

    outs = []
    for kvh in range(NSA_KV_HEADS):
        qg = jnp.concatenate([qt[(kvh * NSA_GROUP + g) * HEAD_DIM:(kvh * NSA_GROUP + g + 1) * HEAD_DIM, :]
                              for g in range(NSA_GROUP)], axis=1)
        qp = jnp.concatenate([qg, zeros_q] if kvh == 0 else [zeros_q, qg], axis=0).astype(BF16)
        vrow = slice(kvh * HEAD_DIM, (kvh + 1) * HEAD_DIM)

        mask_c = cend <= t4
        s = jnp.where(mask_c, _dot(kc, qp), NEG_INF)
        m = jnp.max(s, axis=0, keepdims=True)
        e = jnp.where(mask_c, jnp.exp(s - m), 0.0)
        p = e / jnp.maximum(jnp.sum(e, axis=0, keepdims=True), 1e-30)
        o_c = _dot(vct[vrow, :], p.astype(BF16))
        imp = (p[:, 0:Q_BLOCK] + p[:, Q_BLOCK:2 * Q_BLOCK]
               + p[:, 2 * Q_BLOCK:3 * Q_BLOCK] + p[:, 3 * Q_BLOCK:4 * Q_BLOCK])

        score = _dot_exact_rhs(msel, imp)
        score = jnp.where(forced, jnp.inf, jnp.where(valid, score, -jnp.inf))
        sel_ref[kvh] = _top_k_rows(score, blk, SLC_TOPK)

        def slc_tile(j, carry):
            m_run, l_run, acc = carry
            k0 = pl.multiple_of(j * KT_SLC, KT_SLC)
            sj = _dot(ksk_ref[pl.ds(k0, KT_SLC), :], qp)
            sel8 = sel_ref[kvh, pl.ds(pl.multiple_of(j * 8, 8), 8), :]
            kpos = k0 + _iota((SLC_BLOCK, 1), 0)
            parts, masks = [], []
            for r in range(KT_SLC // SLC_BLOCK):
                mr = (_tile4(sel8[r:r + 1, :]) > 0.5) & (kpos + r * SLC_BLOCK <= t4)
                masks.append(mr)
                parts.append(jnp.where(mr, sj[r * SLC_BLOCK:(r + 1) * SLC_BLOCK, :], NEG_INF))
            m_new = m_run
            for part in parts:
                m_new = jnp.maximum(m_new, jnp.max(part, axis=0, keepdims=True))
            alpha = jnp.exp(m_run - m_new)
            es = [jnp.where(mr, jnp.exp(part - m_new), 0.0) for mr, part in zip(masks, parts)]
            ej = jnp.concatenate(es, axis=0)
            l_new = alpha * l_run + jnp.sum(ej, axis=0, keepdims=True)
            acc = alpha * acc + _dot(ksvt_ref[j, vrow, :], ej.astype(BF16))
            return m_new, l_new, acc

        init = (jnp.full((1, NSA_GROUP * Q_BLOCK), NEG_INF, F32),
                jnp.zeros((1, NSA_GROUP * Q_BLOCK), F32),
                jnp.zeros((HEAD_DIM, NSA_GROUP * Q_BLOCK), F32))
        n_tiles = (q0 + Q_BLOCK + KT_SLC - 1) // KT_SLC
        _, l_s, acc_s = lax.fori_loop(0, n_tiles, slc_tile, init)
        o_s = acc_s / jnp.maximum(l_s, 1e-30)

        wt = jnp.maximum(i - WINDOW // Q_BLOCK, 0)
        w0 = pl.multiple_of(wt * Q_BLOCK, Q_BLOCK)
        sw = _dot(kwk_ref[pl.ds(w0, WIN_KEYS), :], qp)
        kpw = w0 + _iota((WIN_KEYS, 1), 0)
        mask_w = (kpw <= t4) & (kpw > t4 - WINDOW)
        sw = jnp.where(mask_w, sw, NEG_INF)
        mw = jnp.max(sw, axis=0, keepdims=True)
        ew = jnp.where(mask_w, jnp.exp(sw - mw), 0.0)
        pw = (ew / jnp.maximum(jnp.sum(ew, axis=0, keepdims=True), 1e-30)).astype(BF16)
        o_w = jnp.zeros((HEAD_DIM, NSA_GROUP * Q_BLOCK), F32)
        for r in range(WIN_KEYS // Q_BLOCK):
            tix = wt + r
            vt = kwvt_ref[tix // 4, vrow, :]
            sub = tix % 4
            vpiece = jnp.where(sub == 0, vt[:, 0:128],
                               jnp.where(sub == 1, vt[:, 128:256],
                                         jnp.where(sub == 2, vt[:, 256:384], vt[:, 384:512])))
            o_w = o_w + _dot(vpiece, pw[r * Q_BLOCK:(r + 1) * Q_BLOCK, :])

        def gate(br):
            base = br * NSA_HEADS + kvh * NSA_GROUP
            return jnp.concatenate([gt[base + g:base + g + 1, :] for g in range(NSA_GROUP)], axis=1)

        o = gate(0) * o_c + gate(1) * o_s + gate(2) * o_w
        outs.extend([o[:, g * Q_BLOCK:(g + 1) * Q_BLOCK] for g in range(NSA_GROUP)])
    o_ref[...] = jnp.concatenate(outs, axis=0).T


def _nsa_prompt(qn, gt, ksk, ksvt, kwk, kwvt, kc, vc, batch):
    nqb = SEQ // Q_BLOCK
    ntile = SEQ // 512
    return pl.pallas_call(
        _nsa_prompt_body,
        grid=(batch, nqb),
        in_specs=[pl.BlockSpec((Q_BLOCK, 512), lambda b, i: (b * nqb + i, 0)),
                  pl.BlockSpec((LANES, Q_BLOCK), lambda b, i: (0, b * nqb + i)),
                  pl.BlockSpec((SEQ, LANES), lambda b, i: (b, 0)),
                  pl.BlockSpec((ntile, LANES, 512), lambda b, i: (b, 0, 0)),
                  pl.BlockSpec((SEQ, LANES), lambda b, i: (b, 0)),
                  pl.BlockSpec((ntile, LANES, 512), lambda b, i: (b, 0, 0)),
                  pl.BlockSpec((1, N_CMP, LANES), lambda b, i: (b, 0, 0)),
                  pl.BlockSpec((1, N_CMP, LANES), lambda b, i: (b, 0, 0))],
        out_specs=pl.BlockSpec((Q_BLOCK, 512), lambda b, i: (b * nqb + i, 0)),
        out_shape=jax.ShapeDtypeStruct((batch * SEQ, 512), F32),
        scratch_shapes=[pltpu.VMEM((NSA_KV_HEADS, N_SLC, Q_BLOCK), F32)],
        compiler_params=_params(("arbitrary", "arbitrary")),
        name="nsa_prompt",
    )(qn, gt, ksk, ksvt, kwk, kwvt, kc, vc)


def _softmax_parts(parts, masks):
    m = None
    for s, mk in zip(parts, masks):
        mi = jnp.max(jnp.where(mk, s, NEG_INF), axis=-1, keepdims=True)
        m = mi if m is None else jnp.maximum(m, mi)
    es = [jnp.where(mk, jnp.exp(jnp.where(mk, s, NEG_INF) - m), 0.0) for s, mk in zip(parts, masks)]
    l = None
    for e in es:
        li = jnp.sum(e, axis=-1, keepdims=True)
        l = li if l is None else l + li
    inv = 1.0 / jnp.maximum(l, 1e-30)
    return [e * inv for e in es]


def _top_k_lanes(score, blk, k):
    sel = jnp.zeros(score.shape, F32)
    nblk = score.shape[1]
    for _ in range(k):
        m = jnp.max(score, axis=1, keepdims=True)
        idx = jnp.min(jnp.where(score == m, blk, nblk), axis=1, keepdims=True)
        hit = blk == idx
        sel = jnp.where(hit & (m > -jnp.inf), 1.0, sel)
        score = jnp.where(hit, -jnp.inf, score)
    return sel


KC_SLC = 2048
ROWS_S = NSA_HEADS * DEC_SEQ


def _pad_rows(x, n):
    if x.shape[0] == n:
        return x
    return jnp.concatenate([x, jnp.zeros((n - x.shape[0], x.shape[1]), x.dtype)], axis=0)


def _nsa_sample_body(pt_ref, pool_ref, qp_ref, g_ref, kc_ref, vc_ref, kvs_ref, kvw_ref, win_ref, expand_ref,
                     o_ref, nwin_ref, slab_ref, sem_ref):
    slot = _stream_slab(pt_ref, pool_ref, slab_ref, sem_ref)
    q = (qp_ref[0] * SCALE).astype(BF16)
    qi = _iota((ROWS_S, 1), 0) % DEC_SEQ
    t = PAST_LEN + qi

    kc = kc_ref[0].astype(BF16)
    vc = vc_ref[0].astype(BF16)
    cend = _iota((1, N_CMP), 1) * CMP_STRIDE + (CMP_LEN - 1)
    (p_c,) = _softmax_parts([_dot_nt(q, kc)], [cend <= t])
    o_c = _dot(p_c.astype(BF16), vc)

    half = ROWS_S // NSA_KV_HEADS
    imp = jnp.concatenate(
        [p_c[k * half:k * half + 8] + p_c[k * half + 8:k * half + 16]
         + p_c[k * half + 16:k * half + 24] + p_c[k * half + 24:k * half + 32] for k in range(NSA_KV_HEADS)], axis=0)
    mi = _iota((N_CMP, N_SLC), 0)
    mj = _iota((N_CMP, N_SLC), 1)
    msel_t = ((mi >= 4 * mj - 1) & (mi <= 4 * mj + 3) & (mi < N_CMP - 1)).astype(BF16)
    score = _dot_exact_lhs(imp, msel_t)
    blk = _iota((2 * DEC_SEQ, N_SLC), 1)
    score = jnp.where((blk == 0) | (blk == N_SLC - 1), jnp.inf, score)
    sel = _top_k_lanes(score, blk, SLC_TOPK - 1)
    sel64 = jnp.concatenate([sel[0:8]] * NSA_GROUP + [sel[8:16]] * NSA_GROUP, axis=0).astype(BF16)

    slab_k = slab_ref.at[slot, 0]
    slab_v = slab_ref.at[slot, 1]
    jn = _iota((1, LANES), 1)
    k_new = _pad_rows(kvs_ref[:, 0:LANES], LANES).astype(BF16)
    v_new = _pad_rows(kvs_ref[:, LANES:2 * LANES], LANES).astype(BF16)
    s_new = _dot_nt(q, k_new)
    mask_new = (jn <= qi) & (jn < DEC_SEQ)
    m_run = jnp.max(jnp.where(mask_new, s_new, NEG_INF), axis=-1, keepdims=True)
    e_new = jnp.where(mask_new, jnp.exp(jnp.where(mask_new, s_new, NEG_INF) - m_run), 0.0)
    l_run = jnp.sum(e_new, axis=-1, keepdims=True)
    acc = _dot(e_new.astype(BF16), v_new)
    for c in range(PAST_LEN // KC_SLC):
        ks = slab_k[pl.ds(c * KC_SLC, KC_SLC), :].astype(BF16)
        vs = slab_v[pl.ds(c * KC_SLC, KC_SLC), :].astype(BF16)
        mk = _dot(sel64, expand_ref[:, c * KC_SLC:(c + 1) * KC_SLC]) > 0.5
        sc = jnp.where(mk, _dot_nt(q, ks), NEG_INF)
        m_new = jnp.maximum(m_run, jnp.max(sc, axis=-1, keepdims=True))
        alpha = jnp.exp(m_run - m_new)
        ec = jnp.where(mk, jnp.exp(sc - m_new), 0.0)
        l_run = alpha * l_run + jnp.sum(ec, axis=-1, keepdims=True)
        acc = alpha * acc + _dot(ec.astype(BF16), vs)
        m_run = m_new
    o_s = acc / jnp.maximum(l_run, 1e-30)

    wb = win_ref.shape[1]
    win = win_ref[0]
    kpos_b = PAST_LEN - wb + _iota((1, wb), 1)
    kpos_n = PAST_LEN + jn
    mask_b = (kpos_b <= t) & (kpos_b > t - WINDOW) & (kpos_b >= 0)
    mask_n = (kpos_n <= t) & (kpos_n > t - WINDOW) & (jn < DEC_SEQ)
    kw_new = _pad_rows(kvw_ref[:, 0:LANES], LANES).astype(BF16)
    vw_new = _pad_rows(kvw_ref[:, LANES:2 * LANES], LANES).astype(BF16)
    p_b, p_n = _softmax_parts([_dot_nt(q, win[:, 0:LANES].astype(BF16)), _dot_nt(q, kw_new)], [mask_b, mask_n])
    o_w = _dot(p_b.astype(BF16), win[:, LANES:2 * LANES].astype(BF16)) + _dot(p_n.astype(BF16), vw_new)

    g = g_ref[0]
    o_ref[0] = g[:, 0:1] * o_c + g[:, 1:2] * o_s + g[:, 2:3] * o_w
    nwin_ref[0, 0:wb - DEC_SEQ, :] = win[DEC_SEQ:, :]
    nwin_ref[0, wb - DEC_SEQ:wb, :] = kvw_ref[...]


def _nsa_sample(page_table, pool_slc, qp, gates, kc, vc, kvs, kvw, win_buf, expand):
    nb = page_table.shape[0]
    wb = win_buf.shape[1]
    per_b = lambda *shape: pl.BlockSpec((1,) + shape, lambda i, pt: (i,) + (0,) * len(shape))
    rows8 = pl.BlockSpec((DEC_SEQ, 256), lambda i, pt: (i, 0))
    grid_spec = pltpu.PrefetchScalarGridSpec(
        num_scalar_prefetch=1,
        grid=(nb,),
        in_specs=[pl.BlockSpec(memory_space=pl.ANY),
                  per_b(ROWS_S, LANES), per_b(ROWS_S, LANES), per_b(N_CMP, LANES), per_b(N_CMP, LANES),
                  rows8, rows8, per_b(wb, 256),
                  pl.BlockSpec((N_SLC, PAST_LEN), lambda i, pt: (0, 0))],
        out_specs=(per_b(ROWS_S, LANES), per_b(wb, 256)),
        scratch_shapes=[pltpu.VMEM((2, 2, PAST_LEN, LANES), F32), pltpu.SemaphoreType.DMA((2,))],
    )
    return pl.pallas_call(
        _nsa_sample_body,
        grid_spec=grid_spec,
        out_shape=(jax.ShapeDtypeStruct((nb, ROWS_S, LANES), F32),
                   jax.ShapeDtypeStruct((nb, wb, 256), F32)),
        compiler_params=_params(("arbitrary",)),
        name="nsa_sample",
    )(page_table, pool_slc, qp, gates, kc, vc, kvs, kvw, win_buf, expand)


RET_W = RET_HEADS * HEAD_DIM


def _retention_body(q_ref, k_ref, v_ref, cos_ref, sin_ref, st0_ref, gn_ref, o_ref, st_ref, state_ref, *, chunk):
    c = pl.program_id(1)
    ck = max(chunk, LANES)

    @pl.when(c == 0)
    def _():
        state_ref[...] = st0_ref[0]

    lane = _iota((1, RET_W), 1)
    head_of_lane = lane // HEAD_DIM
    low_half = (lane % HEAD_DIM) < HEAD_DIM // 2
    cos = cos_ref[...]
    sin = sin_ref[...]

    def rope(x):
        rot = jnp.where(low_half, -pltpu.roll(x, RET_W - HEAD_DIM // 2, 1), pltpu.roll(x, HEAD_DIM // 2, 1))
        return x * cos + rot * sin

    logg_lane = jnp.zeros((1, RET_W), F32)
    for h in range(RET_HEADS):
        logg_lane = jnp.where(head_of_lane == h, _LOG_G[h], logg_lane)

    q = rope(q_ref[...]) * SCALE
    k = rope(k_ref[...])
    v = v_ref[...]
    kp = _pad_rows(k, ck)
    vp = _pad_rows(v, ck)
    n = _iota((chunk, 1), 0).astype(F32)
    nk = _iota((ck, 1), 0).astype(F32)
    diff = n - _iota((1, ck), 1).astype(F32)
    in_chunk = _iota((1, ck), 1) < chunk

    inner = jnp.zeros((chunk, RET_W), F32)
    for h in range(RET_HEADS):
        hm = head_of_lane == h
        dmat = jnp.where((diff >= 0) & in_chunk, jnp.exp(_LOG_G[h] * jnp.maximum(diff, 0.0)), 0.0)
        s = _dot_nt(jnp.where(hm, q, 0.0), kp) * dmat
        inner = inner + jnp.where(hm, _dot(s, vp), 0.0)
    state = state_ref[...]
    xi = jnp.exp(logg_lane * (n + 1.0))
    cross = _dot(q * xi, state)
    zeta = jnp.where(nk < chunk, jnp.exp(logg_lane * (chunk - 1.0 - nk)), 0.0)
    kz_t = (kp * zeta).T
    row_head = _iota((RET_W, 1), 0) // HEAD_DIM
    decay_rows = jnp.zeros((RET_W, 1), F32)
    for h in range(RET_HEADS):
        decay_rows = jnp.where(row_head == h, float(np.exp(_LOG_G[h] * chunk)), decay_rows)
    new_state = decay_rows * state + jnp.where(row_head == head_of_lane, _dot(kz_t, vp), 0.0)
    state_ref[...] = new_state
    st_ref[0] = new_state

    o = inner + cross
    avg = jnp.where(_iota((RET_W, RET_W), 0) // HEAD_DIM == _iota((RET_W, RET_W), 1) // HEAD_DIM,
                    1.0 / HEAD_DIM, 0.0).astype(BF16)
    mu = _dot_exact_lhs(o, avg)
    d = o - mu
    var = _dot_exact_lhs(d * d, avg)
    o_ref[...] = d * lax.rsqrt(var + EPS) * gn_ref[...]


def _retention(qkvr, cos, sin, state_bd, gn, batch, t_len, chunk):
    nch = t_len // chunk
    col = lambda j: pl.BlockSpec((chunk, RET_W), lambda b, c: (b * nch + c, j))
    tab = pl.BlockSpec((chunk, RET_W), lambda b, c: (c, 0))
    st = pl.BlockSpec((1, RET_W, RET_W), lambda b, c: (b, 0, 0))
    return pl.pallas_call(
        functools.partial(_retention_body, chunk=chunk),
        grid=(batch, nch),
        in_specs=[col(0), col(1), col(2), tab, tab, st, pl.BlockSpec((1, RET_W), lambda b, c: (0, 0))],
        out_specs=(pl.BlockSpec((chunk, RET_W), lambda b, c: (b * nch + c, 0)), st),
        out_shape=(jax.ShapeDtypeStruct((batch * t_len, RET_W), F32),
                   jax.ShapeDtypeStruct((batch, RET_W, RET_W), F32)),
        scratch_shapes=[pltpu.VMEM((RET_W, RET_W), F32)],
        compiler_params=_params(("arbitrary", "arbitrary")),
        name="retention",
    )(qkvr, qkvr, qkvr, cos, sin, state_bd, gn)


MEM_W = 4 * HEAD_DIM


def _mem_attn_body(q_ref, mkv_ref, o_ref):
    q = q_ref[...] * SCALE
    mk = mkv_ref[0, :, 0:MEM_W].astype(BF16)
    mv = mkv_ref[0, :, MEM_W:2 * MEM_W].astype(BF16)
    head_of_lane = _iota((1, MEM_W), 1) // HEAD_DIM
    o = jnp.zeros(q.shape, F32)
    for h in range(4):
        hm = head_of_lane == h
        s = _dot_nt(jnp.where(hm, q, 0.0).astype(BF16), mk)
        e = jnp.exp(s - jnp.max(s, axis=-1, keepdims=True))
        p = e / jnp.sum(e, axis=-1, keepdims=True)
        o = o + jnp.where(hm, _dot(p.astype(BF16), mv), 0.0)
    o_ref[...] = o


def _mem_attn(qm, mkv, batch, rows_per_batch, tm):
    nt = rows_per_batch // tm
    return pl.pallas_call(
        _mem_attn_body,
        grid=(batch, nt),
        in_specs=[pl.BlockSpec((tm, MEM_W), lambda b, i: (b * nt + i, 0)),
                  pl.BlockSpec((1, N_MEM, 2 * MEM_W), lambda b, i: (b, 0, 0))],
        out_specs=pl.BlockSpec((tm, MEM_W), lambda b, i: (b * nt + i, 0)),
        out_shape=jax.ShapeDtypeStruct((batch * rows_per_batch, MEM_W), F32),
        compiler_params=_params(("arbitrary", "arbitrary")),
        name="mem_attn",
    )(qm, mkv)


def _finish_body(x_ref, on_ref, or_ref, om_ref, z_ref, w_ref, g_ref, y_ref):
    o = jnp.concatenate([on_ref[...], or_ref[...], om_ref[...]], axis=-1)
    z = z_ref[...]
    mix = o * (z * jax.nn.sigmoid(z))
    xo = x_ref[...] + _dot(mix.astype(BF16), w_ref[...])
    y_ref[...] = xo * lax.rsqrt(jnp.mean(xo * xo, axis=-1, keepdims=True) + EPS) * g_ref[...]


def _finish(x2d, o_n, o_r, o_m, z, w_out, ln_final):
    n = x2d.shape[0]
    tm = 512
    row = lambda w: pl.BlockSpec((tm, w), lambda i: (i, 0))
    return pl.pallas_call(
        _finish_body,
        grid=(n // tm,),
        in_specs=[row(D_MODEL), row(512), row(256), row(256), row(D_MODEL),
                  pl.BlockSpec((D_MODEL, D_MODEL), lambda i: (0, 0)),
                  pl.BlockSpec((1, D_MODEL), lambda i: (0, 0))],
        out_specs=row(D_MODEL),
        out_shape=jax.ShapeDtypeStruct((n, D_MODEL), F32),
        compiler_params=_params(("arbitrary",)),
        name="finish",
    )(x2d, o_n, o_r, o_m, z, w_out, ln_final)


def _permute_w_in(w):
    sizes = (512, 256, 256, 256, 24, 512, 256, 256, 256, 256, 256, 256)
    offs = np.concatenate([[0], np.cumsum(sizes)])
    part = lambda i: w[:, int(offs[i]):int(offs[i + 1])]
    q_n, kv_c, kv_s, kv_w, g_n, z_n, q_r, k_r, v_r, z_r, q_m, z_m = [part(i) for i in range(12)]
    g_pad = jnp.pad(g_n, ((0, 0), (0, LANES - g_n.shape[1])))
    return jnp.concatenate([q_n, kv_c, kv_s, kv_w, q_r, k_r, v_r, q_m, z_n, z_r, z_m, g_pad], axis=1).astype(BF16)


def _compress_weights(pe, w1, b1, w2):
    pe_r = pe.reshape(2, 2, CMP_STRIDE, HEAD_DIM)
    pe_l = jnp.concatenate([pe_r, pe_r], axis=-1).reshape(2, 2, CMP_STRIDE, 1, LANES)
    w1_r = w1.reshape(2, 2, CMP_STRIDE, HEAD_DIM, CMP_HID)
    zw = jnp.zeros_like(w1_r)
    w1_bd = jnp.concatenate([jnp.concatenate([w1_r, zw], axis=-1),
                             jnp.concatenate([zw, w1_r], axis=-1)], axis=-2).astype(BF16)
    b1_l = jnp.concatenate([b1, b1], axis=-1).reshape(2, 1, 2 * CMP_HID)
    z2 = jnp.zeros_like(w2)
    w2_bd = jnp.concatenate([jnp.concatenate([w2, z2], axis=-1),
                             jnp.concatenate([z2, w2], axis=-1)], axis=-2).astype(BF16)
    return pe_l, w1_bd, b1_l, w2_bd


def _rope_tables(pos):
    half = HEAD_DIM // 2
    inv = ROPE_BASE ** (-jnp.arange(half, dtype=F32) / half)
    ang = pos.astype(F32)[:, None] * inv[None, :]
    cos, sin = jnp.cos(ang), jnp.sin(ang)
    cos_l = jnp.tile(jnp.concatenate([cos, cos], axis=-1), (1, RET_HEADS))
    sin_l = jnp.tile(jnp.concatenate([sin, sin], axis=-1), (1, RET_HEADS))
    return cos_l, sin_l


def _block_diag_state(st):
    b = st.shape[0]
    eye = jnp.eye(RET_HEADS, dtype=st.dtype)
    return jnp.einsum("bhde,hg->bhdge", st, eye).reshape(b, RET_W, RET_W)


def _diag_blocks(st_bd):
    b = st_bd.shape[0]
    r = st_bd.reshape(b, RET_HEADS, HEAD_DIM, RET_HEADS, HEAD_DIM)
    return jnp.stack([r[:, h, :, h, :] for h in range(RET_HEADS)], axis=1)


def _kv_shape(a, b, t):
    return a.reshape(1, b, t, 2, NSA_KV_HEADS, HEAD_DIM)


def kernel(x_prompt, mem_prompt, x_sample, cache_nsa_cmp, cache_nsa_slc, cache_nsa_win, state_ret, cache_mem,
           page_table, ln_mix, w_in, cmp_pe, cmp_w1, cmp_b1, cmp_w2, ret_gn, ln_mem, w_mem_kv, w_out, ln_final):
    bp, t_len = x_prompt.shape[:2]
    bs, s_len = x_sample.shape[:2]
    assert (t_len, s_len) == (SEQ, DEC_SEQ) and ln_mix.shape[0] == 1
    w_perm = _permute_w_in(w_in[0])
    comp_w = _compress_weights(cmp_pe[0], cmp_w1[0], cmp_b1[0], cmp_w2[0])
    ln_g = ln_mix[0].reshape(1, D_MODEL)
    gn = ret_gn[0].reshape(1, RET_W)
    w_out_b = w_out[0].astype(BF16)
    ln_f = ln_final.reshape(1, D_MODEL)

    xp2 = x_prompt.reshape(bp * SEQ, D_MODEL)
    (qn, kvc, kvs, kvw, qkvr, qm, z, gt, ksk, ksvt, kwk, kwvt) = _inproj(xp2, ln_g, w_perm)
    kc, vc = _compress_prompt(kvc.reshape(bp, SEQ, 256), comp_w)
    o_n = _nsa_prompt(qn, gt, ksk, ksvt, kwk, kwvt, kc, vc, bp)
    cos_p, sin_p = _rope_tables(jnp.arange(SEQ))
    o_r, st_p = _retention(qkvr, cos_p, sin_p, jnp.zeros((bp, RET_W, RET_W), F32), gn, bp, SEQ, 256)
    mkv = _rms_matmul(mem_prompt.reshape(bp * N_MEM, D_MODEL), ln_mem[0].reshape(1, D_MODEL),
                      w_mem_kv[0].astype(BF16))
    o_m = _mem_attn(qm, mkv.reshape(bp, N_MEM, 2 * MEM_W), bp, SEQ, 512)
    y_prompt = _finish(xp2, o_n, o_r, o_m, z, w_out_b, ln_f).reshape(bp, SEQ, D_MODEL)
    new_cmp_p = _kv_shape(kvc, bp, SEQ)
    new_slc_p = _kv_shape(kvs, bp, SEQ)
    new_win_p = _kv_shape(kvw.reshape(bp, SEQ, 256)[:, SEQ - WINDOW:], bp, WINDOW)
    new_ret_p = _diag_blocks(st_p)[None]
    new_mem_p = mkv.reshape(1, bp, N_MEM, 2, 4, HEAD_DIM)

    xs2 = x_sample.reshape(bs * DEC_SEQ, D_MODEL)
    (qn_s, kvc_s, kvs_s, kvw_s, qkvr_s, qm_s, z_s, gt_s, _, _, _, _) = _inproj(xs2, ln_g, w_perm)
    pool_cmp = cache_nsa_cmp[0].reshape(-1, PAGE_SIZE, 256)
    pool_slc = cache_nsa_slc[0].reshape(-1, PAGE_SIZE, 256)
    kc_s, vc_s = _compress_sample(page_table, pool_cmp, comp_w)
    q5 = qn_s.astype(F32).reshape(bs, DEC_SEQ, NSA_KV_HEADS, NSA_GROUP, HEAD_DIM).transpose(0, 2, 3, 1, 4)
    zq = jnp.zeros_like(q5[:, 0])
    qp = jnp.stack([jnp.concatenate([q5[:, 0], zq], axis=-1), jnp.concatenate([zq, q5[:, 1]], axis=-1)], axis=1)
    qp = qp.reshape(bs, ROWS_S, LANES)
    g5 = gt_s[:3 * NSA_HEADS].reshape(3, NSA_KV_HEADS, NSA_GROUP, bs, DEC_SEQ).transpose(3, 1, 2, 4, 0)
    gates = jnp.pad(g5.reshape(bs, ROWS_S, 3), ((0, 0), (0, 0), (0, LANES - 3)))
    expand = jnp.asarray(np.repeat(np.eye(N_SLC, dtype=np.float32), SLC_BLOCK, axis=1), dtype=BF16)
    win_buf = cache_nsa_win[0].reshape(bs, -1, 256)
    o_sn, new_win = _nsa_sample(page_table, pool_slc, qp, gates, kc_s, vc_s, kvs_s, kvw_s, win_buf, expand)
    o6 = o_sn.reshape(bs, NSA_KV_HEADS, NSA_GROUP, DEC_SEQ, NSA_KV_HEADS, HEAD_DIM)
    o_n_s = jnp.stack([o6[:, 0, :, :, 0], o6[:, 1, :, :, 1]], axis=1)
    o_n_s = o_n_s.transpose(0, 3, 1, 2, 4).reshape(bs * DEC_SEQ, 512)
    cos_s, sin_s = _rope_tables(PAST_LEN + jnp.arange(DEC_SEQ))
    o_r_s, st_s = _retention(qkvr_s, cos_s, sin_s, _block_diag_state(state_ret[0]), gn, bs, DEC_SEQ, DEC_SEQ)
    o_m_s = _mem_attn(qm_s, cache_mem[0].reshape(bs, N_MEM, 2 * MEM_W), bs, DEC_SEQ, DEC_SEQ)
    y_sample = _finish(xs2, o_n_s, o_r_s, o_m_s, z_s, w_out_b, ln_f).reshape(bs, DEC_SEQ, D_MODEL)
    new_cmp_s = _kv_shape(kvc_s, bs, DEC_SEQ)
    new_slc_s = _kv_shape(kvs_s, bs, DEC_SEQ)
    new_win_s = new_win.reshape(1, bs, -1, 2, NSA_KV_HEADS, HEAD_DIM)
    new_ret_s = _diag_blocks(st_s)[None]

    return (y_prompt, y_sample, new_cmp_p, new_cmp_s, new_slc_p, new_slc_s, new_win_p, new_win_s,
            new_ret_p, new_ret_s, new_mem_p)
```

```python
import functools

import numpy as np
import jax
import jax.numpy as jnp
from jax import lax
from jax.experimental import pallas as pl
from jax.experimental.pallas import tpu as pltpu

D_MODEL = 1024
SEQ = 8192
DEC_SEQ = 8
PAST_LEN = 8192
PAGE_SIZE = 128
HEAD_DIM = 64
NSA_HEADS = 8
NSA_KV_HEADS = 2
NSA_GROUP = 4
CMP_LEN = 32
CMP_STRIDE = 16
CMP_HID = 128
SLC_BLOCK = 64
SLC_TOPK = 16
WINDOW = 512
RET_HEADS = 4
N_MEM = 256
Q_BLOCK = 128
ROPE_BASE = 10000.0
EPS = 1e-6
NEG_INF = -1e30
SCALE = HEAD_DIM ** -0.5

N_CMP = 512
N_SLC = 128
LANES = 128
VMEM_LIMIT = 56 * 1024 * 1024

_W_Q, _W_KVC, _W_KVS, _W_KVW, _W_QKVR, _W_QM, _W_Z, _W_G = 0, 512, 768, 1024, 1280, 2048, 2304, 3328
IN_WP = 3456

F32 = jnp.float32
BF16 = jnp.bfloat16

_LOG_G = [float(np.log1p(-(2.0 ** (-5.0 - h)))) for h in range(RET_HEADS)]


def _dot(a, b):
    return jnp.dot(a, b, preferred_element_type=F32)


def _dot_nt(a, b):
    return lax.dot_general(a, b, (((1,), (1,)), ((), ())), preferred_element_type=F32)


def _split3(x):
    hi = x.astype(BF16)
    r1 = x - hi.astype(F32)
    mid = r1.astype(BF16)
    lo = (r1 - mid.astype(F32)).astype(BF16)
    return hi, mid, lo


def _dot_exact_rhs(a_bf16, x):
    hi, mid, lo = _split3(x)
    return _dot(a_bf16, hi) + _dot(a_bf16, mid) + _dot(a_bf16, lo)


def _dot_exact_lhs(x, b_bf16):
    hi, mid, lo = _split3(x)
    return _dot(hi, b_bf16) + _dot(mid, b_bf16) + _dot(lo, b_bf16)


def _iota(shape, dim):
    return lax.broadcasted_iota(jnp.int32, shape, dim)


def _params(sem):
    return pltpu.CompilerParams(dimension_semantics=sem, vmem_limit_bytes=VMEM_LIMIT)


def _inproj_body(x_ref, g_ref, w_ref, qn_ref, kvc_ref, kvs_ref, kvw_ref, qkvr_ref, qm_ref, z_ref, gt_ref,
                 *attn_refs, transposed_kv):
    x = x_ref[...]
    xn = x * lax.rsqrt(jnp.mean(x * x, axis=-1, keepdims=True) + EPS) * g_ref[...]
    xb = xn.astype(BF16)

    def proj(a, b):
        return _dot(xb, w_ref[:, a:b])

    qn_ref[...] = proj(_W_Q, _W_KVC).astype(BF16)
    kvc = proj(_W_KVC, _W_KVS)
    kvs = proj(_W_KVS, _W_KVW)
    kvw = proj(_W_KVW, _W_QKVR)
    if transposed_kv:
        ksk_ref, ksvt_ref, kwk_ref, kwvt_ref = attn_refs
        kvc_ref[0] = kvc.T
        kvs_t = kvs.T
        kvs_ref[0] = kvs_t
        ksk_ref[...] = kvs[:, :LANES].astype(BF16)
        ksvt_ref[0] = kvs_t[LANES:, :].astype(BF16)
        kvw_t = kvw.T
        kvw_ref[0] = kvw_t
        kwk_ref[...] = kvw[:, :LANES].astype(BF16)
        kwvt_ref[0] = kvw_t[LANES:, :].astype(BF16)
    else:
        kvc_ref[...] = kvc
        kvs_ref[...] = kvs
        kvw_ref[...] = kvw
    qkvr_ref[...] = proj(_W_QKVR, _W_QM)
    qm_ref[...] = proj(_W_QM, _W_Z)
    z_ref[...] = proj(_W_Z, _W_G)
    gt_ref[...] = jax.nn.sigmoid(proj(_W_G, IN_WP)).T


def _inproj(x2d, ln_g, w_perm, rows_per_batch, transposed_kv):
    n = x2d.shape[0]
    tm = 512
    nt = n // tm
    row = lambda w: pl.BlockSpec((tm, w), lambda i: (i, 0))
    if transposed_kv:
        tpb = rows_per_batch // tm
        kv_shape = jax.ShapeDtypeStruct((n // rows_per_batch, 256, rows_per_batch), F32)
        kv_spec = pl.BlockSpec((1, 256, tm), lambda i: (i // tpb, 0, i % tpb))
    else:
        kv_shape = jax.ShapeDtypeStruct((n, 256), F32)
        kv_spec = row(256)
    out_shape = [
        jax.ShapeDtypeStruct((n, 512), BF16),
        kv_shape, kv_shape, kv_shape,
        jax.ShapeDtypeStruct((n, 768), F32),
        jax.ShapeDtypeStruct((n, 256), F32),
        jax.ShapeDtypeStruct((n, 1024), F32),
        jax.ShapeDtypeStruct((LANES, n), F32),
    ]
    out_specs = [row(512), kv_spec, kv_spec, kv_spec, row(768), row(256), row(1024),
                 pl.BlockSpec((LANES, tm), lambda i: (0, i))]
    if transposed_kv:
        tile_t = pl.BlockSpec((1, LANES, tm), lambda i: (i, 0, 0))
        out_shape += [jax.ShapeDtypeStruct((n, LANES), BF16),
                      jax.ShapeDtypeStruct((nt, LANES, tm), BF16),
                      jax.ShapeDtypeStruct((n, LANES), BF16),
                      jax.ShapeDtypeStruct((nt, LANES, tm), BF16)]
        out_specs += [row(LANES), tile_t, row(LANES), tile_t]
    return pl.pallas_call(
        functools.partial(_inproj_body, transposed_kv=transposed_kv),
        grid=(nt,),
        in_specs=[row(D_MODEL),
                  pl.BlockSpec((1, D_MODEL), lambda i: (0, 0)),
                  pl.BlockSpec((D_MODEL, IN_WP), lambda i: (0, 0))],
        out_specs=tuple(out_specs),
        out_shape=tuple(out_shape),
        compiler_params=_params(("arbitrary",)),
        name="inproj",
    )(x2d, ln_g, w_perm)


def _mem_kv_body(x_ref, g_ref, w_ref, o_ref):
    x = x_ref[0]
    xn = x * lax.rsqrt(jnp.mean(x * x, axis=-1, keepdims=True) + EPS) * g_ref[...]
    o_ref[0] = _dot(xn.astype(BF16), w_ref[...]).T


def _mem_kv(mem, ln_g, w_bf16):
    b = mem.shape[0]
    nw = w_bf16.shape[1]
    return pl.pallas_call(
        _mem_kv_body,
        grid=(b,),
        in_specs=[pl.BlockSpec((1, N_MEM, D_MODEL), lambda i: (i, 0, 0)),
                  pl.BlockSpec((1, D_MODEL), lambda i: (0, 0)),
                  pl.BlockSpec((D_MODEL, nw), lambda i: (0, 0))],
        out_specs=pl.BlockSpec((1, nw, N_MEM), lambda i: (i, 0, 0)),
        out_shape=jax.ShapeDtypeStruct((b, nw, N_MEM), F32),
        compiler_params=_params(("arbitrary",)),
        name="mem_kv",
    )(mem, ln_g, w_bf16)


def _compress_slab(slabs_t, x_ref, pe_ref, w1_ref, b1_ref, w2_ref, out_refs):
    rows = _iota((N_CMP, LANES), 0)
    for v in range(2):
        for j in range(PAST_LEN // LANES):
            x_ref[j * LANES:(j + 1) * LANES, :] = slabs_t[v][:, j * LANES:(j + 1) * LANES].T
        acc = [jnp.zeros((N_CMP, 2 * CMP_HID), F32) for _ in range(2)]
        for p in range(CMP_STRIDE):
            xp = x_ref[pl.ds(p, N_CMP, stride=CMP_STRIDE), :]
            for half in range(2):
                xs = (xp + pe_ref[v, half, p]).astype(BF16)
                acc[half] = acc[half] + _dot(xs, w1_ref[v, half, p])
        nxt = pltpu.roll(acc[1], N_CMP - 1, 0)
        hid = jax.nn.gelu(acc[0] + nxt + b1_ref[v])
        out = _dot(hid.astype(BF16), w2_ref[v])
        out_refs[v][...] = jnp.where(rows < N_CMP - 1, out, 0.0)


def _compress_prompt_body(k_ref, v_ref, pe_ref, w1_ref, b1_ref, w2_ref, kc_ref, vc_ref, x_ref):
    _compress_slab((k_ref.at[0], v_ref.at[0]), x_ref, pe_ref, w1_ref, b1_ref, w2_ref, (kc_ref.at[0], vc_ref.at[0]))


def _comp_weight_specs():
    z = lambda *_: (0, 0, 0)
    z5 = lambda *_: (0, 0, 0, 0, 0)
    return [pl.BlockSpec((2, 2, CMP_STRIDE, 1, LANES), z5),
            pl.BlockSpec((2, 2, CMP_STRIDE, LANES, 2 * CMP_HID), z5),
            pl.BlockSpec((2, 1, 2 * CMP_HID), z),
            pl.BlockSpec((2, 2 * CMP_HID, LANES), z)]


def _compress_prompt(kvc_t, comp_w):
    b = kvc_t.shape[0]
    out = jax.ShapeDtypeStruct((b, N_CMP, LANES), F32)
    return pl.pallas_call(
        _compress_prompt_body,
        grid=(b,),
        in_specs=[pl.BlockSpec((1, LANES, SEQ), lambda i: (i, 0, 0)),
                  pl.BlockSpec((1, LANES, SEQ), lambda i: (i, 1, 0))] + _comp_weight_specs(),
        out_specs=(pl.BlockSpec((1, N_CMP, LANES), lambda i: (i, 0, 0)),) * 2,
        out_shape=(out, out),
        scratch_shapes=[pltpu.VMEM((SEQ, LANES), F32)],
        compiler_params=_params(("arbitrary",)),
        name="compress_prompt",
    )(kvc_t, kvc_t, *comp_w)


N_PAGES = PAST_LEN // PAGE_SIZE


def _page_copy(pt_ref, pool_ref, slab_ref, sem_ref, b, slot, j, v):
    return pltpu.make_async_copy(pool_ref.at[pt_ref[b, j], v],
                                 slab_ref.at[slot, v, :, pl.ds(j * PAGE_SIZE, PAGE_SIZE)],
                                 sem_ref.at[slot])


def _start_pages(pt_ref, pool_ref, slab_ref, sem_ref, b, slot):
    def body(j, c):
        for v in range(2):
            _page_copy(pt_ref, pool_ref, slab_ref, sem_ref, b, slot, j, v).start()
        return c
    lax.fori_loop(0, N_PAGES, body, 0)


def _wait_pages(pt_ref, pool_ref, slab_ref, sem_ref, b, slot):
    def body(j, c):
        for v in range(2):
            _page_copy(pt_ref, pool_ref, slab_ref, sem_ref, b, slot, j, v).wait()
        return c
    lax.fori_loop(0, N_PAGES, body, 0)


def _stream_slab(pt_ref, pool_ref, slab_ref, sem_ref):
    b = pl.program_id(0)
    nb = pl.num_programs(0)
    slot = b % 2

    @pl.when(b == 0)
    def _():
        _start_pages(pt_ref, pool_ref, slab_ref, sem_ref, 0, 0)

    @pl.when(b + 1 < nb)
    def _():
        _start_pages(pt_ref, pool_ref, slab_ref, sem_ref, b + 1, 1 - slot)

    _wait_pages(pt_ref, pool_ref, slab_ref, sem_ref, b, slot)
    return slot


def _compress_sample_body(pt_ref, pool_ref, pe_ref, w1_ref, b1_ref, w2_ref, kc_ref, vc_ref, slab_ref, sem_ref,
                          x_ref):
    slot = _stream_slab(pt_ref, pool_ref, slab_ref, sem_ref)
    _compress_slab((slab_ref.at[slot, 0], slab_ref.at[slot, 1]), x_ref, pe_ref, w1_ref, b1_ref, w2_ref,
                   (kc_ref.at[0], vc_ref.at[0]))


def _slab_scratch():
    return [pltpu.VMEM((2, 2, LANES, PAST_LEN), F32), pltpu.SemaphoreType.DMA((2,))]


def _compress_sample(page_table, pool, comp_w):
    nb = page_table.shape[0]
    out = jax.ShapeDtypeStruct((nb, N_CMP, LANES), F32)
    grid_spec = pltpu.PrefetchScalarGridSpec(
        num_scalar_prefetch=1,
        grid=(nb,),
        in_specs=[pl.BlockSpec(memory_space=pl.ANY)] + _comp_weight_specs(),
        out_specs=(pl.BlockSpec((1, N_CMP, LANES), lambda i, pt: (i, 0, 0)),) * 2,
        scratch_shapes=_slab_scratch() + [pltpu.VMEM((PAST_LEN, LANES), F32)],
    )
    return pl.pallas_call(
        _compress_sample_body,
        grid_spec=grid_spec,
        out_shape=(out, out),
        compiler_params=_params(("arbitrary",)),
        name="compress_sample",
    )(page_table, pool, *comp_w)


def _top_k_rows(score, blk, k):
    sel = jnp.zeros(score.shape, F32)
    nblk = score.shape[0]
    for _ in range(k):
        m = jnp.max(score, axis=0, keepdims=True)
        idx = jnp.min(jnp.where(score == m, blk, nblk), axis=0, keepdims=True)
        hit = blk == idx
        sel = jnp.where(hit & (m > -jnp.inf), 1.0, sel)
        score = jnp.where(hit, -jnp.inf, score)
    return sel


def _tile4(row):
    return jnp.concatenate([row, row, row, row], axis=1)


KT_SLC = 512
WIN_KEYS = WINDOW + Q_BLOCK


def _nsa_prompt_body(qn_ref, gt_ref, ksk_ref, ksvt_ref, kwk_ref, kwvt_ref, kc_ref, vc_ref, o_ref, sel_ref):
    i = pl.program_id(1)
    q0 = i * Q_BLOCK
    qt = (qn_ref[...].astype(F32) * SCALE).T
    t_row = q0 + _iota((1, Q_BLOCK), 1)
    t4 = _tile4(t_row)
    kc = kc_ref[0].astype(BF16)
    vct = vc_ref[0].T.astype(BF16)
    gt = gt_ref[...]
    zeros_q = jnp.zeros((HEAD_DIM, NSA_GROUP * Q_BLOCK), F32)

    mi = _iota((N_SLC, N_CMP), 0)
    mj = _iota((N_SLC, N_CMP), 1)
    msel = ((mj >= 4 * mi - 1) & (mj <= 4 * mi + 3) & (mj < N_CMP - 1)).astype(BF16)
    blk = _iota((N_SLC, Q_BLOCK), 0)
    cur = t_row // SLC_BLOCK
    forced = (blk == 0) | (blk == cur) | (blk == cur - 1)
    valid = blk * SLC_BLOCK <= t_row
    cend = _iota((N_CMP, 1), 0) * CMP_STRIDE + (CMP_LEN - 1)

    outs = []
    for kvh in range(NSA_KV_HEADS):
        qg = jnp.concatenate([qt[(kvh * NSA_GROUP + g) * HEAD_DIM:(kvh * NSA_GROUP + g + 1) * HEAD_DIM, :]
                              for g in range(NSA_GROUP)], axis=1)
        qp = jnp.concatenate([qg, zeros_q] if kvh == 0 else [zeros_q, qg], axis=0).astype(BF16)
        vrow = slice(kvh * HEAD_DIM, (kvh + 1) * HEAD_DIM)

        mask_c = cend <= t4
        s = jnp.where(mask_c, _dot(kc, qp), NEG_INF)
        m = jnp.max(s, axis=0, keepdims=True)
        e = jnp.where(mask_c, jnp.exp(s - m), 0.0)
        p = e / jnp.maximum(jnp.sum(e, axis=0, keepdims=True), 1e-30)
        o_c = _dot(vct[vrow, :], p.astype(BF16))
        imp = (p[:, 0:Q_BLOCK] + p[:, Q_BLOCK:2 * Q_BLOCK]
               + p[:, 2 * Q_BLOCK:3 * Q_BLOCK] + p[:, 3 * Q_BLOCK:4 * Q_BLOCK])

        score = _dot_exact_rhs(msel, imp)
        score = jnp.where(forced, jnp.inf, jnp.where(valid, score, -jnp.inf))
        sel_ref[kvh] = _top_k_rows(score, blk, SLC_TOPK)

        def slc_tile(j, carry):
            m_run, l_run, acc = carry
            k0 = pl.multiple_of(j * KT_SLC, KT_SLC)
            sj = _dot(ksk_ref[pl.ds(k0, KT_SLC), :], qp)
            sel8 = sel_ref[kvh, pl.ds(pl.multiple_of(j * 8, 8), 8), :]
            kpos = k0 + _iota((SLC_BLOCK, 1), 0)
            parts, masks = [], []
            for r in range(KT_SLC // SLC_BLOCK):
                mr = (_tile4(sel8[r:r + 1, :]) > 0.5) & (kpos + r * SLC_BLOCK <= t4)
                masks.append(mr)
                parts.append(jnp.where(mr, sj[r * SLC_BLOCK:(r + 1) * SLC_BLOCK, :], NEG_INF))
            m_new = m_run
            for part in parts:
                m_new = jnp.maximum(m_new, jnp.max(part, axis=0, keepdims=True))
            alpha = jnp.exp(m_run - m_new)
            es = [jnp.where(mr, jnp.exp(part - m_new), 0.0) for mr, part in zip(masks, parts)]
            ej = jnp.concatenate(es, axis=0)
            l_new = alpha * l_run + jnp.sum(ej, axis=0, keepdims=True)
            acc = alpha * acc + _dot(ksvt_ref[j, vrow, :], ej.astype(BF16))
            return m_new, l_new, acc

        init = (jnp.full((1, NSA_GROUP * Q_BLOCK), NEG_INF, F32),
                jnp.zeros((1, NSA_GROUP * Q_BLOCK), F32),
                jnp.zeros((HEAD_DIM, NSA_GROUP * Q_BLOCK), F32))
        n_tiles = (q0 + Q_BLOCK + KT_SLC - 1) // KT_SLC
        _, l_s, acc_s = lax.fori_loop(0, n_tiles, slc_tile, init)
        o_s = acc_s / jnp.maximum(l_s, 1e-30)

        wt = jnp.maximum(i - WINDOW // Q_BLOCK, 0)
        w0 = pl.multiple_of(wt * Q_BLOCK, Q_BLOCK)
        sw = _dot(kwk_ref[pl.ds(w0, WIN_KEYS), :], qp)
        kpw = w0 + _iota((WIN_KEYS, 1), 0)
        mask_w = (kpw <= t4) & (kpw > t4 - WINDOW)
        sw = jnp.where(mask_w, sw, NEG_INF)
        mw = jnp.max(sw, axis=0, keepdims=True)
        ew = jnp.where(mask_w, jnp.exp(sw - mw), 0.0)
        pw = (ew / jnp.maximum(jnp.sum(ew, axis=0, keepdims=True), 1e-30)).astype(BF16)
        o_w = jnp.zeros((HEAD_DIM, NSA_GROUP * Q_BLOCK), F32)
        for r in range(WIN_KEYS // Q_BLOCK):
            tix = wt + r
            vt = kwvt_ref[tix // 4, vrow, :]
            sub = tix % 4
            vpiece = jnp.where(sub == 0, vt[:, 0:128],
                               jnp.where(sub == 1, vt[:, 128:256],
                                         jnp.where(sub == 2, vt[:, 256:384], vt[:, 384:512])))
            o_w = o_w + _dot(vpiece, pw[r * Q_BLOCK:(r + 1) * Q_BLOCK, :])

        def gate(br):
            base = br * NSA_HEADS + kvh * NSA_GROUP
            return jnp.concatenate([gt[base + g:base + g + 1, :] for g in range(NSA_GROUP)], axis=1)

        o = gate(0) * o_c + gate(1) * o_s + gate(2) * o_w
        outs.extend([o[:, g * Q_BLOCK:(g + 1) * Q_BLOCK] for g in range(NSA_GROUP)])
    o_ref[...] = jnp.concatenate(outs, axis=0).T


def _nsa_prompt(qn, gt, ksk, ksvt, kwk, kwvt, kc, vc, batch):
    nqb = SEQ // Q_BLOCK
    ntile = SEQ // 512
    return pl.pallas_call(
        _nsa_prompt_body,
        grid=(batch, nqb),
        in_specs=[pl.BlockSpec((Q_BLOCK, 512), lambda b, i: (b * nqb + i, 0)),
                  pl.BlockSpec((LANES, Q_BLOCK), lambda b, i: (0, b * nqb + i)),
                  pl.BlockSpec((SEQ, LANES), lambda b, i: (b, 0)),
                  pl.BlockSpec((ntile, LANES, 512), lambda b, i: (b, 0, 0)),
                  pl.BlockSpec((SEQ, LANES), lambda b, i: (b, 0)),
                  pl.BlockSpec((ntile, LANES, 512), lambda b, i: (b, 0, 0)),
                  pl.BlockSpec((1, N_CMP, LANES), lambda b, i: (b, 0, 0)),
                  pl.BlockSpec((1, N_CMP, LANES), lambda b, i: (b, 0, 0))],
        out_specs=pl.BlockSpec((Q_BLOCK, 512), lambda b, i: (b * nqb + i, 0)),
        out_shape=jax.ShapeDtypeStruct((batch * SEQ, 512), F32),
        scratch_shapes=[pltpu.VMEM((NSA_KV_HEADS, N_SLC, Q_BLOCK), F32)],
        compiler_params=_params(("arbitrary", "arbitrary")),
        name="nsa_prompt",
    )(qn, gt, ksk, ksvt, kwk, kwvt, kc, vc)


def _softmax_parts(parts, masks):
    m = None
    for s, mk in zip(parts, masks):
        mi = jnp.max(jnp.where(mk, s, NEG_INF), axis=-1, keepdims=True)
        m = mi if m is None else jnp.maximum(m, mi)
    es = [jnp.where(mk, jnp.exp(jnp.where(mk, s, NEG_INF) - m), 0.0) for s, mk in zip(parts, masks)]
    l = None
    for e in es:
        li = jnp.sum(e, axis=-1, keepdims=True)
        l = li if l is None else l + li
    inv = 1.0 / jnp.maximum(l, 1e-30)
    return [e * inv for e in es]


def _top_k_lanes(score, blk, k):
    sel = jnp.zeros(score.shape, F32)
    nblk = score.shape[1]
    for _ in range(k):
        m = jnp.max(score, axis=1, keepdims=True)
        idx = jnp.min(jnp.where(score == m, blk, nblk), axis=1, keepdims=True)
        hit = blk == idx
        sel = jnp.where(hit & (m > -jnp.inf), 1.0, sel)
        score = jnp.where(hit, -jnp.inf, score)
    return sel


KC_SLC = 2048
ROWS_S = NSA_HEADS * DEC_SEQ


def _pad_rows(x, n):
    if x.shape[0] == n:
        return x
    return jnp.concatenate([x, jnp.zeros((n - x.shape[0], x.shape[1]), x.dtype)], axis=0)


def _nsa_sample_body(pt_ref, pool_ref, qp_ref, g_ref, kc_ref, vc_ref, kvs_ref, kvw_ref, win_ref, wtail_ref,
                     expand_ref, o_ref, nwin_ref, slab_ref, sem_ref):
    slot = _stream_slab(pt_ref, pool_ref, slab_ref, sem_ref)
    q = (qp_ref[0] * SCALE).astype(BF16)
    qi = _iota((ROWS_S, 1), 0) % DEC_SEQ
    t = PAST_LEN + qi

    kc = kc_ref[0].astype(BF16)
    vc = vc_ref[0].astype(BF16)
    cend = _iota((1, N_CMP), 1) * CMP_STRIDE + (CMP_LEN - 1)
    (p_c,) = _softmax_parts([_dot_nt(q, kc)], [cend <= t])
    o_c = _dot(p_c.astype(BF16), vc)

    half = ROWS_S // NSA_KV_HEADS
    imp = jnp.concatenate(
        [p_c[k * half:k * half + 8] + p_c[k * half + 8:k * half + 16]
         + p_c[k * half + 16:k * half + 24] + p_c[k * half + 24:k * half + 32] for k in range(NSA_KV_HEADS)], axis=0)
    mi = _iota((N_CMP, N_SLC), 0)
    mj = _iota((N_CMP, N_SLC), 1)
    msel_t = ((mi >= 4 * mj - 1) & (mi <= 4 * mj + 3) & (mi < N_CMP - 1)).astype(BF16)
    score = _dot_exact_lhs(imp, msel_t)
    blk = _iota((2 * DEC_SEQ, N_SLC), 1)
    score = jnp.where((blk == 0) | (blk == N_SLC - 1), jnp.inf, score)
    sel = _top_k_lanes(score, blk, SLC_TOPK - 1)
    sel64 = jnp.concatenate([sel[0:8]] * NSA_GROUP + [sel[8:16]] * NSA_GROUP, axis=0).astype(BF16)

    slab_k = slab_ref.at[slot, 0]
    slab_v = slab_ref.at[slot, 1]
    jn = _iota((1, LANES), 1)
    k_new = _pad_rows(kvs_ref[:, 0:LANES], LANES).astype(BF16)
    v_new = _pad_rows(kvs_ref[:, LANES:2 * LANES], LANES).astype(BF16)
    s_new = _dot_nt(q, k_new)
    mask_new = (jn <= qi) & (jn < DEC_SEQ)
    m_run = jnp.max(jnp.where(mask_new, s_new, NEG_INF), axis=-1, keepdims=True)
    e_new = jnp.where(mask_new, jnp.exp(jnp.where(mask_new, s_new, NEG_INF) - m_run), 0.0)
    l_run = jnp.sum(e_new, axis=-1, keepdims=True)
    acc = _dot(e_new.astype(BF16), v_new)
    for c in range(PAST_LEN // KC_SLC):
        ks_t = slab_k[:, c * KC_SLC:(c + 1) * KC_SLC].astype(BF16)
        vs_t = slab_v[:, c * KC_SLC:(c + 1) * KC_SLC].astype(BF16)
        mk = _dot(sel64, expand_ref[:, c * KC_SLC:(c + 1) * KC_SLC]) > 0.5
        sc = jnp.where(mk, _dot(q, ks_t), NEG_INF)
        m_new = jnp.maximum(m_run, jnp.max(sc, axis=-1, keepdims=True))
        alpha = jnp.exp(m_run - m_new)
        ec = jnp.where(mk, jnp.exp(sc - m_new), 0.0)
        l_run = alpha * l_run + jnp.sum(ec, axis=-1, keepdims=True)
        acc = alpha * acc + _dot_nt(ec.astype(BF16), vs_t)
        m_run = m_new
    o_s = acc / jnp.maximum(l_run, 1e-30)

    wb = win_ref.shape[2]
    win_t = win_ref[0]
    kpos_b = PAST_LEN - wb + _iota((1, wb), 1)
    kpos_n = PAST_LEN + jn
    mask_b = (kpos_b <= t) & (kpos_b > t - WINDOW) & (kpos_b >= 0)
    mask_n = (kpos_n <= t) & (kpos_n > t - WINDOW) & (jn < DEC_SEQ)
    kw_new = _pad_rows(kvw_ref[:, 0:LANES], LANES).astype(BF16)
    vw_new = _pad_rows(kvw_ref[:, LANES:2 * LANES], LANES).astype(BF16)
    p_b, p_n = _softmax_parts([_dot(q, win_t[0:LANES, :].astype(BF16)), _dot_nt(q, kw_new)], [mask_b, mask_n])
    o_w = _dot_nt(p_b.astype(BF16), win_t[LANES:2 * LANES, :].astype(BF16)) + _dot(p_n.astype(BF16), vw_new)

    g = g_ref[0]
    o_ref[0] = g[:, 0:1] * o_c + g[:, 1:2] * o_s + g[:, 2:3] * o_w
    shifted = pltpu.roll(win_t, wb - DEC_SEQ, 1)
    nwin_ref[0, :, 0:wb - LANES] = shifted[:, 0:wb - LANES]
    nwin_ref[0, :, wb - LANES:wb] = jnp.where(jn >= LANES - DEC_SEQ, wtail_ref[0], shifted[:, wb - LANES:wb])


def _nsa_sample(page_table, pool_slc, qp, gates, kc, vc, kvs, kvw, win_t, wtail, expand):
    nb = page_table.shape[0]
    wb = win_t.shape[2]
    per_b = lambda *shape: pl.BlockSpec((1,) + shape, lambda i, pt: (i,) + (0,) * len(shape))
    rows8 = pl.BlockSpec((DEC_SEQ, 256), lambda i, pt: (i, 0))
    grid_spec = pltpu.PrefetchScalarGridSpec(
        num_scalar_prefetch=1,
        grid=(nb,),
        in_specs=[pl.BlockSpec(memory_space=pl.ANY),
                  per_b(ROWS_S, LANES), per_b(ROWS_S, LANES), per_b(N_CMP, LANES), per_b(N_CMP, LANES),
                  rows8, rows8, per_b(256, wb), per_b(256, LANES),
                  pl.BlockSpec((N_SLC, PAST_LEN), lambda i, pt: (0, 0))],
        out_specs=(per_b(ROWS_S, LANES), per_b(256, wb)),
        scratch_shapes=_slab_scratch(),
    )
    return pl.pallas_call(
        _nsa_sample_body,
        grid_spec=grid_spec,
        out_shape=(jax.ShapeDtypeStruct((nb, ROWS_S, LANES), F32),
                   jax.ShapeDtypeStruct((nb, 256, wb), F32)),
        compiler_params=_params(("arbitrary",)),
        name="nsa_sample",
    )(page_table, pool_slc, qp, gates, kc, vc, kvs, kvw, win_t, wtail, expand)


RET_W = RET_HEADS * HEAD_DIM


def _retention_body(q_ref, k_ref, v_ref, cos_ref, sin_ref, st0_ref, gn_ref, o_ref, st_ref, state_ref, *, chunk):
    c = pl.program_id(1)
    ck = max(chunk, LANES)

    @pl.when(c == 0)
    def _():
        state_ref[...] = st0_ref[0]

    lane = _iota((1, RET_W), 1)
    head_of_lane = lane // HEAD_DIM
    low_half = (lane % HEAD_DIM) < HEAD_DIM // 2
    cos = cos_ref[...]
    sin = sin_ref[...]

    def rope(x):
        rot = jnp.where(low_half, -pltpu.roll(x, RET_W - HEAD_DIM // 2, 1), pltpu.roll(x, HEAD_DIM // 2, 1))
        return x * cos + rot * sin

    logg_lane = jnp.zeros((1, RET_W), F32)
    for h in range(RET_HEADS):
        logg_lane = jnp.where(head_of_lane == h, _LOG_G[h], logg_lane)

    q = rope(q_ref[...]) * SCALE
    k = rope(k_ref[...])
    v = v_ref[...]
    kp = _pad_rows(k, ck)
    vp = _pad_rows(v, ck)
    n = _iota((chunk, 1), 0).astype(F32)
    nk = _iota((ck, 1), 0).astype(F32)
    diff = n - _iota((1, ck), 1).astype(F32)
    in_chunk = _iota((1, ck), 1) < chunk

    inner = jnp.zeros((chunk, RET_W), F32)
    for h in range(RET_HEADS):
        hm = head_of_lane == h
        dmat = jnp.where((diff >= 0) & in_chunk, jnp.exp(_LOG_G[h] * jnp.maximum(diff, 0.0)), 0.0)
        s = _dot_nt(jnp.where(hm, q, 0.0), kp) * dmat
        inner = inner + jnp.where(hm, _dot(s, vp), 0.0)
    state = state_ref[...]
    xi = jnp.exp(logg_lane * (n + 1.0))
    cross = _dot(q * xi, state)
    zeta = jnp.where(nk < chunk, jnp.exp(logg_lane * (chunk - 1.0 - nk)), 0.0)
    kz_t = (kp * zeta).T
    row_head = _iota((RET_W, 1), 0) // HEAD_DIM
    decay_rows = jnp.zeros((RET_W, 1), F32)
    for h in range(RET_HEADS):
        decay_rows = jnp.where(row_head == h, float(np.exp(_LOG_G[h] * chunk)), decay_rows)
    new_state = decay_rows * state + jnp.where(row_head == head_of_lane, _dot(kz_t, vp), 0.0)
    state_ref[...] = new_state
    st_ref[0] = new_state

    o = inner + cross
    avg = jnp.where(_iota((RET_W, RET_W), 0) // HEAD_DIM == _iota((RET_W, RET_W), 1) // HEAD_DIM,
                    1.0 / HEAD_DIM, 0.0).astype(BF16)
    mu = _dot_exact_lhs(o, avg)
    d = o - mu
    var = _dot_exact_lhs(d * d, avg)
    o_ref[...] = d * lax.rsqrt(var + EPS) * gn_ref[...]


def _retention(qkvr, cos, sin, state_bd, gn, batch, t_len, chunk):
    nch = t_len // chunk
    col = lambda j: pl.BlockSpec((chunk, RET_W), lambda b, c: (b * nch + c, j))
    tab = pl.BlockSpec((chunk, RET_W), lambda b, c: (c, 0))
    st = pl.BlockSpec((1, RET_W, RET_W), lambda b, c: (b, 0, 0))
    return pl.pallas_call(
        functools.partial(_retention_body, chunk=chunk),
        grid=(batch, nch),
        in_specs=[col(0), col(1), col(2), tab, tab, st, pl.BlockSpec((1, RET_W), lambda b, c: (0, 0))],
        out_specs=(pl.BlockSpec((chunk, RET_W), lambda b, c: (b * nch + c, 0)), st),
        out_shape=(jax.ShapeDtypeStruct((batch * t_len, RET_W), F32),
                   jax.ShapeDtypeStruct((batch, RET_W, RET_W), F32)),
        scratch_shapes=[pltpu.VMEM((RET_W, RET_W), F32)],
        compiler_params=_params(("arbitrary", "arbitrary")),
        name="retention",
    )(qkvr, qkvr, qkvr, cos, sin, state_bd, gn)


MEM_W = 4 * HEAD_DIM


def _mem_attn_body(q_ref, mkv_ref, o_ref):
    q = q_ref[...] * SCALE
    mk_t = mkv_ref[0, 0:MEM_W, :].astype(BF16)
    mv_t = mkv_ref[0, MEM_W:2 * MEM_W, :].astype(BF16)
    head_of_lane = _iota((1, MEM_W), 1) // HEAD_DIM
    o = jnp.zeros(q.shape, F32)
    for h in range(4):
        hm = head_of_lane == h
        s = _dot(jnp.where(hm, q, 0.0).astype(BF16), mk_t)
        e = jnp.exp(s - jnp.max(s, axis=-1, keepdims=True))
        p = e / jnp.sum(e, axis=-1, keepdims=True)
        o = o + jnp.where(hm, _dot_nt(p.astype(BF16), mv_t), 0.0)
    o_ref[...] = o


def _mem_attn(qm, mkv_t, batch, rows_per_batch, tm):
    nt = rows_per_batch // tm
    return pl.pallas_call(
        _mem_attn_body,
        grid=(batch, nt),
        in_specs=[pl.BlockSpec((tm, MEM_W), lambda b, i: (b * nt + i, 0)),
                  pl.BlockSpec((1, 2 * MEM_W, N_MEM), lambda b, i: (b, 0, 0))],
        out_specs=pl.BlockSpec((tm, MEM_W), lambda b, i: (b * nt + i, 0)),
        out_shape=jax.ShapeDtypeStruct((batch * rows_per_batch, MEM_W), F32),
        compiler_params=_params(("arbitrary", "arbitrary")),
        name="mem_attn",
    )(qm, mkv_t)


def _finish_body(x_ref, on_ref, or_ref, om_ref, z_ref, w_ref, g_ref, y_ref):
    o = jnp.concatenate([on_ref[...], or_ref[...], om_ref[...]], axis=-1)
    z = z_ref[...]
    mix = o * (z * jax.nn.sigmoid(z))
    xo = x_ref[...] + _dot(mix.astype(BF16), w_ref[...])
    y_ref[...] = xo * lax.rsqrt(jnp.mean(xo * xo, axis=-1, keepdims=True) + EPS) * g_ref[...]


def _finish(x2d, o_n, o_r, o_m, z, w_out, ln_final):
    n = x2d.shape[0]
    tm = 512
    row = lambda w: pl.BlockSpec((tm, w), lambda i: (i, 0))
    return pl.pallas_call(
        _finish_body,
        grid=(n // tm,),
        in_specs=[row(D_MODEL), row(512), row(256), row(256), row(D_MODEL),
                  pl.BlockSpec((D_MODEL, D_MODEL), lambda i: (0, 0)),
                  pl.BlockSpec((1, D_MODEL), lambda i: (0, 0))],
        out_specs=row(D_MODEL),
        out_shape=jax.ShapeDtypeStruct((n, D_MODEL), F32),
        compiler_params=_params(("arbitrary",)),
        name="finish",
    )(x2d, o_n, o_r, o_m, z, w_out, ln_final)


def _permute_w_in(w):
    sizes = (512, 256, 256, 256, 24, 512, 256, 256, 256, 256, 256, 256)
    offs = np.concatenate([[0], np.cumsum(sizes)])
    part = lambda i: w[:, int(offs[i]):int(offs[i + 1])]
    q_n, kv_c, kv_s, kv_w, g_n, z_n, q_r, k_r, v_r, z_r, q_m, z_m = [part(i) for i in range(12)]
    g_pad = jnp.pad(g_n, ((0, 0), (0, LANES - g_n.shape[1])))
    return jnp.concatenate([q_n, kv_c, kv_s, kv_w, q_r, k_r, v_r, q_m, z_n, z_r, z_m, g_pad], axis=1).astype(BF16)


def _compress_weights(pe, w1, b1, w2):
    pe_r = pe.reshape(2, 2, CMP_STRIDE, HEAD_DIM)
    pe_l = jnp.concatenate([pe_r, pe_r], axis=-1).reshape(2, 2, CMP_STRIDE, 1, LANES)
    w1_r = w1.reshape(2, 2, CMP_STRIDE, HEAD_DIM, CMP_HID)
    zw = jnp.zeros_like(w1_r)
    w1_bd = jnp.concatenate([jnp.concatenate([w1_r, zw], axis=-1),
                             jnp.concatenate([zw, w1_r], axis=-1)], axis=-2).astype(BF16)
    b1_l = jnp.concatenate([b1, b1], axis=-1).reshape(2, 1, 2 * CMP_HID)
    z2 = jnp.zeros_like(w2)
    w2_bd = jnp.concatenate([jnp.concatenate([w2, z2], axis=-1),
                             jnp.concatenate([z2, w2], axis=-1)], axis=-2).astype(BF16)
    return pe_l, w1_bd, b1_l, w2_bd


def _rope_tables(pos):
    half = HEAD_DIM // 2
    inv = ROPE_BASE ** (-jnp.arange(half, dtype=F32) / half)
    ang = pos.astype(F32)[:, None] * inv[None, :]
    cos, sin = jnp.cos(ang), jnp.sin(ang)
    cos_l = jnp.tile(jnp.concatenate([cos, cos], axis=-1), (1, RET_HEADS))
    sin_l = jnp.tile(jnp.concatenate([sin, sin], axis=-1), (1, RET_HEADS))
    return cos_l, sin_l


def _block_diag_state(st):
    b = st.shape[0]
    eye = jnp.eye(RET_HEADS, dtype=st.dtype)
    return jnp.einsum("bhde,hg->bhdge", st, eye).reshape(b, RET_W, RET_W)


def _diag_blocks(st_bd):
    b = st_bd.shape[0]
    r = st_bd.reshape(b, RET_HEADS, HEAD_DIM, RET_HEADS, HEAD_DIM)
    return jnp.stack([r[:, h, :, h, :] for h in range(RET_HEADS)], axis=1)


def _kv_shape(a, b, t):
    return a.reshape(1, b, t, 2, NSA_KV_HEADS, HEAD_DIM)


def _position_minor(cache, heads):
    b, l = cache.shape[:2]
    return jnp.transpose(cache, (0, 2, 3, 4, 1)).reshape(b, 2 * heads * HEAD_DIM, l)


def _from_position_minor(a, heads):
    b, _, l = a.shape
    return jnp.transpose(a.reshape(b, 2, heads, HEAD_DIM, l), (0, 4, 1, 2, 3))[None]


def kernel(x_prompt, mem_prompt, x_sample, cache_nsa_cmp, cache_nsa_slc, cache_nsa_win, state_ret, cache_mem,
           page_table, ln_mix, w_in, cmp_pe, cmp_w1, cmp_b1, cmp_w2, ret_gn, ln_mem, w_mem_kv, w_out, ln_final):
    bp, t_len = x_prompt.shape[:2]
    bs, s_len = x_sample.shape[:2]
    assert (t_len, s_len) == (SEQ, DEC_SEQ) and ln_mix.shape[0] == 1
    w_perm = _permute_w_in(w_in[0])
    comp_w = _compress_weights(cmp_pe[0], cmp_w1[0], cmp_b1[0], cmp_w2[0])
    ln_g = ln_mix[0].reshape(1, D_MODEL)
    gn = ret_gn[0].reshape(1, RET_W)
    w_out_b = w_out[0].astype(BF16)
    ln_f = ln_final.reshape(1, D_MODEL)

    xp2 = x_prompt.reshape(bp * SEQ, D_MODEL)
    (qn, kvc_t, kvs_t, kvw_t, qkvr, qm, z, gt, ksk, ksvt, kwk, kwvt) = _inproj(xp2, ln_g, w_perm, SEQ, True)
    kc, vc = _compress_prompt(kvc_t, comp_w)
    o_n = _nsa_prompt(qn, gt, ksk, ksvt, kwk, kwvt, kc, vc, bp)
    cos_p, sin_p = _rope_tables(jnp.arange(SEQ))
    o_r, st_p = _retention(qkvr, cos_p, sin_p, jnp.zeros((bp, RET_W, RET_W), F32), gn, bp, SEQ, 256)
    mkv_t = _mem_kv(mem_prompt, ln_mem[0].reshape(1, D_MODEL), w_mem_kv[0].astype(BF16))
    o_m = _mem_attn(qm, mkv_t, bp, SEQ, 512)
    y_prompt = _finish(xp2, o_n, o_r, o_m, z, w_out_b, ln_f).reshape(bp, SEQ, D_MODEL)
    new_cmp_p = _from_position_minor(kvc_t, NSA_KV_HEADS)
    new_slc_p = _from_position_minor(kvs_t, NSA_KV_HEADS)
    new_win_p = _from_position_minor(kvw_t[:, :, SEQ - WINDOW:], NSA_KV_HEADS)
    new_ret_p = _diag_blocks(st_p)[None]
    new_mem_p = _from_position_minor(mkv_t, 4)

    xs2 = x_sample.reshape(bs * DEC_SEQ, D_MODEL)
    (qn_s, kvc_s, kvs_s, kvw_s, qkvr_s, qm_s, z_s, gt_s) = _inproj(xs2, ln_g, w_perm, DEC_SEQ, False)
    pool_cmp = _position_minor(cache_nsa_cmp[0], NSA_KV_HEADS).reshape(-1, 2, LANES, PAGE_SIZE)
    pool_slc = _position_minor(cache_nsa_slc[0], NSA_KV_HEADS).reshape(-1, 2, LANES, PAGE_SIZE)
    kc_s, vc_s = _compress_sample(page_table, pool_cmp, comp_w)
    q5 = qn_s.astype(F32).reshape(bs, DEC_SEQ, NSA_KV_HEADS, NSA_GROUP, HEAD_DIM).transpose(0, 2, 3, 1, 4)
    zq = jnp.zeros_like(q5[:, 0])
    qp = jnp.stack([jnp.concatenate([q5[:, 0], zq], axis=-1), jnp.concatenate([zq, q5[:, 1]], axis=-1)], axis=1)
    qp = qp.reshape(bs, ROWS_S, LANES)
    g5 = gt_s[:3 * NSA_HEADS].reshape(3, NSA_KV_HEADS, NSA_GROUP, bs, DEC_SEQ).transpose(3, 1, 2, 4, 0)
    gates = jnp.pad(g5.reshape(bs, ROWS_S, 3), ((0, 0), (0, 0), (0, LANES - 3)))
    expand = jnp.asarray(np.repeat(np.eye(N_SLC, dtype=np.float32), SLC_BLOCK, axis=1), dtype=BF16)
    win_t = _position_minor(cache_nsa_win[0], NSA_KV_HEADS)
    wtail = jnp.pad(kvw_s.reshape(bs, DEC_SEQ, 256).transpose(0, 2, 1), ((0, 0), (0, 0), (LANES - DEC_SEQ, 0)))
    o_sn, new_win_t = _nsa_sample(page_table, pool_slc, qp, gates, kc_s, vc_s, kvs_s, kvw_s, win_t, wtail, expand)
    o6 = o_sn.reshape(bs, NSA_KV_HEADS, NSA_GROUP, DEC_SEQ, NSA_KV_HEADS, HEAD_DIM)
    o_n_s = jnp.stack([o6[:, 0, :, :, 0], o6[:, 1, :, :, 1]], axis=1)
    o_n_s = o_n_s.transpose(0, 3, 1, 2, 4).reshape(bs * DEC_SEQ, 512)
    cos_s, sin_s = _rope_tables(PAST_LEN + jnp.arange(DEC_SEQ))
    o_r_s, st_s = _retention(qkvr_s, cos_s, sin_s, _block_diag_state(state_ret[0]), gn, bs, DEC_SEQ, DEC_SEQ)
    o_m_s = _mem_attn(qm_s, _position_minor(cache_mem[0], 4), bs, DEC_SEQ, DEC_SEQ)
    y_sample = _finish(xs2, o_n_s, o_r_s, o_m_s, z_s, w_out_b, ln_f).reshape(bs, DEC_SEQ, D_MODEL)
    new_cmp_s = _kv_shape(kvc_s, bs, DEC_SEQ)
    new_slc_s = _kv_shape(kvs_s, bs, DEC_SEQ)
    new_win_s = _from_position_minor(new_win_t, NSA_KV_HEADS)
    new_ret_s = _diag_blocks(st_s)[None]

    return (y_prompt, y_sample, new_cmp_p, new_cmp_s, new_slc_p, new_slc_s, new_win_p, new_win_s,
            new_ret_p, new_ret_s, new_mem_p)
```

```python
import functools

import numpy as np
import jax
import jax.numpy as jnp
from jax import lax
from jax.experimental import pallas as pl
from jax.experimental.pallas import tpu as pltpu

D_MODEL = 1024
SEQ = 8192
DEC_SEQ = 8
PAST_LEN = 8192
PAGE_SIZE = 128
HEAD_DIM = 64
NSA_HEADS = 8
NSA_KV_HEADS = 2
NSA_GROUP = 4
CMP_LEN = 32
CMP_STRIDE = 16
CMP_HID = 128
SLC_BLOCK = 64
SLC_TOPK = 16
WINDOW = 512
RET_HEADS = 4
N_MEM = 256
Q_BLOCK = 128
ROPE_BASE = 10000.0
EPS = 1e-6
NEG_INF = -1e30
SCALE = HEAD_DIM ** -0.5

N_CMP = 512
N_SLC = 128
LANES = 128
VMEM_LIMIT = 56 * 1024 * 1024

_W_Q, _W_KVC, _W_KVS, _W_KVW, _W_QKVR, _W_QM, _W_Z, _W_G = 0, 512, 768, 1024, 1280, 2048, 2304, 3328
IN_WP = 3456

F32 = jnp.float32
BF16 = jnp.bfloat16

_LOG_G = [float(np.log1p(-(2.0 ** (-5.0 - h)))) for h in range(RET_HEADS)]


def _dot(a, b):
    return jnp.dot(a, b, preferred_element_type=F32)


def _dot_nt(a, b):
    return lax.dot_general(a, b, (((1,), (1,)), ((), ())), preferred_element_type=F32)


def _split3(x):
    hi = x.astype(BF16)
    r1 = x - hi.astype(F32)
    mid = r1.astype(BF16)
    lo = (r1 - mid.astype(F32)).astype(BF16)
    return hi, mid, lo


def _dot_exact_rhs(a_bf16, x):
    hi, mid, lo = _split3(x)
    return _dot(a_bf16, hi) + _dot(a_bf16, mid) + _dot(a_bf16, lo)


def _dot_exact_lhs(x, b_bf16):
    hi, mid, lo = _split3(x)
    return _dot(hi, b_bf16) + _dot(mid, b_bf16) + _dot(lo, b_bf16)


def _iota(shape, dim):
    return lax.broadcasted_iota(jnp.int32, shape, dim)


def _params(sem):
    return pltpu.CompilerParams(dimension_semantics=sem, vmem_limit_bytes=VMEM_LIMIT)


def _inproj_body(x_ref, g_ref, w_ref, qn_ref, kvc_ref, kvs_ref, kvw_ref, qkvr_ref, qm_ref, z_ref, gt_ref,
                 *attn_refs, transposed_kv):
    x = x_ref[...]
    xn = x * lax.rsqrt(jnp.mean(x * x, axis=-1, keepdims=True) + EPS) * g_ref[...]
    xb = xn.astype(BF16)

    def proj(a, b):
        return _dot(xb, w_ref[:, a:b])

    qn_ref[...] = proj(_W_Q, _W_KVC).astype(BF16)
    kvc = proj(_W_KVC, _W_KVS)
    kvs = proj(_W_KVS, _W_KVW)
    kvw = proj(_W_KVW, _W_QKVR)
    if transposed_kv:
        ksk_ref, ksvt_ref, kwk_ref, kwvt_ref = attn_refs
        kvc_ref[0] = kvc.T
        kvs_t = kvs.T
        kvs_ref[0] = kvs_t
        ksk_ref[...] = kvs[:, :LANES].astype(BF16)
        ksvt_ref[0] = kvs_t[LANES:, :].astype(BF16)
        kvw_t = kvw.T
        kvw_ref[0] = kvw_t
        kwk_ref[...] = kvw[:, :LANES].astype(BF16)
        kwvt_ref[0] = kvw_t[LANES:, :].astype(BF16)
    else:
        kvc_ref[...] = kvc
        kvs_ref[...] = kvs
        kvw_ref[...] = kvw
    qkvr_ref[...] = proj(_W_QKVR, _W_QM)
    qm_ref[...] = proj(_W_QM, _W_Z)
    z_ref[...] = proj(_W_Z, _W_G)
    gt_ref[...] = jax.nn.sigmoid(proj(_W_G, IN_WP)).T


def _inproj(x2d, ln_g, w_perm, rows_per_batch, transposed_kv):
    n = x2d.shape[0]
    tm = 512
    nt = n // tm
    row = lambda w: pl.BlockSpec((tm, w), lambda i: (i, 0))
    if transposed_kv:
        tpb = rows_per_batch // tm
        kv_shape = jax.ShapeDtypeStruct((n // rows_per_batch, 256, rows_per_batch), F32)
        kv_spec = pl.BlockSpec((1, 256, tm), lambda i: (i // tpb, 0, i % tpb))
    else:
        kv_shape = jax.ShapeDtypeStruct((n, 256), F32)
        kv_spec = row(256)
    out_shape = [
        jax.ShapeDtypeStruct((n, 512), BF16),
        kv_shape, kv_shape, kv_shape,
        jax.ShapeDtypeStruct((n, 768), F32),
        jax.ShapeDtypeStruct((n, 256), F32),
        jax.ShapeDtypeStruct((n, 1024), F32),
        jax.ShapeDtypeStruct((LANES, n), F32),
    ]
    out_specs = [row(512), kv_spec, kv_spec, kv_spec, row(768), row(256), row(1024),
                 pl.BlockSpec((LANES, tm), lambda i: (0, i))]
    if transposed_kv:
        tile_t = pl.BlockSpec((1, LANES, tm), lambda i: (i, 0, 0))
        out_shape += [jax.ShapeDtypeStruct((n, LANES), BF16),
                      jax.ShapeDtypeStruct((nt, LANES, tm), BF16),
                      jax.ShapeDtypeStruct((n, LANES), BF16),
                      jax.ShapeDtypeStruct((nt, LANES, tm), BF16)]
        out_specs += [row(LANES), tile_t, row(LANES), tile_t]
    return pl.pallas_call(
        functools.partial(_inproj_body, transposed_kv=transposed_kv),
        grid=(nt,),
        in_specs=[row(D_MODEL),
                  pl.BlockSpec((1, D_MODEL), lambda i: (0, 0)),
                  pl.BlockSpec((D_MODEL, IN_WP), lambda i: (0, 0))],
        out_specs=tuple(out_specs),
        out_shape=tuple(out_shape),
        compiler_params=_params(("arbitrary",)),
        name="inproj",
    )(x2d, ln_g, w_perm)


def _mem_kv_body(x_ref, g_ref, w_ref, o_ref):
    x = x_ref[0]
    xn = x * lax.rsqrt(jnp.mean(x * x, axis=-1, keepdims=True) + EPS) * g_ref[...]
    o_ref[0] = _dot(xn.astype(BF16), w_ref[...]).T


def _mem_kv(mem, ln_g, w_bf16):
    b = mem.shape[0]
    nw = w_bf16.shape[1]
    return pl.pallas_call(
        _mem_kv_body,
        grid=(b,),
        in_specs=[pl.BlockSpec((1, N_MEM, D_MODEL), lambda i: (i, 0, 0)),
                  pl.BlockSpec((1, D_MODEL), lambda i: (0, 0)),
                  pl.BlockSpec((D_MODEL, nw), lambda i: (0, 0))],
        out_specs=pl.BlockSpec((1, nw, N_MEM), lambda i: (i, 0, 0)),
        out_shape=jax.ShapeDtypeStruct((b, nw, N_MEM), F32),
        compiler_params=_params(("arbitrary",)),
        name="mem_kv",
    )(mem, ln_g, w_bf16)


def _compress_slab(slabs_t, x_ref, pe_ref, w1_ref, b1_ref, w2_ref, out_refs):
    rows = _iota((N_CMP, LANES), 0)
    hw = 2 * CMP_HID
    for v in range(2):
        for j in range(PAST_LEN // LANES):
            x_ref[j * LANES:(j + 1) * LANES, :] = slabs_t[v][:, j * LANES:(j + 1) * LANES].T
        acc = jnp.zeros((N_CMP + PE_ROWS, 2 * hw), F32)
        for pp in range(CMP_STRIDE // 2):
            xcat = jnp.concatenate([x_ref[pl.ds(2 * pp, N_CMP, stride=CMP_STRIDE), :],
                                    x_ref[pl.ds(2 * pp + 1, N_CMP, stride=CMP_STRIDE), :]], axis=1)
            lhs = jnp.concatenate([xcat.astype(BF16), pe_ref[v, pp]], axis=0)
            acc = acc + _dot(lhs, w1_ref[v, pp])
        bias = b1_ref[v] + acc[N_CMP:N_CMP + 1, 0:hw] + acc[N_CMP + 1:N_CMP + 2, hw:2 * hw]
        nxt = pltpu.roll(acc[0:N_CMP, hw:2 * hw], N_CMP - 1, 0)
        hid = jax.nn.gelu(acc[0:N_CMP, 0:hw] + nxt + bias)
        out = _dot(hid.astype(BF16), w2_ref[v])
        out_refs[v][...] = jnp.where(rows < N_CMP - 1, out, 0.0)


def _compress_prompt_body(k_ref, v_ref, pe_ref, w1_ref, b1_ref, w2_ref, kc_ref, vc_ref, x_ref):
    _compress_slab((k_ref.at[0], v_ref.at[0]), x_ref, pe_ref, w1_ref, b1_ref, w2_ref, (kc_ref.at[0], vc_ref.at[0]))


PE_ROWS = 16


def _comp_weight_specs():
    z = lambda *_: (0, 0, 0)
    z4 = lambda *_: (0, 0, 0, 0)
    return [pl.BlockSpec((2, CMP_STRIDE // 2, PE_ROWS, 2 * LANES), z4),
            pl.BlockSpec((2, CMP_STRIDE // 2, 2 * LANES, 4 * CMP_HID), z4),
            pl.BlockSpec((2, 1, 2 * CMP_HID), z),
            pl.BlockSpec((2, 2 * CMP_HID, LANES), z)]


def _compress_prompt(kvc_t, comp_w):
    b = kvc_t.shape[0]
    out = jax.ShapeDtypeStruct((b, N_CMP, LANES), F32)
    return pl.pallas_call(
        _compress_prompt_body,
        grid=(b,),
        in_specs=[pl.BlockSpec((1, LANES, SEQ), lambda i: (i, 0, 0)),
                  pl.BlockSpec((1, LANES, SEQ), lambda i: (i, 1, 0))] + _comp_weight_specs(),
        out_specs=(pl.BlockSpec((1, N_CMP, LANES), lambda i: (i, 0, 0)),) * 2,
        out_shape=(out, out),
        scratch_shapes=[pltpu.VMEM((SEQ, LANES), F32)],
        compiler_params=_params(("arbitrary",)),
        name="compress_prompt",
    )(kvc_t, kvc_t, *comp_w)


N_PAGES = PAST_LEN // PAGE_SIZE


def _page_copy(pt_ref, pool_ref, slab_ref, sem_ref, b, slot, j):
    return pltpu.make_async_copy(pool_ref.at[pt_ref[b, j]],
                                 slab_ref.at[slot, :, :, pl.ds(j * PAGE_SIZE, PAGE_SIZE)],
                                 sem_ref.at[slot])


def _start_pages(pt_ref, pool_ref, slab_ref, sem_ref, b, slot):
    def body(j, c):
        _page_copy(pt_ref, pool_ref, slab_ref, sem_ref, b, slot, j).start()
        return c
    lax.fori_loop(0, N_PAGES, body, 0, unroll=8)


def _wait_pages(pt_ref, pool_ref, slab_ref, sem_ref, b, slot):
    def body(j, c):
        _page_copy(pt_ref, pool_ref, slab_ref, sem_ref, b, slot, j).wait()
        return c
    lax.fori_loop(0, N_PAGES, body, 0, unroll=8)


def _stream_slab(pt_ref, pool_ref, slab_ref, sem_ref):
    b = pl.program_id(0)
    nb = pl.num_programs(0)
    slot = b % 2

    @pl.when(b == 0)
    def _():
        _start_pages(pt_ref, pool_ref, slab_ref, sem_ref, 0, 0)

    @pl.when(b + 1 < nb)
    def _():
        _start_pages(pt_ref, pool_ref, slab_ref, sem_ref, b + 1, 1 - slot)

    _wait_pages(pt_ref, pool_ref, slab_ref, sem_ref, b, slot)
    return slot


def _compress_sample_body(pt_ref, pool_ref, pe_ref, w1_ref, b1_ref, w2_ref, kc_ref, vc_ref, slab_ref, sem_ref,
                          x_ref):
    slot = _stream_slab(pt_ref, pool_ref, slab_ref, sem_ref)
    _compress_slab((slab_ref.at[slot, 0], slab_ref.at[slot, 1]), x_ref, pe_ref, w1_ref, b1_ref, w2_ref,
                   (kc_ref.at[0], vc_ref.at[0]))


def _slab_scratch():
    return [pltpu.VMEM((2, 2, LANES, PAST_LEN), F32), pltpu.SemaphoreType.DMA((2,))]


def _compress_sample(page_table, pool, comp_w):
    nb = page_table.shape[0]
    out = jax.ShapeDtypeStruct((nb, N_CMP, LANES), F32)
    grid_spec = pltpu.PrefetchScalarGridSpec(
        num_scalar_prefetch=1,
        grid=(nb,),
        in_specs=[pl.BlockSpec(memory_space=pl.ANY)] + _comp_weight_specs(),
        out_specs=(pl.BlockSpec((1, N_CMP, LANES), lambda i, pt: (i, 0, 0)),) * 2,
        scratch_shapes=_slab_scratch() + [pltpu.VMEM((PAST_LEN, LANES), F32)],
    )
    return pl.pallas_call(
        _compress_sample_body,
        grid_spec=grid_spec,
        out_shape=(out, out),
        compiler_params=_params(("arbitrary",)),
        name="compress_sample",
    )(page_table, pool, *comp_w)


def _top_k_rows(score, blk, k):
    cand = score > -jnp.inf
    nblk = score.shape[0]
    for _ in range(k):
        m = jnp.max(score, axis=0, keepdims=True)
        idx = jnp.min(jnp.where(score == m, blk, nblk), axis=0, keepdims=True)
        score = jnp.where(blk == idx, -jnp.inf, score)
    return cand & (score == -jnp.inf)


def _add_q_bias(s, bias):
    return jnp.concatenate([s[:, g * Q_BLOCK:(g + 1) * Q_BLOCK] + bias for g in range(NSA_GROUP)], axis=1)


def _safe_inv(l):
    return 1.0 / jnp.maximum(l, 1e-30)


KT_SLC = 512
WIN_KEYS = WINDOW + Q_BLOCK
N_FORCED = 3


def _nsa_prompt_body(qn_ref, gt_ref, ksk_ref, ksvt_ref, kwk_ref, kwvt_ref, kc_ref, vc_ref, o_ref, selb_ref):
    i = pl.program_id(1)
    q0 = i * Q_BLOCK
    qt = (qn_ref[...].astype(F32) * SCALE).T
    t_row = q0 + _iota((1, Q_BLOCK), 1)
    kc = kc_ref[0].astype(BF16)
    vct = vc_ref[0].T.astype(BF16)
    gt = gt_ref[...]
    zeros_q = jnp.zeros((HEAD_DIM, NSA_GROUP * Q_BLOCK), F32)

    mi = _iota((N_SLC, N_CMP), 0)
    mj = _iota((N_SLC, N_CMP), 1)
    msel = ((mj >= 4 * mi - 1) & (mj <= 4 * mi + 3) & (mj < N_CMP - 1)).astype(BF16)
    blk = _iota((N_SLC, Q_BLOCK), 0)
    cur = t_row // SLC_BLOCK
    forced = (blk == 0) | (blk == cur) | (blk == cur - 1)
    free = (blk * SLC_BLOCK <= t_row) & jnp.logical_not(forced)
    cend = _iota((N_CMP, 1), 0) * CMP_STRIDE + (CMP_LEN - 1)
    bias_c = jnp.where(cend <= t_row, 0.0, NEG_INF)
    wt = jnp.maximum(i - WINDOW // Q_BLOCK, 0)
    w0 = pl.multiple_of(wt * Q_BLOCK, Q_BLOCK)
    kpw = w0 + _iota((WIN_KEYS, 1), 0)
    bias_w = jnp.where((kpw <= t_row) & (kpw > t_row - WINDOW), 0.0, NEG_INF)
    krow = _iota((SLC_BLOCK, 1), 0)

    qps, o_cs, o_ws = [], [], []
    vrows = [slice(kvh * HEAD_DIM, (kvh + 1) * HEAD_DIM) for kvh in range(NSA_KV_HEADS)]
    for kvh in range(NSA_KV_HEADS):
        qg = jnp.concatenate([qt[(kvh * NSA_GROUP + g) * HEAD_DIM:(kvh * NSA_GROUP + g + 1) * HEAD_DIM, :]
                              for g in range(NSA_GROUP)], axis=1)
        qp = jnp.concatenate([qg, zeros_q] if kvh == 0 else [zeros_q, qg], axis=0).astype(BF16)
        vrow = vrows[kvh]
        qps.append(qp)

        s = _add_q_bias(_dot(kc, qp), bias_c)
        m = jnp.max(s, axis=0, keepdims=True)
        e = jnp.exp(s - m)
        inv = jnp.where(m > 0.5 * NEG_INF, _safe_inv(jnp.sum(e, axis=0, keepdims=True)), 0.0)
        p = e * inv
        o_c = _dot(vct[vrow, :], p.astype(BF16))
        imp = (p[:, 0:Q_BLOCK] + p[:, Q_BLOCK:2 * Q_BLOCK]
               + p[:, 2 * Q_BLOCK:3 * Q_BLOCK] + p[:, 3 * Q_BLOCK:4 * Q_BLOCK])

        score = jnp.where(free, _dot_exact_rhs(msel, imp), -jnp.inf)
        picked = _top_k_rows(score, blk, SLC_TOPK - N_FORCED)
        selb_ref[kvh] = jnp.where(forced | picked, 0.0, NEG_INF)
        o_cs.append(o_c)

        sw = _add_q_bias(_dot(kwk_ref[pl.ds(w0, WIN_KEYS), :], qp), bias_w)
        ew = jnp.exp(sw - jnp.max(sw, axis=0, keepdims=True))
        pw = (ew * _safe_inv(jnp.sum(ew, axis=0, keepdims=True))).astype(BF16)
        o_w = jnp.zeros((HEAD_DIM, NSA_GROUP * Q_BLOCK), F32)
        for r in range(WIN_KEYS // Q_BLOCK):
            tix = wt + r
            vt = kwvt_ref[tix // 4, vrow, :]
            sub = tix % 4
            vpiece = jnp.where(sub == 0, vt[:, 0:128],
                               jnp.where(sub == 1, vt[:, 128:256],
                                         jnp.where(sub == 2, vt[:, 256:384], vt[:, 384:512])))
            o_w = o_w + _dot(vpiece, pw[r * Q_BLOCK:(r + 1) * Q_BLOCK, :])
        o_ws.append(o_w)

    def slc_tile(j, carry, causal):
        k0 = pl.multiple_of(j * KT_SLC, KT_SLC)
        k_tile = ksk_ref[pl.ds(k0, KT_SLC), :]
        out = []
        for kvh in range(NSA_KV_HEADS):
            m_run, l_run, acc = carry[kvh]
            sj = _dot(k_tile, qps[kvh])
            b8 = selb_ref[kvh, pl.ds(pl.multiple_of(j * 8, 8), 8), :]
            parts = []
            for r in range(KT_SLC // SLC_BLOCK):
                b = b8[r:r + 1, :]
                if causal:
                    b = jnp.where(k0 + r * SLC_BLOCK + krow <= t_row, b, NEG_INF)
                parts.append(_add_q_bias(sj[r * SLC_BLOCK:(r + 1) * SLC_BLOCK, :], b))
            m_new = m_run
            for part in parts:
                m_new = jnp.maximum(m_new, jnp.max(part, axis=0, keepdims=True))
            alpha = jnp.exp(m_run - m_new)
            ej = jnp.concatenate([jnp.exp(part - m_new) for part in parts], axis=0)
            l_new = alpha * l_run + jnp.sum(ej, axis=0, keepdims=True)
            acc = alpha * acc + _dot(ksvt_ref[j, vrows[kvh], :], ej.astype(BF16))
            out.append((m_new, l_new, acc))
        return tuple(out)

    init1 = (jnp.full((1, NSA_GROUP * Q_BLOCK), NEG_INF, F32),
             jnp.zeros((1, NSA_GROUP * Q_BLOCK), F32),
             jnp.zeros((HEAD_DIM, NSA_GROUP * Q_BLOCK), F32))
    n_full = q0 // KT_SLC
    carry = lax.fori_loop(0, n_full, functools.partial(slc_tile, causal=False), (init1,) * NSA_KV_HEADS)
    carry = slc_tile(n_full, carry, True)

    outs = []
    for kvh in range(NSA_KV_HEADS):
        _, l_s, acc_s = carry[kvh]
        o_s = acc_s * _safe_inv(l_s)

        def gate(br):
            base = br * NSA_HEADS + kvh * NSA_GROUP
            return jnp.concatenate([gt[base + g:base + g + 1, :] for g in range(NSA_GROUP)], axis=1)

        o = gate(0) * o_cs[kvh] + gate(1) * o_s + gate(2) * o_ws[kvh]
        outs.extend([o[:, g * Q_BLOCK:(g + 1) * Q_BLOCK] for g in range(NSA_GROUP)])
    o_ref[...] = jnp.concatenate(outs, axis=0).T


def _nsa_prompt(qn, gt, ksk, ksvt, kwk, kwvt, kc, vc, batch):
    nqb = SEQ // Q_BLOCK
    ntile = SEQ // 512
    return pl.pallas_call(
        _nsa_prompt_body,
        grid=(batch, nqb),
        in_specs=[pl.BlockSpec((Q_BLOCK, 512), lambda b, i: (b * nqb + i, 0)),
                  pl.BlockSpec((LANES, Q_BLOCK), lambda b, i: (0, b * nqb + i)),
                  pl.BlockSpec((SEQ, LANES), lambda b, i: (b, 0)),
                  pl.BlockSpec((ntile, LANES, 512), lambda b, i: (b, 0, 0)),
                  pl.BlockSpec((SEQ, LANES), lambda b, i: (b, 0)),
                  pl.BlockSpec((ntile, LANES, 512), lambda b, i: (b, 0, 0)),
                  pl.BlockSpec((1, N_CMP, LANES), lambda b, i: (b, 0, 0)),
                  pl.BlockSpec((1, N_CMP, LANES), lambda b, i: (b, 0, 0))],
        out_specs=pl.BlockSpec((Q_BLOCK, 512), lambda b, i: (b * nqb + i, 0)),
        out_shape=jax.ShapeDtypeStruct((batch * SEQ, 512), F32),
        scratch_shapes=[pltpu.VMEM((NSA_KV_HEADS, N_SLC, Q_BLOCK), F32)],
        compiler_params=_params(("arbitrary", "arbitrary")),
        name="nsa_prompt",
    )(qn, gt, ksk, ksvt, kwk, kwvt, kc, vc)


def _softmax_parts(parts, masks):
    m = None
    for s, mk in zip(parts, masks):
        mi = jnp.max(jnp.where(mk, s, NEG_INF), axis=-1, keepdims=True)
        m = mi if m is None else jnp.maximum(m, mi)
    es = [jnp.where(mk, jnp.exp(jnp.where(mk, s, NEG_INF) - m), 0.0) for s, mk in zip(parts, masks)]
    l = None
    for e in es:
        li = jnp.sum(e, axis=-1, keepdims=True)
        l = li if l is None else l + li
    inv = 1.0 / jnp.maximum(l, 1e-30)
    return [e * inv for e in es]


KC_SLC = 2048
ROWS_S = NSA_HEADS * DEC_SEQ


def _pad_rows(x, n):
    if x.shape[0] == n:
        return x
    return jnp.concatenate([x, jnp.zeros((n - x.shape[0], x.shape[1]), x.dtype)], axis=0)


def _nsa_sample_body(pt_ref, pool_ref, qp_ref, g_ref, kc_ref, vc_ref, kvs_ref, kvw_ref, win_ref, wtail_ref,
                     expand_ref, o_ref, nwin_ref, slab_ref, sem_ref):
    slot = _stream_slab(pt_ref, pool_ref, slab_ref, sem_ref)
    q = (qp_ref[0] * SCALE).astype(BF16)
    qi = _iota((ROWS_S, 1), 0) % DEC_SEQ
    t = PAST_LEN + qi

    kc = kc_ref[0].astype(BF16)
    vc = vc_ref[0].astype(BF16)
    cend = _iota((1, N_CMP), 1) * CMP_STRIDE + (CMP_LEN - 1)
    (p_c,) = _softmax_parts([_dot_nt(q, kc)], [cend <= t])
    o_c = _dot(p_c.astype(BF16), vc)

    half = ROWS_S // NSA_KV_HEADS
    imp = jnp.concatenate(
        [p_c[k * half:k * half + 8] + p_c[k * half + 8:k * half + 16]
         + p_c[k * half + 16:k * half + 24] + p_c[k * half + 24:k * half + 32] for k in range(NSA_KV_HEADS)], axis=0)
    mi = _iota((N_CMP, N_SLC), 0)
    mj = _iota((N_CMP, N_SLC), 1)
    msel_t = ((mi >= 4 * mj - 1) & (mi <= 4 * mj + 3) & (mi < N_CMP - 1)).astype(BF16)
    score = _dot_exact_lhs(imp, msel_t)
    score_t = _pad_rows(score, LANES).T
    blk = _iota((N_SLC, LANES), 0)
    forced = (blk == 0) | (blk == N_SLC - 1)
    picked = _top_k_rows(jnp.where(forced, -jnp.inf, score_t), blk, SLC_TOPK - 1 - 2)
    unsel = jnp.where(forced | picked, 0.0, 1.0).T
    unsel64 = jnp.concatenate([unsel[0:8]] * NSA_GROUP + [unsel[8:16]] * NSA_GROUP, axis=0).astype(BF16)

    slab_k = slab_ref.at[slot, 0]
    slab_v = slab_ref.at[slot, 1]
    jn = _iota((1, LANES), 1)
    k_new = _pad_rows(kvs_ref[:, 0:LANES], LANES).astype(BF16)
    v_new = _pad_rows(kvs_ref[:, LANES:2 * LANES], LANES).astype(BF16)
    s_new = _dot_nt(q, k_new) + jnp.where((jn <= qi) & (jn < DEC_SEQ), 0.0, NEG_INF)
    n_chunk = PAST_LEN // KC_SLC
    scs = []
    for c in range(n_chunk):
        ks_t = slab_k[:, c * KC_SLC:(c + 1) * KC_SLC].astype(BF16)
        bias = _dot(unsel64, expand_ref[:, c * KC_SLC:(c + 1) * KC_SLC]) * NEG_INF
        scs.append(_dot(q, ks_t) + bias)
    m_s = jnp.max(s_new, axis=-1, keepdims=True)
    for sc in scs:
        m_s = jnp.maximum(m_s, jnp.max(sc, axis=-1, keepdims=True))
    e_new = jnp.exp(s_new - m_s)
    l_s = jnp.sum(e_new, axis=-1, keepdims=True)
    acc = _dot(e_new.astype(BF16), v_new)
    for c in range(n_chunk):
        ec = jnp.exp(scs[c] - m_s)
        l_s = l_s + jnp.sum(ec, axis=-1, keepdims=True)
        acc = acc + _dot_nt(ec.astype(BF16), slab_v[:, c * KC_SLC:(c + 1) * KC_SLC].astype(BF16))
    o_s = acc * _safe_inv(l_s)

    wb = win_ref.shape[2]
    win_t = win_ref[0]
    kpos_b = PAST_LEN - wb + _iota((1, wb), 1)
    kpos_n = PAST_LEN + jn
    mask_b = (kpos_b <= t) & (kpos_b > t - WINDOW) & (kpos_b >= 0)
    mask_n = (kpos_n <= t) & (kpos_n > t - WINDOW) & (jn < DEC_SEQ)
    kw_new = _pad_rows(kvw_ref[:, 0:LANES], LANES).astype(BF16)
    vw_new = _pad_rows(kvw_ref[:, LANES:2 * LANES], LANES).astype(BF16)
    p_b, p_n = _softmax_parts([_dot(q, win_t[0:LANES, :].astype(BF16)), _dot_nt(q, kw_new)], [mask_b, mask_n])
    o_w = _dot_nt(p_b.astype(BF16), win_t[LANES:2 * LANES, :].astype(BF16)) + _dot(p_n.astype(BF16), vw_new)

    g = g_ref[0]
    o_ref[0] = g[:, 0:1] * o_c + g[:, 1:2] * o_s + g[:, 2:3] * o_w
    shifted = pltpu.roll(win_t, wb - DEC_SEQ, 1)
    nwin_ref[0, :, 0:wb - LANES] = shifted[:, 0:wb - LANES]
    nwin_ref[0, :, wb - LANES:wb] = jnp.where(jn >= LANES - DEC_SEQ, wtail_ref[0], shifted[:, wb - LANES:wb])


def _nsa_sample(page_table, pool_slc, qp, gates, kc, vc, kvs, kvw, win_t, wtail, expand):
    nb = page_table.shape[0]
    wb = win_t.shape[2]
    per_b = lambda *shape: pl.BlockSpec((1,) + shape, lambda i, pt: (i,) + (0,) * len(shape))
    rows8 = pl.BlockSpec((DEC_SEQ, 256), lambda i, pt: (i, 0))
    grid_spec = pltpu.PrefetchScalarGridSpec(
        num_scalar_prefetch=1,
        grid=(nb,),
        in_specs=[pl.BlockSpec(memory_space=pl.ANY),
                  per_b(ROWS_S, LANES), per_b(ROWS_S, LANES), per_b(N_CMP, LANES), per_b(N_CMP, LANES),
                  rows8, rows8, per_b(256, wb), per_b(256, LANES),
                  pl.BlockSpec((N_SLC, PAST_LEN), lambda i, pt: (0, 0))],
        out_specs=(per_b(ROWS_S, LANES), per_b(256, wb)),
        scratch_shapes=_slab_scratch(),
    )
    return pl.pallas_call(
        _nsa_sample_body,
        grid_spec=grid_spec,
        out_shape=(jax.ShapeDtypeStruct((nb, ROWS_S, LANES), F32),
                   jax.ShapeDtypeStruct((nb, 256, wb), F32)),
        compiler_params=_params(("arbitrary",)),
        name="nsa_sample",
    )(page_table, pool_slc, qp, gates, kc, vc, kvs, kvw, win_t, wtail, expand)


RET_W = RET_HEADS * HEAD_DIM


def _retention_body(q_ref, k_ref, v_ref, cos_ref, sin_ref, st0_ref, gn_ref, o_ref, st_ref, state_ref, *, chunk):
    c = pl.program_id(1)
    ck = max(chunk, LANES)

    @pl.when(c == 0)
    def _():
        state_ref[...] = st0_ref[0]

    lane = _iota((1, RET_W), 1)
    head_of_lane = lane // HEAD_DIM
    low_half = (lane % HEAD_DIM) < HEAD_DIM // 2
    cos = cos_ref[...]
    sin = sin_ref[...]

    def rope(x):
        rot = jnp.where(low_half, -pltpu.roll(x, RET_W - HEAD_DIM // 2, 1), pltpu.roll(x, HEAD_DIM // 2, 1))
        return x * cos + rot * sin

    logg_lane = jnp.zeros((1, RET_W), F32)
    for h in range(RET_HEADS):
        logg_lane = jnp.where(head_of_lane == h, _LOG_G[h], logg_lane)

    q = rope(q_ref[...]) * SCALE
    k = rope(k_ref[...])
    v = v_ref[...]
    kp = _pad_rows(k, ck)
    vp = _pad_rows(v, ck)
    n = _iota((chunk, 1), 0).astype(F32)
    nk = _iota((ck, 1), 0).astype(F32)
    diff = n - _iota((1, ck), 1).astype(F32)
    in_chunk = _iota((1, ck), 1) < chunk

    inner = jnp.zeros((chunk, RET_W), F32)
    for h in range(RET_HEADS):
        hm = head_of_lane == h
        dmat = jnp.where((diff >= 0) & in_chunk, jnp.exp(_LOG_G[h] * jnp.maximum(diff, 0.0)), 0.0)
        s = _dot_nt(jnp.where(hm, q, 0.0), kp) * dmat
        inner = inner + jnp.where(hm, _dot(s, vp), 0.0)
    state = state_ref[...]
    xi = jnp.exp(logg_lane * (n + 1.0))
    cross = _dot(q * xi, state)
    zeta = jnp.where(nk < chunk, jnp.exp(logg_lane * (chunk - 1.0 - nk)), 0.0)
    kz_t = (kp * zeta).T
    row_head = _iota((RET_W, 1), 0) // HEAD_DIM
    decay_rows = jnp.zeros((RET_W, 1), F32)
    for h in range(RET_HEADS):
        decay_rows = jnp.where(row_head == h, float(np.exp(_LOG_G[h] * chunk)), decay_rows)
    new_state = decay_rows * state + jnp.where(row_head == head_of_lane, _dot(kz_t, vp), 0.0)
    state_ref[...] = new_state
    st_ref[0] = new_state

    o = inner + cross
    avg = jnp.where(_iota((RET_W, RET_W), 0) // HEAD_DIM == _iota((RET_W, RET_W), 1) // HEAD_DIM,
                    1.0 / HEAD_DIM, 0.0).astype(BF16)
    mu = _dot_exact_lhs(o, avg)
    d = o - mu
    var = _dot_exact_lhs(d * d, avg)
    o_ref[...] = d * lax.rsqrt(var + EPS) * gn_ref[...]


def _retention(qkvr, cos, sin, state_bd, gn, batch, t_len, chunk):
    nch = t_len // chunk
    col = lambda j: pl.BlockSpec((chunk, RET_W), lambda b, c: (b * nch + c, j))
    tab = pl.BlockSpec((chunk, RET_W), lambda b, c: (c, 0))
    st = pl.BlockSpec((1, RET_W, RET_W), lambda b, c: (b, 0, 0))
    return pl.pallas_call(
        functools.partial(_retention_body, chunk=chunk),
        grid=(batch, nch),
        in_specs=[col(0), col(1), col(2), tab, tab, st, pl.BlockSpec((1, RET_W), lambda b, c: (0, 0))],
        out_specs=(pl.BlockSpec((chunk, RET_W), lambda b, c: (b * nch + c, 0)), st),
        out_shape=(jax.ShapeDtypeStruct((batch * t_len, RET_W), F32),
                   jax.ShapeDtypeStruct((batch, RET_W, RET_W), F32)),
        scratch_shapes=[pltpu.VMEM((RET_W, RET_W), F32)],
        compiler_params=_params(("arbitrary", "arbitrary")),
        name="retention",
    )(qkvr, qkvr, qkvr, cos, sin, state_bd, gn)


MEM_W = 4 * HEAD_DIM


def _mem_attn_body(q_ref, mkv_ref, o_ref):
    q = q_ref[...] * SCALE
    mk_t = mkv_ref[0, 0:MEM_W, :].astype(BF16)
    mv_t = mkv_ref[0, MEM_W:2 * MEM_W, :].astype(BF16)
    head_of_lane = _iota((1, MEM_W), 1) // HEAD_DIM
    o = jnp.zeros(q.shape, F32)
    for h in range(4):
        hm = head_of_lane == h
        s = _dot(jnp.where(hm, q, 0.0).astype(BF16), mk_t)
        e = jnp.exp(s - jnp.max(s, axis=-1, keepdims=True))
        p = e / jnp.sum(e, axis=-1, keepdims=True)
        o = o + jnp.where(hm, _dot_nt(p.astype(BF16), mv_t), 0.0)
    o_ref[...] = o


def _mem_attn(qm, mkv_t, batch, rows_per_batch, tm):
    nt = rows_per_batch // tm
    return pl.pallas_call(
        _mem_attn_body,
        grid=(batch, nt),
        in_specs=[pl.BlockSpec((tm, MEM_W), lambda b, i: (b * nt + i, 0)),
                  pl.BlockSpec((1, 2 * MEM_W, N_MEM), lambda b, i: (b, 0, 0))],
        out_specs=pl.BlockSpec((tm, MEM_W), lambda b, i: (b * nt + i, 0)),
        out_shape=jax.ShapeDtypeStruct((batch * rows_per_batch, MEM_W), F32),
        compiler_params=_params(("arbitrary", "arbitrary")),
        name="mem_attn",
    )(qm, mkv_t)


def _finish_body(x_ref, on_ref, or_ref, om_ref, z_ref, w_ref, g_ref, y_ref):
    o = jnp.concatenate([on_ref[...], or_ref[...], om_ref[...]], axis=-1)
    z = z_ref[...]
    mix = o * (z * jax.nn.sigmoid(z))
    xo = x_ref[...] + _dot(mix.astype(BF16), w_ref[...])
    y_ref[...] = xo * lax.rsqrt(jnp.mean(xo * xo, axis=-1, keepdims=True) + EPS) * g_ref[...]


def _finish(x2d, o_n, o_r, o_m, z, w_out, ln_final):
    n = x2d.shape[0]
    tm = 512
    row = lambda w: pl.BlockSpec((tm, w), lambda i: (i, 0))
    return pl.pallas_call(
        _finish_body,
        grid=(n // tm,),
        in_specs=[row(D_MODEL), row(512), row(256), row(256), row(D_MODEL),
                  pl.BlockSpec((D_MODEL, D_MODEL), lambda i: (0, 0)),
                  pl.BlockSpec((1, D_MODEL), lambda i: (0, 0))],
        out_specs=row(D_MODEL),
        out_shape=jax.ShapeDtypeStruct((n, D_MODEL), F32),
        compiler_params=_params(("arbitrary",)),
        name="finish",
    )(x2d, o_n, o_r, o_m, z, w_out, ln_final)


def _permute_w_in(w):
    sizes = (512, 256, 256, 256, 24, 512, 256, 256, 256, 256, 256, 256)
    offs = np.concatenate([[0], np.cumsum(sizes)])
    part = lambda i: w[:, int(offs[i]):int(offs[i + 1])]
    q_n, kv_c, kv_s, kv_w, g_n, z_n, q_r, k_r, v_r, z_r, q_m, z_m = [part(i) for i in range(12)]
    g_pad = jnp.pad(g_n, ((0, 0), (0, LANES - g_n.shape[1])))
    return jnp.concatenate([q_n, kv_c, kv_s, kv_w, q_r, k_r, v_r, q_m, z_n, z_r, z_m, g_pad], axis=1).astype(BF16)


def _compress_weights(pe, w1, b1, w2):
    npair = CMP_STRIDE // 2
    pe_r = pe.reshape(2, 2, npair, 2, HEAD_DIM)
    pe_l = jnp.broadcast_to(pe_r[:, :, :, :, None, :], (2, 2, npair, 2, NSA_KV_HEADS, HEAD_DIM))
    pe_l = pe_l.reshape(2, 2, npair, 2 * LANES).transpose(0, 2, 1, 3)
    pe_l = jnp.pad(pe_l, ((0, 0), (0, 0), (0, PE_ROWS - 2), (0, 0))).astype(BF16)
    w1_r = w1.reshape(2, 2, CMP_STRIDE, HEAD_DIM, CMP_HID)
    zw = jnp.zeros_like(w1_r)
    w1_bd = jnp.concatenate([jnp.concatenate([w1_r, zw], axis=-1),
                             jnp.concatenate([zw, w1_r], axis=-1)], axis=-2)
    w1_bd = w1_bd.reshape(2, 2, npair, 2 * LANES, 2 * CMP_HID)
    w1_bd = jnp.concatenate([w1_bd[:, 0], w1_bd[:, 1]], axis=-1).astype(BF16)
    b1_l = jnp.concatenate([b1, b1], axis=-1).reshape(2, 1, 2 * CMP_HID)
    z2 = jnp.zeros_like(w2)
    w2_bd = jnp.concatenate([jnp.concatenate([w2, z2], axis=-1),
                             jnp.concatenate([z2, w2], axis=-1)], axis=-2).astype(BF16)
    return pe_l, w1_bd, b1_l, w2_bd


def _rope_tables(pos):
    half = HEAD_DIM // 2
    inv = ROPE_BASE ** (-jnp.arange(half, dtype=F32) / half)
    ang = pos.astype(F32)[:, None] * inv[None, :]
    cos, sin = jnp.cos(ang), jnp.sin(ang)
    cos_l = jnp.tile(jnp.concatenate([cos, cos], axis=-1), (1, RET_HEADS))
    sin_l = jnp.tile(jnp.concatenate([sin, sin], axis=-1), (1, RET_HEADS))
    return cos_l, sin_l


def _block_diag_state(st):
    b = st.shape[0]
    eye = jnp.eye(RET_HEADS, dtype=st.dtype)
    return jnp.einsum("bhde,hg->bhdge", st, eye).reshape(b, RET_W, RET_W)


def _diag_blocks(st_bd):
    b = st_bd.shape[0]
    r = st_bd.reshape(b, RET_HEADS, HEAD_DIM, RET_HEADS, HEAD_DIM)
    return jnp.stack([r[:, h, :, h, :] for h in range(RET_HEADS)], axis=1)


def _kv_shape(a, b, t):
    return a.reshape(1, b, t, 2, NSA_KV_HEADS, HEAD_DIM)


def _position_minor(cache, heads):
    b, l = cache.shape[:2]
    return jnp.transpose(cache, (0, 2, 3, 4, 1)).reshape(b, 2 * heads * HEAD_DIM, l)


def _from_position_minor(a, heads):
    b, _, l = a.shape
    return jnp.transpose(a.reshape(b, 2, heads, HEAD_DIM, l), (0, 4, 1, 2, 3))[None]


def kernel(x_prompt, mem_prompt, x_sample, cache_nsa_cmp, cache_nsa_slc, cache_nsa_win, state_ret, cache_mem,
           page_table, ln_mix, w_in, cmp_pe, cmp_w1, cmp_b1, cmp_w2, ret_gn, ln_mem, w_mem_kv, w_out, ln_final):
    bp, t_len = x_prompt.shape[:2]
    bs, s_len = x_sample.shape[:2]
    assert (t_len, s_len) == (SEQ, DEC_SEQ) and ln_mix.shape[0] == 1
    w_perm = _permute_w_in(w_in[0])
    comp_w = _compress_weights(cmp_pe[0], cmp_w1[0], cmp_b1[0], cmp_w2[0])
    ln_g = ln_mix[0].reshape(1, D_MODEL)
    gn = ret_gn[0].reshape(1, RET_W)
    w_out_b = w_out[0].astype(BF16)
    ln_f = ln_final.reshape(1, D_MODEL)

    xp2 = x_prompt.reshape(bp * SEQ, D_MODEL)
    (qn, kvc_t, kvs_t, kvw_t, qkvr, qm, z, gt, ksk, ksvt, kwk, kwvt) = _inproj(xp2, ln_g, w_perm, SEQ, True)
    kc, vc = _compress_prompt(kvc_t, comp_w)
    o_n = _nsa_prompt(qn, gt, ksk, ksvt, kwk, kwvt, kc, vc, bp)
    cos_p, sin_p = _rope_tables(jnp.arange(SEQ))
    o_r, st_p = _retention(qkvr, cos_p, sin_p, jnp.zeros((bp, RET_W, RET_W), F32), gn, bp, SEQ, 256)
    mkv_t = _mem_kv(mem_prompt, ln_mem[0].reshape(1, D_MODEL), w_mem_kv[0].astype(BF16))
    o_m = _mem_attn(qm, mkv_t, bp, SEQ, 512)
    y_prompt = _finish(xp2, o_n, o_r, o_m, z, w_out_b, ln_f).reshape(bp, SEQ, D_MODEL)
    new_cmp_p = _from_position_minor(kvc_t, NSA_KV_HEADS)
    new_slc_p = _from_position_minor(kvs_t, NSA_KV_HEADS)
    new_win_p = _from_position_minor(kvw_t[:, :, SEQ - WINDOW:], NSA_KV_HEADS)
    new_ret_p = _diag_blocks(st_p)[None]
    new_mem_p = _from_position_minor(mkv_t, 4)

    xs2 = x_sample.reshape(bs * DEC_SEQ, D_MODEL)
    (qn_s, kvc_s, kvs_s, kvw_s, qkvr_s, qm_s, z_s, gt_s) = _inproj(xs2, ln_g, w_perm, DEC_SEQ, False)
    pool_cmp = _position_minor(cache_nsa_cmp[0], NSA_KV_HEADS).reshape(-1, 2, LANES, PAGE_SIZE)
    pool_slc = _position_minor(cache_nsa_slc[0], NSA_KV_HEADS).reshape(-1, 2, LANES, PAGE_SIZE)
    kc_s, vc_s = _compress_sample(page_table, pool_cmp, comp_w)
    q5 = qn_s.astype(F32).reshape(bs, DEC_SEQ, NSA_KV_HEADS, NSA_GROUP, HEAD_DIM).transpose(0, 2, 3, 1, 4)
    zq = jnp.zeros_like(q5[:, 0])
    qp = jnp.stack([jnp.concatenate([q5[:, 0], zq], axis=-1), jnp.concatenate([zq, q5[:, 1]], axis=-1)], axis=1)
    qp = qp.reshape(bs, ROWS_S, LANES)
    g5 = gt_s[:3 * NSA_HEADS].reshape(3, NSA_KV_HEADS, NSA_GROUP, bs, DEC_SEQ).transpose(3, 1, 2, 4, 0)
    gates = jnp.pad(g5.reshape(bs, ROWS_S, 3), ((0, 0), (0, 0), (0, LANES - 3)))
    expand = jnp.asarray(np.repeat(np.eye(N_SLC, dtype=np.float32), SLC_BLOCK, axis=1), dtype=BF16)
    win_t = _position_minor(cache_nsa_win[0], NSA_KV_HEADS)
    wtail = jnp.pad(kvw_s.reshape(bs, DEC_SEQ, 256).transpose(0, 2, 1), ((0, 0), (0, 0), (LANES - DEC_SEQ, 0)))
    o_sn, new_win_t = _nsa_sample(page_table, pool_slc, qp, gates, kc_s, vc_s, kvs_s, kvw_s, win_t, wtail, expand)
    o6 = o_sn.reshape(bs, NSA_KV_HEADS, NSA_GROUP, DEC_SEQ, NSA_KV_HEADS, HEAD_DIM)
    o_n_s = jnp.stack([o6[:, 0, :, :, 0], o6[:, 1, :, :, 1]], axis=1)
    o_n_s = o_n_s.transpose(0, 3, 1, 2, 4).reshape(bs * DEC_SEQ, 512)
    cos_s, sin_s = _rope_tables(PAST_LEN + jnp.arange(DEC_SEQ))
    o_r_s, st_s = _retention(qkvr_s, cos_s, sin_s, _block_diag_state(state_ret[0]), gn, bs, DEC_SEQ, DEC_SEQ)
    o_m_s = _mem_attn(qm_s, _position_minor(cache_mem[0], 4), bs, DEC_SEQ, DEC_SEQ)
    y_sample = _finish(xs2, o_n_s, o_r_s, o_m_s, z_s, w_out_b, ln_f).reshape(bs, DEC_SEQ, D_MODEL)
    new_cmp_s = _kv_shape(kvc_s, bs, DEC_SEQ)
    new_slc_s = _kv_shape(kvs_s, bs, DEC_SEQ)
    new_win_s = _from_position_minor(new_win_t, NSA_KV_HEADS)
    new_ret_s = _diag_blocks(st_s)[None]

    return (y_prompt, y_sample, new_cmp_p, new_cmp_s, new_slc_p, new_slc_s, new_win_p, new_win_s,
            new_ret_p, new_ret_s, new_mem_p)
```

```python
import functools

import numpy as np
import jax
import jax.numpy as jnp
from jax import lax
from jax.experimental import pallas as pl
from jax.experimental.pallas import tpu as pltpu

D_MODEL = 1024
SEQ = 8192
DEC_SEQ = 8
PAST_LEN = 8192
PAGE_SIZE = 128
HEAD_DIM = 64
NSA_HEADS = 8
NSA_KV_HEADS = 2
NSA_GROUP = 4
CMP_LEN = 32
CMP_STRIDE = 16
CMP_HID = 128
SLC_BLOCK = 64
SLC_TOPK = 16
WINDOW = 512
RET_HEADS = 4
N_MEM = 256
Q_BLOCK = 128
ROPE_BASE = 10000.0
EPS = 1e-6
NEG_INF = -1e30
SCALE = HEAD_DIM ** -0.5

N_CMP = 512
N_SLC = 128
LANES = 128
ONES_ROWS = 16
V_ROWS = HEAD_DIM + ONES_ROWS
LOG2E = 1.4426950408889634
VMEM_LIMIT = 56 * 1024 * 1024

_W_Q, _W_KVC, _W_KVS, _W_KVW, _W_QKVR, _W_QM, _W_Z, _W_G = 0, 512, 768, 1024, 1280, 2048, 2304, 3328
IN_WP = 3456

F32 = jnp.float32
BF16 = jnp.bfloat16

_LOG_G = [float(np.log1p(-(2.0 ** (-5.0 - h)))) for h in range(RET_HEADS)]


def _dot(a, b):
    return jnp.dot(a, b, preferred_element_type=F32)


def _dot_nt(a, b):
    return lax.dot_general(a, b, (((1,), (1,)), ((), ())), preferred_element_type=F32)


def _split3(x):
    hi = x.astype(BF16)
    r1 = x - hi.astype(F32)
    mid = r1.astype(BF16)
    lo = (r1 - mid.astype(F32)).astype(BF16)
    return hi, mid, lo


def _dot_exact_rhs(a_bf16, x):
    hi, mid, lo = _split3(x)
    return _dot(a_bf16, hi) + _dot(a_bf16, mid) + _dot(a_bf16, lo)


def _dot_exact_lhs(x, b_bf16):
    hi, mid, lo = _split3(x)
    return _dot(hi, b_bf16) + _dot(mid, b_bf16) + _dot(lo, b_bf16)


def _iota(shape, dim):
    return lax.broadcasted_iota(jnp.int32, shape, dim)


def _params(sem):
    return pltpu.CompilerParams(dimension_semantics=sem, vmem_limit_bytes=VMEM_LIMIT)


def _inproj_body(x_ref, g_ref, w_ref, qn_ref, kvc_ref, kvs_ref, kvw_ref, qkvr_ref, qm_ref, z_ref, gt_ref,
                 *attn_refs, transposed_kv):
    x = x_ref[...]
    xn = x * lax.rsqrt(jnp.mean(x * x, axis=-1, keepdims=True) + EPS) * g_ref[...]
    xb = xn.astype(BF16)

    def proj(a, b):
        return _dot(xb, w_ref[:, a:b])

    qn_ref[...] = proj(_W_Q, _W_KVC).astype(BF16)
    kvc = proj(_W_KVC, _W_KVS)
    kvs = proj(_W_KVS, _W_KVW)
    kvw = proj(_W_KVW, _W_QKVR)
    if transposed_kv:
        ksk_ref, ksvt_ref, kwk_ref, kwvt_ref = attn_refs
        kvc_ref[0] = kvc.T
        kvs_t = kvs.T
        kvs_ref[0] = kvs_t
        tm = kvs.shape[0]
        lane = _iota((tm, LANES), 1)
        blk_in_tile = _iota((tm, LANES), 0) // SLC_BLOCK
        k2 = kvs[:, :LANES]
        ksk_ref[:, 0:LANES] = jnp.where(lane < HEAD_DIM, k2,
                                        jnp.where(lane - HEAD_DIM == blk_in_tile, 1.0, 0.0)).astype(BF16)
        ksk_ref[:, LANES:2 * LANES] = jnp.where(lane >= HEAD_DIM, k2,
                                                jnp.where(lane == blk_in_tile, 1.0, 0.0)).astype(BF16)
        ones_rows = jnp.where(_iota((ONES_ROWS, tm), 0) == 0, 1.0, 0.0)
        ksvt_ref[0] = jnp.concatenate([kvs_t[LANES:LANES + HEAD_DIM, :], ones_rows,
                                       kvs_t[LANES + HEAD_DIM:, :], ones_rows], axis=0).astype(BF16)
        kvw_t = kvw.T
        kvw_ref[0] = kvw_t
        kwk_ref[...] = kvw[:, :LANES].astype(BF16)
        kwvt_ref[0] = kvw_t[LANES:, :].astype(BF16)
    else:
        kvc_ref[...] = kvc
        kvs_ref[...] = kvs
        kvw_ref[...] = kvw
    qkvr_ref[...] = proj(_W_QKVR, _W_QM)
    qm_ref[...] = proj(_W_QM, _W_Z)
    z_ref[...] = proj(_W_Z, _W_G)
    gt_ref[...] = jax.nn.sigmoid(proj(_W_G, IN_WP)).T


def _inproj(x2d, ln_g, w_perm, rows_per_batch, transposed_kv):
    n = x2d.shape[0]
    tm = 512
    nt = n // tm
    row = lambda w: pl.BlockSpec((tm, w), lambda i: (i, 0))
    if transposed_kv:
        tpb = rows_per_batch // tm
        kv_shape = jax.ShapeDtypeStruct((n // rows_per_batch, 256, rows_per_batch), F32)
        kv_spec = pl.BlockSpec((1, 256, tm), lambda i: (i // tpb, 0, i % tpb))
    else:
        kv_shape = jax.ShapeDtypeStruct((n, 256), F32)
        kv_spec = row(256)
    out_shape = [
        jax.ShapeDtypeStruct((n, 512), BF16),
        kv_shape, kv_shape, kv_shape,
        jax.ShapeDtypeStruct((n, 768), F32),
        jax.ShapeDtypeStruct((n, 256), F32),
        jax.ShapeDtypeStruct((n, 1024), F32),
        jax.ShapeDtypeStruct((LANES, n), F32),
    ]
    out_specs = [row(512), kv_spec, kv_spec, kv_spec, row(768), row(256), row(1024),
                 pl.BlockSpec((LANES, tm), lambda i: (0, i))]
    if transposed_kv:
        tile_t = pl.BlockSpec((1, LANES, tm), lambda i: (i, 0, 0))
        tile_v = pl.BlockSpec((1, NSA_KV_HEADS * V_ROWS, tm), lambda i: (i, 0, 0))
        out_shape += [jax.ShapeDtypeStruct((n, NSA_KV_HEADS * LANES), BF16),
                      jax.ShapeDtypeStruct((nt, NSA_KV_HEADS * V_ROWS, tm), BF16),
                      jax.ShapeDtypeStruct((n, LANES), BF16),
                      jax.ShapeDtypeStruct((nt, LANES, tm), BF16)]
        out_specs += [row(NSA_KV_HEADS * LANES), tile_v, row(LANES), tile_t]
    return pl.pallas_call(
        functools.partial(_inproj_body, transposed_kv=transposed_kv),
        grid=(nt,),
        in_specs=[row(D_MODEL),
                  pl.BlockSpec((1, D_MODEL), lambda i: (0, 0)),
                  pl.BlockSpec((D_MODEL, IN_WP), lambda i: (0, 0))],
        out_specs=tuple(out_specs),
        out_shape=tuple(out_shape),
        compiler_params=_params(("arbitrary",)),
        name="inproj",
    )(x2d, ln_g, w_perm)


def _mem_kv_body(x_ref, g_ref, w_ref, o_ref):
    x = x_ref[0]
    xn = x * lax.rsqrt(jnp.mean(x * x, axis=-1, keepdims=True) + EPS) * g_ref[...]
    o_ref[0] = _dot(xn.astype(BF16), w_ref[...]).T


def _mem_kv(mem, ln_g, w_bf16):
    b = mem.shape[0]
    nw = w_bf16.shape[1]
    return pl.pallas_call(
        _mem_kv_body,
        grid=(b,),
        in_specs=[pl.BlockSpec((1, N_MEM, D_MODEL), lambda i: (i, 0, 0)),
                  pl.BlockSpec((1, D_MODEL), lambda i: (0, 0)),
                  pl.BlockSpec((D_MODEL, nw), lambda i: (0, 0))],
        out_specs=pl.BlockSpec((1, nw, N_MEM), lambda i: (i, 0, 0)),
        out_shape=jax.ShapeDtypeStruct((b, nw, N_MEM), F32),
        compiler_params=_params(("arbitrary",)),
        name="mem_kv",
    )(mem, ln_g, w_bf16)


def _compress_slab(slabs_t, x_ref, pe_ref, w1_ref, b1_ref, w2_ref, out_refs):
    rows = _iota((N_CMP, LANES), 0)
    hw = 2 * CMP_HID
    chunks_per_page = LANES // CMP_STRIDE
    for v in range(2):
        for j in range(PAST_LEN // LANES):
            xt = slabs_t[v][:, j * LANES:(j + 1) * LANES].T
            for cc in range(chunks_per_page):
                r0 = (j * chunks_per_page + cc) * X_PITCH
                x_ref[v, r0:r0 + CMP_STRIDE, :] = xt[cc * CMP_STRIDE:(cc + 1) * CMP_STRIDE, :]
        acc = jnp.zeros((N_CMP + PE_ROWS, 2 * hw), F32)
        for pp in range(CMP_STRIDE // 2):
            xcat = jnp.concatenate([x_ref[v, pl.ds(2 * pp, N_CMP, stride=X_PITCH), :],
                                    x_ref[v, pl.ds(2 * pp + 1, N_CMP, stride=X_PITCH), :]], axis=1)
            lhs = jnp.concatenate([xcat.astype(BF16), pe_ref[v, pp]], axis=0)
            acc = acc + _dot(lhs, w1_ref[v, pp])
        bias = b1_ref[v] + acc[N_CMP:N_CMP + 1, 0:hw] + acc[N_CMP + 1:N_CMP + 2, hw:2 * hw]
        nxt = pltpu.roll(acc[0:N_CMP, hw:2 * hw], N_CMP - 1, 0)
        hid = jax.nn.gelu(acc[0:N_CMP, 0:hw] + nxt + bias)
        out = _dot(hid.astype(BF16), w2_ref[v])
        out_refs[v][...] = jnp.where(rows < N_CMP - 1, out, 0.0)


def _compress_prompt_body(k_ref, v_ref, pe_ref, w1_ref, b1_ref, w2_ref, kc_ref, vc_ref, x_ref):
    _compress_slab((k_ref.at[0], v_ref.at[0]), x_ref, pe_ref, w1_ref, b1_ref, w2_ref, (kc_ref.at[0], vc_ref.at[0]))


PE_ROWS = 16
X_PITCH = 24


def _x_scratch():
    return pltpu.VMEM((2, N_CMP * X_PITCH, LANES), F32)


def _comp_weight_specs():
    z = lambda *_: (0, 0, 0)
    z4 = lambda *_: (0, 0, 0, 0)
    return [pl.BlockSpec((2, CMP_STRIDE // 2, PE_ROWS, 2 * LANES), z4),
            pl.BlockSpec((2, CMP_STRIDE // 2, 2 * LANES, 4 * CMP_HID), z4),
            pl.BlockSpec((2, 1, 2 * CMP_HID), z),
            pl.BlockSpec((2, 2 * CMP_HID, LANES), z)]


def _compress_prompt(kvc_t, comp_w):
    b = kvc_t.shape[0]
    out = jax.ShapeDtypeStruct((b, N_CMP, LANES), F32)
    return pl.pallas_call(
        _compress_prompt_body,
        grid=(b,),
        in_specs=[pl.BlockSpec((1, LANES, SEQ), lambda i: (i, 0, 0)),
                  pl.BlockSpec((1, LANES, SEQ), lambda i: (i, 1, 0))] + _comp_weight_specs(),
        out_specs=(pl.BlockSpec((1, N_CMP, LANES), lambda i: (i, 0, 0)),) * 2,
        out_shape=(out, out),
        scratch_shapes=[_x_scratch()],
        compiler_params=_params(("arbitrary",)),
        name="compress_prompt",
    )(kvc_t, kvc_t, *comp_w)


N_PAGES = PAST_LEN // PAGE_SIZE


def _page_copy(pt_ref, pool_ref, slab_ref, sem_ref, b, slot, j):
    return pltpu.make_async_copy(pool_ref.at[pt_ref[b, j]],
                                 slab_ref.at[slot, :, :, pl.ds(j * PAGE_SIZE, PAGE_SIZE)],
                                 sem_ref.at[slot])


def _start_pages(pt_ref, pool_ref, slab_ref, sem_ref, b, slot):
    def body(j, c):
        _page_copy(pt_ref, pool_ref, slab_ref, sem_ref, b, slot, j).start()
        return c
    lax.fori_loop(0, N_PAGES, body, 0, unroll=8)


def _wait_pages(pt_ref, pool_ref, slab_ref, sem_ref, b, slot):
    def body(j, c):
        _page_copy(pt_ref, pool_ref, slab_ref, sem_ref, b, slot, j).wait()
        return c
    lax.fori_loop(0, N_PAGES, body, 0, unroll=8)


def _stream_slab(pt_ref, pool_ref, slab_ref, sem_ref):
    b = pl.program_id(0)
    nb = pl.num_programs(0)
    slot = b % 2

    @pl.when(b == 0)
    def _():
        _start_pages(pt_ref, pool_ref, slab_ref, sem_ref, 0, 0)

    @pl.when(b + 1 < nb)
    def _():
        _start_pages(pt_ref, pool_ref, slab_ref, sem_ref, b + 1, 1 - slot)

    _wait_pages(pt_ref, pool_ref, slab_ref, sem_ref, b, slot)
    return slot


def _compress_sample_body(pt_ref, pool_ref, pe_ref, w1_ref, b1_ref, w2_ref, kc_ref, vc_ref, slab_ref, sem_ref,
                          x_ref):
    slot = _stream_slab(pt_ref, pool_ref, slab_ref, sem_ref)
    _compress_slab((slab_ref.at[slot, 0], slab_ref.at[slot, 1]), x_ref, pe_ref, w1_ref, b1_ref, w2_ref,
                   (kc_ref.at[0], vc_ref.at[0]))


def _slab_scratch():
    return [pltpu.VMEM((2, 2, LANES, PAST_LEN), F32), pltpu.SemaphoreType.DMA((2,))]


def _compress_sample(page_table, pool, comp_w):
    nb = page_table.shape[0]
    out = jax.ShapeDtypeStruct((nb, N_CMP, LANES), F32)
    grid_spec = pltpu.PrefetchScalarGridSpec(
        num_scalar_prefetch=1,
        grid=(nb,),
        in_specs=[pl.BlockSpec(memory_space=pl.ANY)] + _comp_weight_specs(),
        out_specs=(pl.BlockSpec((1, N_CMP, LANES), lambda i, pt: (i, 0, 0)),) * 2,
        scratch_shapes=_slab_scratch() + [_x_scratch()],
    )
    return pl.pallas_call(
        _compress_sample_body,
        grid_spec=grid_spec,
        out_shape=(out, out),
        compiler_params=_params(("arbitrary",)),
        name="compress_sample",
    )(page_table, pool, *comp_w)


def _top_k_rows(score, blk, k):
    cand = score > -jnp.inf
    nblk = score.shape[0]
    for _ in range(k):
        m = jnp.max(score, axis=0, keepdims=True)
        idx = jnp.min(jnp.where(score == m, blk, nblk), axis=0, keepdims=True)
        score = jnp.where(blk == idx, -jnp.inf, score)
    return cand & (score == -jnp.inf)


def _add_q_bias(s, bias):
    return jnp.concatenate([s[:, g * Q_BLOCK:(g + 1) * Q_BLOCK] + bias for g in range(NSA_GROUP)], axis=1)


def _safe_inv(l):
    return 1.0 / jnp.maximum(l, 1e-30)


KT_SLC = 512
WIN_KEYS = WINDOW + Q_BLOCK
N_FORCED = 3


def _nsa_prompt_body(qn_ref, gt_ref, ksk_ref, ksvt_ref, kwk_ref, kwvt_ref, kc_ref, vc_ref, o_ref, selb_ref, s_ref):
    i = pl.program_id(1)
    q0 = i * Q_BLOCK
    qt = (qn_ref[...].astype(F32) * (SCALE * LOG2E)).T
    t_row = q0 + _iota((1, Q_BLOCK), 1)
    kc = kc_ref[0].astype(BF16)
    vct = vc_ref[0].T.astype(BF16)
    gt = gt_ref[...]
    zeros_q = jnp.zeros((HEAD_DIM, NSA_GROUP * Q_BLOCK), F32)

    mi = _iota((N_SLC, N_CMP), 0)
    mj = _iota((N_SLC, N_CMP), 1)
    msel = ((mj >= 4 * mi - 1) & (mj <= 4 * mi + 3) & (mj < N_CMP - 1)).astype(BF16)
    blk = _iota((N_SLC, Q_BLOCK), 0)
    cur = t_row // SLC_BLOCK
    forced = (blk == 0) | (blk == cur) | (blk == cur - 1)
    free = (blk * SLC_BLOCK <= t_row) & jnp.logical_not(forced)
    cend = _iota((N_CMP, 1), 0) * CMP_STRIDE + (CMP_LEN - 1)
    bias_c = jnp.where(cend <= t_row, 0.0, NEG_INF)
    wt = jnp.maximum(i - WINDOW // Q_BLOCK, 0)
    w0 = pl.multiple_of(wt * Q_BLOCK, Q_BLOCK)
    kpw = w0 + _iota((WIN_KEYS, 1), 0)
    bias_w = jnp.where((kpw <= t_row) & (kpw > t_row - WINDOW), 0.0, NEG_INF)
    krow = _iota((SLC_BLOCK, 1), 0)

    qgs, o_cs, o_ws = [], [], []
    vrows = [slice(kvh * HEAD_DIM, (kvh + 1) * HEAD_DIM) for kvh in range(NSA_KV_HEADS)]
    for kvh in range(NSA_KV_HEADS):
        qg = jnp.concatenate([qt[(kvh * NSA_GROUP + g) * HEAD_DIM:(kvh * NSA_GROUP + g + 1) * HEAD_DIM, :]
                              for g in range(NSA_GROUP)], axis=1)
        qp = jnp.concatenate([qg, zeros_q] if kvh == 0 else [zeros_q, qg], axis=0).astype(BF16)
        vrow = vrows[kvh]
        qgs.append(qg.astype(BF16))

        s = _add_q_bias(_dot(kc, qp), bias_c)
        m = jnp.max(s, axis=0, keepdims=True)
        e = jnp.exp2(s - m)
        inv = jnp.where(m > 0.5 * NEG_INF, _safe_inv(jnp.sum(e, axis=0, keepdims=True)), 0.0)
        p = e * inv
        o_c = _dot(vct[vrow, :], p.astype(BF16))
        imp = (p[:, 0:Q_BLOCK] + p[:, Q_BLOCK:2 * Q_BLOCK]
               + p[:, 2 * Q_BLOCK:3 * Q_BLOCK] + p[:, 3 * Q_BLOCK:4 * Q_BLOCK])

        score = jnp.where(free, _dot_exact_rhs(msel, imp), -jnp.inf)
        picked = _top_k_rows(score, blk, SLC_TOPK - N_FORCED)
        selb_ref[kvh] = jnp.where(forced | picked, 0.0, NEG_INF)
        o_cs.append(o_c)

        sw = _add_q_bias(_dot(kwk_ref[pl.ds(w0, WIN_KEYS), :], qp), bias_w)
        ew = jnp.exp2(sw - jnp.max(sw, axis=0, keepdims=True))
        pw = (ew * _safe_inv(jnp.sum(ew, axis=0, keepdims=True))).astype(BF16)
        o_w = jnp.zeros((HEAD_DIM, NSA_GROUP * Q_BLOCK), F32)
        for r in range(WIN_KEYS // Q_BLOCK):
            tix = wt + r
            vt = kwvt_ref[tix // 4, vrow, :]
            sub = tix % 4
            vpiece = jnp.where(sub == 0, vt[:, 0:128],
                               jnp.where(sub == 1, vt[:, 128:256],
                                         jnp.where(sub == 2, vt[:, 256:384], vt[:, 384:512])))
            o_w = o_w + _dot(vpiece, pw[r * Q_BLOCK:(r + 1) * Q_BLOCK, :])
        o_ws.append(o_w)

    bias_pad = jnp.zeros((HEAD_DIM - 8, NSA_GROUP * Q_BLOCK), F32)

    def scores(j, buf):
        k0 = pl.multiple_of(j * KT_SLC, KT_SLC)
        for kvh in range(NSA_KV_HEADS):
            b8 = selb_ref[kvh, pl.ds(pl.multiple_of(j * 8, 8), 8), :]
            b_rows = jnp.concatenate([jnp.concatenate([b8] * NSA_GROUP, axis=1), bias_pad], axis=0).astype(BF16)
            q_aug = jnp.concatenate([qgs[kvh], b_rows] if kvh == 0 else [b_rows, qgs[kvh]], axis=0)
            s_ref[buf, kvh] = _dot(ksk_ref[pl.ds(k0, KT_SLC), kvh * LANES:(kvh + 1) * LANES], q_aug)

    def consume(j, buf, state, causal):
        k0 = j * KT_SLC
        out = []
        for kvh in range(NSA_KV_HEADS):
            m_run, acc = state[kvh]
            sj = s_ref[buf, kvh]
            if causal:
                parts = []
                for r in range(KT_SLC // SLC_BLOCK):
                    ok = k0 + r * SLC_BLOCK + krow <= t_row
                    parts.append(jnp.concatenate(
                        [jnp.where(ok, sj[r * SLC_BLOCK:(r + 1) * SLC_BLOCK, g * Q_BLOCK:(g + 1) * Q_BLOCK], NEG_INF)
                         for g in range(NSA_GROUP)], axis=1))
                sj = jnp.concatenate(parts, axis=0)
            m_new = jnp.maximum(m_run, jnp.max(sj, axis=0, keepdims=True))
            alpha = jnp.exp2(m_run - m_new)
            ej = jnp.exp2(sj - m_new).astype(BF16)
            acc = alpha * acc + _dot(ksvt_ref[j, kvh * V_ROWS:(kvh + 1) * V_ROWS, :], ej)
            out.append((m_new, acc))
        return tuple(out)

    init1 = (jnp.full((1, NSA_GROUP * Q_BLOCK), NEG_INF, F32),
             jnp.zeros((V_ROWS, NSA_GROUP * Q_BLOCK), F32))
    n_pairs = (q0 // KT_SLC + 2) // 2

    def pair(jj, state):
        scores(2 * jj + 1, 1)
        state = consume(2 * jj, 0, state, False)
        scores(2 * jj + 2, 0)
        return consume(2 * jj + 1, 1, state, False)

    scores(0, 0)
    state = lax.fori_loop(0, n_pairs - 1, pair, (init1,) * NSA_KV_HEADS)
    last = 2 * n_pairs - 2
    scores(last + 1, 1)
    state = consume(last, 0, state, True)
    state = consume(last + 1, 1, state, True)

    outs = []
    for kvh in range(NSA_KV_HEADS):
        acc_s = state[kvh][1]
        o_s = acc_s[0:HEAD_DIM, :] * _safe_inv(acc_s[HEAD_DIM:HEAD_DIM + 1, :])

        def gate(br):
            base = br * NSA_HEADS + kvh * NSA_GROUP
            return jnp.concatenate([gt[base + g:base + g + 1, :] for g in range(NSA_GROUP)], axis=1)

        o = gate(0) * o_cs[kvh] + gate(1) * o_s + gate(2) * o_ws[kvh]
        outs.extend([o[:, g * Q_BLOCK:(g + 1) * Q_BLOCK] for g in range(NSA_GROUP)])
    o_ref[...] = jnp.concatenate(outs, axis=0).T


def _nsa_prompt(qn, gt, ksk, ksvt, kwk, kwvt, kc, vc, batch):
    nqb = SEQ // Q_BLOCK
    ntile = SEQ // 512
    return pl.pallas_call(
        _nsa_prompt_body,
        grid=(batch, nqb),
        in_specs=[pl.BlockSpec((Q_BLOCK, 512), lambda b, i: (b * nqb + i, 0)),
                  pl.BlockSpec((LANES, Q_BLOCK), lambda b, i: (0, b * nqb + i)),
                  pl.BlockSpec((SEQ, NSA_KV_HEADS * LANES), lambda b, i: (b, 0)),
                  pl.BlockSpec((ntile, NSA_KV_HEADS * V_ROWS, 512), lambda b, i: (b, 0, 0)),
                  pl.BlockSpec((SEQ, LANES), lambda b, i: (b, 0)),
                  pl.BlockSpec((ntile, LANES, 512), lambda b, i: (b, 0, 0)),
                  pl.BlockSpec((1, N_CMP, LANES), lambda b, i: (b, 0, 0)),
                  pl.BlockSpec((1, N_CMP, LANES), lambda b, i: (b, 0, 0))],
        out_specs=pl.BlockSpec((Q_BLOCK, 512), lambda b, i: (b * nqb + i, 0)),
        out_shape=jax.ShapeDtypeStruct((batch * SEQ, 512), F32),
        scratch_shapes=[pltpu.VMEM((NSA_KV_HEADS, N_SLC, Q_BLOCK), F32),
                        pltpu.VMEM((2, NSA_KV_HEADS, KT_SLC, NSA_GROUP * Q_BLOCK), F32)],
        compiler_params=_params(("arbitrary", "arbitrary")),
        name="nsa_prompt",
    )(qn, gt, ksk, ksvt, kwk, kwvt, kc, vc)


def _softmax_parts(parts, masks):
    m = None
    for s, mk in zip(parts, masks):
        mi = jnp.max(jnp.where(mk, s, NEG_INF), axis=-1, keepdims=True)
        m = mi if m is None else jnp.maximum(m, mi)
    es = [jnp.where(mk, jnp.exp(jnp.where(mk, s, NEG_INF) - m), 0.0) for s, mk in zip(parts, masks)]
    l = None
    for e in es:
        li = jnp.sum(e, axis=-1, keepdims=True)
        l = li if l is None else l + li
    inv = 1.0 / jnp.maximum(l, 1e-30)
    return [e * inv for e in es]


KC_SLC = 2048
ROWS_S = NSA_HEADS * DEC_SEQ


def _pad_rows(x, n):
    if x.shape[0] == n:
        return x
    return jnp.concatenate([x, jnp.zeros((n - x.shape[0], x.shape[1]), x.dtype)], axis=0)


def _nsa_sample_body(pt_ref, pool_ref, qp_ref, g_ref, kc_ref, vc_ref, kvs_ref, kvw_ref, win_ref, wtail_ref,
                     expand_ref, o_ref, nwin_ref, slab_ref, sem_ref):
    slot = _stream_slab(pt_ref, pool_ref, slab_ref, sem_ref)
    q = (qp_ref[0] * SCALE).astype(BF16)
    qi = _iota((ROWS_S, 1), 0) % DEC_SEQ
    t = PAST_LEN + qi

    kc = kc_ref[0].astype(BF16)
    vc = vc_ref[0].astype(BF16)
    cend = _iota((1, N_CMP), 1) * CMP_STRIDE + (CMP_LEN - 1)
    (p_c,) = _softmax_parts([_dot_nt(q, kc)], [cend <= t])
    o_c = _dot(p_c.astype(BF16), vc)

    half = ROWS_S // NSA_KV_HEADS
    imp = jnp.concatenate(
        [p_c[k * half:k * half + 8] + p_c[k * half + 8:k * half + 16]
         + p_c[k * half + 16:k * half + 24] + p_c[k * half + 24:k * half + 32] for k in range(NSA_KV_HEADS)], axis=0)
    mi = _iota((N_CMP, N_SLC), 0)
    mj = _iota((N_CMP, N_SLC), 1)
    msel_t = ((mi >= 4 * mj - 1) & (mi <= 4 * mj + 3) & (mi < N_CMP - 1)).astype(BF16)
    score = _dot_exact_lhs(imp, msel_t)
    score_t = _pad_rows(score, LANES).T
    blk = _iota((N_SLC, LANES), 0)
    forced = (blk == 0) | (blk == N_SLC - 1)
    picked = _top_k_rows(jnp.where(forced, -jnp.inf, score_t), blk, SLC_TOPK - 1 - 2)
    unsel = jnp.where(forced | picked, 0.0, 1.0).T
    unsel64 = jnp.concatenate([unsel[0:8]] * NSA_GROUP + [unsel[8:16]] * NSA_GROUP, axis=0).astype(BF16)

    slab_k = slab_ref.at[slot, 0]
    slab_v = slab_ref.at[slot, 1]
    jn = _iota((1, LANES), 1)
    k_new = _pad_rows(kvs_ref[:, 0:LANES], LANES).astype(BF16)
    v_new = _pad_rows(kvs_ref[:, LANES:2 * LANES], LANES).astype(BF16)
    s_new = _dot_nt(q, k_new) + jnp.where((jn <= qi) & (jn < DEC_SEQ), 0.0, NEG_INF)
    n_chunk = PAST_LEN // KC_SLC
    scs = []
    for c in range(n_chunk):
        ks_t = slab_k[:, c * KC_SLC:(c + 1) * KC_SLC].astype(BF16)
        bias = _dot(unsel64, expand_ref[:, c * KC_SLC:(c + 1) * KC_SLC]) * NEG_INF
        scs.append(_dot(q, ks_t) + bias)
    m_s = jnp.max(s_new, axis=-1, keepdims=True)
    for sc in scs:
        m_s = jnp.maximum(m_s, jnp.max(sc, axis=-1, keepdims=True))
    e_new = jnp.exp(s_new - m_s)
    l_s = jnp.sum(e_new, axis=-1, keepdims=True)
    acc = _dot(e_new.astype(BF16), v_new)
    for c in range(n_chunk):
        ec = jnp.exp(scs[c] - m_s)
        l_s = l_s + jnp.sum(ec, axis=-1, keepdims=True)
        acc = acc + _dot_nt(ec.astype(BF16), slab_v[:, c * KC_SLC:(c + 1) * KC_SLC].astype(BF16))
    o_s = acc * _safe_inv(l_s)

    wb = win_ref.shape[2]
    win_t = win_ref[0]
    kpos_b = PAST_LEN - wb + _iota((1, wb), 1)
    kpos_n = PAST_LEN + jn
    mask_b = (kpos_b <= t) & (kpos_b > t - WINDOW) & (kpos_b >= 0)
    mask_n = (kpos_n <= t) & (kpos_n > t - WINDOW) & (jn < DEC_SEQ)
    kw_new = _pad_rows(kvw_ref[:, 0:LANES], LANES).astype(BF16)
    vw_new = _pad_rows(kvw_ref[:, LANES:2 * LANES], LANES).astype(BF16)
    p_b, p_n = _softmax_parts([_dot(q, win_t[0:LANES, :].astype(BF16)), _dot_nt(q, kw_new)], [mask_b, mask_n])
    o_w = _dot_nt(p_b.astype(BF16), win_t[LANES:2 * LANES, :].astype(BF16)) + _dot(p_n.astype(BF16), vw_new)

    g = g_ref[0]
    o_ref[0] = g[:, 0:1] * o_c + g[:, 1:2] * o_s + g[:, 2:3] * o_w
    shifted = pltpu.roll(win_t, wb - DEC_SEQ, 1)
    nwin_ref[0, :, 0:wb - LANES] = shifted[:, 0:wb - LANES]
    nwin_ref[0, :, wb - LANES:wb] = jnp.where(jn >= LANES - DEC_SEQ, wtail_ref[0], shifted[:, wb - LANES:wb])


def _nsa_sample(page_table, pool_slc, qp, gates, kc, vc, kvs, kvw, win_t, wtail, expand):
    nb = page_table.shape[0]
    wb = win_t.shape[2]
    per_b = lambda *shape: pl.BlockSpec((1,) + shape, lambda i, pt: (i,) + (0,) * len(shape))
    rows8 = pl.BlockSpec((DEC_SEQ, 256), lambda i, pt: (i, 0))
    grid_spec = pltpu.PrefetchScalarGridSpec(
        num_scalar_prefetch=1,
        grid=(nb,),
        in_specs=[pl.BlockSpec(memory_space=pl.ANY),
                  per_b(ROWS_S, LANES), per_b(ROWS_S, LANES), per_b(N_CMP, LANES), per_b(N_CMP, LANES),
                  rows8, rows8, per_b(256, wb), per_b(256, LANES),
                  pl.BlockSpec((N_SLC, PAST_LEN), lambda i, pt: (0, 0))],
        out_specs=(per_b(ROWS_S, LANES), per_b(256, wb)),
        scratch_shapes=_slab_scratch(),
    )
    return pl.pallas_call(
        _nsa_sample_body,
        grid_spec=grid_spec,
        out_shape=(jax.ShapeDtypeStruct((nb, ROWS_S, LANES), F32),
                   jax.ShapeDtypeStruct((nb, 256, wb), F32)),
        compiler_params=_params(("arbitrary",)),
        name="nsa_sample",
    )(page_table, pool_slc, qp, gates, kc, vc, kvs, kvw, win_t, wtail, expand)


RET_W = RET_HEADS * HEAD_DIM


def _retention_body(q_ref, k_ref, v_ref, cos_ref, sin_ref, st0_ref, gn_ref, o_ref, st_ref, state_ref, *, chunk):
    c = pl.program_id(1)
    ck = max(chunk, LANES)

    @pl.when(c == 0)
    def _():
        state_ref[...] = st0_ref[0]

    lane = _iota((1, RET_W), 1)
    head_of_lane = lane // HEAD_DIM
    low_half = (lane % HEAD_DIM) < HEAD_DIM // 2
    cos = cos_ref[...]
    sin = sin_ref[...]

    def rope(x):
        rot = jnp.where(low_half, -pltpu.roll(x, RET_W - HEAD_DIM // 2, 1), pltpu.roll(x, HEAD_DIM // 2, 1))
        return x * cos + rot * sin

    logg_lane = jnp.zeros((1, RET_W), F32)
    for h in range(RET_HEADS):
        logg_lane = jnp.where(head_of_lane == h, _LOG_G[h], logg_lane)

    q = rope(q_ref[...]) * SCALE
    k = rope(k_ref[...])
    v = v_ref[...]
    kp = _pad_rows(k, ck)
    vp = _pad_rows(v, ck)
    n = _iota((chunk, 1), 0).astype(F32)
    nk = _iota((ck, 1), 0).astype(F32)
    diff = n - _iota((1, ck), 1).astype(F32)
    in_chunk = _iota((1, ck), 1) < chunk

    inner = jnp.zeros((chunk, RET_W), F32)
    for h in range(RET_HEADS):
        hm = head_of_lane == h
        dmat = jnp.where((diff >= 0) & in_chunk, jnp.exp(_LOG_G[h] * jnp.maximum(diff, 0.0)), 0.0)
        s = _dot_nt(jnp.where(hm, q, 0.0), kp) * dmat
        inner = inner + jnp.where(hm, _dot(s, vp), 0.0)
    state = state_ref[...]
    xi = jnp.exp(logg_lane * (n + 1.0))
    cross = _dot(q * xi, state)
    zeta = jnp.where(nk < chunk, jnp.exp(logg_lane * (chunk - 1.0 - nk)), 0.0)
    kz_t = (kp * zeta).T
    row_head = _iota((RET_W, 1), 0) // HEAD_DIM
    decay_rows = jnp.zeros((RET_W, 1), F32)
    for h in range(RET_HEADS):
        decay_rows = jnp.where(row_head == h, float(np.exp(_LOG_G[h] * chunk)), decay_rows)
    new_state = decay_rows * state + jnp.where(row_head == head_of_lane, _dot(kz_t, vp), 0.0)
    state_ref[...] = new_state
    st_ref[0] = new_state

    o = inner + cross
    avg = jnp.where(_iota((RET_W, RET_W), 0) // HEAD_DIM == _iota((RET_W, RET_W), 1) // HEAD_DIM,
                    1.0 / HEAD_DIM, 0.0).astype(BF16)
    mu = _dot_exact_lhs(o, avg)
    d = o - mu
    var = _dot_exact_lhs(d * d, avg)
    o_ref[...] = d * lax.rsqrt(var + EPS) * gn_ref[...]


def _retention(qkvr, cos, sin, state_bd, gn, batch, t_len, chunk):
    nch = t_len // chunk
    col = lambda j: pl.BlockSpec((chunk, RET_W), lambda b, c: (b * nch + c, j))
    tab = pl.BlockSpec((chunk, RET_W), lambda b, c: (c, 0))
    st = pl.BlockSpec((1, RET_W, RET_W), lambda b, c: (b, 0, 0))
    return pl.pallas_call(
        functools.partial(_retention_body, chunk=chunk),
        grid=(batch, nch),
        in_specs=[col(0), col(1), col(2), tab, tab, st, pl.BlockSpec((1, RET_W), lambda b, c: (0, 0))],
        out_specs=(pl.BlockSpec((chunk, RET_W), lambda b, c: (b * nch + c, 0)), st),
        out_shape=(jax.ShapeDtypeStruct((batch * t_len, RET_W), F32),
                   jax.ShapeDtypeStruct((batch, RET_W, RET_W), F32)),
        scratch_shapes=[pltpu.VMEM((RET_W, RET_W), F32)],
        compiler_params=_params(("arbitrary", "arbitrary")),
        name="retention",
    )(qkvr, qkvr, qkvr, cos, sin, state_bd, gn)


MEM_W = 4 * HEAD_DIM


def _mem_attn_body(q_ref, mkv_ref, o_ref):
    q = q_ref[...] * SCALE
    mk_t = mkv_ref[0, 0:MEM_W, :].astype(BF16)
    mv_t = mkv_ref[0, MEM_W:2 * MEM_W, :].astype(BF16)
    head_of_lane = _iota((1, MEM_W), 1) // HEAD_DIM
    o = jnp.zeros(q.shape, F32)
    for h in range(4):
        hm = head_of_lane == h
        s = _dot(jnp.where(hm, q, 0.0).astype(BF16), mk_t)
        e = jnp.exp(s - jnp.max(s, axis=-1, keepdims=True))
        p = e / jnp.sum(e, axis=-1, keepdims=True)
        o = o + jnp.where(hm, _dot_nt(p.astype(BF16), mv_t), 0.0)
    o_ref[...] = o


def _mem_attn(qm, mkv_t, batch, rows_per_batch, tm):
    nt = rows_per_batch // tm
    return pl.pallas_call(
        _mem_attn_body,
        grid=(batch, nt),
        in_specs=[pl.BlockSpec((tm, MEM_W), lambda b, i: (b * nt + i, 0)),
                  pl.BlockSpec((1, 2 * MEM_W, N_MEM), lambda b, i: (b, 0, 0))],
        out_specs=pl.BlockSpec((tm, MEM_W), lambda b, i: (b * nt + i, 0)),
        out_shape=jax.ShapeDtypeStruct((batch * rows_per_batch, MEM_W), F32),
        compiler_params=_params(("arbitrary", "arbitrary")),
        name="mem_attn",
    )(qm, mkv_t)


def _finish_body(x_ref, on_ref, or_ref, om_ref, z_ref, w_ref, g_ref, y_ref):
    o = jnp.concatenate([on_ref[...], or_ref[...], om_ref[...]], axis=-1)
    z = z_ref[...]
    mix = o * (z * jax.nn.sigmoid(z))
    xo = x_ref[...] + _dot(mix.astype(BF16), w_ref[...])
    y_ref[...] = xo * lax.rsqrt(jnp.mean(xo * xo, axis=-1, keepdims=True) + EPS) * g_ref[...]


def _finish(x2d, o_n, o_r, o_m, z, w_out, ln_final):
    n = x2d.shape[0]
    tm = 512
    row = lambda w: pl.BlockSpec((tm, w), lambda i: (i, 0))
    return pl.pallas_call(
        _finish_body,
        grid=(n // tm,),
        in_specs=[row(D_MODEL), row(512), row(256), row(256), row(D_MODEL),
                  pl.BlockSpec((D_MODEL, D_MODEL), lambda i: (0, 0)),
                  pl.BlockSpec((1, D_MODEL), lambda i: (0, 0))],
        out_specs=row(D_MODEL),
        out_shape=jax.ShapeDtypeStruct((n, D_MODEL), F32),
        compiler_params=_params(("arbitrary",)),
        name="finish",
    )(x2d, o_n, o_r, o_m, z, w_out, ln_final)


def _permute_w_in(w):
    sizes = (512, 256, 256, 256, 24, 512, 256, 256, 256, 256, 256, 256)
    offs = np.concatenate([[0], np.cumsum(sizes)])
    part = lambda i: w[:, int(offs[i]):int(offs[i + 1])]
    q_n, kv_c, kv_s, kv_w, g_n, z_n, q_r, k_r, v_r, z_r, q_m, z_m = [part(i) for i in range(12)]
    g_pad = jnp.pad(g_n, ((0, 0), (0, LANES - g_n.shape[1])))
    return jnp.concatenate([q_n, kv_c, kv_s, kv_w, q_r, k_r, v_r, q_m, z_n, z_r, z_m, g_pad], axis=1).astype(BF16)


def _compress_weights(pe, w1, b1, w2):
    npair = CMP_STRIDE // 2
    pe_r = pe.reshape(2, 2, npair, 2, HEAD_DIM)
    pe_l = jnp.broadcast_to(pe_r[:, :, :, :, None, :], (2, 2, npair, 2, NSA_KV_HEADS, HEAD_DIM))
    pe_l = pe_l.reshape(2, 2, npair, 2 * LANES).transpose(0, 2, 1, 3)
    pe_l = jnp.pad(pe_l, ((0, 0), (0, 0), (0, PE_ROWS - 2), (0, 0))).astype(BF16)
    w1_r = w1.reshape(2, 2, CMP_STRIDE, HEAD_DIM, CMP_HID)
    zw = jnp.zeros_like(w1_r)
    w1_bd = jnp.concatenate([jnp.concatenate([w1_r, zw], axis=-1),
                             jnp.concatenate([zw, w1_r], axis=-1)], axis=-2)
    w1_bd = w1_bd.reshape(2, 2, npair, 2 * LANES, 2 * CMP_HID)
    w1_bd = jnp.concatenate([w1_bd[:, 0], w1_bd[:, 1]], axis=-1).astype(BF16)
    b1_l = jnp.concatenate([b1, b1], axis=-1).reshape(2, 1, 2 * CMP_HID)
    z2 = jnp.zeros_like(w2)
    w2_bd = jnp.concatenate([jnp.concatenate([w2, z2], axis=-1),
                             jnp.concatenate([z2, w2], axis=-1)], axis=-2).astype(BF16)
    return pe_l, w1_bd, b1_l, w2_bd


def _rope_tables(pos):
    half = HEAD_DIM // 2
    inv = ROPE_BASE ** (-jnp.arange(half, dtype=F32) / half)
    ang = pos.astype(F32)[:, None] * inv[None, :]
    cos, sin = jnp.cos(ang), jnp.sin(ang)
    cos_l = jnp.tile(jnp.concatenate([cos, cos], axis=-1), (1, RET_HEADS))
    sin_l = jnp.tile(jnp.concatenate([sin, sin], axis=-1), (1, RET_HEADS))
    return cos_l, sin_l


def _block_diag_state(st):
    b = st.shape[0]
    eye = jnp.eye(RET_HEADS, dtype=st.dtype)
    return jnp.einsum("bhde,hg->bhdge", st, eye).reshape(b, RET_W, RET_W)


def _diag_blocks(st_bd):
    b = st_bd.shape[0]
    r = st_bd.reshape(b, RET_HEADS, HEAD_DIM, RET_HEADS, HEAD_DIM)
    return jnp.stack([r[:, h, :, h, :] for h in range(RET_HEADS)], axis=1)


def _kv_shape(a, b, t):
    return a.reshape(1, b, t, 2, NSA_KV_HEADS, HEAD_DIM)


def _position_minor(cache, heads):
    b, l = cache.shape[:2]
    return jnp.transpose(cache, (0, 2, 3, 4, 1)).reshape(b, 2 * heads * HEAD_DIM, l)


def _from_position_minor(a, heads):
    b, _, l = a.shape
    return jnp.transpose(a.reshape(b, 2, heads, HEAD_DIM, l), (0, 4, 1, 2, 3))[None]


def kernel(x_prompt, mem_prompt, x_sample, cache_nsa_cmp, cache_nsa_slc, cache_nsa_win, state_ret, cache_mem,
           page_table, ln_mix, w_in, cmp_pe, cmp_w1, cmp_b1, cmp_w2, ret_gn, ln_mem, w_mem_kv, w_out, ln_final):
    bp, t_len = x_prompt.shape[:2]
    bs, s_len = x_sample.shape[:2]
    assert (t_len, s_len) == (SEQ, DEC_SEQ) and ln_mix.shape[0] == 1
    w_perm = _permute_w_in(w_in[0])
    comp_w = _compress_weights(cmp_pe[0], cmp_w1[0], cmp_b1[0], cmp_w2[0])
    ln_g = ln_mix[0].reshape(1, D_MODEL)
    gn = ret_gn[0].reshape(1, RET_W)
    w_out_b = w_out[0].astype(BF16)
    ln_f = ln_final.reshape(1, D_MODEL)

    xp2 = x_prompt.reshape(bp * SEQ, D_MODEL)
    (qn, kvc_t, kvs_t, kvw_t, qkvr, qm, z, gt, ksk, ksvt, kwk, kwvt) = _inproj(xp2, ln_g, w_perm, SEQ, True)
    kc, vc = _compress_prompt(kvc_t, comp_w)
    o_n = _nsa_prompt(qn, gt, ksk, ksvt, kwk, kwvt, kc, vc, bp)
    cos_p, sin_p = _rope_tables(jnp.arange(SEQ))
    o_r, st_p = _retention(qkvr, cos_p, sin_p, jnp.zeros((bp, RET_W, RET_W), F32), gn, bp, SEQ, 256)
    mkv_t = _mem_kv(mem_prompt, ln_mem[0].reshape(1, D_MODEL), w_mem_kv[0].astype(BF16))
    o_m = _mem_attn(qm, mkv_t, bp, SEQ, 512)
    y_prompt = _finish(xp2, o_n, o_r, o_m, z, w_out_b, ln_f).reshape(bp, SEQ, D_MODEL)
    new_cmp_p = _from_position_minor(kvc_t, NSA_KV_HEADS)
    new_slc_p = _from_position_minor(kvs_t, NSA_KV_HEADS)
    new_win_p = _from_position_minor(kvw_t[:, :, SEQ - WINDOW:], NSA_KV_HEADS)
    new_ret_p = _diag_blocks(st_p)[None]
    new_mem_p = _from_position_minor(mkv_t, 4)

    xs2 = x_sample.reshape(bs * DEC_SEQ, D_MODEL)
    (qn_s, kvc_s, kvs_s, kvw_s, qkvr_s, qm_s, z_s, gt_s) = _inproj(xs2, ln_g, w_perm, DEC_SEQ, False)
    pool_cmp = _position_minor(cache_nsa_cmp[0], NSA_KV_HEADS).reshape(-1, 2, LANES, PAGE_SIZE)
    pool_slc = _position_minor(cache_nsa_slc[0], NSA_KV_HEADS).reshape(-1, 2, LANES, PAGE_SIZE)
    kc_s, vc_s = _compress_sample(page_table, pool_cmp, comp_w)
    q5 = qn_s.astype(F32).reshape(bs, DEC_SEQ, NSA_KV_HEADS, NSA_GROUP, HEAD_DIM).transpose(0, 2, 3, 1, 4)
    zq = jnp.zeros_like(q5[:, 0])
    qp = jnp.stack([jnp.concatenate([q5[:, 0], zq], axis=-1), jnp.concatenate([zq, q5[:, 1]], axis=-1)], axis=1)
    qp = qp.reshape(bs, ROWS_S, LANES)
    g5 = gt_s[:3 * NSA_HEADS].reshape(3, NSA_KV_HEADS, NSA_GROUP, bs, DEC_SEQ).transpose(3, 1, 2, 4, 0)
    gates = jnp.pad(g5.reshape(bs, ROWS_S, 3), ((0, 0), (0, 0), (0, LANES - 3)))
    expand = jnp.asarray(np.repeat(np.eye(N_SLC, dtype=np.float32), SLC_BLOCK, axis=1), dtype=BF16)
    win_t = _position_minor(cache_nsa_win[0], NSA_KV_HEADS)
    wtail = jnp.pad(kvw_s.reshape(bs, DEC_SEQ, 256).transpose(0, 2, 1), ((0, 0), (0, 0), (LANES - DEC_SEQ, 0)))
    o_sn, new_win_t = _nsa_sample(page_table, pool_slc, qp, gates, kc_s, vc_s, kvs_s, kvw_s, win_t, wtail, expand)
    o6 = o_sn.reshape(bs, NSA_KV_HEADS, NSA_GROUP, DEC_SEQ, NSA_KV_HEADS, HEAD_DIM)
    o_n_s = jnp.stack([o6[:, 0, :, :, 0], o6[:, 1, :, :, 1]], axis=1)
    o_n_s = o_n_s.transpose(0, 3, 1, 2, 4).reshape(bs * DEC_SEQ, 512)
    cos_s, sin_s = _rope_tables(PAST_LEN + jnp.arange(DEC_SEQ))
    o_r_s, st_s = _retention(qkvr_s, cos_s, sin_s, _block_diag_state(state_ret[0]), gn, bs, DEC_SEQ, DEC_SEQ)
    o_m_s = _mem_attn(qm_s, _position_minor(cache_mem[0], 4), bs, DEC_SEQ, DEC_SEQ)
    y_sample = _finish(xs2, o_n_s, o_r_s, o_m_s, z_s, w_out_b, ln_f).reshape(bs, DEC_SEQ, D_MODEL)
    new_cmp_s = _kv_shape(kvc_s, bs, DEC_SEQ)
    new_slc_s = _kv_shape(kvs_s, bs, DEC_SEQ)
    new_win_s = _from_position_minor(new_win_t, NSA_KV_HEADS)
    new_ret_s = _diag_blocks(st_s)[None]

    return (y_prompt, y_sample, new_cmp_p, new_cmp_s, new_slc_p, new_slc_s, new_win_p, new_win_s,
            new_ret_p, new_ret_s, new_mem_p)
```

```python
import functools

import numpy as np
import jax
import jax.numpy as jnp
from jax import lax
from jax.experimental import pallas as pl
from jax.experimental.pallas import tpu as pltpu

D_MODEL = 1024
SEQ = 8192
DEC_SEQ = 8
PAST_LEN = 8192
PAGE_SIZE = 128
HEAD_DIM = 64
NSA_HEADS = 8
NSA_KV_HEADS = 2
NSA_GROUP = 4
CMP_LEN = 32
CMP_STRIDE = 16
CMP_HID = 128
SLC_BLOCK = 64
SLC_TOPK = 16
WINDOW = 512
RET_HEADS = 4
N_MEM = 256
Q_BLOCK = 128
ROPE_BASE = 10000.0
EPS = 1e-6
NEG_INF = -1e30
SCALE = HEAD_DIM ** -0.5

N_CMP = 512
N_SLC = 128
LANES = 128
ONES_ROWS = 16
V_ROWS = HEAD_DIM + ONES_ROWS
LOG2E = 1.4426950408889634
SAMPLE_SEQS_PER_STEP = 8
VMEM_LIMIT = 56 * 1024 * 1024

_W_Q, _W_KVC, _W_KVS, _W_KVW, _W_QKVR, _W_QM, _W_Z, _W_G = 0, 512, 768, 1024, 1280, 2048, 2304, 3328
IN_WP = 3456

F32 = jnp.float32
BF16 = jnp.bfloat16

_LOG_G = [float(np.log1p(-(2.0 ** (-5.0 - h)))) for h in range(RET_HEADS)]


def _dot(a, b):
    return jnp.dot(a, b, preferred_element_type=F32)


def _dot_nt(a, b):
    return lax.dot_general(a, b, (((1,), (1,)), ((), ())), preferred_element_type=F32)


def _split3(x):
    hi = x.astype(BF16)
    r1 = x - hi.astype(F32)
    mid = r1.astype(BF16)
    lo = (r1 - mid.astype(F32)).astype(BF16)
    return hi, mid, lo


def _dot_exact_rhs(a_bf16, x):
    hi, mid, lo = _split3(x)
    return _dot(a_bf16, hi) + _dot(a_bf16, mid) + _dot(a_bf16, lo)


def _dot_exact_lhs(x, b_bf16):
    hi, mid, lo = _split3(x)
    return _dot(hi, b_bf16) + _dot(mid, b_bf16) + _dot(lo, b_bf16)


def _iota(shape, dim):
    return lax.broadcasted_iota(jnp.int32, shape, dim)


def _params(sem):
    return pltpu.CompilerParams(dimension_semantics=sem, vmem_limit_bytes=VMEM_LIMIT)


def _inproj_body(x_ref, g_ref, w_ref, qn_ref, kvc_ref, kvs_ref, kvw_ref, qkvr_ref, qm_ref, z_ref, gt_ref,
                 *attn_refs, transposed_kv):
    x = x_ref[...]
    xn = x * lax.rsqrt(jnp.mean(x * x, axis=-1, keepdims=True) + EPS) * g_ref[...]
    xb = xn.astype(BF16)

    def proj(a, b):
        return _dot(xb, w_ref[:, a:b])

    qn_ref[...] = proj(_W_Q, _W_KVC).astype(BF16)
    kvc = proj(_W_KVC, _W_KVS)
    kvs = proj(_W_KVS, _W_KVW)
    kvw = proj(_W_KVW, _W_QKVR)
    if transposed_kv:
        ksk_ref, ksvt_ref, kwk_ref, kwvt_ref = attn_refs
        kvc_ref[0] = kvc.T
        kvs_t = kvs.T
        kvs_ref[0] = kvs_t
        tm = kvs.shape[0]
        lane = _iota((tm, LANES), 1)
        blk_in_tile = _iota((tm, LANES), 0) // SLC_BLOCK
        k2 = kvs[:, :LANES]
        ksk_ref[:, 0:LANES] = jnp.where(lane < HEAD_DIM, k2,
                                        jnp.where(lane - HEAD_DIM == blk_in_tile, 1.0, 0.0)).astype(BF16)
        ksk_ref[:, LANES:2 * LANES] = jnp.where(lane >= HEAD_DIM, k2,
                                                jnp.where(lane == blk_in_tile, 1.0, 0.0)).astype(BF16)
        ones_rows = jnp.where(_iota((ONES_ROWS, tm), 0) == 0, 1.0, 0.0)
        ksvt_ref[0] = jnp.concatenate([kvs_t[LANES:LANES + HEAD_DIM, :], ones_rows,
                                       kvs_t[LANES + HEAD_DIM:, :], ones_rows], axis=0).astype(BF16)
        kvw_t = kvw.T
        kvw_ref[0] = kvw_t
        kwk_ref[...] = kvw[:, :LANES].astype(BF16)
        kwvt_ref[0] = kvw_t[LANES:, :].astype(BF16)
    else:
        kvc_ref[...] = kvc
        kvs_ref[...] = kvs
        kvw_ref[...] = kvw
    qkvr_ref[...] = proj(_W_QKVR, _W_QM)
    qm_ref[...] = proj(_W_QM, _W_Z)
    z_ref[...] = proj(_W_Z, _W_G)
    gt_ref[...] = jax.nn.sigmoid(proj(_W_G, IN_WP)).T


def _inproj(x2d, ln_g, w_perm, rows_per_batch, transposed_kv):
    n = x2d.shape[0]
    tm = 512
    nt = n // tm
    row = lambda w: pl.BlockSpec((tm, w), lambda i: (i, 0))
    if transposed_kv:
        tpb = rows_per_batch // tm
        kv_shape = jax.ShapeDtypeStruct((n // rows_per_batch, 256, rows_per_batch), F32)
        kv_spec = pl.BlockSpec((1, 256, tm), lambda i: (i // tpb, 0, i % tpb))
    else:
        kv_shape = jax.ShapeDtypeStruct((n, 256), F32)
        kv_spec = row(256)
    out_shape = [
        jax.ShapeDtypeStruct((n, 512), BF16),
        kv_shape, kv_shape, kv_shape,
        jax.ShapeDtypeStruct((n, 768), F32),
        jax.ShapeDtypeStruct((n, 256), F32),
        jax.ShapeDtypeStruct((n, 1024), F32),
        jax.ShapeDtypeStruct((LANES, n), F32),
    ]
    out_specs = [row(512), kv_spec, kv_spec, kv_spec, row(768), row(256), row(1024),
                 pl.BlockSpec((LANES, tm), lambda i: (0, i))]
    if transposed_kv:
        tile_t = pl.BlockSpec((1, LANES, tm), lambda i: (i, 0, 0))
        tile_v = pl.BlockSpec((1, NSA_KV_HEADS * V_ROWS, tm), lambda i: (i, 0, 0))
        out_shape += [jax.ShapeDtypeStruct((n, NSA_KV_HEADS * LANES), BF16),
                      jax.ShapeDtypeStruct((nt, NSA_KV_HEADS * V_ROWS, tm), BF16),
                      jax.ShapeDtypeStruct((n, LANES), BF16),
                      jax.ShapeDtypeStruct((nt, LANES, tm), BF16)]
        out_specs += [row(NSA_KV_HEADS * LANES), tile_v, row(LANES), tile_t]
    return pl.pallas_call(
        functools.partial(_inproj_body, transposed_kv=transposed_kv),
        grid=(nt,),
        in_specs=[row(D_MODEL),
                  pl.BlockSpec((1, D_MODEL), lambda i: (0, 0)),
                  pl.BlockSpec((D_MODEL, IN_WP), lambda i: (0, 0))],
        out_specs=tuple(out_specs),
        out_shape=tuple(out_shape),
        compiler_params=_params(("arbitrary",)),
        name="inproj",
    )(x2d, ln_g, w_perm)


def _mem_kv_body(x_ref, g_ref, w_ref, o_ref):
    x = x_ref[0]
    xn = x * lax.rsqrt(jnp.mean(x * x, axis=-1, keepdims=True) + EPS) * g_ref[...]
    o_ref[0] = _dot(xn.astype(BF16), w_ref[...]).T


def _mem_kv(mem, ln_g, w_bf16):
    b = mem.shape[0]
    nw = w_bf16.shape[1]
    return pl.pallas_call(
        _mem_kv_body,
        grid=(b,),
        in_specs=[pl.BlockSpec((1, N_MEM, D_MODEL), lambda i: (i, 0, 0)),
                  pl.BlockSpec((1, D_MODEL), lambda i: (0, 0)),
                  pl.BlockSpec((D_MODEL, nw), lambda i: (0, 0))],
        out_specs=pl.BlockSpec((1, nw, N_MEM), lambda i: (i, 0, 0)),
        out_shape=jax.ShapeDtypeStruct((b, nw, N_MEM), F32),
        compiler_params=_params(("arbitrary",)),
        name="mem_kv",
    )(mem, ln_g, w_bf16)


def _compress_slab(slabs_t, x_refs, pe_ref, w1_ref, b1_ref, w2_ref, out_refs):
    rows = _iota((N_CMP, LANES), 0)
    hw = 2 * CMP_HID
    chunks_per_page = LANES // CMP_STRIDE
    n_pages = PAST_LEN // LANES
    npair = CMP_STRIDE // 2

    def transpose_pages(v, j0, j1):
        for j in range(j0, j1):
            xt = slabs_t[v][:, j * LANES:(j + 1) * LANES].T
            for cc in range(chunks_per_page):
                r0 = (j * chunks_per_page + cc) * X_PITCH
                x_refs[v][r0:r0 + CMP_STRIDE, :] = xt[cc * CMP_STRIDE:(cc + 1) * CMP_STRIDE, :]

    transpose_pages(0, 0, n_pages)
    for v in range(2):
        acc = jnp.zeros((N_CMP + PE_ROWS, 2 * hw), F32)
        for pp in range(CMP_STRIDE // 2):
            xcat = jnp.concatenate([x_refs[v][pl.ds(2 * pp, N_CMP, stride=X_PITCH), :],
                                    x_refs[v][pl.ds(2 * pp + 1, N_CMP, stride=X_PITCH), :]], axis=1)
            lhs = jnp.concatenate([xcat.astype(BF16), pe_ref[v, pp]], axis=0)
            acc = acc + _dot(lhs, w1_ref[v, pp])
            if v == 0:
                transpose_pages(1, pp * n_pages // npair, (pp + 1) * n_pages // npair)
        bias = b1_ref[v] + acc[N_CMP:N_CMP + 1, 0:hw] + acc[N_CMP + 1:N_CMP + 2, hw:2 * hw]
        nxt = pltpu.roll(acc[0:N_CMP, hw:2 * hw], N_CMP - 1, 0)
        hid = jax.nn.gelu(acc[0:N_CMP, 0:hw] + nxt + bias)
        out = _dot(hid.astype(BF16), w2_ref[v])
        out_refs[v][...] = jnp.where(rows < N_CMP - 1, out, 0.0)


def _compress_prompt_body(k_ref, v_ref, pe_ref, w1_ref, b1_ref, w2_ref, kc_ref, vc_ref, xk_ref, xv_ref):
    _compress_slab((k_ref.at[0], v_ref.at[0]), (xk_ref, xv_ref), pe_ref, w1_ref, b1_ref, w2_ref,
                   (kc_ref.at[0], vc_ref.at[0]))


PE_ROWS = 16
X_PITCH = 24


def _x_scratch():
    return [pltpu.VMEM((N_CMP * X_PITCH, LANES), F32) for _ in range(2)]


def _comp_weight_specs():
    z = lambda *_: (0, 0, 0)
    z4 = lambda *_: (0, 0, 0, 0)
    return [pl.BlockSpec((2, CMP_STRIDE // 2, PE_ROWS, 2 * LANES), z4),
            pl.BlockSpec((2, CMP_STRIDE // 2, 2 * LANES, 4 * CMP_HID), z4),
            pl.BlockSpec((2, 1, 2 * CMP_HID), z),
            pl.BlockSpec((2, 2 * CMP_HID, LANES), z)]


def _compress_prompt(kvc_t, comp_w):
    b = kvc_t.shape[0]
    out = jax.ShapeDtypeStruct((b, N_CMP, LANES), F32)
    return pl.pallas_call(
        _compress_prompt_body,
        grid=(b,),
        in_specs=[pl.BlockSpec((1, LANES, SEQ), lambda i: (i, 0, 0)),
                  pl.BlockSpec((1, LANES, SEQ), lambda i: (i, 1, 0))] + _comp_weight_specs(),
        out_specs=(pl.BlockSpec((1, N_CMP, LANES), lambda i: (i, 0, 0)),) * 2,
        out_shape=(out, out),
        scratch_shapes=_x_scratch(),
        compiler_params=_params(("arbitrary",)),
        name="compress_prompt",
    )(kvc_t, kvc_t, *comp_w)


N_PAGES = PAST_LEN // PAGE_SIZE


def _page_copy(pt_ref, pool_ref, slab_ref, sem_ref, seq, slot, s, j):
    return pltpu.make_async_copy(pool_ref.at[pt_ref[seq, j]],
                                 slab_ref.at[slot, s, :, :, pl.ds(j * PAGE_SIZE, PAGE_SIZE)],
                                 sem_ref.at[slot])


def _start_pages(pt_ref, pool_ref, slab_ref, sem_ref, step, slot):
    spb = slab_ref.shape[1]
    for s in range(spb):
        def body(j, c):
            _page_copy(pt_ref, pool_ref, slab_ref, sem_ref, step * spb + s, slot, s, j).start()
            return c
        lax.fori_loop(0, N_PAGES, body, 0, unroll=8)


def _wait_pages(pt_ref, pool_ref, slab_ref, sem_ref, step, slot):
    spb = slab_ref.shape[1]
    for s in range(spb):
        def body(j, c):
            _page_copy(pt_ref, pool_ref, slab_ref, sem_ref, step * spb + s, slot, s, j).wait()
            return c
        lax.fori_loop(0, N_PAGES, body, 0, unroll=8)


def _stream_slab(pt_ref, pool_ref, slab_ref, sem_ref):
    b = pl.program_id(0)
    nb = pl.num_programs(0)
    slot = b % 2

    @pl.when(b == 0)
    def _():
        _start_pages(pt_ref, pool_ref, slab_ref, sem_ref, 0, 0)

    @pl.when(b + 1 < nb)
    def _():
        _start_pages(pt_ref, pool_ref, slab_ref, sem_ref, b + 1, 1 - slot)

    _wait_pages(pt_ref, pool_ref, slab_ref, sem_ref, b, slot)
    return slot


def _compress_sample_body(pt_ref, pool_ref, pe_ref, w1_ref, b1_ref, w2_ref, kc_ref, vc_ref, slab_ref, sem_ref,
                          xk_ref, xv_ref):
    slot = _stream_slab(pt_ref, pool_ref, slab_ref, sem_ref)
    _compress_slab((slab_ref.at[slot, 0, 0], slab_ref.at[slot, 0, 1]), (xk_ref, xv_ref), pe_ref, w1_ref, b1_ref,
                   w2_ref, (kc_ref.at[0], vc_ref.at[0]))


def _slab_scratch(seqs_per_step=1):
    return [pltpu.VMEM((2, seqs_per_step, 2, LANES, PAST_LEN), F32), pltpu.SemaphoreType.DMA((2,))]


def _compress_sample(page_table, pool, comp_w):
    nb = page_table.shape[0]
    out = jax.ShapeDtypeStruct((nb, N_CMP, LANES), F32)
    grid_spec = pltpu.PrefetchScalarGridSpec(
        num_scalar_prefetch=1,
        grid=(nb,),
        in_specs=[pl.BlockSpec(memory_space=pl.ANY)] + _comp_weight_specs(),
        out_specs=(pl.BlockSpec((1, N_CMP, LANES), lambda i, pt: (i, 0, 0)),) * 2,
        scratch_shapes=_slab_scratch() + _x_scratch(),
    )
    return pl.pallas_call(
        _compress_sample_body,
        grid_spec=grid_spec,
        out_shape=(out, out),
        compiler_params=_params(("arbitrary",)),
        name="compress_sample",
    )(page_table, pool, *comp_w)


def _top_k_rows(score, blk, k):
    cand = score > -jnp.inf
    nblk = score.shape[0]
    for _ in range(k):
        m = jnp.max(score, axis=0, keepdims=True)
        idx = jnp.min(jnp.where(score == m, blk, nblk), axis=0, keepdims=True)
        score = jnp.where(blk == idx, -jnp.inf, score)
    return cand & (score == -jnp.inf)


def _add_q_bias(s, bias):
    return jnp.concatenate([s[:, g * Q_BLOCK:(g + 1) * Q_BLOCK] + bias for g in range(NSA_GROUP)], axis=1)


def _safe_inv(l):
    return 1.0 / jnp.maximum(l, 1e-30)


KT_SLC = 512
WIN_KEYS = WINDOW + Q_BLOCK
N_FORCED = 3


def _nsa_prompt_body(qn_ref, gt_ref, ksk_ref, ksvt_ref, kwk_ref, kwvt_ref, kc_ref, vc_ref, o_ref, selb_ref, s_ref):
    i = pl.program_id(1)
    q0 = i * Q_BLOCK
    qt = (qn_ref[...].astype(F32) * (SCALE * LOG2E)).T
    t_row = q0 + _iota((1, Q_BLOCK), 1)
    kc = kc_ref[0].astype(BF16)
    vct = vc_ref[0].T.astype(BF16)
    gt = gt_ref[...]
    zeros_q = jnp.zeros((HEAD_DIM, NSA_GROUP * Q_BLOCK), F32)

    mi = _iota((N_SLC, N_CMP), 0)
    mj = _iota((N_SLC, N_CMP), 1)
    msel = ((mj >= 4 * mi - 1) & (mj <= 4 * mi + 3) & (mj < N_CMP - 1)).astype(BF16)
    blk = _iota((N_SLC, Q_BLOCK), 0)
    cur = t_row // SLC_BLOCK
    forced = (blk == 0) | (blk == cur) | (blk == cur - 1)
    free = (blk * SLC_BLOCK <= t_row) & jnp.logical_not(forced)
    cend = _iota((N_CMP, 1), 0) * CMP_STRIDE + (CMP_LEN - 1)
    bias_c = jnp.where(cend <= t_row, 0.0, NEG_INF)
    wt = jnp.maximum(i - WINDOW // Q_BLOCK, 0)
    w0 = pl.multiple_of(wt * Q_BLOCK, Q_BLOCK)
    kpw = w0 + _iota((WIN_KEYS, 1), 0)
    bias_w = jnp.where((kpw <= t_row) & (kpw > t_row - WINDOW), 0.0, NEG_INF)
    krow = _iota((SLC_BLOCK, 1), 0)

    qgs, o_cs, o_ws = [], [], []
    vrows = [slice(kvh * HEAD_DIM, (kvh + 1) * HEAD_DIM) for kvh in range(NSA_KV_HEADS)]
    for kvh in range(NSA_KV_HEADS):
        qg = jnp.concatenate([qt[(kvh * NSA_GROUP + g) * HEAD_DIM:(kvh * NSA_GROUP + g + 1) * HEAD_DIM, :]
                              for g in range(NSA_GROUP)], axis=1)
        qp = jnp.concatenate([qg, zeros_q] if kvh == 0 else [zeros_q, qg], axis=0).astype(BF16)
        vrow = vrows[kvh]
        qgs.append(qg.astype(BF16))

        s = _add_q_bias(_dot(kc, qp), bias_c)
        m = jnp.max(s, axis=0, keepdims=True)
        e = jnp.exp2(s - m)
        inv = jnp.where(m > 0.5 * NEG_INF, _safe_inv(jnp.sum(e, axis=0, keepdims=True)), 0.0)
        p = e * inv
        o_c = _dot(vct[vrow, :], p.astype(BF16))
        imp = (p[:, 0:Q_BLOCK] + p[:, Q_BLOCK:2 * Q_BLOCK]
               + p[:, 2 * Q_BLOCK:3 * Q_BLOCK] + p[:, 3 * Q_BLOCK:4 * Q_BLOCK])

        score = jnp.where(free, _dot_exact_rhs(msel, imp), -jnp.inf)
        picked = _top_k_rows(score, blk, SLC_TOPK - N_FORCED)
        selb_ref[kvh] = jnp.where(forced | picked, 0.0, NEG_INF)
        o_cs.append(o_c)

        sw = _add_q_bias(_dot(kwk_ref[pl.ds(w0, WIN_KEYS), :], qp), bias_w)
        ew = jnp.exp2(sw - jnp.max(sw, axis=0, keepdims=True))
        pw = (ew * _safe_inv(jnp.sum(ew, axis=0, keepdims=True))).astype(BF16)
        o_w = jnp.zeros((HEAD_DIM, NSA_GROUP * Q_BLOCK), F32)
        for r in range(WIN_KEYS // Q_BLOCK):
            tix = wt + r
            vt = kwvt_ref[tix // 4, vrow, :]
            sub = tix % 4
            vpiece = jnp.where(sub == 0, vt[:, 0:128],
                               jnp.where(sub == 1, vt[:, 128:256],
                                         jnp.where(sub == 2, vt[:, 256:384], vt[:, 384:512])))
            o_w = o_w + _dot(vpiece, pw[r * Q_BLOCK:(r + 1) * Q_BLOCK, :])
        o_ws.append(o_w)

    bias_pad = jnp.zeros((HEAD_DIM - 8, NSA_GROUP * Q_BLOCK), F32)

    def scores(j, buf):
        k0 = pl.multiple_of(j * KT_SLC, KT_SLC)
        for kvh in range(NSA_KV_HEADS):
            b8 = selb_ref[kvh, pl.ds(pl.multiple_of(j * 8, 8), 8), :]
            b_rows = jnp.concatenate([jnp.concatenate([b8] * NSA_GROUP, axis=1), bias_pad], axis=0).astype(BF16)
            q_aug = jnp.concatenate([qgs[kvh], b_rows] if kvh == 0 else [b_rows, qgs[kvh]], axis=0)
            s_ref[buf, kvh] = _dot(ksk_ref[pl.ds(k0, KT_SLC), kvh * LANES:(kvh + 1) * LANES], q_aug)

    def consume(j, buf, state, causal):
        k0 = j * KT_SLC
        out = []
        for kvh in range(NSA_KV_HEADS):
            m_run, acc = state[kvh]
            sj = s_ref[buf, kvh]
            if causal:
                parts = []
                for r in range(KT_SLC // SLC_BLOCK):
                    ok = k0 + r * SLC_BLOCK + krow <= t_row
                    parts.append(jnp.concatenate(
                        [jnp.where(ok, sj[r * SLC_BLOCK:(r + 1) * SLC_BLOCK, g * Q_BLOCK:(g + 1) * Q_BLOCK], NEG_INF)
                         for g in range(NSA_GROUP)], axis=1))
                sj = jnp.concatenate(parts, axis=0)
            m_new = jnp.maximum(m_run, jnp.max(sj, axis=0, keepdims=True))
            alpha = jnp.exp2(m_run - m_new)
            ej = jnp.exp2(sj - m_new).astype(BF16)
            acc = alpha * acc + _dot(ksvt_ref[j, kvh * V_ROWS:(kvh + 1) * V_ROWS, :], ej)
            out.append((m_new, acc))
        return tuple(out)

    init1 = (jnp.full((1, NSA_GROUP * Q_BLOCK), NEG_INF, F32),
             jnp.zeros((V_ROWS, NSA_GROUP * Q_BLOCK), F32))
    n_pairs = (q0 // KT_SLC + 2) // 2

    def pair(jj, state):
        scores(2 * jj + 1, 1)
        state = consume(2 * jj, 0, state, False)
        scores(2 * jj + 2, 0)
        return consume(2 * jj + 1, 1, state, False)

    scores(0, 0)
    state = lax.fori_loop(0, n_pairs - 1, pair, (init1,) * NSA_KV_HEADS)
    last = 2 * n_pairs - 2
    scores(last + 1, 1)
    state = consume(last, 0, state, True)
    state = consume(last + 1, 1, state, True)

    outs = []
    for kvh in range(NSA_KV_HEADS):
        acc_s = state[kvh][1]
        o_s = acc_s[0:HEAD_DIM, :] * _safe_inv(acc_s[HEAD_DIM:HEAD_DIM + 1, :])

        def gate(br):
            base = br * NSA_HEADS + kvh * NSA_GROUP
            return jnp.concatenate([gt[base + g:base + g + 1, :] for g in range(NSA_GROUP)], axis=1)

        o = gate(0) * o_cs[kvh] + gate(1) * o_s + gate(2) * o_ws[kvh]
        outs.extend([o[:, g * Q_BLOCK:(g + 1) * Q_BLOCK] for g in range(NSA_GROUP)])
    o_ref[...] = jnp.concatenate(outs, axis=0).T


def _nsa_prompt(qn, gt, ksk, ksvt, kwk, kwvt, kc, vc, batch):
    nqb = SEQ // Q_BLOCK
    ntile = SEQ // 512
    return pl.pallas_call(
        _nsa_prompt_body,
        grid=(batch, nqb),
        in_specs=[pl.BlockSpec((Q_BLOCK, 512), lambda b, i: (b * nqb + i, 0)),
                  pl.BlockSpec((LANES, Q_BLOCK), lambda b, i: (0, b * nqb + i)),
                  pl.BlockSpec((SEQ, NSA_KV_HEADS * LANES), lambda b, i: (b, 0)),
                  pl.BlockSpec((ntile, NSA_KV_HEADS * V_ROWS, 512), lambda b, i: (b, 0, 0)),
                  pl.BlockSpec((SEQ, LANES), lambda b, i: (b, 0)),
                  pl.BlockSpec((ntile, LANES, 512), lambda b, i: (b, 0, 0)),
                  pl.BlockSpec((1, N_CMP, LANES), lambda b, i: (b, 0, 0)),
                  pl.BlockSpec((1, N_CMP, LANES), lambda b, i: (b, 0, 0))],
        out_specs=pl.BlockSpec((Q_BLOCK, 512), lambda b, i: (b * nqb + i, 0)),
        out_shape=jax.ShapeDtypeStruct((batch * SEQ, 512), F32),
        scratch_shapes=[pltpu.VMEM((NSA_KV_HEADS, N_SLC, Q_BLOCK), F32),
                        pltpu.VMEM((2, NSA_KV_HEADS, KT_SLC, NSA_GROUP * Q_BLOCK), F32)],
        compiler_params=_params(("arbitrary", "arbitrary")),
        name="nsa_prompt",
    )(qn, gt, ksk, ksvt, kwk, kwvt, kc, vc)


def _softmax_parts(parts, masks):
    m = None
    for s, mk in zip(parts, masks):
        mi = jnp.max(jnp.where(mk, s, NEG_INF), axis=-1, keepdims=True)
        m = mi if m is None else jnp.maximum(m, mi)
    es = [jnp.where(mk, jnp.exp(jnp.where(mk, s, NEG_INF) - m), 0.0) for s, mk in zip(parts, masks)]
    l = None
    for e in es:
        li = jnp.sum(e, axis=-1, keepdims=True)
        l = li if l is None else l + li
    inv = 1.0 / jnp.maximum(l, 1e-30)
    return [e * inv for e in es]


KC_SLC = 2048
ROWS_S = NSA_HEADS * DEC_SEQ


def _pad_rows(x, n):
    if x.shape[0] == n:
        return x
    return jnp.concatenate([x, jnp.zeros((n - x.shape[0], x.shape[1]), x.dtype)], axis=0)


def _nsa_sample_body(pt_ref, pool_ref, qp_ref, g_ref, kc_ref, vc_ref, kvs_ref, kvw_ref, win_ref, wtail_ref,
                     expand_ref, o_ref, nwin_ref, slab_ref, sem_ref):
    slot = _stream_slab(pt_ref, pool_ref, slab_ref, sem_ref)
    spb = slab_ref.shape[1]
    seqs = range(spb)
    qi = _iota((ROWS_S, 1), 0) % DEC_SEQ
    t = PAST_LEN + qi
    jn = _iota((1, LANES), 1)
    n_chunk = PAST_LEN // KC_SLC
    qs = [(qp_ref[s] * SCALE).astype(BF16) for s in seqs]

    cend = _iota((1, N_CMP), 1) * CMP_STRIDE + (CMP_LEN - 1)
    p_cs = [_softmax_parts([_dot_nt(qs[s], kc_ref[s].astype(BF16))], [cend <= t])[0] for s in seqs]
    o_cs = [_dot(p_cs[s].astype(BF16), vc_ref[s].astype(BF16)) for s in seqs]

    half = ROWS_S // NSA_KV_HEADS
    imp = jnp.concatenate(
        [p_cs[s][k * half:k * half + 8] + p_cs[s][k * half + 8:k * half + 16]
         + p_cs[s][k * half + 16:k * half + 24] + p_cs[s][k * half + 24:k * half + 32]
         for s in seqs for k in range(NSA_KV_HEADS)], axis=0)
    mi = _iota((N_CMP, N_SLC), 0)
    mj = _iota((N_CMP, N_SLC), 1)
    msel_t = ((mi >= 4 * mj - 1) & (mi <= 4 * mj + 3) & (mi < N_CMP - 1)).astype(BF16)
    score = _dot_exact_lhs(imp, msel_t)
    score_t = _pad_rows(score, LANES).T
    blk = _iota((N_SLC, LANES), 0)
    forced = (blk == 0) | (blk == N_SLC - 1)
    picked = _top_k_rows(jnp.where(forced, -jnp.inf, score_t), blk, SLC_TOPK - 1 - 2)
    unsel = jnp.where(forced | picked, 0.0, 1.0).T
    unsel64 = [jnp.concatenate([unsel[s * 16:s * 16 + 8]] * NSA_GROUP + [unsel[s * 16 + 8:s * 16 + 16]] * NSA_GROUP,
                               axis=0).astype(BF16) for s in seqs]

    new_bias = jnp.where((jn <= qi) & (jn < DEC_SEQ), 0.0, NEG_INF)
    k_news = [_pad_rows(kvs_ref[s * DEC_SEQ:(s + 1) * DEC_SEQ, 0:LANES], LANES).astype(BF16) for s in seqs]
    v_news = [_pad_rows(kvs_ref[s * DEC_SEQ:(s + 1) * DEC_SEQ, LANES:2 * LANES], LANES).astype(BF16) for s in seqs]
    s_news = [_dot_nt(qs[s], k_news[s]) + new_bias for s in seqs]
    scs = [[_dot(qs[s], slab_ref[slot, s, 0, :, c * KC_SLC:(c + 1) * KC_SLC].astype(BF16))
            + _dot(unsel64[s], expand_ref[:, c * KC_SLC:(c + 1) * KC_SLC]) * NEG_INF
            for c in range(n_chunk)] for s in seqs]
    o_ss = []
    for s in seqs:
        m_s = jnp.max(s_news[s], axis=-1, keepdims=True)
        for sc in scs[s]:
            m_s = jnp.maximum(m_s, jnp.max(sc, axis=-1, keepdims=True))
        e_new = jnp.exp(s_news[s] - m_s)
        l_s = jnp.sum(e_new, axis=-1, keepdims=True)
        acc = _dot(e_new.astype(BF16), v_news[s])
        for c in range(n_chunk):
            ec = jnp.exp(scs[s][c] - m_s)
            l_s = l_s + jnp.sum(ec, axis=-1, keepdims=True)
            acc = acc + _dot_nt(ec.astype(BF16), slab_ref[slot, s, 1, :, c * KC_SLC:(c + 1) * KC_SLC].astype(BF16))
        o_ss.append(acc * _safe_inv(l_s))

    wb = win_ref.shape[2]
    kpos_b = PAST_LEN - wb + _iota((1, wb), 1)
    kpos_n = PAST_LEN + jn
    mask_b = (kpos_b <= t) & (kpos_b > t - WINDOW) & (kpos_b >= 0)
    mask_n = (kpos_n <= t) & (kpos_n > t - WINDOW) & (jn < DEC_SEQ)
    for s in seqs:
        win_t = win_ref[s]
        kw_new = _pad_rows(kvw_ref[s * DEC_SEQ:(s + 1) * DEC_SEQ, 0:LANES], LANES).astype(BF16)
        vw_new = _pad_rows(kvw_ref[s * DEC_SEQ:(s + 1) * DEC_SEQ, LANES:2 * LANES], LANES).astype(BF16)
        p_b, p_n = _softmax_parts([_dot(qs[s], win_t[0:LANES, :].astype(BF16)), _dot_nt(qs[s], kw_new)],
                                  [mask_b, mask_n])
        o_w = _dot_nt(p_b.astype(BF16), win_t[LANES:2 * LANES, :].astype(BF16)) + _dot(p_n.astype(BF16), vw_new)

        g = g_ref[s]
        o_ref[s] = g[:, 0:1] * o_cs[s] + g[:, 1:2] * o_ss[s] + g[:, 2:3] * o_w
        shifted = pltpu.roll(win_t, wb - DEC_SEQ, 1)
        nwin_ref[s, :, 0:wb - LANES] = shifted[:, 0:wb - LANES]
        nwin_ref[s, :, wb - LANES:wb] = jnp.where(jn >= LANES - DEC_SEQ, wtail_ref[s], shifted[:, wb - LANES:wb])


NSA_SAMPLE_SEQS_PER_STEP = 2


def _nsa_sample(page_table, pool_slc, qp, gates, kc, vc, kvs, kvw, win_t, wtail, expand):
    nb = page_table.shape[0]
    wb = win_t.shape[2]
    spb = NSA_SAMPLE_SEQS_PER_STEP
    per_b = lambda *shape: pl.BlockSpec((spb,) + shape, lambda i, pt: (i,) + (0,) * len(shape))
    rows8 = pl.BlockSpec((spb * DEC_SEQ, 256), lambda i, pt: (i, 0))
    grid_spec = pltpu.PrefetchScalarGridSpec(
        num_scalar_prefetch=1,
        grid=(nb // spb,),
        in_specs=[pl.BlockSpec(memory_space=pl.ANY),
                  per_b(ROWS_S, LANES), per_b(ROWS_S, LANES), per_b(N_CMP, LANES), per_b(N_CMP, LANES),
                  rows8, rows8, per_b(256, wb), per_b(256, LANES),
                  pl.BlockSpec((N_SLC, PAST_LEN), lambda i, pt: (0, 0))],
        out_specs=(per_b(ROWS_S, LANES), per_b(256, wb)),
        scratch_shapes=_slab_scratch(spb),
    )
    return pl.pallas_call(
        _nsa_sample_body,
        grid_spec=grid_spec,
        out_shape=(jax.ShapeDtypeStruct((nb, ROWS_S, LANES), F32),
                   jax.ShapeDtypeStruct((nb, 256, wb), F32)),
        compiler_params=_params(("arbitrary",)),
        name="nsa_sample",
    )(page_table, pool_slc, qp, gates, kc, vc, kvs, kvw, win_t, wtail, expand)


RET_W = RET_HEADS * HEAD_DIM


def _retention_body(q_ref, k_ref, v_ref, cos_ref, sin_ref, st0_ref, gn_ref, o_ref, st_ref, state_ref, *, chunk):
    c = pl.program_id(1)
    ck = max(chunk, LANES)
    spb = st0_ref.shape[0]

    @pl.when(c == 0)
    def _():
        state_ref[...] = st0_ref[...]

    lane = _iota((1, RET_W), 1)
    head_of_lane = lane // HEAD_DIM
    low_half = (lane % HEAD_DIM) < HEAD_DIM // 2
    cos = cos_ref[...]
    sin = sin_ref[...]

    def rope(x):
        rot = jnp.where(low_half, -pltpu.roll(x, RET_W - HEAD_DIM // 2, 1), pltpu.roll(x, HEAD_DIM // 2, 1))
        return x * cos + rot * sin

    logg_lane = jnp.zeros((1, RET_W), F32)
    for h in range(RET_HEADS):
        logg_lane = jnp.where(head_of_lane == h, _LOG_G[h], logg_lane)

    n = _iota((chunk, 1), 0).astype(F32)
    nk = _iota((ck, 1), 0).astype(F32)
    diff = n - _iota((1, ck), 1).astype(F32)
    in_chunk = _iota((1, ck), 1) < chunk
    dmats = [jnp.where((diff >= 0) & in_chunk, jnp.exp(_LOG_G[h] * jnp.maximum(diff, 0.0)), 0.0)
             for h in range(RET_HEADS)]
    xi = jnp.exp(logg_lane * (n + 1.0))
    zeta = jnp.where(nk < chunk, jnp.exp(logg_lane * (chunk - 1.0 - nk)), 0.0)
    row_head = _iota((RET_W, 1), 0) // HEAD_DIM
    decay_rows = jnp.zeros((RET_W, 1), F32)
    for h in range(RET_HEADS):
        decay_rows = jnp.where(row_head == h, float(np.exp(_LOG_G[h] * chunk)), decay_rows)
    avg = jnp.where(_iota((RET_W, RET_W), 0) // HEAD_DIM == _iota((RET_W, RET_W), 1) // HEAD_DIM,
                    1.0 / HEAD_DIM, 0.0).astype(BF16)
    gn = gn_ref[...]

    seqs = range(spb)
    rows = [slice(bb * chunk, (bb + 1) * chunk) for bb in seqs]
    qs = [rope(q_ref[r, :]) * SCALE for r in rows]
    kps = [_pad_rows(rope(k_ref[r, :]), ck) for r in rows]
    vps = [_pad_rows(v_ref[r, :], ck) for r in rows]
    hms = [head_of_lane == h for h in range(RET_HEADS)]
    ss = [[_dot_nt(jnp.where(hms[h], qs[bb], 0.0), kps[bb]) * dmats[h] for h in range(RET_HEADS)] for bb in seqs]
    inners = []
    for bb in seqs:
        inner = jnp.zeros((chunk, RET_W), F32)
        for h in range(RET_HEADS):
            inner = inner + jnp.where(hms[h], _dot(ss[bb][h], vps[bb]), 0.0)
        inners.append(inner)
    states = [state_ref[bb] for bb in seqs]
    os_ = [inners[bb] + _dot(qs[bb] * xi, states[bb]) for bb in seqs]
    for bb in seqs:
        kz_t = (kps[bb] * zeta).T
        new_state = decay_rows * states[bb] + jnp.where(row_head == head_of_lane, _dot(kz_t, vps[bb]), 0.0)
        state_ref[bb] = new_state
        st_ref[bb] = new_state

    o = jnp.concatenate(os_, axis=0) if spb > 1 else os_[0]
    mu = _dot_exact_lhs(o, avg)
    d = o - mu
    var = _dot_exact_lhs(d * d, avg)
    o_ref[...] = d * lax.rsqrt(var + EPS) * gn


def _retention(qkvr, cos, sin, state_bd, gn, batch, t_len, chunk, seqs_per_step=1):
    nch = t_len // chunk
    spb = seqs_per_step
    assert spb == 1 or nch == 1
    col = lambda j: pl.BlockSpec((spb * chunk, RET_W), lambda b, c: (b * nch + c, j))
    tab = pl.BlockSpec((chunk, RET_W), lambda b, c: (c, 0))
    st = pl.BlockSpec((spb, RET_W, RET_W), lambda b, c: (b, 0, 0))
    return pl.pallas_call(
        functools.partial(_retention_body, chunk=chunk),
        grid=(batch // spb, nch),
        in_specs=[col(0), col(1), col(2), tab, tab, st, pl.BlockSpec((1, RET_W), lambda b, c: (0, 0))],
        out_specs=(pl.BlockSpec((spb * chunk, RET_W), lambda b, c: (b * nch + c, 0)), st),
        out_shape=(jax.ShapeDtypeStruct((batch * t_len, RET_W), F32),
                   jax.ShapeDtypeStruct((batch, RET_W, RET_W), F32)),
        scratch_shapes=[pltpu.VMEM((spb, RET_W, RET_W), F32)],
        compiler_params=_params(("arbitrary", "arbitrary")),
        name="retention",
    )(qkvr, qkvr, qkvr, cos, sin, state_bd, gn)


MEM_W = 4 * HEAD_DIM


def _mem_attn_body(q_ref, mkv_ref, o_ref, *, tm):
    head_of_lane = _iota((1, MEM_W), 1) // HEAD_DIM
    spb = mkv_ref.shape[0]
    q_all = q_ref[...] * SCALE
    scores = []
    for bb in range(spb):
        q = q_all[bb * tm:(bb + 1) * tm, :]
        q4 = jnp.concatenate([jnp.where(head_of_lane == h, q, 0.0) for h in range(4)], axis=0).astype(BF16)
        scores.append(_dot(q4, mkv_ref[bb, 0:MEM_W, :].astype(BF16)))
    s = jnp.concatenate(scores, axis=0)
    e = jnp.exp(s - jnp.max(s, axis=-1, keepdims=True))
    p = (e / jnp.sum(e, axis=-1, keepdims=True)).astype(BF16)
    for bb in range(spb):
        pv = _dot_nt(p[bb * 4 * tm:(bb + 1) * 4 * tm, :], mkv_ref[bb, MEM_W:2 * MEM_W, :].astype(BF16))
        o = jnp.zeros((tm, MEM_W), F32)
        for h in range(4):
            o = o + jnp.where(head_of_lane == h, pv[h * tm:(h + 1) * tm, :], 0.0)
        o_ref[bb * tm:(bb + 1) * tm, :] = o


def _mem_attn(qm, mkv_t, batch, rows_per_batch, tm, seqs_per_step=1):
    nt = rows_per_batch // tm
    spb = seqs_per_step
    assert spb == 1 or nt == 1
    return pl.pallas_call(
        functools.partial(_mem_attn_body, tm=tm),
        grid=(batch // spb, nt),
        in_specs=[pl.BlockSpec((spb * tm, MEM_W), lambda b, i: (b * nt + i, 0)),
                  pl.BlockSpec((spb, 2 * MEM_W, N_MEM), lambda b, i: (b, 0, 0))],
        out_specs=pl.BlockSpec((spb * tm, MEM_W), lambda b, i: (b * nt + i, 0)),
        out_shape=jax.ShapeDtypeStruct((batch * rows_per_batch, MEM_W), F32),
        compiler_params=_params(("arbitrary", "arbitrary")),
        name="mem_attn",
    )(qm, mkv_t)


def _finish_body(x_ref, on_ref, or_ref, om_ref, z_ref, w_ref, g_ref, y_ref):
    o = jnp.concatenate([on_ref[...], or_ref[...], om_ref[...]], axis=-1)
    z = z_ref[...]
    mix = o * (z * jax.nn.sigmoid(z))
    xo = x_ref[...] + _dot(mix.astype(BF16), w_ref[...])
    y_ref[...] = xo * lax.rsqrt(jnp.mean(xo * xo, axis=-1, keepdims=True) + EPS) * g_ref[...]


def _finish(x2d, o_n, o_r, o_m, z, w_out, ln_final):
    n = x2d.shape[0]
    tm = 512
    row = lambda w: pl.BlockSpec((tm, w), lambda i: (i, 0))
    return pl.pallas_call(
        _finish_body,
        grid=(n // tm,),
        in_specs=[row(D_MODEL), row(512), row(256), row(256), row(D_MODEL),
                  pl.BlockSpec((D_MODEL, D_MODEL), lambda i: (0, 0)),
                  pl.BlockSpec((1, D_MODEL), lambda i: (0, 0))],
        out_specs=row(D_MODEL),
        out_shape=jax.ShapeDtypeStruct((n, D_MODEL), F32),
        compiler_params=_params(("arbitrary",)),
        name="finish",
    )(x2d, o_n, o_r, o_m, z, w_out, ln_final)


def _permute_w_in(w):
    sizes = (512, 256, 256, 256, 24, 512, 256, 256, 256, 256, 256, 256)
    offs = np.concatenate([[0], np.cumsum(sizes)])
    part = lambda i: w[:, int(offs[i]):int(offs[i + 1])]
    q_n, kv_c, kv_s, kv_w, g_n, z_n, q_r, k_r, v_r, z_r, q_m, z_m = [part(i) for i in range(12)]
    g_pad = jnp.pad(g_n, ((0, 0), (0, LANES - g_n.shape[1])))
    return jnp.concatenate([q_n, kv_c, kv_s, kv_w, q_r, k_r, v_r, q_m, z_n, z_r, z_m, g_pad], axis=1).astype(BF16)


def _compress_weights(pe, w1, b1, w2):
    npair = CMP_STRIDE // 2
    pe_r = pe.reshape(2, 2, npair, 2, HEAD_DIM)
    pe_l = jnp.broadcast_to(pe_r[:, :, :, :, None, :], (2, 2, npair, 2, NSA_KV_HEADS, HEAD_DIM))
    pe_l = pe_l.reshape(2, 2, npair, 2 * LANES).transpose(0, 2, 1, 3)
    pe_l = jnp.pad(pe_l, ((0, 0), (0, 0), (0, PE_ROWS - 2), (0, 0))).astype(BF16)
    w1_r = w1.reshape(2, 2, CMP_STRIDE, HEAD_DIM, CMP_HID)
    zw = jnp.zeros_like(w1_r)
    w1_bd = jnp.concatenate([jnp.concatenate([w1_r, zw], axis=-1),
                             jnp.concatenate([zw, w1_r], axis=-1)], axis=-2)
    w1_bd = w1_bd.reshape(2, 2, npair, 2 * LANES, 2 * CMP_HID)
    w1_bd = jnp.concatenate([w1_bd[:, 0], w1_bd[:, 1]], axis=-1).astype(BF16)
    b1_l = jnp.concatenate([b1, b1], axis=-1).reshape(2, 1, 2 * CMP_HID)
    z2 = jnp.zeros_like(w2)
    w2_bd = jnp.concatenate([jnp.concatenate([w2, z2], axis=-1),
                             jnp.concatenate([z2, w2], axis=-1)], axis=-2).astype(BF16)
    return pe_l, w1_bd, b1_l, w2_bd


def _rope_tables(pos):
    half = HEAD_DIM // 2
    inv = ROPE_BASE ** (-jnp.arange(half, dtype=F32) / half)
    ang = pos.astype(F32)[:, None] * inv[None, :]
    cos, sin = jnp.cos(ang), jnp.sin(ang)
    cos_l = jnp.tile(jnp.concatenate([cos, cos], axis=-1), (1, RET_HEADS))
    sin_l = jnp.tile(jnp.concatenate([sin, sin], axis=-1), (1, RET_HEADS))
    return cos_l, sin_l


def _block_diag_state(st):
    b = st.shape[0]
    eye = jnp.eye(RET_HEADS, dtype=st.dtype)
    return jnp.einsum("bhde,hg->bhdge", st, eye).reshape(b, RET_W, RET_W)


def _diag_blocks(st_bd):
    b = st_bd.shape[0]
    r = st_bd.reshape(b, RET_HEADS, HEAD_DIM, RET_HEADS, HEAD_DIM)
    return jnp.stack([r[:, h, :, h, :] for h in range(RET_HEADS)], axis=1)


def _kv_shape(a, b, t):
    return a.reshape(1, b, t, 2, NSA_KV_HEADS, HEAD_DIM)


def _position_minor(cache, heads):
    b, l = cache.shape[:2]
    return jnp.transpose(cache, (0, 2, 3, 4, 1)).reshape(b, 2 * heads * HEAD_DIM, l)


def _from_position_minor(a, heads):
    b, _, l = a.shape
    return jnp.transpose(a.reshape(b, 2, heads, HEAD_DIM, l), (0, 4, 1, 2, 3))[None]


def kernel(x_prompt, mem_prompt, x_sample, cache_nsa_cmp, cache_nsa_slc, cache_nsa_win, state_ret, cache_mem,
           page_table, ln_mix, w_in, cmp_pe, cmp_w1, cmp_b1, cmp_w2, ret_gn, ln_mem, w_mem_kv, w_out, ln_final):
    bp, t_len = x_prompt.shape[:2]
    bs, s_len = x_sample.shape[:2]
    assert (t_len, s_len) == (SEQ, DEC_SEQ) and ln_mix.shape[0] == 1
    w_perm = _permute_w_in(w_in[0])
    comp_w = _compress_weights(cmp_pe[0], cmp_w1[0], cmp_b1[0], cmp_w2[0])
    ln_g = ln_mix[0].reshape(1, D_MODEL)
    gn = ret_gn[0].reshape(1, RET_W)
    w_out_b = w_out[0].astype(BF16)
    ln_f = ln_final.reshape(1, D_MODEL)

    xp2 = x_prompt.reshape(bp * SEQ, D_MODEL)
    (qn, kvc_t, kvs_t, kvw_t, qkvr, qm, z, gt, ksk, ksvt, kwk, kwvt) = _inproj(xp2, ln_g, w_perm, SEQ, True)
    kc, vc = _compress_prompt(kvc_t, comp_w)
    o_n = _nsa_prompt(qn, gt, ksk, ksvt, kwk, kwvt, kc, vc, bp)
    cos_p, sin_p = _rope_tables(jnp.arange(SEQ))
    o_r, st_p = _retention(qkvr, cos_p, sin_p, jnp.zeros((bp, RET_W, RET_W), F32), gn, bp, SEQ, 256)
    mkv_t = _mem_kv(mem_prompt, ln_mem[0].reshape(1, D_MODEL), w_mem_kv[0].astype(BF16))
    o_m = _mem_attn(qm, mkv_t, bp, SEQ, 512)
    y_prompt = _finish(xp2, o_n, o_r, o_m, z, w_out_b, ln_f).reshape(bp, SEQ, D_MODEL)
    new_cmp_p = _from_position_minor(kvc_t, NSA_KV_HEADS)
    new_slc_p = _from_position_minor(kvs_t, NSA_KV_HEADS)
    new_win_p = _from_position_minor(kvw_t[:, :, SEQ - WINDOW:], NSA_KV_HEADS)
    new_ret_p = _diag_blocks(st_p)[None]
    new_mem_p = _from_position_minor(mkv_t, 4)

    xs2 = x_sample.reshape(bs * DEC_SEQ, D_MODEL)
    (qn_s, kvc_s, kvs_s, kvw_s, qkvr_s, qm_s, z_s, gt_s) = _inproj(xs2, ln_g, w_perm, DEC_SEQ, False)
    pool_cmp = _position_minor(cache_nsa_cmp[0], NSA_KV_HEADS).reshape(-1, 2, LANES, PAGE_SIZE)
    pool_slc = _position_minor(cache_nsa_slc[0], NSA_KV_HEADS).reshape(-1, 2, LANES, PAGE_SIZE)
    kc_s, vc_s = _compress_sample(page_table, pool_cmp, comp_w)
    q5 = qn_s.astype(F32).reshape(bs, DEC_SEQ, NSA_KV_HEADS, NSA_GROUP, HEAD_DIM).transpose(0, 2, 3, 1, 4)
    zq = jnp.zeros_like(q5[:, 0])
    qp = jnp.stack([jnp.concatenate([q5[:, 0], zq], axis=-1), jnp.concatenate([zq, q5[:, 1]], axis=-1)], axis=1)
    qp = qp.reshape(bs, ROWS_S, LANES)
    g5 = gt_s[:3 * NSA_HEADS].reshape(3, NSA_KV_HEADS, NSA_GROUP, bs, DEC_SEQ).transpose(3, 1, 2, 4, 0)
    gates = jnp.pad(g5.reshape(bs, ROWS_S, 3), ((0, 0), (0, 0), (0, LANES - 3)))
    expand = jnp.asarray(np.repeat(np.eye(N_SLC, dtype=np.float32), SLC_BLOCK, axis=1), dtype=BF16)
    win_t = _position_minor(cache_nsa_win[0], NSA_KV_HEADS)
    wtail = jnp.pad(kvw_s.reshape(bs, DEC_SEQ, 256).transpose(0, 2, 1), ((0, 0), (0, 0), (LANES - DEC_SEQ, 0)))
    o_sn, new_win_t = _nsa_sample(page_table, pool_slc, qp, gates, kc_s, vc_s, kvs_s, kvw_s, win_t, wtail, expand)
    o6 = o_sn.reshape(bs, NSA_KV_HEADS, NSA_GROUP, DEC_SEQ, NSA_KV_HEADS, HEAD_DIM)
    o_n_s = jnp.stack([o6[:, 0, :, :, 0], o6[:, 1, :, :, 1]], axis=1)
    o_n_s = o_n_s.transpose(0, 3, 1, 2, 4).reshape(bs * DEC_SEQ, 512)
    cos_s, sin_s = _rope_tables(PAST_LEN + jnp.arange(DEC_SEQ))
    o_r_s, st_s = _retention(qkvr_s, cos_s, sin_s, _block_diag_state(state_ret[0]), gn, bs, DEC_SEQ, DEC_SEQ,
                             seqs_per_step=SAMPLE_SEQS_PER_STEP)
    o_m_s = _mem_attn(qm_s, _position_minor(cache_mem[0], 4), bs, DEC_SEQ, DEC_SEQ,
                      seqs_per_step=SAMPLE_SEQS_PER_STEP)
    y_sample = _finish(xs2, o_n_s, o_r_s, o_m_s, z_s, w_out_b, ln_f).reshape(bs, DEC_SEQ, D_MODEL)
    new_cmp_s = _kv_shape(kvc_s, bs, DEC_SEQ)
    new_slc_s = _kv_shape(kvs_s, bs, DEC_SEQ)
    new_win_s = _from_position_minor(new_win_t, NSA_KV_HEADS)
    new_ret_s = _diag_blocks(st_s)[None]

    return (y_prompt, y_sample, new_cmp_p, new_cmp_s, new_slc_p, new_slc_s, new_win_p, new_win_s,
            new_ret_p, new_ret_s, new_mem_p)
```

```python
import functools

import numpy as np
import jax
import jax.numpy as jnp
from jax import lax
from jax.experimental import pallas as pl
from jax.experimental.pallas import tpu as pltpu

D_MODEL = 1024
SEQ = 8192
DEC_SEQ = 8
PAST_LEN = 8192
PAGE_SIZE = 128
HEAD_DIM = 64
NSA_HEADS = 8
NSA_KV_HEADS = 2
NSA_GROUP = 4
CMP_LEN = 32
CMP_STRIDE = 16
CMP_HID = 128
SLC_BLOCK = 64
SLC_TOPK = 16
WINDOW = 512
RET_HEADS = 4
N_MEM = 256
Q_BLOCK = 128
ROPE_BASE = 10000.0
EPS = 1e-6
NEG_INF = -1e30
SCALE = HEAD_DIM ** -0.5

N_CMP = 512
N_SLC = 128
LANES = 128
ONES_ROWS = 16
V_ROWS = HEAD_DIM + ONES_ROWS
LOG2E = 1.4426950408889634
SAMPLE_SEQS_PER_STEP = 8
VMEM_LIMIT = 56 * 1024 * 1024

_W_Q, _W_KVC, _W_KVS, _W_KVW, _W_QKVR, _W_QM, _W_Z, _W_G = 0, 512, 768, 1024, 1280, 2048, 2304, 3328
IN_WP = 3456

F32 = jnp.float32
BF16 = jnp.bfloat16

_LOG_G = [float(np.log1p(-(2.0 ** (-5.0 - h)))) for h in range(RET_HEADS)]


def _dot(a, b):
    return jnp.dot(a, b, preferred_element_type=F32)


def _dot_nt(a, b):
    return lax.dot_general(a, b, (((1,), (1,)), ((), ())), preferred_element_type=F32)


def _split3(x):
    hi = x.astype(BF16)
    r1 = x - hi.astype(F32)
    mid = r1.astype(BF16)
    lo = (r1 - mid.astype(F32)).astype(BF16)
    return hi, mid, lo


def _dot_exact_rhs(a_bf16, x):
    hi, mid, lo = _split3(x)
    return _dot(a_bf16, hi) + _dot(a_bf16, mid) + _dot(a_bf16, lo)


def _dot_exact_lhs(x, b_bf16):
    hi, mid, lo = _split3(x)
    return _dot(hi, b_bf16) + _dot(mid, b_bf16) + _dot(lo, b_bf16)


def _iota(shape, dim):
    return lax.broadcasted_iota(jnp.int32, shape, dim)


def _params(sem):
    return pltpu.CompilerParams(dimension_semantics=sem, vmem_limit_bytes=VMEM_LIMIT)


def _inproj_body(x_ref, g_ref, w_ref, qn_ref, kvc_ref, kvs_ref, kvw_ref, qkvr_ref, qm_ref, z_ref, gt_ref,
                 *attn_refs, transposed_kv):
    x = x_ref[...]
    xn = x * lax.rsqrt(jnp.mean(x * x, axis=-1, keepdims=True) + EPS) * g_ref[...]
    xb = xn.astype(BF16)

    def proj(a, b):
        return _dot(xb, w_ref[:, a:b])

    qn_ref[...] = proj(_W_Q, _W_KVC).astype(BF16)
    kvc = proj(_W_KVC, _W_KVS)
    kvs = proj(_W_KVS, _W_KVW)
    kvw = proj(_W_KVW, _W_QKVR)
    if transposed_kv:
        ksk_ref, ksvt_ref, kwk_ref, kwvt_ref = attn_refs
        kvc_ref[0] = kvc.T
        kvs_t = kvs.T
        kvs_ref[0] = kvs_t
        tm = kvs.shape[0]
        lane = _iota((tm, LANES), 1)
        blk_in_tile = _iota((tm, LANES), 0) // SLC_BLOCK
        k2 = kvs[:, :LANES]
        ksk_ref[:, 0:LANES] = jnp.where(lane < HEAD_DIM, k2,
                                        jnp.where(lane - HEAD_DIM == blk_in_tile, 1.0, 0.0)).astype(BF16)
        ksk_ref[:, LANES:2 * LANES] = jnp.where(lane >= HEAD_DIM, k2,
                                                jnp.where(lane == blk_in_tile, 1.0, 0.0)).astype(BF16)
        ones_rows = jnp.where(_iota((ONES_ROWS, tm), 0) == 0, 1.0, 0.0)
        ksvt_ref[0] = jnp.concatenate([kvs_t[LANES:LANES + HEAD_DIM, :], ones_rows,
                                       kvs_t[LANES + HEAD_DIM:, :], ones_rows], axis=0).astype(BF16)
        kvw_t = kvw.T
        kvw_ref[0] = kvw_t
        kwk_ref[...] = kvw[:, :LANES].astype(BF16)
        kwvt_ref[0] = kvw_t[LANES:, :].astype(BF16)
    else:
        kvc_ref[...] = kvc
        kvs_ref[...] = kvs
        kvw_ref[...] = kvw
    qkvr_ref[...] = proj(_W_QKVR, _W_QM)
    qm_ref[...] = proj(_W_QM, _W_Z)
    z_ref[...] = proj(_W_Z, _W_G)
    gt_ref[...] = jax.nn.sigmoid(proj(_W_G, IN_WP)).T


def _inproj(x2d, ln_g, w_perm, rows_per_batch, transposed_kv):
    n = x2d.shape[0]
    tm = 512
    nt = n // tm
    row = lambda w: pl.BlockSpec((tm, w), lambda i: (i, 0))
    if transposed_kv:
        tpb = rows_per_batch // tm
        kv_shape = jax.ShapeDtypeStruct((n // rows_per_batch, 256, rows_per_batch), F32)
        kv_spec = pl.BlockSpec((1, 256, tm), lambda i: (i // tpb, 0, i % tpb))
    else:
        kv_shape = jax.ShapeDtypeStruct((n, 256), F32)
        kv_spec = row(256)
    out_shape = [
        jax.ShapeDtypeStruct((n, 512), BF16),
        kv_shape, kv_shape, kv_shape,
        jax.ShapeDtypeStruct((n, 768), F32),
        jax.ShapeDtypeStruct((n, 256), F32),
        jax.ShapeDtypeStruct((n, 1024), F32),
        jax.ShapeDtypeStruct((LANES, n), F32),
    ]
    out_specs = [row(512), kv_spec, kv_spec, kv_spec, row(768), row(256), row(1024),
                 pl.BlockSpec((LANES, tm), lambda i: (0, i))]
    if transposed_kv:
        tile_t = pl.BlockSpec((1, LANES, tm), lambda i: (i, 0, 0))
        tile_v = pl.BlockSpec((1, NSA_KV_HEADS * V_ROWS, tm), lambda i: (i, 0, 0))
        out_shape += [jax.ShapeDtypeStruct((n, NSA_KV_HEADS * LANES), BF16),
                      jax.ShapeDtypeStruct((nt, NSA_KV_HEADS * V_ROWS, tm), BF16),
                      jax.ShapeDtypeStruct((n, LANES), BF16),
                      jax.ShapeDtypeStruct((nt, LANES, tm), BF16)]
        out_specs += [row(NSA_KV_HEADS * LANES), tile_v, row(LANES), tile_t]
    return pl.pallas_call(
        functools.partial(_inproj_body, transposed_kv=transposed_kv),
        grid=(nt,),
        in_specs=[row(D_MODEL),
                  pl.BlockSpec((1, D_MODEL), lambda i: (0, 0)),
                  pl.BlockSpec((D_MODEL, IN_WP), lambda i: (0, 0))],
        out_specs=tuple(out_specs),
        out_shape=tuple(out_shape),
        compiler_params=_params(("arbitrary",)),
        name="inproj",
    )(x2d, ln_g, w_perm)


def _mem_kv_body(x_ref, g_ref, w_ref, o_ref):
    x = x_ref[0]
    xn = x * lax.rsqrt(jnp.mean(x * x, axis=-1, keepdims=True) + EPS) * g_ref[...]
    o_ref[0] = _dot(xn.astype(BF16), w_ref[...]).T


def _mem_kv(mem, ln_g, w_bf16):
    b = mem.shape[0]
    nw = w_bf16.shape[1]
    return pl.pallas_call(
        _mem_kv_body,
        grid=(b,),
        in_specs=[pl.BlockSpec((1, N_MEM, D_MODEL), lambda i: (i, 0, 0)),
                  pl.BlockSpec((1, D_MODEL), lambda i: (0, 0)),
                  pl.BlockSpec((D_MODEL, nw), lambda i: (0, 0))],
        out_specs=pl.BlockSpec((1, nw, N_MEM), lambda i: (i, 0, 0)),
        out_shape=jax.ShapeDtypeStruct((b, nw, N_MEM), F32),
        compiler_params=_params(("arbitrary",)),
        name="mem_kv",
    )(mem, ln_g, w_bf16)


def _compress_slab(slabs_t, x_refs, pe_ref, w1_ref, b1_ref, w2_ref, out_refs):
    rows = _iota((N_CMP, LANES), 0)
    hw = 2 * CMP_HID
    chunks_per_page = LANES // CMP_STRIDE
    for v in range(2):
        for j in range(PAST_LEN // LANES):
            xt = slabs_t[v][:, j * LANES:(j + 1) * LANES].T
            for cc in range(chunks_per_page):
                r0 = (j * chunks_per_page + cc) * X_PITCH
                x_refs[v][r0:r0 + CMP_STRIDE, :] = xt[cc * CMP_STRIDE:(cc + 1) * CMP_STRIDE, :]
        nh = N_CMP // CMP_ROW_SPLITS
        accs = []
        for hf in range(CMP_ROW_SPLITS):
            acc = jnp.zeros((nh + (PE_ROWS if hf == 0 else 0), 2 * hw), F32)
            for pp in range(CMP_STRIDE // 2):
                r0 = hf * nh * X_PITCH + 2 * pp
                xcat = jnp.concatenate([x_refs[v][pl.ds(r0, nh, stride=X_PITCH), :],
                                        x_refs[v][pl.ds(r0 + 1, nh, stride=X_PITCH), :]], axis=1).astype(BF16)
                lhs = jnp.concatenate([xcat, pe_ref[v, pp]], axis=0) if hf == 0 else xcat
                acc = acc + _dot(lhs, w1_ref[v, pp])
            accs.append(acc)
        bias = b1_ref[v] + accs[0][nh:nh + 1, 0:hw] + accs[0][nh + 1:nh + 2, hw:2 * hw]
        acc = jnp.concatenate([accs[0][0:nh]] + accs[1:], axis=0)
        nxt = pltpu.roll(acc[:, hw:2 * hw], N_CMP - 1, 0)
        hid = jax.nn.gelu(acc[:, 0:hw] + nxt + bias)
        out = _dot(hid.astype(BF16), w2_ref[v])
        out_refs[v][...] = jnp.where(rows < N_CMP - 1, out, 0.0)


def _compress_prompt_body(k_ref, v_ref, pe_ref, w1_ref, b1_ref, w2_ref, kc_ref, vc_ref, xk_ref, xv_ref):
    _compress_slab((k_ref.at[0], v_ref.at[0]), (xk_ref, xv_ref), pe_ref, w1_ref, b1_ref, w2_ref,
                   (kc_ref.at[0], vc_ref.at[0]))


PE_ROWS = 16
CMP_ROW_SPLITS = 4
X_PITCH = 24


def _x_scratch():
    return [pltpu.VMEM((N_CMP * X_PITCH, LANES), F32) for _ in range(2)]


def _comp_weight_specs():
    z = lambda *_: (0, 0, 0)
    z4 = lambda *_: (0, 0, 0, 0)
    return [pl.BlockSpec((2, CMP_STRIDE // 2, PE_ROWS, 2 * LANES), z4),
            pl.BlockSpec((2, CMP_STRIDE // 2, 2 * LANES, 4 * CMP_HID), z4),
            pl.BlockSpec((2, 1, 2 * CMP_HID), z),
            pl.BlockSpec((2, 2 * CMP_HID, LANES), z)]


def _compress_prompt(kvc_t, comp_w):
    b = kvc_t.shape[0]
    out = jax.ShapeDtypeStruct((b, N_CMP, LANES), F32)
    return pl.pallas_call(
        _compress_prompt_body,
        grid=(b,),
        in_specs=[pl.BlockSpec((1, LANES, SEQ), lambda i: (i, 0, 0)),
                  pl.BlockSpec((1, LANES, SEQ), lambda i: (i, 1, 0))] + _comp_weight_specs(),
        out_specs=(pl.BlockSpec((1, N_CMP, LANES), lambda i: (i, 0, 0)),) * 2,
        out_shape=(out, out),
        scratch_shapes=_x_scratch(),
        compiler_params=_params(("arbitrary",)),
        name="compress_prompt",
    )(kvc_t, kvc_t, *comp_w)


N_PAGES = PAST_LEN // PAGE_SIZE


def _page_copy(pt_ref, pool_ref, slab_ref, sem_ref, seq, slot, s, j):
    return pltpu.make_async_copy(pool_ref.at[pt_ref[seq, j]],
                                 slab_ref.at[slot, s, :, :, pl.ds(j * PAGE_SIZE, PAGE_SIZE)],
                                 sem_ref.at[slot])


def _start_pages(pt_ref, pool_ref, slab_ref, sem_ref, step, slot):
    spb = slab_ref.shape[1]
    for s in range(spb):
        def body(j, c):
            _page_copy(pt_ref, pool_ref, slab_ref, sem_ref, step * spb + s, slot, s, j).start()
            return c
        lax.fori_loop(0, N_PAGES, body, 0, unroll=8)


def _wait_pages(pt_ref, pool_ref, slab_ref, sem_ref, step, slot):
    spb = slab_ref.shape[1]
    for s in range(spb):
        def body(j, c):
            _page_copy(pt_ref, pool_ref, slab_ref, sem_ref, step * spb + s, slot, s, j).wait()
            return c
        lax.fori_loop(0, N_PAGES, body, 0, unroll=8)


def _stream_slab(pt_ref, pool_ref, slab_ref, sem_ref):
    b = pl.program_id(0)
    nb = pl.num_programs(0)
    slot = b % 2

    @pl.when(b == 0)
    def _():
        _start_pages(pt_ref, pool_ref, slab_ref, sem_ref, 0, 0)

    @pl.when(b + 1 < nb)
    def _():
        _start_pages(pt_ref, pool_ref, slab_ref, sem_ref, b + 1, 1 - slot)

    _wait_pages(pt_ref, pool_ref, slab_ref, sem_ref, b, slot)
    return slot


def _compress_sample_body(pt_ref, pool_ref, pe_ref, w1_ref, b1_ref, w2_ref, kc_ref, vc_ref, slab_ref, sem_ref,
                          xk_ref, xv_ref):
    slot = _stream_slab(pt_ref, pool_ref, slab_ref, sem_ref)
    _compress_slab((slab_ref.at[slot, 0, 0], slab_ref.at[slot, 0, 1]), (xk_ref, xv_ref), pe_ref, w1_ref, b1_ref,
                   w2_ref, (kc_ref.at[0], vc_ref.at[0]))


def _slab_scratch(seqs_per_step=1):
    return [pltpu.VMEM((2, seqs_per_step, 2, LANES, PAST_LEN), F32), pltpu.SemaphoreType.DMA((2,))]


def _compress_sample(page_table, pool, comp_w):
    nb = page_table.shape[0]
    out = jax.ShapeDtypeStruct((nb, N_CMP, LANES), F32)
    grid_spec = pltpu.PrefetchScalarGridSpec(
        num_scalar_prefetch=1,
        grid=(nb,),
        in_specs=[pl.BlockSpec(memory_space=pl.ANY)] + _comp_weight_specs(),
        out_specs=(pl.BlockSpec((1, N_CMP, LANES), lambda i, pt: (i, 0, 0)),) * 2,
        scratch_shapes=_slab_scratch() + _x_scratch(),
    )
    return pl.pallas_call(
        _compress_sample_body,
        grid_spec=grid_spec,
        out_shape=(out, out),
        compiler_params=_params(("arbitrary",)),
        name="compress_sample",
    )(page_table, pool, *comp_w)


def _top_k_rows(score, blk, k):
    cand = score > -jnp.inf
    nblk = score.shape[0]
    for _ in range(k):
        m = jnp.max(score, axis=0, keepdims=True)
        idx = jnp.min(jnp.where(score == m, blk, nblk), axis=0, keepdims=True)
        score = jnp.where(blk == idx, -jnp.inf, score)
    return cand & (score == -jnp.inf)


def _add_q_bias(s, bias):
    return jnp.concatenate([s[:, g * Q_BLOCK:(g + 1) * Q_BLOCK] + bias for g in range(NSA_GROUP)], axis=1)


def _safe_inv(l):
    return 1.0 / jnp.maximum(l, 1e-30)


KT_SLC = 512
WIN_KEYS = WINDOW + Q_BLOCK
N_FORCED = 3


def _nsa_prompt_body(qn_ref, gt_ref, ksk_ref, ksvt_ref, kwk_ref, kwvt_ref, kc_ref, vc_ref, o_ref, selb_ref, s_ref):
    i = pl.program_id(1)
    q0 = i * Q_BLOCK
    qt = (qn_ref[...].astype(F32) * (SCALE * LOG2E)).T
    t_row = q0 + _iota((1, Q_BLOCK), 1)
    kc = kc_ref[0].astype(BF16)
    vct = vc_ref[0].T.astype(BF16)
    gt = gt_ref[...]
    zeros_q = jnp.zeros((HEAD_DIM, NSA_GROUP * Q_BLOCK), F32)

    mi = _iota((N_SLC, N_CMP), 0)
    mj = _iota((N_SLC, N_CMP), 1)
    msel = ((mj >= 4 * mi - 1) & (mj <= 4 * mi + 3) & (mj < N_CMP - 1)).astype(BF16)
    blk = _iota((N_SLC, Q_BLOCK), 0)
    cur = t_row // SLC_BLOCK
    forced = (blk == 0) | (blk == cur) | (blk == cur - 1)
    free = (blk * SLC_BLOCK <= t_row) & jnp.logical_not(forced)
    cend = _iota((N_CMP, 1), 0) * CMP_STRIDE + (CMP_LEN - 1)
    bias_c = jnp.where(cend <= t_row, 0.0, NEG_INF)
    wt = jnp.maximum(i - WINDOW // Q_BLOCK, 0)
    w0 = pl.multiple_of(wt * Q_BLOCK, Q_BLOCK)
    kpw = w0 + _iota((WIN_KEYS, 1), 0)
    bias_w = jnp.where((kpw <= t_row) & (kpw > t_row - WINDOW), 0.0, NEG_INF)
    krow = _iota((SLC_BLOCK, 1), 0)

    qgs, o_cs, o_ws = [], [], []
    vrows = [slice(kvh * HEAD_DIM, (kvh + 1) * HEAD_DIM) for kvh in range(NSA_KV_HEADS)]
    for kvh in range(NSA_KV_HEADS):
        qg = jnp.concatenate([qt[(kvh * NSA_GROUP + g) * HEAD_DIM:(kvh * NSA_GROUP + g + 1) * HEAD_DIM, :]
                              for g in range(NSA_GROUP)], axis=1)
        qp = jnp.concatenate([qg, zeros_q] if kvh == 0 else [zeros_q, qg], axis=0).astype(BF16)
        vrow = vrows[kvh]
        qgs.append(qg.astype(BF16))

        s = _add_q_bias(_dot(kc, qp), bias_c)
        m = jnp.max(s, axis=0, keepdims=True)
        e = jnp.exp2(s - m)
        inv = jnp.where(m > 0.5 * NEG_INF, _safe_inv(jnp.sum(e, axis=0, keepdims=True)), 0.0)
        p = e * inv
        o_c = _dot(vct[vrow, :], p.astype(BF16))
        imp = (p[:, 0:Q_BLOCK] + p[:, Q_BLOCK:2 * Q_BLOCK]
               + p[:, 2 * Q_BLOCK:3 * Q_BLOCK] + p[:, 3 * Q_BLOCK:4 * Q_BLOCK])

        score = jnp.where(free, _dot_exact_rhs(msel, imp), -jnp.inf)
        picked = _top_k_rows(score, blk, SLC_TOPK - N_FORCED)
        selb_ref[kvh] = jnp.where(forced | picked, 0.0, NEG_INF)
        o_cs.append(o_c)

        sw = _add_q_bias(_dot(kwk_ref[pl.ds(w0, WIN_KEYS), :], qp), bias_w)
        ew = jnp.exp2(sw - jnp.max(sw, axis=0, keepdims=True))
        pw = (ew * _safe_inv(jnp.sum(ew, axis=0, keepdims=True))).astype(BF16)
        o_w = jnp.zeros((HEAD_DIM, NSA_GROUP * Q_BLOCK), F32)
        for r in range(WIN_KEYS // Q_BLOCK):
            tix = wt + r
            vt = kwvt_ref[tix // 4, vrow, :]
            sub = tix % 4
            vpiece = jnp.where(sub == 0, vt[:, 0:128],
                               jnp.where(sub == 1, vt[:, 128:256],
                                         jnp.where(sub == 2, vt[:, 256:384], vt[:, 384:512])))
            o_w = o_w + _dot(vpiece, pw[r * Q_BLOCK:(r + 1) * Q_BLOCK, :])
        o_ws.append(o_w)

    bias_pad = jnp.zeros((HEAD_DIM - 8, NSA_GROUP * Q_BLOCK), F32)

    def scores(j, buf):
        k0 = pl.multiple_of(j * KT_SLC, KT_SLC)
        for kvh in range(NSA_KV_HEADS):
            b8 = selb_ref[kvh, pl.ds(pl.multiple_of(j * 8, 8), 8), :]
            b_rows = jnp.concatenate([jnp.concatenate([b8] * NSA_GROUP, axis=1), bias_pad], axis=0).astype(BF16)
            q_aug = jnp.concatenate([qgs[kvh], b_rows] if kvh == 0 else [b_rows, qgs[kvh]], axis=0)
            s_ref[buf, kvh] = _dot(ksk_ref[pl.ds(k0, KT_SLC), kvh * LANES:(kvh + 1) * LANES], q_aug)

    def consume(j, buf, state, causal):
        k0 = j * KT_SLC
        out = []
        for kvh in range(NSA_KV_HEADS):
            m_run, acc = state[kvh]
            sj = s_ref[buf, kvh]
            if causal:
                parts = []
                for r in range(KT_SLC // SLC_BLOCK):
                    ok = k0 + r * SLC_BLOCK + krow <= t_row
                    parts.append(jnp.concatenate(
                        [jnp.where(ok, sj[r * SLC_BLOCK:(r + 1) * SLC_BLOCK, g * Q_BLOCK:(g + 1) * Q_BLOCK], NEG_INF)
                         for g in range(NSA_GROUP)], axis=1))
                sj = jnp.concatenate(parts, axis=0)
            m_new = jnp.maximum(m_run, jnp.max(sj, axis=0, keepdims=True))
            alpha = jnp.exp2(m_run - m_new)
            ej = jnp.exp2(sj - m_new).astype(BF16)
            acc = alpha * acc + _dot(ksvt_ref[j, kvh * V_ROWS:(kvh + 1) * V_ROWS, :], ej)
            out.append((m_new, acc))
        return tuple(out)

    init1 = (jnp.full((1, NSA_GROUP * Q_BLOCK), NEG_INF, F32),
             jnp.zeros((V_ROWS, NSA_GROUP * Q_BLOCK), F32))
    n_tiles = q0 // KT_SLC + 1
    n_pairs = (n_tiles + 1) // 2

    def pair(jj, state):
        scores(2 * jj + 1, 1)
        state = consume(2 * jj, 0, state, False)
        scores(2 * jj + 2, 0)
        return consume(2 * jj + 1, 1, state, False)

    scores(0, 0)
    state = lax.fori_loop(0, n_pairs - 1, pair, (init1,) * NSA_KV_HEADS)
    last = 2 * n_pairs - 2

    def last_two(st):
        scores(last + 1, 1)
        return consume(last + 1, 1, consume(last, 0, st, True), True)

    state = lax.cond(n_tiles % 2 == 0, last_two, lambda st: consume(last, 0, st, True), state)

    outs = []
    for kvh in range(NSA_KV_HEADS):
        acc_s = state[kvh][1]
        o_s = acc_s[0:HEAD_DIM, :] * _safe_inv(acc_s[HEAD_DIM:HEAD_DIM + 1, :])

        def gate(br):
            base = br * NSA_HEADS + kvh * NSA_GROUP
            return jnp.concatenate([gt[base + g:base + g + 1, :] for g in range(NSA_GROUP)], axis=1)

        o = gate(0) * o_cs[kvh] + gate(1) * o_s + gate(2) * o_ws[kvh]
        outs.extend([o[:, g * Q_BLOCK:(g + 1) * Q_BLOCK] for g in range(NSA_GROUP)])
    o_ref[...] = jnp.concatenate(outs, axis=0).T


def _nsa_prompt(qn, gt, ksk, ksvt, kwk, kwvt, kc, vc, batch):
    nqb = SEQ // Q_BLOCK
    ntile = SEQ // 512
    return pl.pallas_call(
        _nsa_prompt_body,
        grid=(batch, nqb),
        in_specs=[pl.BlockSpec((Q_BLOCK, 512), lambda b, i: (b * nqb + i, 0)),
                  pl.BlockSpec((LANES, Q_BLOCK), lambda b, i: (0, b * nqb + i)),
                  pl.BlockSpec((SEQ, NSA_KV_HEADS * LANES), lambda b, i: (b, 0)),
                  pl.BlockSpec((ntile, NSA_KV_HEADS * V_ROWS, 512), lambda b, i: (b, 0, 0)),
                  pl.BlockSpec((SEQ, LANES), lambda b, i: (b, 0)),
                  pl.BlockSpec((ntile, LANES, 512), lambda b, i: (b, 0, 0)),
                  pl.BlockSpec((1, N_CMP, LANES), lambda b, i: (b, 0, 0)),
                  pl.BlockSpec((1, N_CMP, LANES), lambda b, i: (b, 0, 0))],
        out_specs=pl.BlockSpec((Q_BLOCK, 512), lambda b, i: (b * nqb + i, 0)),
        out_shape=jax.ShapeDtypeStruct((batch * SEQ, 512), F32),
        scratch_shapes=[pltpu.VMEM((NSA_KV_HEADS, N_SLC, Q_BLOCK), F32),
                        pltpu.VMEM((2, NSA_KV_HEADS, KT_SLC, NSA_GROUP * Q_BLOCK), F32)],
        compiler_params=_params(("arbitrary", "arbitrary")),
        name="nsa_prompt",
    )(qn, gt, ksk, ksvt, kwk, kwvt, kc, vc)


def _softmax_parts(parts, masks):
    m = None
    for s, mk in zip(parts, masks):
        mi = jnp.max(jnp.where(mk, s, NEG_INF), axis=-1, keepdims=True)
        m = mi if m is None else jnp.maximum(m, mi)
    es = [jnp.where(mk, jnp.exp(jnp.where(mk, s, NEG_INF) - m), 0.0) for s, mk in zip(parts, masks)]
    l = None
    for e in es:
        li = jnp.sum(e, axis=-1, keepdims=True)
        l = li if l is None else l + li
    inv = 1.0 / jnp.maximum(l, 1e-30)
    return [e * inv for e in es]


KC_SLC = 2048
ROWS_S = NSA_HEADS * DEC_SEQ


def _pad_rows(x, n):
    if x.shape[0] == n:
        return x
    return jnp.concatenate([x, jnp.zeros((n - x.shape[0], x.shape[1]), x.dtype)], axis=0)


def _nsa_sample_body(pt_ref, pool_ref, qp_ref, g_ref, kc_ref, vc_ref, kvs_ref, kvw_ref, win_ref, wtail_ref,
                     expand_ref, o_ref, nwin_ref, slab_ref, sem_ref):
    slot = _stream_slab(pt_ref, pool_ref, slab_ref, sem_ref)
    spb = slab_ref.shape[1]
    seqs = range(spb)
    qi = _iota((ROWS_S, 1), 0) % DEC_SEQ
    t = PAST_LEN + qi
    jn = _iota((1, LANES), 1)
    n_chunk = PAST_LEN // KC_SLC
    qs = [(qp_ref[s] * SCALE).astype(BF16) for s in seqs]

    cend = _iota((1, N_CMP), 1) * CMP_STRIDE + (CMP_LEN - 1)
    p_cs = [_softmax_parts([_dot_nt(qs[s], kc_ref[s].astype(BF16))], [cend <= t])[0] for s in seqs]
    o_cs = [_dot(p_cs[s].astype(BF16), vc_ref[s].astype(BF16)) for s in seqs]

    half = ROWS_S // NSA_KV_HEADS
    imp = jnp.concatenate(
        [p_cs[s][k * half:k * half + 8] + p_cs[s][k * half + 8:k * half + 16]
         + p_cs[s][k * half + 16:k * half + 24] + p_cs[s][k * half + 24:k * half + 32]
         for s in seqs for k in range(NSA_KV_HEADS)], axis=0)
    mi = _iota((N_CMP, N_SLC), 0)
    mj = _iota((N_CMP, N_SLC), 1)
    msel_t = ((mi >= 4 * mj - 1) & (mi <= 4 * mj + 3) & (mi < N_CMP - 1)).astype(BF16)
    score = _dot_exact_lhs(imp, msel_t)
    score_t = _pad_rows(score, LANES).T
    blk = _iota((N_SLC, LANES), 0)
    forced = (blk == 0) | (blk == N_SLC - 1)
    picked = _top_k_rows(jnp.where(forced, -jnp.inf, score_t), blk, SLC_TOPK - 1 - 2)
    unsel = jnp.where(forced | picked, 0.0, 1.0).T
    unsel64 = [jnp.concatenate([unsel[s * 16:s * 16 + 8]] * NSA_GROUP + [unsel[s * 16 + 8:s * 16 + 16]] * NSA_GROUP,
                               axis=0).astype(BF16) for s in seqs]

    new_bias = jnp.where((jn <= qi) & (jn < DEC_SEQ), 0.0, NEG_INF)
    k_news = [_pad_rows(kvs_ref[s * DEC_SEQ:(s + 1) * DEC_SEQ, 0:LANES], LANES).astype(BF16) for s in seqs]
    v_news = [_pad_rows(kvs_ref[s * DEC_SEQ:(s + 1) * DEC_SEQ, LANES:2 * LANES], LANES).astype(BF16) for s in seqs]
    s_news = [_dot_nt(qs[s], k_news[s]) + new_bias for s in seqs]
    scs = [[_dot(qs[s], slab_ref[slot, s, 0, :, c * KC_SLC:(c + 1) * KC_SLC].astype(BF16))
            + _dot(unsel64[s], expand_ref[:, c * KC_SLC:(c + 1) * KC_SLC]) * NEG_INF
            for c in range(n_chunk)] for s in seqs]
    o_ss = []
    for s in seqs:
        m_s = jnp.max(s_news[s], axis=-1, keepdims=True)
        for sc in scs[s]:
            m_s = jnp.maximum(m_s, jnp.max(sc, axis=-1, keepdims=True))
        e_new = jnp.exp(s_news[s] - m_s)
        l_s = jnp.sum(e_new, axis=-1, keepdims=True)
        acc = _dot(e_new.astype(BF16), v_news[s])
        for c in range(n_chunk):
            ec = jnp.exp(scs[s][c] - m_s)
            l_s = l_s + jnp.sum(ec, axis=-1, keepdims=True)
            acc = acc + _dot_nt(ec.astype(BF16), slab_ref[slot, s, 1, :, c * KC_SLC:(c + 1) * KC_SLC].astype(BF16))
        o_ss.append(acc * _safe_inv(l_s))

    wb = win_ref.shape[2]
    kpos_b = PAST_LEN - wb + _iota((1, wb), 1)
    kpos_n = PAST_LEN + jn
    mask_b = (kpos_b <= t) & (kpos_b > t - WINDOW) & (kpos_b >= 0)
    mask_n = (kpos_n <= t) & (kpos_n > t - WINDOW) & (jn < DEC_SEQ)
    for s in seqs:
        win_t = win_ref[s]
        kw_new = _pad_rows(kvw_ref[s * DEC_SEQ:(s + 1) * DEC_SEQ, 0:LANES], LANES).astype(BF16)
        vw_new = _pad_rows(kvw_ref[s * DEC_SEQ:(s + 1) * DEC_SEQ, LANES:2 * LANES], LANES).astype(BF16)
        p_b, p_n = _softmax_parts([_dot(qs[s], win_t[0:LANES, :].astype(BF16)), _dot_nt(qs[s], kw_new)],
                                  [mask_b, mask_n])
        o_w = _dot_nt(p_b.astype(BF16), win_t[LANES:2 * LANES, :].astype(BF16)) + _dot(p_n.astype(BF16), vw_new)

        g = g_ref[s]
        o_ref[s] = g[:, 0:1] * o_cs[s] + g[:, 1:2] * o_ss[s] + g[:, 2:3] * o_w
        shifted = pltpu.roll(win_t, wb - DEC_SEQ, 1)
        nwin_ref[s, :, 0:wb - LANES] = shifted[:, 0:wb - LANES]
        nwin_ref[s, :, wb - LANES:wb] = jnp.where(jn >= LANES - DEC_SEQ, wtail_ref[s], shifted[:, wb - LANES:wb])


NSA_SAMPLE_SEQS_PER_STEP = 2


def _nsa_sample(page_table, pool_slc, qp, gates, kc, vc, kvs, kvw, win_t, wtail, expand):
    nb = page_table.shape[0]
    wb = win_t.shape[2]
    spb = NSA_SAMPLE_SEQS_PER_STEP
    per_b = lambda *shape: pl.BlockSpec((spb,) + shape, lambda i, pt: (i,) + (0,) * len(shape))
    rows8 = pl.BlockSpec((spb * DEC_SEQ, 256), lambda i, pt: (i, 0))
    grid_spec = pltpu.PrefetchScalarGridSpec(
        num_scalar_prefetch=1,
        grid=(nb // spb,),
        in_specs=[pl.BlockSpec(memory_space=pl.ANY),
                  per_b(ROWS_S, LANES), per_b(ROWS_S, LANES), per_b(N_CMP, LANES), per_b(N_CMP, LANES),
                  rows8, rows8, per_b(256, wb), per_b(256, LANES),
                  pl.BlockSpec((N_SLC, PAST_LEN), lambda i, pt: (0, 0))],
        out_specs=(per_b(ROWS_S, LANES), per_b(256, wb)),
        scratch_shapes=_slab_scratch(spb),
    )
    return pl.pallas_call(
        _nsa_sample_body,
        grid_spec=grid_spec,
        out_shape=(jax.ShapeDtypeStruct((nb, ROWS_S, LANES), F32),
                   jax.ShapeDtypeStruct((nb, 256, wb), F32)),
        compiler_params=_params(("arbitrary",)),
        name="nsa_sample",
    )(page_table, pool_slc, qp, gates, kc, vc, kvs, kvw, win_t, wtail, expand)


RET_W = RET_HEADS * HEAD_DIM


def _retention_body(q_ref, k_ref, v_ref, cos_ref, sin_ref, st0_ref, gn_ref, o_ref, st_ref, state_ref, *, chunk):
    c = pl.program_id(1)
    ck = max(chunk, LANES)
    spb = st0_ref.shape[0]

    @pl.when(c == 0)
    def _():
        state_ref[...] = st0_ref[...]

    lane = _iota((1, RET_W), 1)
    head_of_lane = lane // HEAD_DIM
    low_half = (lane % HEAD_DIM) < HEAD_DIM // 2
    cos = cos_ref[...]
    sin = sin_ref[...]

    def rope(x):
        rot = jnp.where(low_half, -pltpu.roll(x, RET_W - HEAD_DIM // 2, 1), pltpu.roll(x, HEAD_DIM // 2, 1))
        return x * cos + rot * sin

    logg_lane = jnp.zeros((1, RET_W), F32)
    for h in range(RET_HEADS):
        logg_lane = jnp.where(head_of_lane == h, _LOG_G[h], logg_lane)

    n = _iota((chunk, 1), 0).astype(F32)
    nk = _iota((ck, 1), 0).astype(F32)
    diff = n - _iota((1, ck), 1).astype(F32)
    in_chunk = _iota((1, ck), 1) < chunk
    dmats = [jnp.where((diff >= 0) & in_chunk, jnp.exp(_LOG_G[h] * jnp.maximum(diff, 0.0)), 0.0)
             for h in range(RET_HEADS)]
    xi = jnp.exp(logg_lane * (n + 1.0))
    zeta = jnp.where(nk < chunk, jnp.exp(logg_lane * (chunk - 1.0 - nk)), 0.0)
    row_head = _iota((RET_W, 1), 0) // HEAD_DIM
    decay_rows = jnp.zeros((RET_W, 1), F32)
    for h in range(RET_HEADS):
        decay_rows = jnp.where(row_head == h, float(np.exp(_LOG_G[h] * chunk)), decay_rows)
    avg = jnp.where(_iota((RET_W, RET_W), 0) // HEAD_DIM == _iota((RET_W, RET_W), 1) // HEAD_DIM,
                    1.0 / HEAD_DIM, 0.0).astype(BF16)
    gn = gn_ref[...]

    seqs = range(spb)
    rows = [slice(bb * chunk, (bb + 1) * chunk) for bb in seqs]
    qs = [rope(q_ref[r, :]) * SCALE for r in rows]
    kps = [_pad_rows(rope(k_ref[r, :]), ck) for r in rows]
    vps = [_pad_rows(v_ref[r, :], ck) for r in rows]
    hms = [head_of_lane == h for h in range(RET_HEADS)]
    ss = [[_dot_nt(jnp.where(hms[h], qs[bb], 0.0), kps[bb]) * dmats[h] for h in range(RET_HEADS)] for bb in seqs]
    inners = []
    for bb in seqs:
        inner = jnp.zeros((chunk, RET_W), F32)
        for h in range(RET_HEADS):
            inner = inner + jnp.where(hms[h], _dot(ss[bb][h], vps[bb]), 0.0)
        inners.append(inner)
    states = [state_ref[bb] for bb in seqs]
    os_ = [inners[bb] + _dot(qs[bb] * xi, states[bb]) for bb in seqs]
    for bb in seqs:
        kz_t = (kps[bb] * zeta).T
        new_state = decay_rows * states[bb] + jnp.where(row_head == head_of_lane, _dot(kz_t, vps[bb]), 0.0)
        state_ref[bb] = new_state
        st_ref[bb] = new_state

    o = jnp.concatenate(os_, axis=0) if spb > 1 else os_[0]
    mu = _dot_exact_lhs(o, avg)
    d = o - mu
    var = _dot_exact_lhs(d * d, avg)
    o_ref[...] = d * lax.rsqrt(var + EPS) * gn


def _retention(qkvr, cos, sin, state_bd, gn, batch, t_len, chunk, seqs_per_step=1):
    nch = t_len // chunk
    spb = seqs_per_step
    assert spb == 1 or nch == 1
    col = lambda j: pl.BlockSpec((spb * chunk, RET_W), lambda b, c: (b * nch + c, j))
    tab = pl.BlockSpec((chunk, RET_W), lambda b, c: (c, 0))
    st = pl.BlockSpec((spb, RET_W, RET_W), lambda b, c: (b, 0, 0))
    return pl.pallas_call(
        functools.partial(_retention_body, chunk=chunk),
        grid=(batch // spb, nch),
        in_specs=[col(0), col(1), col(2), tab, tab, st, pl.BlockSpec((1, RET_W), lambda b, c: (0, 0))],
        out_specs=(pl.BlockSpec((spb * chunk, RET_W), lambda b, c: (b * nch + c, 0)), st),
        out_shape=(jax.ShapeDtypeStruct((batch * t_len, RET_W), F32),
                   jax.ShapeDtypeStruct((batch, RET_W, RET_W), F32)),
        scratch_shapes=[pltpu.VMEM((spb, RET_W, RET_W), F32)],
        compiler_params=_params(("arbitrary", "arbitrary")),
        name="retention",
    )(qkvr, qkvr, qkvr, cos, sin, state_bd, gn)


MEM_W = 4 * HEAD_DIM


def _mem_attn_body(q_ref, mkv_ref, o_ref, *, tm):
    head_of_lane = _iota((1, MEM_W), 1) // HEAD_DIM
    spb = mkv_ref.shape[0]
    q_all = q_ref[...] * SCALE
    scores = []
    for bb in range(spb):
        q = q_all[bb * tm:(bb + 1) * tm, :]
        q4 = jnp.concatenate([jnp.where(head_of_lane == h, q, 0.0) for h in range(4)], axis=0).astype(BF16)
        scores.append(_dot(q4, mkv_ref[bb, 0:MEM_W, :].astype(BF16)))
    s = jnp.concatenate(scores, axis=0)
    e = jnp.exp(s - jnp.max(s, axis=-1, keepdims=True))
    p = (e / jnp.sum(e, axis=-1, keepdims=True)).astype(BF16)
    for bb in range(spb):
        pv = _dot_nt(p[bb * 4 * tm:(bb + 1) * 4 * tm, :], mkv_ref[bb, MEM_W:2 * MEM_W, :].astype(BF16))
        o = jnp.zeros((tm, MEM_W), F32)
        for h in range(4):
            o = o + jnp.where(head_of_lane == h, pv[h * tm:(h + 1) * tm, :], 0.0)
        o_ref[bb * tm:(bb + 1) * tm, :] = o


def _mem_attn(qm, mkv_t, batch, rows_per_batch, tm, seqs_per_step=1):
    nt = rows_per_batch // tm
    spb = seqs_per_step
    assert spb == 1 or nt == 1
    return pl.pallas_call(
        functools.partial(_mem_attn_body, tm=tm),
        grid=(batch // spb, nt),
        in_specs=[pl.BlockSpec((spb * tm, MEM_W), lambda b, i: (b * nt + i, 0)),
                  pl.BlockSpec((spb, 2 * MEM_W, N_MEM), lambda b, i: (b, 0, 0))],
        out_specs=pl.BlockSpec((spb * tm, MEM_W), lambda b, i: (b * nt + i, 0)),
        out_shape=jax.ShapeDtypeStruct((batch * rows_per_batch, MEM_W), F32),
        compiler_params=_params(("arbitrary", "arbitrary")),
        name="mem_attn",
    )(qm, mkv_t)


def _finish_body(x_ref, on_ref, or_ref, om_ref, z_ref, w_ref, g_ref, y_ref):
    o = jnp.concatenate([on_ref[...], or_ref[...], om_ref[...]], axis=-1)
    z = z_ref[...]
    mix = o * (z * jax.nn.sigmoid(z))
    xo = x_ref[...] + _dot(mix.astype(BF16), w_ref[...])
    y_ref[...] = xo * lax.rsqrt(jnp.mean(xo * xo, axis=-1, keepdims=True) + EPS) * g_ref[...]


def _finish(x2d, o_n, o_r, o_m, z, w_out, ln_final):
    n = x2d.shape[0]
    tm = 512
    row = lambda w: pl.BlockSpec((tm, w), lambda i: (i, 0))
    return pl.pallas_call(
        _finish_body,
        grid=(n // tm,),
        in_specs=[row(D_MODEL), row(512), row(256), row(256), row(D_MODEL),
                  pl.BlockSpec((D_MODEL, D_MODEL), lambda i: (0, 0)),
                  pl.BlockSpec((1, D_MODEL), lambda i: (0, 0))],
        out_specs=row(D_MODEL),
        out_shape=jax.ShapeDtypeStruct((n, D_MODEL), F32),
        compiler_params=_params(("arbitrary",)),
        name="finish",
    )(x2d, o_n, o_r, o_m, z, w_out, ln_final)


def _permute_w_in(w):
    sizes = (512, 256, 256, 256, 24, 512, 256, 256, 256, 256, 256, 256)
    offs = np.concatenate([[0], np.cumsum(sizes)])
    part = lambda i: w[:, int(offs[i]):int(offs[i + 1])]
    q_n, kv_c, kv_s, kv_w, g_n, z_n, q_r, k_r, v_r, z_r, q_m, z_m = [part(i) for i in range(12)]
    g_pad = jnp.pad(g_n, ((0, 0), (0, LANES - g_n.shape[1])))
    return jnp.concatenate([q_n, kv_c, kv_s, kv_w, q_r, k_r, v_r, q_m, z_n, z_r, z_m, g_pad], axis=1).astype(BF16)


def _compress_weights(pe, w1, b1, w2):
    npair = CMP_STRIDE // 2
    pe_r = pe.reshape(2, 2, npair, 2, HEAD_DIM)
    pe_l = jnp.broadcast_to(pe_r[:, :, :, :, None, :], (2, 2, npair, 2, NSA_KV_HEADS, HEAD_DIM))
    pe_l = pe_l.reshape(2, 2, npair, 2 * LANES).transpose(0, 2, 1, 3)
    pe_l = jnp.pad(pe_l, ((0, 0), (0, 0), (0, PE_ROWS - 2), (0, 0))).astype(BF16)
    w1_r = w1.reshape(2, 2, CMP_STRIDE, HEAD_DIM, CMP_HID)
    zw = jnp.zeros_like(w1_r)
    w1_bd = jnp.concatenate([jnp.concatenate([w1_r, zw], axis=-1),
                             jnp.concatenate([zw, w1_r], axis=-1)], axis=-2)
    w1_bd = w1_bd.reshape(2, 2, npair, 2 * LANES, 2 * CMP_HID)
    w1_bd = jnp.concatenate([w1_bd[:, 0], w1_bd[:, 1]], axis=-1).astype(BF16)
    b1_l = jnp.concatenate([b1, b1], axis=-1).reshape(2, 1, 2 * CMP_HID)
    z2 = jnp.zeros_like(w2)
    w2_bd = jnp.concatenate([jnp.concatenate([w2, z2], axis=-1),
                             jnp.concatenate([z2, w2], axis=-1)], axis=-2).astype(BF16)
    return pe_l, w1_bd, b1_l, w2_bd


def _rope_tables(pos):
    half = HEAD_DIM // 2
    inv = ROPE_BASE ** (-jnp.arange(half, dtype=F32) / half)
    ang = pos.astype(F32)[:, None] * inv[None, :]
    cos, sin = jnp.cos(ang), jnp.sin(ang)
    cos_l = jnp.tile(jnp.concatenate([cos, cos], axis=-1), (1, RET_HEADS))
    sin_l = jnp.tile(jnp.concatenate([sin, sin], axis=-1), (1, RET_HEADS))
    return cos_l, sin_l


def _block_diag_state(st):
    b = st.shape[0]
    eye = jnp.eye(RET_HEADS, dtype=st.dtype)
    return jnp.einsum("bhde,hg->bhdge", st, eye).reshape(b, RET_W, RET_W)


def _diag_blocks(st_bd):
    b = st_bd.shape[0]
    r = st_bd.reshape(b, RET_HEADS, HEAD_DIM, RET_HEADS, HEAD_DIM)
    return jnp.stack([r[:, h, :, h, :] for h in range(RET_HEADS)], axis=1)


def _kv_shape(a, b, t):
    return a.reshape(1, b, t, 2, NSA_KV_HEADS, HEAD_DIM)


def _position_minor(cache, heads):
    b, l = cache.shape[:2]
    return jnp.transpose(cache, (0, 2, 3, 4, 1)).reshape(b, 2 * heads * HEAD_DIM, l)


def _from_position_minor(a, heads):
    b, _, l = a.shape
    return jnp.transpose(a.reshape(b, 2, heads, HEAD_DIM, l), (0, 4, 1, 2, 3))[None]


def kernel(x_prompt, mem_prompt, x_sample, cache_nsa_cmp, cache_nsa_slc, cache_nsa_win, state_ret, cache_mem,
           page_table, ln_mix, w_in, cmp_pe, cmp_w1, cmp_b1, cmp_w2, ret_gn, ln_mem, w_mem_kv, w_out, ln_final):
    bp, t_len = x_prompt.shape[:2]
    bs, s_len = x_sample.shape[:2]
    assert (t_len, s_len) == (SEQ, DEC_SEQ) and ln_mix.shape[0] == 1
    w_perm = _permute_w_in(w_in[0])
    comp_w = _compress_weights(cmp_pe[0], cmp_w1[0], cmp_b1[0], cmp_w2[0])
    ln_g = ln_mix[0].reshape(1, D_MODEL)
    gn = ret_gn[0].reshape(1, RET_W)
    w_out_b = w_out[0].astype(BF16)
    ln_f = ln_final.reshape(1, D_MODEL)

    xp2 = x_prompt.reshape(bp * SEQ, D_MODEL)
    (qn, kvc_t, kvs_t, kvw_t, qkvr, qm, z, gt, ksk, ksvt, kwk, kwvt) = _inproj(xp2, ln_g, w_perm, SEQ, True)
    kc, vc = _compress_prompt(kvc_t, comp_w)
    o_n = _nsa_prompt(qn, gt, ksk, ksvt, kwk, kwvt, kc, vc, bp)
    cos_p, sin_p = _rope_tables(jnp.arange(SEQ))
    o_r, st_p = _retention(qkvr, cos_p, sin_p, jnp.zeros((bp, RET_W, RET_W), F32), gn, bp, SEQ, 256)
    mkv_t = _mem_kv(mem_prompt, ln_mem[0].reshape(1, D_MODEL), w_mem_kv[0].astype(BF16))
    o_m = _mem_attn(qm, mkv_t, bp, SEQ, 512)
    y_prompt = _finish(xp2, o_n, o_r, o_m, z, w_out_b, ln_f).reshape(bp, SEQ, D_MODEL)
    new_cmp_p = _from_position_minor(kvc_t, NSA_KV_HEADS)
    new_slc_p = _from_position_minor(kvs_t, NSA_KV_HEADS)
    new_win_p = _from_position_minor(kvw_t[:, :, SEQ - WINDOW:], NSA_KV_HEADS)
    new_ret_p = _diag_blocks(st_p)[None]
    new_mem_p = _from_position_minor(mkv_t, 4)

    xs2 = x_sample.reshape(bs * DEC_SEQ, D_MODEL)
    (qn_s, kvc_s, kvs_s, kvw_s, qkvr_s, qm_s, z_s, gt_s) = _inproj(xs2, ln_g, w_perm, DEC_SEQ, False)
    pool_cmp = _position_minor(cache_nsa_cmp[0], NSA_KV_HEADS).reshape(-1, 2, LANES, PAGE_SIZE)
    pool_slc = _position_minor(cache_nsa_slc[0], NSA_KV_HEADS).reshape(-1, 2, LANES, PAGE_SIZE)
    kc_s, vc_s = _compress_sample(page_table, pool_cmp, comp_w)
    q5 = qn_s.astype(F32).reshape(bs, DEC_SEQ, NSA_KV_HEADS, NSA_GROUP, HEAD_DIM).transpose(0, 2, 3, 1, 4)
    zq = jnp.zeros_like(q5[:, 0])
    qp = jnp.stack([jnp.concatenate([q5[:, 0], zq], axis=-1), jnp.concatenate([zq, q5[:, 1]], axis=-1)], axis=1)
    qp = qp.reshape(bs, ROWS_S, LANES)
    g5 = gt_s[:3 * NSA_HEADS].reshape(3, NSA_KV_HEADS, NSA_GROUP, bs, DEC_SEQ).transpose(3, 1, 2, 4, 0)
    gates = jnp.pad(g5.reshape(bs, ROWS_S, 3), ((0, 0), (0, 0), (0, LANES - 3)))
    expand = jnp.asarray(np.repeat(np.eye(N_SLC, dtype=np.float32), SLC_BLOCK, axis=1), dtype=BF16)
    win_t = _position_minor(cache_nsa_win[0], NSA_KV_HEADS)
    wtail = jnp.pad(kvw_s.reshape(bs, DEC_SEQ, 256).transpose(0, 2, 1), ((0, 0), (0, 0), (LANES - DEC_SEQ, 0)))
    o_sn, new_win_t = _nsa_sample(page_table, pool_slc, qp, gates, kc_s, vc_s, kvs_s, kvw_s, win_t, wtail, expand)
    o6 = o_sn.reshape(bs, NSA_KV_HEADS, NSA_GROUP, DEC_SEQ, NSA_KV_HEADS, HEAD_DIM)
    o_n_s = jnp.stack([o6[:, 0, :, :, 0], o6[:, 1, :, :, 1]], axis=1)
    o_n_s = o_n_s.transpose(0, 3, 1, 2, 4).reshape(bs * DEC_SEQ, 512)
    cos_s, sin_s = _rope_tables(PAST_LEN + jnp.arange(DEC_SEQ))
    o_r_s, st_s = _retention(qkvr_s, cos_s, sin_s, _block_diag_state(state_ret[0]), gn, bs, DEC_SEQ, DEC_SEQ,
                             seqs_per_step=SAMPLE_SEQS_PER_STEP)
    o_m_s = _mem_attn(qm_s, _position_minor(cache_mem[0], 4), bs, DEC_SEQ, DEC_SEQ,
                      seqs_per_step=SAMPLE_SEQS_PER_STEP)
    y_sample = _finish(xs2, o_n_s, o_r_s, o_m_s, z_s, w_out_b, ln_f).reshape(bs, DEC_SEQ, D_MODEL)
    new_cmp_s = _kv_shape(kvc_s, bs, DEC_SEQ)
    new_slc_s = _kv_shape(kvs_s, bs, DEC_SEQ)
    new_win_s = _from_position_minor(new_win_t, NSA_KV_HEADS)
    new_ret_s = _diag_blocks(st_s)[None]

    return (y_prompt, y_sample, new_cmp_p, new_cmp_s, new_slc_p, new_slc_s, new_win_p, new_win_s,
            new_ret_p, new_ret_s, new_mem_p)
```

```python
import functools

import numpy as np
import jax
import jax.numpy as jnp
from jax import lax
from jax.experimental import pallas as pl
from jax.experimental.pallas import tpu as pltpu

D_MODEL = 1024
SEQ = 8192
DEC_SEQ = 8
PAST_LEN = 8192
PAGE_SIZE = 128
HEAD_DIM = 64
NSA_HEADS = 8
NSA_KV_HEADS = 2
NSA_GROUP = 4
CMP_LEN = 32
CMP_STRIDE = 16
CMP_HID = 128
SLC_BLOCK = 64
SLC_TOPK = 16
WINDOW = 512
RET_HEADS = 4
N_MEM = 256
Q_BLOCK = 128
ROPE_BASE = 10000.0
EPS = 1e-6
NEG_INF = -1e30
SCALE = HEAD_DIM ** -0.5

N_CMP = 512
N_SLC = 128
LANES = 128
ONES_ROWS = 16
V_ROWS = HEAD_DIM + ONES_ROWS
LOG2E = 1.4426950408889634
SAMPLE_SEQS_PER_STEP = 8
VMEM_LIMIT = 56 * 1024 * 1024

_W_Q, _W_KVC, _W_KVS, _W_KVW, _W_QKVR, _W_QM, _W_Z, _W_G = 0, 512, 768, 1024, 1280, 2048, 2304, 3328
IN_WP = 3456

F32 = jnp.float32
BF16 = jnp.bfloat16

_LOG_G = [float(np.log1p(-(2.0 ** (-5.0 - h)))) for h in range(RET_HEADS)]


def _dot(a, b):
    return jnp.dot(a, b, preferred_element_type=F32)


def _dot_nt(a, b):
    return lax.dot_general(a, b, (((1,), (1,)), ((), ())), preferred_element_type=F32)


def _split3(x):
    hi = x.astype(BF16)
    r1 = x - hi.astype(F32)
    mid = r1.astype(BF16)
    lo = (r1 - mid.astype(F32)).astype(BF16)
    return hi, mid, lo


def _dot_exact_rhs(a_bf16, x):
    hi, mid, lo = _split3(x)
    return _dot(a_bf16, hi) + _dot(a_bf16, mid) + _dot(a_bf16, lo)


def _dot_exact_lhs(x, b_bf16):
    hi, mid, lo = _split3(x)
    return _dot(hi, b_bf16) + _dot(mid, b_bf16) + _dot(lo, b_bf16)


def _iota(shape, dim):
    return lax.broadcasted_iota(jnp.int32, shape, dim)


def _params(sem):
    return pltpu.CompilerParams(dimension_semantics=sem, vmem_limit_bytes=VMEM_LIMIT)


def _inproj_body(x_ref, g_ref, w_ref, qn_ref, kvc_ref, kvs_ref, kvw_ref, qkvr_ref, qm_ref, z_ref, gt_ref,
                 *attn_refs, transposed_kv):
    x = x_ref[...]
    xn = x * lax.rsqrt(jnp.mean(x * x, axis=-1, keepdims=True) + EPS) * g_ref[...]
    xb = xn.astype(BF16)

    def proj(a, b):
        return _dot(xb, w_ref[:, a:b])

    qn_ref[...] = proj(_W_Q, _W_KVC).astype(BF16)
    kvc = proj(_W_KVC, _W_KVS)
    kvs = proj(_W_KVS, _W_KVW)
    kvw = proj(_W_KVW, _W_QKVR)
    if transposed_kv:
        ksk_ref, ksvt_ref, kwk_ref, kwvt_ref = attn_refs
        kvc_ref[0] = kvc.T
        kvs_t = kvs.T
        kvs_ref[0] = kvs_t
        tm = kvs.shape[0]
        lane = _iota((tm, LANES), 1)
        blk_in_tile = _iota((tm, LANES), 0) // SLC_BLOCK
        k2 = kvs[:, :LANES]
        ksk_ref[:, 0:LANES] = jnp.where(lane < HEAD_DIM, k2,
                                        jnp.where(lane - HEAD_DIM == blk_in_tile, 1.0, 0.0)).astype(BF16)
        ksk_ref[:, LANES:2 * LANES] = jnp.where(lane >= HEAD_DIM, k2,
                                                jnp.where(lane == blk_in_tile, 1.0, 0.0)).astype(BF16)
        ones_rows = jnp.where(_iota((ONES_ROWS, tm), 0) == 0, 1.0, 0.0)
        ksvt_ref[0] = jnp.concatenate([kvs_t[LANES:LANES + HEAD_DIM, :], ones_rows,
                                       kvs_t[LANES + HEAD_DIM:, :], ones_rows], axis=0).astype(BF16)
        kvw_t = kvw.T
        kvw_ref[0] = kvw_t
        kwk_ref[...] = kvw[:, :LANES].astype(BF16)
        kwvt_ref[0] = kvw_t[LANES:, :].astype(BF16)
    else:
        kvc_ref[...] = kvc
        kvs_ref[...] = kvs
        kvw_ref[...] = kvw
    qkvr_ref[...] = proj(_W_QKVR, _W_QM)
    qm_ref[...] = proj(_W_QM, _W_Z)
    z_ref[...] = proj(_W_Z, _W_G)
    gt_ref[...] = jax.nn.sigmoid(proj(_W_G, IN_WP)).T


def _inproj(x2d, ln_g, w_perm, rows_per_batch, transposed_kv):
    n = x2d.shape[0]
    tm = 512
    nt = n // tm
    row = lambda w: pl.BlockSpec((tm, w), lambda i: (i, 0))
    if transposed_kv:
        tpb = rows_per_batch // tm
        kv_shape = jax.ShapeDtypeStruct((n // rows_per_batch, 256, rows_per_batch), F32)
        kv_spec = pl.BlockSpec((1, 256, tm), lambda i: (i // tpb, 0, i % tpb))
    else:
        kv_shape = jax.ShapeDtypeStruct((n, 256), F32)
        kv_spec = row(256)
    out_shape = [
        jax.ShapeDtypeStruct((n, 512), BF16),
        kv_shape, kv_shape, kv_shape,
        jax.ShapeDtypeStruct((n, 768), F32),
        jax.ShapeDtypeStruct((n, 256), F32),
        jax.ShapeDtypeStruct((n, 1024), F32),
        jax.ShapeDtypeStruct((LANES, n), F32),
    ]
    out_specs = [row(512), kv_spec, kv_spec, kv_spec, row(768), row(256), row(1024),
                 pl.BlockSpec((LANES, tm), lambda i: (0, i))]
    if transposed_kv:
        tile_t = pl.BlockSpec((1, LANES, tm), lambda i: (i, 0, 0))
        tile_v = pl.BlockSpec((1, NSA_KV_HEADS * V_ROWS, tm), lambda i: (i, 0, 0))
        out_shape += [jax.ShapeDtypeStruct((n, NSA_KV_HEADS * LANES), BF16),
                      jax.ShapeDtypeStruct((nt, NSA_KV_HEADS * V_ROWS, tm), BF16),
                      jax.ShapeDtypeStruct((n, LANES), BF16),
                      jax.ShapeDtypeStruct((nt, LANES, tm), BF16)]
        out_specs += [row(NSA_KV_HEADS * LANES), tile_v, row(LANES), tile_t]
    return pl.pallas_call(
        functools.partial(_inproj_body, transposed_kv=transposed_kv),
        grid=(nt,),
        in_specs=[row(D_MODEL),
                  pl.BlockSpec((1, D_MODEL), lambda i: (0, 0)),
                  pl.BlockSpec((D_MODEL, IN_WP), lambda i: (0, 0))],
        out_specs=tuple(out_specs),
        out_shape=tuple(out_shape),
        compiler_params=_params(("arbitrary",)),
        name="inproj",
    )(x2d, ln_g, w_perm)


def _mem_kv_body(x_ref, g_ref, w_ref, o_ref):
    x = x_ref[0]
    xn = x * lax.rsqrt(jnp.mean(x * x, axis=-1, keepdims=True) + EPS) * g_ref[...]
    o_ref[0] = _dot(xn.astype(BF16), w_ref[...]).T


def _mem_kv(mem, ln_g, w_bf16):
    b = mem.shape[0]
    nw = w_bf16.shape[1]
    return pl.pallas_call(
        _mem_kv_body,
        grid=(b,),
        in_specs=[pl.BlockSpec((1, N_MEM, D_MODEL), lambda i: (i, 0, 0)),
                  pl.BlockSpec((1, D_MODEL), lambda i: (0, 0)),
                  pl.BlockSpec((D_MODEL, nw), lambda i: (0, 0))],
        out_specs=pl.BlockSpec((1, nw, N_MEM), lambda i: (i, 0, 0)),
        out_shape=jax.ShapeDtypeStruct((b, nw, N_MEM), F32),
        compiler_params=_params(("arbitrary",)),
        name="mem_kv",
    )(mem, ln_g, w_bf16)


def _compress_slab(slabs_t, x_refs, pe_ref, w1_ref, b1_ref, w2_ref, out_refs):
    rows = _iota((N_CMP, LANES), 0)
    hw = 2 * CMP_HID
    chunks_per_page = LANES // CMP_STRIDE
    for v in range(2):
        for j in range(PAST_LEN // LANES):
            xt = slabs_t[v][:, j * LANES:(j + 1) * LANES].T
            for cc in range(chunks_per_page):
                r0 = (j * chunks_per_page + cc) * X_PITCH
                x_refs[v][r0:r0 + CMP_STRIDE, :] = xt[cc * CMP_STRIDE:(cc + 1) * CMP_STRIDE, :]
        nh = N_CMP // CMP_ROW_SPLITS
        accs = []
        for hf in range(CMP_ROW_SPLITS):
            acc = jnp.zeros((nh + (PE_ROWS if hf == 0 else 0), 2 * hw), F32)
            for pp in range(CMP_STRIDE // 2):
                r0 = hf * nh * X_PITCH + 2 * pp
                xcat = jnp.concatenate([x_refs[v][pl.ds(r0, nh, stride=X_PITCH), :],
                                        x_refs[v][pl.ds(r0 + 1, nh, stride=X_PITCH), :]], axis=1).astype(BF16)
                lhs = jnp.concatenate([xcat, pe_ref[v, pp]], axis=0) if hf == 0 else xcat
                acc = acc + _dot(lhs, w1_ref[v, pp])
            accs.append(acc)
        bias = b1_ref[v] + accs[0][nh:nh + 1, 0:hw] + accs[0][nh + 1:nh + 2, hw:2 * hw]
        acc = jnp.concatenate([accs[0][0:nh]] + accs[1:], axis=0)
        nxt = pltpu.roll(acc[:, hw:2 * hw], N_CMP - 1, 0)
        hid = jax.nn.gelu(acc[:, 0:hw] + nxt + bias)
        out = _dot(hid.astype(BF16), w2_ref[v])
        out_refs[v][...] = jnp.where(rows < N_CMP - 1, out, 0.0)


def _compress_prompt_body(k_ref, v_ref, pe_ref, w1_ref, b1_ref, w2_ref, kc_ref, vc_ref, xk_ref, xv_ref):
    _compress_slab((k_ref.at[0], v_ref.at[0]), (xk_ref, xv_ref), pe_ref, w1_ref, b1_ref, w2_ref,
                   (kc_ref.at[0], vc_ref.at[0]))


PE_ROWS = 16
CMP_ROW_SPLITS = 4
X_PITCH = 24


def _x_scratch():
    return [pltpu.VMEM((N_CMP * X_PITCH, LANES), F32) for _ in range(2)]


def _comp_weight_specs():
    z = lambda *_: (0, 0, 0)
    z4 = lambda *_: (0, 0, 0, 0)
    return [pl.BlockSpec((2, CMP_STRIDE // 2, PE_ROWS, 2 * LANES), z4),
            pl.BlockSpec((2, CMP_STRIDE // 2, 2 * LANES, 4 * CMP_HID), z4),
            pl.BlockSpec((2, 1, 2 * CMP_HID), z),
            pl.BlockSpec((2, 2 * CMP_HID, LANES), z)]


def _compress_prompt(kvc_t, comp_w):
    b = kvc_t.shape[0]
    out = jax.ShapeDtypeStruct((b, N_CMP, LANES), F32)
    return pl.pallas_call(
        _compress_prompt_body,
        grid=(b,),
        in_specs=[pl.BlockSpec((1, LANES, SEQ), lambda i: (i, 0, 0)),
                  pl.BlockSpec((1, LANES, SEQ), lambda i: (i, 1, 0))] + _comp_weight_specs(),
        out_specs=(pl.BlockSpec((1, N_CMP, LANES), lambda i: (i, 0, 0)),) * 2,
        out_shape=(out, out),
        scratch_shapes=_x_scratch(),
        compiler_params=_params(("arbitrary",)),
        name="compress_prompt",
    )(kvc_t, kvc_t, *comp_w)


N_PAGES = PAST_LEN // PAGE_SIZE


def _page_copy(pt_ref, pool_ref, slab_ref, sem_ref, seq, slot, s, j):
    return pltpu.make_async_copy(pool_ref.at[pt_ref[seq, j]],
                                 slab_ref.at[slot, s, :, :, pl.ds(j * PAGE_SIZE, PAGE_SIZE)],
                                 sem_ref.at[slot])


def _start_pages(pt_ref, pool_ref, slab_ref, sem_ref, step, slot):
    spb = slab_ref.shape[1]
    for s in range(spb):
        for j in range(N_PAGES):
            _page_copy(pt_ref, pool_ref, slab_ref, sem_ref, step * spb + s, slot, s, j).start()


def _wait_pages(pt_ref, pool_ref, slab_ref, sem_ref, step, slot):
    spb = slab_ref.shape[1]
    for s in range(spb):
        def body(j, c):
            _page_copy(pt_ref, pool_ref, slab_ref, sem_ref, step * spb + s, slot, s, j).wait()
            return c
        lax.fori_loop(0, N_PAGES, body, 0, unroll=8)


SLAB_SLOTS = 2


def _slab_substeps(pt_ref, pool_ref, slab_ref, sem_ref, compute):
    g = pl.program_id(0)
    n_sub = pl.num_programs(0) * SLAB_SLOTS
    args = (pt_ref, pool_ref, slab_ref, sem_ref)

    @pl.when(g == 0)
    def _():
        _start_pages(*args, 0, 0)

    for slot in range(SLAB_SLOTS):
        step = g * SLAB_SLOTS + slot
        _wait_pages(*args, step, slot)
        _start_pages(*args, jnp.minimum(step + 1, n_sub - 1), (slot + 1) % SLAB_SLOTS)
        compute(slot)

    @pl.when(g == pl.num_programs(0) - 1)
    def _():
        _wait_pages(*args, n_sub - 1, 0)


def _compress_sample_body(pt_ref, pool_ref, pe_ref, w1_ref, b1_ref, w2_ref, kc_ref, vc_ref, slab_ref, sem_ref,
                          xk_ref, xv_ref):
    def compute(slot):
        _compress_slab((slab_ref.at[slot, 0, 0], slab_ref.at[slot, 0, 1]), (xk_ref, xv_ref), pe_ref, w1_ref, b1_ref,
                       w2_ref, (kc_ref.at[slot], vc_ref.at[slot]))

    _slab_substeps(pt_ref, pool_ref, slab_ref, sem_ref, compute)


def _slab_scratch(seqs_per_step=1):
    return [pltpu.VMEM((2, seqs_per_step, 2, LANES, PAST_LEN), F32), pltpu.SemaphoreType.DMA((2,))]


def _compress_sample(page_table, pool, comp_w):
    nb = page_table.shape[0]
    out = jax.ShapeDtypeStruct((nb, N_CMP, LANES), F32)
    grid_spec = pltpu.PrefetchScalarGridSpec(
        num_scalar_prefetch=1,
        grid=(nb // SLAB_SLOTS,),
        in_specs=[pl.BlockSpec(memory_space=pl.ANY)] + _comp_weight_specs(),
        out_specs=(pl.BlockSpec((SLAB_SLOTS, N_CMP, LANES), lambda i, pt: (i, 0, 0)),) * 2,
        scratch_shapes=_slab_scratch() + _x_scratch(),
    )
    return pl.pallas_call(
        _compress_sample_body,
        grid_spec=grid_spec,
        out_shape=(out, out),
        compiler_params=_params(("arbitrary",)),
        name="compress_sample",
    )(page_table, pool, *comp_w)


def _top_k_rows(score, blk, k):
    cand = score > -jnp.inf
    nblk = score.shape[0]
    for _ in range(k):
        m = jnp.max(score, axis=0, keepdims=True)
        idx = jnp.min(jnp.where(score == m, blk, nblk), axis=0, keepdims=True)
        score = jnp.where(blk == idx, -jnp.inf, score)
    return cand & (score == -jnp.inf)


def _add_q_bias(s, bias):
    return jnp.concatenate([s[:, g * Q_BLOCK:(g + 1) * Q_BLOCK] + bias for g in range(NSA_GROUP)], axis=1)


def _safe_inv(l):
    return 1.0 / jnp.maximum(l, 1e-30)


KT_SLC = 512
WIN_KEYS = WINDOW + Q_BLOCK
N_FORCED = 3


def _nsa_prompt_body(qn_ref, gt_ref, ksk_ref, ksvt_ref, kwk_ref, kwvt_ref, kc_ref, vc_ref, o_ref, selb_ref, s_ref):
    i = pl.program_id(1)
    q0 = i * Q_BLOCK
    qt = (qn_ref[...].astype(F32) * (SCALE * LOG2E)).T
    t_row = q0 + _iota((1, Q_BLOCK), 1)
    kc = kc_ref[0].astype(BF16)
    vct = vc_ref[0].T.astype(BF16)
    gt = gt_ref[...]
    zeros_q = jnp.zeros((HEAD_DIM, NSA_GROUP * Q_BLOCK), F32)

    mi = _iota((N_SLC, N_CMP), 0)
    mj = _iota((N_SLC, N_CMP), 1)
    msel = ((mj >= 4 * mi - 1) & (mj <= 4 * mi + 3) & (mj < N_CMP - 1)).astype(BF16)
    blk = _iota((N_SLC, Q_BLOCK), 0)
    cur = t_row // SLC_BLOCK
    forced = (blk == 0) | (blk == cur) | (blk == cur - 1)
    free = (blk * SLC_BLOCK <= t_row) & jnp.logical_not(forced)
    cend = _iota((N_CMP, 1), 0) * CMP_STRIDE + (CMP_LEN - 1)
    bias_c = jnp.where(cend <= t_row, 0.0, NEG_INF)
    wt = jnp.maximum(i - WINDOW // Q_BLOCK, 0)
    w0 = pl.multiple_of(wt * Q_BLOCK, Q_BLOCK)
    kpw = w0 + _iota((WIN_KEYS, 1), 0)
    bias_w = jnp.where((kpw <= t_row) & (kpw > t_row - WINDOW), 0.0, NEG_INF)
    krow = _iota((SLC_BLOCK, 1), 0)

    qgs, qps, o_cs, o_ws, cand_scores = [], [], [], [], []
    vrows = [slice(kvh * HEAD_DIM, (kvh + 1) * HEAD_DIM) for kvh in range(NSA_KV_HEADS)]
    for kvh in range(NSA_KV_HEADS):
        qg = jnp.concatenate([qt[(kvh * NSA_GROUP + g) * HEAD_DIM:(kvh * NSA_GROUP + g + 1) * HEAD_DIM, :]
                              for g in range(NSA_GROUP)], axis=1)
        qp = jnp.concatenate([qg, zeros_q] if kvh == 0 else [zeros_q, qg], axis=0).astype(BF16)
        vrow = vrows[kvh]
        qgs.append(qg.astype(BF16))
        qps.append(qp)

        s = _add_q_bias(_dot(kc, qp), bias_c)
        m = jnp.max(s, axis=0, keepdims=True)
        e = jnp.exp2(s - m)
        inv = jnp.where(m > 0.5 * NEG_INF, _safe_inv(jnp.sum(e, axis=0, keepdims=True)), 0.0)
        p = e * inv
        o_c = _dot(vct[vrow, :], p.astype(BF16))
        imp = (p[:, 0:Q_BLOCK] + p[:, Q_BLOCK:2 * Q_BLOCK]
               + p[:, 2 * Q_BLOCK:3 * Q_BLOCK] + p[:, 3 * Q_BLOCK:4 * Q_BLOCK])

        cand_scores.append(jnp.where(free, _dot_exact_rhs(msel, imp), -jnp.inf))
        o_cs.append(o_c)

    blk2 = jnp.concatenate([blk] * NSA_KV_HEADS, axis=1)
    picked = _top_k_rows(jnp.concatenate(cand_scores, axis=1), blk2, SLC_TOPK - N_FORCED)
    for kvh in range(NSA_KV_HEADS):
        selb_ref[kvh] = jnp.where(forced | picked[:, kvh * Q_BLOCK:(kvh + 1) * Q_BLOCK], 0.0, NEG_INF)

    for kvh in range(NSA_KV_HEADS):
        qp, vrow = qps[kvh], vrows[kvh]
        sw = _add_q_bias(_dot(kwk_ref[pl.ds(w0, WIN_KEYS), :], qp), bias_w)
        ew = jnp.exp2(sw - jnp.max(sw, axis=0, keepdims=True))
        pw = (ew * _safe_inv(jnp.sum(ew, axis=0, keepdims=True))).astype(BF16)
        o_w = jnp.zeros((HEAD_DIM, NSA_GROUP * Q_BLOCK), F32)
        for r in range(WIN_KEYS // Q_BLOCK):
            tix = wt + r
            vt = kwvt_ref[tix // 4, vrow, :]
            sub = tix % 4
            vpiece = jnp.where(sub == 0, vt[:, 0:128],
                               jnp.where(sub == 1, vt[:, 128:256],
                                         jnp.where(sub == 2, vt[:, 256:384], vt[:, 384:512])))
            o_w = o_w + _dot(vpiece, pw[r * Q_BLOCK:(r + 1) * Q_BLOCK, :])
        o_ws.append(o_w)

    bias_pad = jnp.zeros((HEAD_DIM - 8, NSA_GROUP * Q_BLOCK), F32)

    def scores(j, buf):
        k0 = pl.multiple_of(j * KT_SLC, KT_SLC)
        for kvh in range(NSA_KV_HEADS):
            b8 = selb_ref[kvh, pl.ds(pl.multiple_of(j * 8, 8), 8), :]
            b_rows = jnp.concatenate([jnp.concatenate([b8] * NSA_GROUP, axis=1), bias_pad], axis=0).astype(BF16)
            q_aug = jnp.concatenate([qgs[kvh], b_rows] if kvh == 0 else [b_rows, qgs[kvh]], axis=0)
            s_ref[buf, kvh] = _dot(ksk_ref[pl.ds(k0, KT_SLC), kvh * LANES:(kvh + 1) * LANES], q_aug)

    def consume(j, buf, state, causal):
        k0 = j * KT_SLC
        out = []
        for kvh in range(NSA_KV_HEADS):
            m_run, acc = state[kvh]
            sj = s_ref[buf, kvh]
            if causal:
                parts = []
                for r in range(KT_SLC // SLC_BLOCK):
                    ok = k0 + r * SLC_BLOCK + krow <= t_row
                    parts.append(jnp.concatenate(
                        [jnp.where(ok, sj[r * SLC_BLOCK:(r + 1) * SLC_BLOCK, g * Q_BLOCK:(g + 1) * Q_BLOCK], NEG_INF)
                         for g in range(NSA_GROUP)], axis=1))
                sj = jnp.concatenate(parts, axis=0)
            m_new = jnp.maximum(m_run, jnp.max(sj, axis=0, keepdims=True))
            alpha = jnp.exp2(m_run - m_new)
            ej = jnp.exp2(sj - m_new).astype(BF16)
            acc = alpha * acc + _dot(ksvt_ref[j, kvh * V_ROWS:(kvh + 1) * V_ROWS, :], ej)
            out.append((m_new, acc))
        return tuple(out)

    init1 = (jnp.full((1, NSA_GROUP * Q_BLOCK), NEG_INF, F32),
             jnp.zeros((V_ROWS, NSA_GROUP * Q_BLOCK), F32))
    n_tiles = q0 // KT_SLC + 1
    n_pairs = (n_tiles + 1) // 2

    def pair(jj, state):
        scores(2 * jj + 1, 1)
        state = consume(2 * jj, 0, state, False)
        scores(2 * jj + 2, 0)
        return consume(2 * jj + 1, 1, state, False)

    scores(0, 0)
    state = lax.fori_loop(0, n_pairs - 1, pair, (init1,) * NSA_KV_HEADS)
    last = 2 * n_pairs - 2

    def last_two(st):
        scores(last + 1, 1)
        return consume(last + 1, 1, consume(last, 0, st, True), True)

    state = lax.cond(n_tiles % 2 == 0, last_two, lambda st: consume(last, 0, st, True), state)

    outs = []
    for kvh in range(NSA_KV_HEADS):
        acc_s = state[kvh][1]
        o_s = acc_s[0:HEAD_DIM, :] * _safe_inv(acc_s[HEAD_DIM:HEAD_DIM + 1, :])

        def gate(br):
            base = br * NSA_HEADS + kvh * NSA_GROUP
            return jnp.concatenate([gt[base + g:base + g + 1, :] for g in range(NSA_GROUP)], axis=1)

        o = gate(0) * o_cs[kvh] + gate(1) * o_s + gate(2) * o_ws[kvh]
        outs.extend([o[:, g * Q_BLOCK:(g + 1) * Q_BLOCK] for g in range(NSA_GROUP)])
    o_ref[...] = jnp.concatenate(outs, axis=0).T


def _nsa_prompt(qn, gt, ksk, ksvt, kwk, kwvt, kc, vc, batch):
    nqb = SEQ // Q_BLOCK
    ntile = SEQ // 512
    return pl.pallas_call(
        _nsa_prompt_body,
        grid=(batch, nqb),
        in_specs=[pl.BlockSpec((Q_BLOCK, 512), lambda b, i: (b * nqb + i, 0)),
                  pl.BlockSpec((LANES, Q_BLOCK), lambda b, i: (0, b * nqb + i)),
                  pl.BlockSpec((SEQ, NSA_KV_HEADS * LANES), lambda b, i: (b, 0)),
                  pl.BlockSpec((ntile, NSA_KV_HEADS * V_ROWS, 512), lambda b, i: (b, 0, 0)),
                  pl.BlockSpec((SEQ, LANES), lambda b, i: (b, 0)),
                  pl.BlockSpec((ntile, LANES, 512), lambda b, i: (b, 0, 0)),
                  pl.BlockSpec((1, N_CMP, LANES), lambda b, i: (b, 0, 0)),
                  pl.BlockSpec((1, N_CMP, LANES), lambda b, i: (b, 0, 0))],
        out_specs=pl.BlockSpec((Q_BLOCK, 512), lambda b, i: (b * nqb + i, 0)),
        out_shape=jax.ShapeDtypeStruct((batch * SEQ, 512), F32),
        scratch_shapes=[pltpu.VMEM((NSA_KV_HEADS, N_SLC, Q_BLOCK), F32),
                        pltpu.VMEM((2, NSA_KV_HEADS, KT_SLC, NSA_GROUP * Q_BLOCK), F32)],
        compiler_params=_params(("arbitrary", "arbitrary")),
        name="nsa_prompt",
    )(qn, gt, ksk, ksvt, kwk, kwvt, kc, vc)


def _softmax_parts(parts, masks):
    m = None
    for s, mk in zip(parts, masks):
        mi = jnp.max(jnp.where(mk, s, NEG_INF), axis=-1, keepdims=True)
        m = mi if m is None else jnp.maximum(m, mi)
    es = [jnp.where(mk, jnp.exp(jnp.where(mk, s, NEG_INF) - m), 0.0) for s, mk in zip(parts, masks)]
    l = None
    for e in es:
        li = jnp.sum(e, axis=-1, keepdims=True)
        l = li if l is None else l + li
    inv = 1.0 / jnp.maximum(l, 1e-30)
    return [e * inv for e in es]


KC_SLC = 2048
ROWS_S = NSA_HEADS * DEC_SEQ


def _pad_rows(x, n):
    if x.shape[0] == n:
        return x
    return jnp.concatenate([x, jnp.zeros((n - x.shape[0], x.shape[1]), x.dtype)], axis=0)


def _nsa_sample_body(pt_ref, pool_ref, qp_ref, g_ref, kc_ref, vc_ref, kvs_ref, kvw_ref, win_ref, wtail_ref,
                     expand_ref, o_ref, nwin_ref, slab_ref, sem_ref):
    refs = (qp_ref, g_ref, kc_ref, vc_ref, kvs_ref, kvw_ref, win_ref, wtail_ref, expand_ref, o_ref, nwin_ref, slab_ref)
    _slab_substeps(pt_ref, pool_ref, slab_ref, sem_ref, lambda slot: _nsa_sample_seqs(slot, *refs))


def _nsa_sample_seqs(slot, qp_ref, g_ref, kc_ref, vc_ref, kvs_ref, kvw_ref, win_ref, wtail_ref, expand_ref,
                     o_ref, nwin_ref, slab_ref):
    spb = slab_ref.shape[1]
    seqs = range(spb)
    bi = [slot * spb + s for s in seqs]
    qi = _iota((ROWS_S, 1), 0) % DEC_SEQ
    t = PAST_LEN + qi
    jn = _iota((1, LANES), 1)
    n_chunk = PAST_LEN // KC_SLC
    qs = [(qp_ref[bi[s]] * SCALE).astype(BF16) for s in seqs]
    new_rows = [slice(bi[s] * DEC_SEQ, (bi[s] + 1) * DEC_SEQ) for s in seqs]

    cend = _iota((1, N_CMP), 1) * CMP_STRIDE + (CMP_LEN - 1)
    p_cs = [_softmax_parts([_dot_nt(qs[s], kc_ref[bi[s]].astype(BF16))], [cend <= t])[0] for s in seqs]
    o_cs = [_dot(p_cs[s].astype(BF16), vc_ref[bi[s]].astype(BF16)) for s in seqs]

    half = ROWS_S // NSA_KV_HEADS
    imp = jnp.concatenate(
        [p_cs[s][k * half:k * half + 8] + p_cs[s][k * half + 8:k * half + 16]
         + p_cs[s][k * half + 16:k * half + 24] + p_cs[s][k * half + 24:k * half + 32]
         for s in seqs for k in range(NSA_KV_HEADS)], axis=0)
    mi = _iota((N_CMP, N_SLC), 0)
    mj = _iota((N_CMP, N_SLC), 1)
    msel_t = ((mi >= 4 * mj - 1) & (mi <= 4 * mj + 3) & (mi < N_CMP - 1)).astype(BF16)
    score = _dot_exact_lhs(imp, msel_t)
    score_t = _pad_rows(score, LANES).T
    blk = _iota((N_SLC, LANES), 0)
    forced = (blk == 0) | (blk == N_SLC - 1)
    picked = _top_k_rows(jnp.where(forced, -jnp.inf, score_t), blk, SLC_TOPK - 1 - 2)
    unsel = jnp.where(forced | picked, 0.0, 1.0).T
    unsel64 = [jnp.concatenate([unsel[s * 16:s * 16 + 8]] * NSA_GROUP + [unsel[s * 16 + 8:s * 16 + 16]] * NSA_GROUP,
                               axis=0).astype(BF16) for s in seqs]

    new_bias = jnp.where((jn <= qi) & (jn < DEC_SEQ), 0.0, NEG_INF)
    k_news = [_pad_rows(kvs_ref[new_rows[s], 0:LANES], LANES).astype(BF16) for s in seqs]
    v_news = [_pad_rows(kvs_ref[new_rows[s], LANES:2 * LANES], LANES).astype(BF16) for s in seqs]
    s_news = [_dot_nt(qs[s], k_news[s]) + new_bias for s in seqs]
    scs = [[_dot(qs[s], slab_ref[slot, s, 0, :, c * KC_SLC:(c + 1) * KC_SLC].astype(BF16))
            + _dot(unsel64[s], expand_ref[:, c * KC_SLC:(c + 1) * KC_SLC]) * NEG_INF
            for c in range(n_chunk)] for s in seqs]
    o_ss = []
    for s in seqs:
        m_s = jnp.max(s_news[s], axis=-1, keepdims=True)
        for sc in scs[s]:
            m_s = jnp.maximum(m_s, jnp.max(sc, axis=-1, keepdims=True))
        e_new = jnp.exp(s_news[s] - m_s)
        l_s = jnp.sum(e_new, axis=-1, keepdims=True)
        acc = _dot(e_new.astype(BF16), v_news[s])
        for c in range(n_chunk):
            ec = jnp.exp(scs[s][c] - m_s)
            l_s = l_s + jnp.sum(ec, axis=-1, keepdims=True)
            acc = acc + _dot_nt(ec.astype(BF16), slab_ref[slot, s, 1, :, c * KC_SLC:(c + 1) * KC_SLC].astype(BF16))
        o_ss.append(acc * _safe_inv(l_s))

    wb = win_ref.shape[2]
    kpos_b = PAST_LEN - wb + _iota((1, wb), 1)
    kpos_n = PAST_LEN + jn
    mask_b = (kpos_b <= t) & (kpos_b > t - WINDOW) & (kpos_b >= 0)
    mask_n = (kpos_n <= t) & (kpos_n > t - WINDOW) & (jn < DEC_SEQ)
    for s in seqs:
        win_t = win_ref[bi[s]]
        kw_new = _pad_rows(kvw_ref[new_rows[s], 0:LANES], LANES).astype(BF16)
        vw_new = _pad_rows(kvw_ref[new_rows[s], LANES:2 * LANES], LANES).astype(BF16)
        p_b, p_n = _softmax_parts([_dot(qs[s], win_t[0:LANES, :].astype(BF16)), _dot_nt(qs[s], kw_new)],
                                  [mask_b, mask_n])
        o_w = _dot_nt(p_b.astype(BF16), win_t[LANES:2 * LANES, :].astype(BF16)) + _dot(p_n.astype(BF16), vw_new)

        g = g_ref[bi[s]]
        o_ref[bi[s]] = g[:, 0:1] * o_cs[s] + g[:, 1:2] * o_ss[s] + g[:, 2:3] * o_w
        shifted = pltpu.roll(win_t, wb - DEC_SEQ, 1)
        nwin_ref[bi[s], :, 0:wb - LANES] = shifted[:, 0:wb - LANES]
        nwin_ref[bi[s], :, wb - LANES:wb] = jnp.where(jn >= LANES - DEC_SEQ, wtail_ref[bi[s]],
                                                      shifted[:, wb - LANES:wb])


NSA_SAMPLE_SEQS_PER_STEP = 2


def _nsa_sample(page_table, pool_slc, qp, gates, kc, vc, kvs, kvw, win_t, wtail, expand):
    nb = page_table.shape[0]
    wb = win_t.shape[2]
    spb = NSA_SAMPLE_SEQS_PER_STEP
    sps = spb * SLAB_SLOTS
    per_b = lambda *shape: pl.BlockSpec((sps,) + shape, lambda i, pt: (i,) + (0,) * len(shape))
    rows8 = pl.BlockSpec((sps * DEC_SEQ, 256), lambda i, pt: (i, 0))
    grid_spec = pltpu.PrefetchScalarGridSpec(
        num_scalar_prefetch=1,
        grid=(nb // sps,),
        in_specs=[pl.BlockSpec(memory_space=pl.ANY),
                  per_b(ROWS_S, LANES), per_b(ROWS_S, LANES), per_b(N_CMP, LANES), per_b(N_CMP, LANES),
                  rows8, rows8, per_b(256, wb), per_b(256, LANES),
                  pl.BlockSpec((N_SLC, PAST_LEN), lambda i, pt: (0, 0))],
        out_specs=(per_b(ROWS_S, LANES), per_b(256, wb)),
        scratch_shapes=_slab_scratch(spb),
    )
    return pl.pallas_call(
        _nsa_sample_body,
        grid_spec=grid_spec,
        out_shape=(jax.ShapeDtypeStruct((nb, ROWS_S, LANES), F32),
                   jax.ShapeDtypeStruct((nb, 256, wb), F32)),
        compiler_params=_params(("arbitrary",)),
        name="nsa_sample",
    )(page_table, pool_slc, qp, gates, kc, vc, kvs, kvw, win_t, wtail, expand)


RET_W = RET_HEADS * HEAD_DIM


def _retention_body(q_ref, k_ref, v_ref, cos_ref, sin_ref, st0_ref, gn_ref, o_ref, st_ref, state_ref, *, chunk):
    c = pl.program_id(1)
    ck = max(chunk, LANES)
    spb = st0_ref.shape[0]

    @pl.when(c == 0)
    def _():
        state_ref[...] = st0_ref[...]

    lane = _iota((1, RET_W), 1)
    head_of_lane = lane // HEAD_DIM
    low_half = (lane % HEAD_DIM) < HEAD_DIM // 2
    cos = cos_ref[...]
    sin = sin_ref[...]

    def rope(x):
        rot = jnp.where(low_half, -pltpu.roll(x, RET_W - HEAD_DIM // 2, 1), pltpu.roll(x, HEAD_DIM // 2, 1))
        return x * cos + rot * sin

    logg_lane = jnp.zeros((1, RET_W), F32)
    for h in range(RET_HEADS):
        logg_lane = jnp.where(head_of_lane == h, _LOG_G[h], logg_lane)

    n = _iota((chunk, 1), 0).astype(F32)
    nk = _iota((ck, 1), 0).astype(F32)
    diff = n - _iota((1, ck), 1).astype(F32)
    in_chunk = _iota((1, ck), 1) < chunk
    dmats = [jnp.where((diff >= 0) & in_chunk, jnp.exp(_LOG_G[h] * jnp.maximum(diff, 0.0)), 0.0)
             for h in range(RET_HEADS)]
    xi = jnp.exp(logg_lane * (n + 1.0))
    zeta = jnp.where(nk < chunk, jnp.exp(logg_lane * (chunk - 1.0 - nk)), 0.0)
    row_head = _iota((RET_W, 1), 0) // HEAD_DIM
    decay_rows = jnp.zeros((RET_W, 1), F32)
    for h in range(RET_HEADS):
        decay_rows = jnp.where(row_head == h, float(np.exp(_LOG_G[h] * chunk)), decay_rows)
    avg = jnp.where(_iota((RET_W, RET_W), 0) // HEAD_DIM == _iota((RET_W, RET_W), 1) // HEAD_DIM,
                    1.0 / HEAD_DIM, 0.0).astype(BF16)
    gn = gn_ref[...]

    seqs = range(spb)
    rows = [slice(bb * chunk, (bb + 1) * chunk) for bb in seqs]
    qs = [rope(q_ref[r, :]) * SCALE for r in rows]
    kps = [_pad_rows(rope(k_ref[r, :]), ck) for r in rows]
    vps = [_pad_rows(v_ref[r, :], ck) for r in rows]
    hms = [head_of_lane == h for h in range(RET_HEADS)]
    ss = [[_dot_nt(jnp.where(hms[h], qs[bb], 0.0), kps[bb]) * dmats[h] for h in range(RET_HEADS)] for bb in seqs]
    inners = []
    for bb in seqs:
        inner = jnp.zeros((chunk, RET_W), F32)
        for h in range(RET_HEADS):
            inner = inner + jnp.where(hms[h], _dot(ss[bb][h], vps[bb]), 0.0)
        inners.append(inner)
    states = [state_ref[bb] for bb in seqs]
    os_ = [inners[bb] + _dot(qs[bb] * xi, states[bb]) for bb in seqs]
    for bb in seqs:
        kz_t = (kps[bb] * zeta).T
        new_state = decay_rows * states[bb] + jnp.where(row_head == head_of_lane, _dot(kz_t, vps[bb]), 0.0)
        state_ref[bb] = new_state
        st_ref[bb] = new_state

    o = jnp.concatenate(os_, axis=0) if spb > 1 else os_[0]
    mu = _dot_exact_lhs(o, avg)
    d = o - mu
    var = _dot_exact_lhs(d * d, avg)
    o_ref[...] = d * lax.rsqrt(var + EPS) * gn


def _retention(qkvr, cos, sin, state_bd, gn, batch, t_len, chunk, seqs_per_step=1):
    nch = t_len // chunk
    spb = seqs_per_step
    assert spb == 1 or nch == 1
    col = lambda j: pl.BlockSpec((spb * chunk, RET_W), lambda b, c: (b * nch + c, j))
    tab = pl.BlockSpec((chunk, RET_W), lambda b, c: (c, 0))
    st = pl.BlockSpec((spb, RET_W, RET_W), lambda b, c: (b, 0, 0))
    return pl.pallas_call(
        functools.partial(_retention_body, chunk=chunk),
        grid=(batch // spb, nch),
        in_specs=[col(0), col(1), col(2), tab, tab, st, pl.BlockSpec((1, RET_W), lambda b, c: (0, 0))],
        out_specs=(pl.BlockSpec((spb * chunk, RET_W), lambda b, c: (b * nch + c, 0)), st),
        out_shape=(jax.ShapeDtypeStruct((batch * t_len, RET_W), F32),
                   jax.ShapeDtypeStruct((batch, RET_W, RET_W), F32)),
        scratch_shapes=[pltpu.VMEM((spb, RET_W, RET_W), F32)],
        compiler_params=_params(("arbitrary", "arbitrary")),
        name="retention",
    )(qkvr, qkvr, qkvr, cos, sin, state_bd, gn)


MEM_W = 4 * HEAD_DIM


def _mem_attn_body(q_ref, mkv_ref, o_ref, *, tm):
    head_of_lane = _iota((1, MEM_W), 1) // HEAD_DIM
    spb = mkv_ref.shape[0]
    q_all = q_ref[...] * SCALE
    scores = []
    for bb in range(spb):
        q = q_all[bb * tm:(bb + 1) * tm, :]
        q4 = jnp.concatenate([jnp.where(head_of_lane == h, q, 0.0) for h in range(4)], axis=0).astype(BF16)
        scores.append(_dot(q4, mkv_ref[bb, 0:MEM_W, :].astype(BF16)))
    s = jnp.concatenate(scores, axis=0)
    e = jnp.exp(s - jnp.max(s, axis=-1, keepdims=True))
    p = (e / jnp.sum(e, axis=-1, keepdims=True)).astype(BF16)
    for bb in range(spb):
        pv = _dot_nt(p[bb * 4 * tm:(bb + 1) * 4 * tm, :], mkv_ref[bb, MEM_W:2 * MEM_W, :].astype(BF16))
        o = jnp.zeros((tm, MEM_W), F32)
        for h in range(4):
            o = o + jnp.where(head_of_lane == h, pv[h * tm:(h + 1) * tm, :], 0.0)
        o_ref[bb * tm:(bb + 1) * tm, :] = o


def _mem_attn(qm, mkv_t, batch, rows_per_batch, tm, seqs_per_step=1):
    nt = rows_per_batch // tm
    spb = seqs_per_step
    assert spb == 1 or nt == 1
    return pl.pallas_call(
        functools.partial(_mem_attn_body, tm=tm),
        grid=(batch // spb, nt),
        in_specs=[pl.BlockSpec((spb * tm, MEM_W), lambda b, i: (b * nt + i, 0)),
                  pl.BlockSpec((spb, 2 * MEM_W, N_MEM), lambda b, i: (b, 0, 0))],
        out_specs=pl.BlockSpec((spb * tm, MEM_W), lambda b, i: (b * nt + i, 0)),
        out_shape=jax.ShapeDtypeStruct((batch * rows_per_batch, MEM_W), F32),
        compiler_params=_params(("arbitrary", "arbitrary")),
        name="mem_attn",
    )(qm, mkv_t)


def _finish_body(x_ref, on_ref, or_ref, om_ref, z_ref, w_ref, g_ref, y_ref):
    o = jnp.concatenate([on_ref[...], or_ref[...], om_ref[...]], axis=-1)
    z = z_ref[...]
    mix = o * (z * jax.nn.sigmoid(z))
    xo = x_ref[...] + _dot(mix.astype(BF16), w_ref[...])
    y_ref[...] = xo * lax.rsqrt(jnp.mean(xo * xo, axis=-1, keepdims=True) + EPS) * g_ref[...]


def _finish(x2d, o_n, o_r, o_m, z, w_out, ln_final):
    n = x2d.shape[0]
    tm = 512
    row = lambda w: pl.BlockSpec((tm, w), lambda i: (i, 0))
    return pl.pallas_call(
        _finish_body,
        grid=(n // tm,),
        in_specs=[row(D_MODEL), row(512), row(256), row(256), row(D_MODEL),
                  pl.BlockSpec((D_MODEL, D_MODEL), lambda i: (0, 0)),
                  pl.BlockSpec((1, D_MODEL), lambda i: (0, 0))],
        out_specs=row(D_MODEL),
        out_shape=jax.ShapeDtypeStruct((n, D_MODEL), F32),
        compiler_params=_params(("arbitrary",)),
        name="finish",
    )(x2d, o_n, o_r, o_m, z, w_out, ln_final)


def _permute_w_in(w):
    sizes = (512, 256, 256, 256, 24, 512, 256, 256, 256, 256, 256, 256)
    offs = np.concatenate([[0], np.cumsum(sizes)])
    part = lambda i: w[:, int(offs[i]):int(offs[i + 1])]
    q_n, kv_c, kv_s, kv_w, g_n, z_n, q_r, k_r, v_r, z_r, q_m, z_m = [part(i) for i in range(12)]
    g_pad = jnp.pad(g_n, ((0, 0), (0, LANES - g_n.shape[1])))
    return jnp.concatenate([q_n, kv_c, kv_s, kv_w, q_r, k_r, v_r, q_m, z_n, z_r, z_m, g_pad], axis=1).astype(BF16)


def _compress_weights(pe, w1, b1, w2):
    npair = CMP_STRIDE // 2
    pe_r = pe.reshape(2, 2, npair, 2, HEAD_DIM)
    pe_l = jnp.broadcast_to(pe_r[:, :, :, :, None, :], (2, 2, npair, 2, NSA_KV_HEADS, HEAD_DIM))
    pe_l = pe_l.reshape(2, 2, npair, 2 * LANES).transpose(0, 2, 1, 3)
    pe_l = jnp.pad(pe_l, ((0, 0), (0, 0), (0, PE_ROWS - 2), (0, 0))).astype(BF16)
    w1_r = w1.reshape(2, 2, CMP_STRIDE, HEAD_DIM, CMP_HID)
    zw = jnp.zeros_like(w1_r)
    w1_bd = jnp.concatenate([jnp.concatenate([w1_r, zw], axis=-1),
                             jnp.concatenate([zw, w1_r], axis=-1)], axis=-2)
    w1_bd = w1_bd.reshape(2, 2, npair, 2 * LANES, 2 * CMP_HID)
    w1_bd = jnp.concatenate([w1_bd[:, 0], w1_bd[:, 1]], axis=-1).astype(BF16)
    b1_l = jnp.concatenate([b1, b1], axis=-1).reshape(2, 1, 2 * CMP_HID)
    z2 = jnp.zeros_like(w2)
    w2_bd = jnp.concatenate([jnp.concatenate([w2, z2], axis=-1),
                             jnp.concatenate([z2, w2], axis=-1)], axis=-2).astype(BF16)
    return pe_l, w1_bd, b1_l, w2_bd


def _rope_tables(pos):
    half = HEAD_DIM // 2
    inv = ROPE_BASE ** (-jnp.arange(half, dtype=F32) / half)
    ang = pos.astype(F32)[:, None] * inv[None, :]
    cos, sin = jnp.cos(ang), jnp.sin(ang)
    cos_l = jnp.tile(jnp.concatenate([cos, cos], axis=-1), (1, RET_HEADS))
    sin_l = jnp.tile(jnp.concatenate([sin, sin], axis=-1), (1, RET_HEADS))
    return cos_l, sin_l


def _block_diag_state(st):
    b = st.shape[0]
    eye = jnp.eye(RET_HEADS, dtype=st.dtype)
    return jnp.einsum("bhde,hg->bhdge", st, eye).reshape(b, RET_W, RET_W)


def _diag_blocks(st_bd):
    b = st_bd.shape[0]
    r = st_bd.reshape(b, RET_HEADS, HEAD_DIM, RET_HEADS, HEAD_DIM)
    return jnp.stack([r[:, h, :, h, :] for h in range(RET_HEADS)], axis=1)


def _kv_shape(a, b, t):
    return a.reshape(1, b, t, 2, NSA_KV_HEADS, HEAD_DIM)


def _position_minor(cache, heads):
    b, l = cache.shape[:2]
    return jnp.transpose(cache, (0, 2, 3, 4, 1)).reshape(b, 2 * heads * HEAD_DIM, l)


def _from_position_minor(a, heads):
    b, _, l = a.shape
    return jnp.transpose(a.reshape(b, 2, heads, HEAD_DIM, l), (0, 4, 1, 2, 3))[None]


def kernel(x_prompt, mem_prompt, x_sample, cache_nsa_cmp, cache_nsa_slc, cache_nsa_win, state_ret, cache_mem,
           page_table, ln_mix, w_in, cmp_pe, cmp_w1, cmp_b1, cmp_w2, ret_gn, ln_mem, w_mem_kv, w_out, ln_final):
    bp, t_len = x_prompt.shape[:2]
    bs, s_len = x_sample.shape[:2]
    assert (t_len, s_len) == (SEQ, DEC_SEQ) and ln_mix.shape[0] == 1
    w_perm = _permute_w_in(w_in[0])
    comp_w = _compress_weights(cmp_pe[0], cmp_w1[0], cmp_b1[0], cmp_w2[0])
    ln_g = ln_mix[0].reshape(1, D_MODEL)
    gn = ret_gn[0].reshape(1, RET_W)
    w_out_b = w_out[0].astype(BF16)
    ln_f = ln_final.reshape(1, D_MODEL)

    xp2 = x_prompt.reshape(bp * SEQ, D_MODEL)
    (qn, kvc_t, kvs_t, kvw_t, qkvr, qm, z, gt, ksk, ksvt, kwk, kwvt) = _inproj(xp2, ln_g, w_perm, SEQ, True)
    kc, vc = _compress_prompt(kvc_t, comp_w)
    o_n = _nsa_prompt(qn, gt, ksk, ksvt, kwk, kwvt, kc, vc, bp)
    cos_p, sin_p = _rope_tables(jnp.arange(SEQ))
    o_r, st_p = _retention(qkvr, cos_p, sin_p, jnp.zeros((bp, RET_W, RET_W), F32), gn, bp, SEQ, 256)
    mkv_t = _mem_kv(mem_prompt, ln_mem[0].reshape(1, D_MODEL), w_mem_kv[0].astype(BF16))
    o_m = _mem_attn(qm, mkv_t, bp, SEQ, 512)
    y_prompt = _finish(xp2, o_n, o_r, o_m, z, w_out_b, ln_f).reshape(bp, SEQ, D_MODEL)
    new_cmp_p = _from_position_minor(kvc_t, NSA_KV_HEADS)
    new_slc_p = _from_position_minor(kvs_t, NSA_KV_HEADS)
    new_win_p = _from_position_minor(kvw_t[:, :, SEQ - WINDOW:], NSA_KV_HEADS)
    new_ret_p = _diag_blocks(st_p)[None]
    new_mem_p = _from_position_minor(mkv_t, 4)

    xs2 = x_sample.reshape(bs * DEC_SEQ, D_MODEL)
    (qn_s, kvc_s, kvs_s, kvw_s, qkvr_s, qm_s, z_s, gt_s) = _inproj(xs2, ln_g, w_perm, DEC_SEQ, False)
    pool_cmp = _position_minor(cache_nsa_cmp[0], NSA_KV_HEADS).reshape(-1, 2, LANES, PAGE_SIZE)
    pool_slc = _position_minor(cache_nsa_slc[0], NSA_KV_HEADS).reshape(-1, 2, LANES, PAGE_SIZE)
    kc_s, vc_s = _compress_sample(page_table, pool_cmp, comp_w)
    q5 = qn_s.astype(F32).reshape(bs, DEC_SEQ, NSA_KV_HEADS, NSA_GROUP, HEAD_DIM).transpose(0, 2, 3, 1, 4)
    zq = jnp.zeros_like(q5[:, 0])
    qp = jnp.stack([jnp.concatenate([q5[:, 0], zq], axis=-1), jnp.concatenate([zq, q5[:, 1]], axis=-1)], axis=1)
    qp = qp.reshape(bs, ROWS_S, LANES)
    g5 = gt_s[:3 * NSA_HEADS].reshape(3, NSA_KV_HEADS, NSA_GROUP, bs, DEC_SEQ).transpose(3, 1, 2, 4, 0)
    gates = jnp.pad(g5.reshape(bs, ROWS_S, 3), ((0, 0), (0, 0), (0, LANES - 3)))
    expand = jnp.asarray(np.repeat(np.eye(N_SLC, dtype=np.float32), SLC_BLOCK, axis=1), dtype=BF16)
    win_t = _position_minor(cache_nsa_win[0], NSA_KV_HEADS)
    wtail = jnp.pad(kvw_s.reshape(bs, DEC_SEQ, 256).transpose(0, 2, 1), ((0, 0), (0, 0), (LANES - DEC_SEQ, 0)))
    o_sn, new_win_t = _nsa_sample(page_table, pool_slc, qp, gates, kc_s, vc_s, kvs_s, kvw_s, win_t, wtail, expand)
    o6 = o_sn.reshape(bs, NSA_KV_HEADS, NSA_GROUP, DEC_SEQ, NSA_KV_HEADS, HEAD_DIM)
    o_n_s = jnp.stack([o6[:, 0, :, :, 0], o6[:, 1, :, :, 1]], axis=1)
    o_n_s = o_n_s.transpose(0, 3, 1, 2, 4).reshape(bs * DEC_SEQ, 512)
    cos_s, sin_s = _rope_tables(PAST_LEN + jnp.arange(DEC_SEQ))
    o_r_s, st_s = _retention(qkvr_s, cos_s, sin_s, _block_diag_state(state_ret[0]), gn, bs, DEC_SEQ, DEC_SEQ,
                             seqs_per_step=SAMPLE_SEQS_PER_STEP)
    o_m_s = _mem_attn(qm_s, _position_minor(cache_mem[0], 4), bs, DEC_SEQ, DEC_SEQ,
                      seqs_per_step=SAMPLE_SEQS_PER_STEP)
    y_sample = _finish(xs2, o_n_s, o_r_s, o_m_s, z_s, w_out_b, ln_f).reshape(bs, DEC_SEQ, D_MODEL)
    new_cmp_s = _kv_shape(kvc_s, bs, DEC_SEQ)
    new_slc_s = _kv_shape(kvs_s, bs, DEC_SEQ)
    new_win_s = _from_position_minor(new_win_t, NSA_KV_HEADS)
    new_ret_s = _diag_blocks(st_s)[None]

    return (y_prompt, y_sample, new_cmp_p, new_cmp_s, new_slc_p, new_slc_s, new_win_p, new_win_s,
            new_ret_p, new_ret_s, new_mem_p)
```

```python
import functools

import numpy as np
import jax
import jax.numpy as jnp
from jax import lax
from jax.experimental import pallas as pl
from jax.experimental.pallas import tpu as pltpu

D_MODEL = 1024
SEQ = 8192
DEC_SEQ = 8
PAST_LEN = 8192
PAGE_SIZE = 128
HEAD_DIM = 64
NSA_HEADS = 8
NSA_KV_HEADS = 2
NSA_GROUP = 4
CMP_LEN = 32
CMP_STRIDE = 16
CMP_HID = 128
SLC_BLOCK = 64
SLC_TOPK = 16
WINDOW = 512
RET_HEADS = 4
N_MEM = 256
Q_BLOCK = 128
ROPE_BASE = 10000.0
EPS = 1e-6
NEG_INF = -1e30
SCALE = HEAD_DIM ** -0.5

N_CMP = 512
N_SLC = 128
LANES = 128
ONES_ROWS = 16
V_ROWS = HEAD_DIM + ONES_ROWS
LOG2E = 1.4426950408889634
SAMPLE_SEQS_PER_STEP = 8
VMEM_LIMIT = 56 * 1024 * 1024

_W_Q, _W_KVC, _W_KVS, _W_KVW, _W_QKVR, _W_QM, _W_Z, _W_G = 0, 512, 768, 1024, 1280, 2048, 2304, 3328
IN_WP = 3456

F32 = jnp.float32
BF16 = jnp.bfloat16

_LOG_G = [float(np.log1p(-(2.0 ** (-5.0 - h)))) for h in range(RET_HEADS)]


def _dot(a, b):
    return jnp.dot(a, b, preferred_element_type=F32)


def _dot_nt(a, b):
    return lax.dot_general(a, b, (((1,), (1,)), ((), ())), preferred_element_type=F32)


def _split3(x):
    hi = x.astype(BF16)
    r1 = x - hi.astype(F32)
    mid = r1.astype(BF16)
    lo = (r1 - mid.astype(F32)).astype(BF16)
    return hi, mid, lo


def _dot_exact_rhs(a_bf16, x):
    hi, mid, lo = _split3(x)
    return _dot(a_bf16, hi) + _dot(a_bf16, mid) + _dot(a_bf16, lo)


def _dot_exact_lhs(x, b_bf16):
    hi, mid, lo = _split3(x)
    return _dot(hi, b_bf16) + _dot(mid, b_bf16) + _dot(lo, b_bf16)


def _iota(shape, dim):
    return lax.broadcasted_iota(jnp.int32, shape, dim)


def _params(sem):
    return pltpu.CompilerParams(dimension_semantics=sem, vmem_limit_bytes=VMEM_LIMIT)


def _inproj_body(x_ref, g_ref, w_ref, qn_ref, kvc_ref, kvs_ref, kvw_ref, qkvr_ref, qm_ref, z_ref, gt_ref,
                 *attn_refs, transposed_kv):
    x = x_ref[...]
    xn = x * lax.rsqrt(jnp.mean(x * x, axis=-1, keepdims=True) + EPS) * g_ref[...]
    xb = xn.astype(BF16)

    def proj(a, b):
        return _dot(xb, w_ref[:, a:b])

    qn_ref[...] = proj(_W_Q, _W_KVC).astype(BF16)
    kvc = proj(_W_KVC, _W_KVS)
    kvs = proj(_W_KVS, _W_KVW)
    kvw = proj(_W_KVW, _W_QKVR)
    if transposed_kv:
        ksk_ref, ksvt_ref, kwk_ref, kwvt_ref = attn_refs
        kvc_ref[0] = kvc.T
        kvs_t = kvs.T
        kvs_ref[0] = kvs_t
        tm = kvs.shape[0]
        lane = _iota((tm, LANES), 1)
        blk_in_tile = _iota((tm, LANES), 0) // SLC_BLOCK
        k2 = kvs[:, :LANES]
        ksk_ref[:, 0:LANES] = jnp.where(lane < HEAD_DIM, k2,
                                        jnp.where(lane - HEAD_DIM == blk_in_tile, 1.0, 0.0)).astype(BF16)
        ksk_ref[:, LANES:2 * LANES] = jnp.where(lane >= HEAD_DIM, k2,
                                                jnp.where(lane == blk_in_tile, 1.0, 0.0)).astype(BF16)
        ones_rows = jnp.where(_iota((ONES_ROWS, tm), 0) == 0, 1.0, 0.0)
        ksvt_ref[0] = jnp.concatenate([kvs_t[LANES:LANES + HEAD_DIM, :], ones_rows,
                                       kvs_t[LANES + HEAD_DIM:, :], ones_rows], axis=0).astype(BF16)
        kvw_t = kvw.T
        kvw_ref[0] = kvw_t
        kwk_ref[...] = kvw[:, :LANES].astype(BF16)
        kwvt_ref[0] = kvw_t[LANES:, :].astype(BF16)
    else:
        kvc_ref[...] = kvc
        kvs_ref[...] = kvs
        kvw_ref[...] = kvw
    qkvr_ref[...] = proj(_W_QKVR, _W_QM)
    qm_ref[...] = proj(_W_QM, _W_Z)
    z_ref[...] = proj(_W_Z, _W_G)
    gt_ref[...] = jax.nn.sigmoid(proj(_W_G, IN_WP)).T


def _inproj(x2d, ln_g, w_perm, rows_per_batch, transposed_kv):
    n = x2d.shape[0]
    tm = 512
    nt = n // tm
    row = lambda w: pl.BlockSpec((tm, w), lambda i: (i, 0))
    if transposed_kv:
        tpb = rows_per_batch // tm
        kv_shape = jax.ShapeDtypeStruct((n // rows_per_batch, 256, rows_per_batch), F32)
        kv_spec = pl.BlockSpec((1, 256, tm), lambda i: (i // tpb, 0, i % tpb))
    else:
        kv_shape = jax.ShapeDtypeStruct((n, 256), F32)
        kv_spec = row(256)
    out_shape = [
        jax.ShapeDtypeStruct((n, 512), BF16),
        kv_shape, kv_shape, kv_shape,
        jax.ShapeDtypeStruct((n, 768), F32),
        jax.ShapeDtypeStruct((n, 256), F32),
        jax.ShapeDtypeStruct((n, 1024), F32),
        jax.ShapeDtypeStruct((LANES, n), F32),
    ]
    out_specs = [row(512), kv_spec, kv_spec, kv_spec, row(768), row(256), row(1024),
                 pl.BlockSpec((LANES, tm), lambda i: (0, i))]
    if transposed_kv:
        tile_t = pl.BlockSpec((1, LANES, tm), lambda i: (i, 0, 0))
        tile_v = pl.BlockSpec((1, NSA_KV_HEADS * V_ROWS, tm), lambda i: (i, 0, 0))
        out_shape += [jax.ShapeDtypeStruct((n, NSA_KV_HEADS * LANES), BF16),
                      jax.ShapeDtypeStruct((nt, NSA_KV_HEADS * V_ROWS, tm), BF16),
                      jax.ShapeDtypeStruct((n, LANES), BF16),
                      jax.ShapeDtypeStruct((nt, LANES, tm), BF16)]
        out_specs += [row(NSA_KV_HEADS * LANES), tile_v, row(LANES), tile_t]
    return pl.pallas_call(
        functools.partial(_inproj_body, transposed_kv=transposed_kv),
        grid=(nt,),
        in_specs=[row(D_MODEL),
                  pl.BlockSpec((1, D_MODEL), lambda i: (0, 0)),
                  pl.BlockSpec((D_MODEL, IN_WP), lambda i: (0, 0))],
        out_specs=tuple(out_specs),
        out_shape=tuple(out_shape),
        compiler_params=_params(("arbitrary",)),
        name="inproj",
    )(x2d, ln_g, w_perm)


def _mem_kv_body(x_ref, g_ref, w_ref, o_ref):
    x = x_ref[0]
    xn = x * lax.rsqrt(jnp.mean(x * x, axis=-1, keepdims=True) + EPS) * g_ref[...]
    o_ref[0] = _dot(xn.astype(BF16), w_ref[...]).T


def _mem_kv(mem, ln_g, w_bf16):
    b = mem.shape[0]
    nw = w_bf16.shape[1]
    return pl.pallas_call(
        _mem_kv_body,
        grid=(b,),
        in_specs=[pl.BlockSpec((1, N_MEM, D_MODEL), lambda i: (i, 0, 0)),
                  pl.BlockSpec((1, D_MODEL), lambda i: (0, 0)),
                  pl.BlockSpec((D_MODEL, nw), lambda i: (0, 0))],
        out_specs=pl.BlockSpec((1, nw, N_MEM), lambda i: (i, 0, 0)),
        out_shape=jax.ShapeDtypeStruct((b, nw, N_MEM), F32),
        compiler_params=_params(("arbitrary",)),
        name="mem_kv",
    )(mem, ln_g, w_bf16)


def _compress_slab(slabs_t, x_refs, pe_ref, w1_ref, b1_ref, w2_ref, out_refs):
    rows = _iota((N_CMP, LANES), 0)
    hw = 2 * CMP_HID
    chunks_per_page = LANES // CMP_STRIDE
    for v in range(2):
        for j in range(PAST_LEN // LANES):
            xt = slabs_t[v][:, j * LANES:(j + 1) * LANES].T
            for cc in range(chunks_per_page):
                r0 = (j * chunks_per_page + cc) * X_PITCH
                x_refs[v][r0:r0 + CMP_STRIDE, :] = xt[cc * CMP_STRIDE:(cc + 1) * CMP_STRIDE, :]
        nh = N_CMP // CMP_ROW_SPLITS
        accs = []
        for hf in range(CMP_ROW_SPLITS):
            acc = jnp.zeros((nh + (PE_ROWS if hf == 0 else 0), 2 * hw), F32)
            for pp in range(CMP_STRIDE // 2):
                r0 = hf * nh * X_PITCH + 2 * pp
                xcat = jnp.concatenate([x_refs[v][pl.ds(r0, nh, stride=X_PITCH), :],
                                        x_refs[v][pl.ds(r0 + 1, nh, stride=X_PITCH), :]], axis=1).astype(BF16)
                lhs = jnp.concatenate([xcat, pe_ref[v, pp]], axis=0) if hf == 0 else xcat
                acc = acc + _dot(lhs, w1_ref[v, pp])
            accs.append(acc)
        bias = b1_ref[v] + accs[0][nh:nh + 1, 0:hw] + accs[0][nh + 1:nh + 2, hw:2 * hw]
        acc = jnp.concatenate([accs[0][0:nh]] + accs[1:], axis=0)
        nxt = pltpu.roll(acc[:, hw:2 * hw], N_CMP - 1, 0)
        hid = jax.nn.gelu(acc[:, 0:hw] + nxt + bias)
        out = _dot(hid.astype(BF16), w2_ref[v])
        out_refs[v][...] = jnp.where(rows < N_CMP - 1, out, 0.0)


def _compress_prompt_body(k_ref, v_ref, pe_ref, w1_ref, b1_ref, w2_ref, kc_ref, vc_ref, xk_ref, xv_ref):
    _compress_slab((k_ref.at[0], v_ref.at[0]), (xk_ref, xv_ref), pe_ref, w1_ref, b1_ref, w2_ref,
                   (kc_ref.at[0], vc_ref.at[0]))


PE_ROWS = 16
CMP_ROW_SPLITS = 4
X_PITCH = 24


def _x_scratch():
    return [pltpu.VMEM((N_CMP * X_PITCH, LANES), F32) for _ in range(2)]


def _comp_weight_specs():
    z = lambda *_: (0, 0, 0)
    z4 = lambda *_: (0, 0, 0, 0)
    return [pl.BlockSpec((2, CMP_STRIDE // 2, PE_ROWS, 2 * LANES), z4),
            pl.BlockSpec((2, CMP_STRIDE // 2, 2 * LANES, 4 * CMP_HID), z4),
            pl.BlockSpec((2, 1, 2 * CMP_HID), z),
            pl.BlockSpec((2, 2 * CMP_HID, LANES), z)]


def _compress_prompt(kvc_t, comp_w):
    b = kvc_t.shape[0]
    out = jax.ShapeDtypeStruct((b, N_CMP, LANES), F32)
    return pl.pallas_call(
        _compress_prompt_body,
        grid=(b,),
        in_specs=[pl.BlockSpec((1, LANES, SEQ), lambda i: (i, 0, 0)),
                  pl.BlockSpec((1, LANES, SEQ), lambda i: (i, 1, 0))] + _comp_weight_specs(),
        out_specs=(pl.BlockSpec((1, N_CMP, LANES), lambda i: (i, 0, 0)),) * 2,
        out_shape=(out, out),
        scratch_shapes=_x_scratch(),
        compiler_params=_params(("arbitrary",)),
        name="compress_prompt",
    )(kvc_t, kvc_t, *comp_w)


N_PAGES = PAST_LEN // PAGE_SIZE


def _page_copy(pt_ref, pool_ref, slab_ref, sem_ref, seq, slot, s, j):
    return pltpu.make_async_copy(pool_ref.at[pt_ref[seq, j]],
                                 slab_ref.at[slot, s, :, :, pl.ds(j * PAGE_SIZE, PAGE_SIZE)],
                                 sem_ref.at[slot])


def _start_pages(pt_ref, pool_ref, slab_ref, sem_ref, step, slot, inline=False):
    spb = slab_ref.shape[1]
    for s in range(spb):
        if inline:
            for j in range(N_PAGES):
                _page_copy(pt_ref, pool_ref, slab_ref, sem_ref, step * spb + s, slot, s, j).start()
        else:
            def body(j, c):
                _page_copy(pt_ref, pool_ref, slab_ref, sem_ref, step * spb + s, slot, s, j).start()
                return c
            lax.fori_loop(0, N_PAGES, body, 0, unroll=8)


def _wait_pages(pt_ref, pool_ref, slab_ref, sem_ref, step, slot):
    spb = slab_ref.shape[1]
    for s in range(spb):
        def body(j, c):
            _page_copy(pt_ref, pool_ref, slab_ref, sem_ref, step * spb + s, slot, s, j).wait()
            return c
        lax.fori_loop(0, N_PAGES, body, 0, unroll=8)


SLAB_SLOTS = 2


def _slab_substeps(pt_ref, pool_ref, slab_ref, sem_ref, compute, fetch_first):
    g = pl.program_id(0)
    n_sub = pl.num_programs(0) * SLAB_SLOTS
    args = (pt_ref, pool_ref, slab_ref, sem_ref)

    @pl.when(g == 0)
    def _():
        _start_pages(*args, 0, 0)

    for slot in range(SLAB_SLOTS):
        step = g * SLAB_SLOTS + slot
        nxt, other = jnp.minimum(step + 1, n_sub - 1), (slot + 1) % SLAB_SLOTS
        if fetch_first:
            _start_pages(*args, nxt, other)
        _wait_pages(*args, step, slot)
        if not fetch_first:
            _start_pages(*args, nxt, other, inline=True)
        compute(slot)

    @pl.when(g == pl.num_programs(0) - 1)
    def _():
        _wait_pages(*args, n_sub - 1, 0)


def _compress_sample_body(pt_ref, pool_ref, pe_ref, w1_ref, b1_ref, w2_ref, kc_ref, vc_ref, slab_ref, sem_ref,
                          xk_ref, xv_ref):
    def compute(slot):
        _compress_slab((slab_ref.at[slot, 0, 0], slab_ref.at[slot, 0, 1]), (xk_ref, xv_ref), pe_ref, w1_ref, b1_ref,
                       w2_ref, (kc_ref.at[slot], vc_ref.at[slot]))

    _slab_substeps(pt_ref, pool_ref, slab_ref, sem_ref, compute, fetch_first=False)


def _slab_scratch(seqs_per_step=1):
    return [pltpu.VMEM((2, seqs_per_step, 2, LANES, PAST_LEN), F32), pltpu.SemaphoreType.DMA((2,))]


def _compress_sample(page_table, pool, comp_w):
    nb = page_table.shape[0]
    out = jax.ShapeDtypeStruct((nb, N_CMP, LANES), F32)
    grid_spec = pltpu.PrefetchScalarGridSpec(
        num_scalar_prefetch=1,
        grid=(nb // SLAB_SLOTS,),
        in_specs=[pl.BlockSpec(memory_space=pl.ANY)] + _comp_weight_specs(),
        out_specs=(pl.BlockSpec((SLAB_SLOTS, N_CMP, LANES), lambda i, pt: (i, 0, 0)),) * 2,
        scratch_shapes=_slab_scratch() + _x_scratch(),
    )
    return pl.pallas_call(
        _compress_sample_body,
        grid_spec=grid_spec,
        out_shape=(out, out),
        compiler_params=_params(("arbitrary",)),
        name="compress_sample",
    )(page_table, pool, *comp_w)


def _top_k_rows(score, blk, k):
    cand = score > -jnp.inf
    nblk = score.shape[0]
    for _ in range(k):
        m = jnp.max(score, axis=0, keepdims=True)
        idx = jnp.min(jnp.where(score == m, blk, nblk), axis=0, keepdims=True)
        score = jnp.where(blk == idx, -jnp.inf, score)
    return cand & (score == -jnp.inf)


def _add_q_bias(s, bias):
    return jnp.concatenate([s[:, g * Q_BLOCK:(g + 1) * Q_BLOCK] + bias for g in range(NSA_GROUP)], axis=1)


def _safe_inv(l):
    return 1.0 / jnp.maximum(l, 1e-30)


KT_SLC = 512
WIN_KEYS = WINDOW + Q_BLOCK
N_FORCED = 3


def _nsa_prompt_body(qn_ref, gt_ref, ksk_ref, ksvt_ref, kwk_ref, kwvt_ref, kc_ref, vc_ref, o_ref, selb_ref, s_ref):
    i = pl.program_id(1)
    q0 = i * Q_BLOCK
    qt = (qn_ref[...].astype(F32) * (SCALE * LOG2E)).T
    t_row = q0 + _iota((1, Q_BLOCK), 1)
    kc = kc_ref[0].astype(BF16)
    vct = vc_ref[0].T.astype(BF16)
    gt = gt_ref[...]
    zeros_q = jnp.zeros((HEAD_DIM, NSA_GROUP * Q_BLOCK), F32)

    mi = _iota((N_SLC, N_CMP), 0)
    mj = _iota((N_SLC, N_CMP), 1)
    msel = ((mj >= 4 * mi - 1) & (mj <= 4 * mi + 3) & (mj < N_CMP - 1)).astype(BF16)
    blk = _iota((N_SLC, Q_BLOCK), 0)
    cur = t_row // SLC_BLOCK
    forced = (blk == 0) | (blk == cur) | (blk == cur - 1)
    free = (blk * SLC_BLOCK <= t_row) & jnp.logical_not(forced)
    cend = _iota((N_CMP, 1), 0) * CMP_STRIDE + (CMP_LEN - 1)
    bias_c = jnp.where(cend <= t_row, 0.0, NEG_INF)
    wt = jnp.maximum(i - WINDOW // Q_BLOCK, 0)
    w0 = pl.multiple_of(wt * Q_BLOCK, Q_BLOCK)
    kpw = w0 + _iota((WIN_KEYS, 1), 0)
    bias_w = jnp.where((kpw <= t_row) & (kpw > t_row - WINDOW), 0.0, NEG_INF)
    krow = _iota((SLC_BLOCK, 1), 0)

    qgs, qps, o_cs, o_ws, cand_scores = [], [], [], [], []
    vrows = [slice(kvh * HEAD_DIM, (kvh + 1) * HEAD_DIM) for kvh in range(NSA_KV_HEADS)]
    for kvh in range(NSA_KV_HEADS):
        qg = jnp.concatenate([qt[(kvh * NSA_GROUP + g) * HEAD_DIM:(kvh * NSA_GROUP + g + 1) * HEAD_DIM, :]
                              for g in range(NSA_GROUP)], axis=1)
        qp = jnp.concatenate([qg, zeros_q] if kvh == 0 else [zeros_q, qg], axis=0).astype(BF16)
        vrow = vrows[kvh]
        qgs.append(qg.astype(BF16))
        qps.append(qp)

        s = _add_q_bias(_dot(kc, qp), bias_c)
        m = jnp.max(s, axis=0, keepdims=True)
        e = jnp.exp2(s - m)
        inv = jnp.where(m > 0.5 * NEG_INF, _safe_inv(jnp.sum(e, axis=0, keepdims=True)), 0.0)
        p = e * inv
        o_c = _dot(vct[vrow, :], p.astype(BF16))
        imp = (p[:, 0:Q_BLOCK] + p[:, Q_BLOCK:2 * Q_BLOCK]
               + p[:, 2 * Q_BLOCK:3 * Q_BLOCK] + p[:, 3 * Q_BLOCK:4 * Q_BLOCK])

        cand_scores.append(jnp.where(free, _dot_exact_rhs(msel, imp), -jnp.inf))
        o_cs.append(o_c)

    blk2 = jnp.concatenate([blk] * NSA_KV_HEADS, axis=1)
    picked = _top_k_rows(jnp.concatenate(cand_scores, axis=1), blk2, SLC_TOPK - N_FORCED)
    for kvh in range(NSA_KV_HEADS):
        selb_ref[kvh] = jnp.where(forced | picked[:, kvh * Q_BLOCK:(kvh + 1) * Q_BLOCK], 0.0, NEG_INF)

    for kvh in range(NSA_KV_HEADS):
        qp, vrow = qps[kvh], vrows[kvh]
        sw = _add_q_bias(_dot(kwk_ref[pl.ds(w0, WIN_KEYS), :], qp), bias_w)
        ew = jnp.exp2(sw - jnp.max(sw, axis=0, keepdims=True))
        inv_w = _safe_inv(jnp.sum(ew, axis=0, keepdims=True))
        pw = ew.astype(BF16)
        o_w = jnp.zeros((HEAD_DIM, NSA_GROUP * Q_BLOCK), F32)
        for r in range(WIN_KEYS // Q_BLOCK):
            tix = wt + r
            vt = kwvt_ref[tix // 4, vrow, :]
            sub = tix % 4
            vpiece = jnp.where(sub == 0, vt[:, 0:128],
                               jnp.where(sub == 1, vt[:, 128:256],
                                         jnp.where(sub == 2, vt[:, 256:384], vt[:, 384:512])))
            o_w = o_w + _dot(vpiece, pw[r * Q_BLOCK:(r + 1) * Q_BLOCK, :])
        o_ws.append(o_w * inv_w)

    bias_pad = jnp.zeros((HEAD_DIM - 8, NSA_GROUP * Q_BLOCK), F32)

    def scores(j, buf):
        k0 = pl.multiple_of(j * KT_SLC, KT_SLC)
        for kvh in range(NSA_KV_HEADS):
            b8 = selb_ref[kvh, pl.ds(pl.multiple_of(j * 8, 8), 8), :]
            b_rows = jnp.concatenate([jnp.concatenate([b8] * NSA_GROUP, axis=1), bias_pad], axis=0).astype(BF16)
            q_aug = jnp.concatenate([qgs[kvh], b_rows] if kvh == 0 else [b_rows, qgs[kvh]], axis=0)
            s_ref[buf, kvh] = _dot(ksk_ref[pl.ds(k0, KT_SLC), kvh * LANES:(kvh + 1) * LANES], q_aug)

    def consume(j, buf, state, causal):
        k0 = j * KT_SLC
        out = []
        for kvh in range(NSA_KV_HEADS):
            m_run, acc = state[kvh]
            sj = s_ref[buf, kvh]
            if causal:
                parts = []
                for r in range(KT_SLC // SLC_BLOCK):
                    ok = k0 + r * SLC_BLOCK + krow <= t_row
                    parts.append(jnp.concatenate(
                        [jnp.where(ok, sj[r * SLC_BLOCK:(r + 1) * SLC_BLOCK, g * Q_BLOCK:(g + 1) * Q_BLOCK], NEG_INF)
                         for g in range(NSA_GROUP)], axis=1))
                sj = jnp.concatenate(parts, axis=0)
            m_new = jnp.maximum(m_run, jnp.max(sj, axis=0, keepdims=True))
            alpha = jnp.exp2(m_run - m_new)
            ej = jnp.exp2(sj - m_new).astype(BF16)
            acc = alpha * acc + _dot(ksvt_ref[j, kvh * V_ROWS:(kvh + 1) * V_ROWS, :], ej)
            out.append((m_new, acc))
        return tuple(out)

    init1 = (jnp.full((1, NSA_GROUP * Q_BLOCK), NEG_INF, F32),
             jnp.zeros((V_ROWS, NSA_GROUP * Q_BLOCK), F32))
    n_tiles = q0 // KT_SLC + 1
    n_pairs = (n_tiles + 1) // 2

    def pair(jj, state):
        scores(2 * jj + 1, 1)
        state = consume(2 * jj, 0, state, False)
        scores(2 * jj + 2, 0)
        return consume(2 * jj + 1, 1, state, False)

    scores(0, 0)
    state = lax.fori_loop(0, n_pairs - 1, pair, (init1,) * NSA_KV_HEADS)
    last = 2 * n_pairs - 2

    def last_two(st):
        scores(last + 1, 1)
        return consume(last + 1, 1, consume(last, 0, st, True), True)

    state = lax.cond(n_tiles % 2 == 0, last_two, lambda st: consume(last, 0, st, True), state)

    outs = []
    for kvh in range(NSA_KV_HEADS):
        acc_s = state[kvh][1]
        o_s = acc_s[0:HEAD_DIM, :] * _safe_inv(acc_s[HEAD_DIM:HEAD_DIM + 1, :])

        def gate(br):
            base = br * NSA_HEADS + kvh * NSA_GROUP
            return jnp.concatenate([gt[base + g:base + g + 1, :] for g in range(NSA_GROUP)], axis=1)

        o = gate(0) * o_cs[kvh] + gate(1) * o_s + gate(2) * o_ws[kvh]
        outs.extend([o[:, g * Q_BLOCK:(g + 1) * Q_BLOCK] for g in range(NSA_GROUP)])
    o_ref[...] = jnp.concatenate(outs, axis=0).T


def _nsa_prompt(qn, gt, ksk, ksvt, kwk, kwvt, kc, vc, batch):
    nqb = SEQ // Q_BLOCK
    ntile = SEQ // 512
    return pl.pallas_call(
        _nsa_prompt_body,
        grid=(batch, nqb),
        in_specs=[pl.BlockSpec((Q_BLOCK, 512), lambda b, i: (b * nqb + i, 0)),
                  pl.BlockSpec((LANES, Q_BLOCK), lambda b, i: (0, b * nqb + i)),
                  pl.BlockSpec((SEQ, NSA_KV_HEADS * LANES), lambda b, i: (b, 0)),
                  pl.BlockSpec((ntile, NSA_KV_HEADS * V_ROWS, 512), lambda b, i: (b, 0, 0)),
                  pl.BlockSpec((SEQ, LANES), lambda b, i: (b, 0)),
                  pl.BlockSpec((ntile, LANES, 512), lambda b, i: (b, 0, 0)),
                  pl.BlockSpec((1, N_CMP, LANES), lambda b, i: (b, 0, 0)),
                  pl.BlockSpec((1, N_CMP, LANES), lambda b, i: (b, 0, 0))],
        out_specs=pl.BlockSpec((Q_BLOCK, 512), lambda b, i: (b * nqb + i, 0)),
        out_shape=jax.ShapeDtypeStruct((batch * SEQ, 512), F32),
        scratch_shapes=[pltpu.VMEM((NSA_KV_HEADS, N_SLC, Q_BLOCK), F32),
                        pltpu.VMEM((2, NSA_KV_HEADS, KT_SLC, NSA_GROUP * Q_BLOCK), F32)],
        compiler_params=_params(("arbitrary", "arbitrary")),
        name="nsa_prompt",
    )(qn, gt, ksk, ksvt, kwk, kwvt, kc, vc)


def _softmax_parts(parts, masks):
    m = None
    for s, mk in zip(parts, masks):
        mi = jnp.max(jnp.where(mk, s, NEG_INF), axis=-1, keepdims=True)
        m = mi if m is None else jnp.maximum(m, mi)
    es = [jnp.where(mk, jnp.exp(jnp.where(mk, s, NEG_INF) - m), 0.0) for s, mk in zip(parts, masks)]
    l = None
    for e in es:
        li = jnp.sum(e, axis=-1, keepdims=True)
        l = li if l is None else l + li
    inv = 1.0 / jnp.maximum(l, 1e-30)
    return [e * inv for e in es]


KC_SLC = 2048
ROWS_S = NSA_HEADS * DEC_SEQ


def _pad_rows(x, n):
    if x.shape[0] == n:
        return x
    return jnp.concatenate([x, jnp.zeros((n - x.shape[0], x.shape[1]), x.dtype)], axis=0)


def _nsa_sample_body(pt_ref, pool_ref, qp_ref, g_ref, kc_ref, vc_ref, kvs_ref, kvw_ref, win_ref, wtail_ref,
                     expand_ref, o_ref, nwin_ref, slab_ref, sem_ref):
    refs = (qp_ref, g_ref, kc_ref, vc_ref, kvs_ref, kvw_ref, win_ref, wtail_ref, expand_ref, o_ref, nwin_ref, slab_ref)
    _slab_substeps(pt_ref, pool_ref, slab_ref, sem_ref, lambda slot: _nsa_sample_seqs(slot, *refs), fetch_first=True)


def _nsa_sample_seqs(slot, qp_ref, g_ref, kc_ref, vc_ref, kvs_ref, kvw_ref, win_ref, wtail_ref, expand_ref,
                     o_ref, nwin_ref, slab_ref):
    spb = slab_ref.shape[1]
    seqs = range(spb)
    bi = [slot * spb + s for s in seqs]
    qi = _iota((ROWS_S, 1), 0) % DEC_SEQ
    t = PAST_LEN + qi
    jn = _iota((1, LANES), 1)
    n_chunk = PAST_LEN // KC_SLC
    qs = [(qp_ref[bi[s]] * SCALE).astype(BF16) for s in seqs]
    new_rows = [slice(bi[s] * DEC_SEQ, (bi[s] + 1) * DEC_SEQ) for s in seqs]

    cend = _iota((1, N_CMP), 1) * CMP_STRIDE + (CMP_LEN - 1)
    p_cs = [_softmax_parts([_dot_nt(qs[s], kc_ref[bi[s]].astype(BF16))], [cend <= t])[0] for s in seqs]
    o_cs = [_dot(p_cs[s].astype(BF16), vc_ref[bi[s]].astype(BF16)) for s in seqs]

    half = ROWS_S // NSA_KV_HEADS
    imp = jnp.concatenate(
        [p_cs[s][k * half:k * half + 8] + p_cs[s][k * half + 8:k * half + 16]
         + p_cs[s][k * half + 16:k * half + 24] + p_cs[s][k * half + 24:k * half + 32]
         for s in seqs for k in range(NSA_KV_HEADS)], axis=0)
    mi = _iota((N_CMP, N_SLC), 0)
    mj = _iota((N_CMP, N_SLC), 1)
    msel_t = ((mi >= 4 * mj - 1) & (mi <= 4 * mj + 3) & (mi < N_CMP - 1)).astype(BF16)
    score = _dot_exact_lhs(imp, msel_t)
    score_t = _pad_rows(score, LANES).T
    blk = _iota((N_SLC, LANES), 0)
    forced = (blk == 0) | (blk == N_SLC - 1)
    picked = _top_k_rows(jnp.where(forced, -jnp.inf, score_t), blk, SLC_TOPK - 1 - 2)
    unsel = jnp.where(forced | picked, 0.0, 1.0).T
    unsel64 = [jnp.concatenate([unsel[s * 16:s * 16 + 8]] * NSA_GROUP + [unsel[s * 16 + 8:s * 16 + 16]] * NSA_GROUP,
                               axis=0).astype(BF16) for s in seqs]

    new_bias = jnp.where((jn <= qi) & (jn < DEC_SEQ), 0.0, NEG_INF)
    k_news = [_pad_rows(kvs_ref[new_rows[s], 0:LANES], LANES).astype(BF16) for s in seqs]
    v_news = [_pad_rows(kvs_ref[new_rows[s], LANES:2 * LANES], LANES).astype(BF16) for s in seqs]
    s_news = [_dot_nt(qs[s], k_news[s]) + new_bias for s in seqs]
    scs = [[_dot(qs[s], slab_ref[slot, s, 0, :, c * KC_SLC:(c + 1) * KC_SLC].astype(BF16))
            + _dot(unsel64[s], expand_ref[:, c * KC_SLC:(c + 1) * KC_SLC]) * NEG_INF
            for c in range(n_chunk)] for s in seqs]
    o_ss = []
    for s in seqs:
        m_s = jnp.max(s_news[s], axis=-1, keepdims=True)
        for sc in scs[s]:
            m_s = jnp.maximum(m_s, jnp.max(sc, axis=-1, keepdims=True))
        e_new = jnp.exp(s_news[s] - m_s)
        l_s = jnp.sum(e_new, axis=-1, keepdims=True)
        acc = _dot(e_new.astype(BF16), v_news[s])
        for c in range(n_chunk):
            ec = jnp.exp(scs[s][c] - m_s)
            l_s = l_s + jnp.sum(ec, axis=-1, keepdims=True)
            acc = acc + _dot_nt(ec.astype(BF16), slab_ref[slot, s, 1, :, c * KC_SLC:(c + 1) * KC_SLC].astype(BF16))
        o_ss.append(acc * _safe_inv(l_s))

    wb = win_ref.shape[2]
    kpos_b = PAST_LEN - wb + _iota((1, wb), 1)
    kpos_n = PAST_LEN + jn
    mask_b = (kpos_b <= t) & (kpos_b > t - WINDOW) & (kpos_b >= 0)
    mask_n = (kpos_n <= t) & (kpos_n > t - WINDOW) & (jn < DEC_SEQ)
    for s in seqs:
        win_t = win_ref[bi[s]]
        kw_new = _pad_rows(kvw_ref[new_rows[s], 0:LANES], LANES).astype(BF16)
        vw_new = _pad_rows(kvw_ref[new_rows[s], LANES:2 * LANES], LANES).astype(BF16)
        p_b, p_n = _softmax_parts([_dot(qs[s], win_t[0:LANES, :].astype(BF16)), _dot_nt(qs[s], kw_new)],
                                  [mask_b, mask_n])
        o_w = _dot_nt(p_b.astype(BF16), win_t[LANES:2 * LANES, :].astype(BF16)) + _dot(p_n.astype(BF16), vw_new)

        g = g_ref[bi[s]]
        o_ref[bi[s]] = g[:, 0:1] * o_cs[s] + g[:, 1:2] * o_ss[s] + g[:, 2:3] * o_w
        shifted = pltpu.roll(win_t, wb - DEC_SEQ, 1)
        nwin_ref[bi[s], :, 0:wb - LANES] = shifted[:, 0:wb - LANES]
        nwin_ref[bi[s], :, wb - LANES:wb] = jnp.where(jn >= LANES - DEC_SEQ, wtail_ref[bi[s]],
                                                      shifted[:, wb - LANES:wb])


NSA_SAMPLE_SEQS_PER_STEP = 2


def _nsa_sample(page_table, pool_slc, qp, gates, kc, vc, kvs, kvw, win_t, wtail, expand):
    nb = page_table.shape[0]
    wb = win_t.shape[2]
    spb = NSA_SAMPLE_SEQS_PER_STEP
    sps = spb * SLAB_SLOTS
    per_b = lambda *shape: pl.BlockSpec((sps,) + shape, lambda i, pt: (i,) + (0,) * len(shape))
    rows8 = pl.BlockSpec((sps * DEC_SEQ, 256), lambda i, pt: (i, 0))
    grid_spec = pltpu.PrefetchScalarGridSpec(
        num_scalar_prefetch=1,
        grid=(nb // sps,),
        in_specs=[pl.BlockSpec(memory_space=pl.ANY),
                  per_b(ROWS_S, LANES), per_b(ROWS_S, LANES), per_b(N_CMP, LANES), per_b(N_CMP, LANES),
                  rows8, rows8, per_b(256, wb), per_b(256, LANES),
                  pl.BlockSpec((N_SLC, PAST_LEN), lambda i, pt: (0, 0))],
        out_specs=(per_b(ROWS_S, LANES), per_b(256, wb)),
        scratch_shapes=_slab_scratch(spb),
    )
    return pl.pallas_call(
        _nsa_sample_body,
        grid_spec=grid_spec,
        out_shape=(jax.ShapeDtypeStruct((nb, ROWS_S, LANES), F32),
                   jax.ShapeDtypeStruct((nb, 256, wb), F32)),
        compiler_params=_params(("arbitrary",)),
        name="nsa_sample",
    )(page_table, pool_slc, qp, gates, kc, vc, kvs, kvw, win_t, wtail, expand)


RET_W = RET_HEADS * HEAD_DIM


def _retention_body(q_ref, k_ref, v_ref, cos_ref, sin_ref, st0_ref, gn_ref, o_ref, st_ref, state_ref, *, chunk):
    c = pl.program_id(1)
    ck = max(chunk, LANES)
    spb = st0_ref.shape[0]

    lane = _iota((1, RET_W), 1)
    head_of_lane = lane // HEAD_DIM
    row_head = _iota((RET_W, 1), 0) // HEAD_DIM

    @pl.when(c == 0)
    def _():
        for bb in range(spb):
            st_c = st0_ref[bb]
            state_ref[bb] = jnp.concatenate([jnp.where(row_head == h, st_c, 0.0) for h in range(RET_HEADS)], axis=1)
    low_half = (lane % HEAD_DIM) < HEAD_DIM // 2
    cos = cos_ref[...]
    sin = sin_ref[...]

    def rope(x):
        rot = jnp.where(low_half, -pltpu.roll(x, RET_W - HEAD_DIM // 2, 1), pltpu.roll(x, HEAD_DIM // 2, 1))
        return x * cos + rot * sin

    logg_lane = jnp.zeros((1, RET_W), F32)
    for h in range(RET_HEADS):
        logg_lane = jnp.where(head_of_lane == h, _LOG_G[h], logg_lane)

    n = _iota((chunk, 1), 0).astype(F32)
    nk = _iota((ck, 1), 0).astype(F32)
    diff = n - _iota((1, ck), 1).astype(F32)
    in_chunk = _iota((1, ck), 1) < chunk
    dmats = [jnp.where((diff >= 0) & in_chunk, jnp.exp(_LOG_G[h] * jnp.maximum(diff, 0.0)), 0.0)
             for h in range(RET_HEADS)]
    xi = jnp.exp(logg_lane * (n + 1.0))
    zeta = jnp.where(nk < chunk, jnp.exp(logg_lane * (chunk - 1.0 - nk)), 0.0)
    decay_rows = jnp.zeros((RET_W, 1), F32)
    for h in range(RET_HEADS):
        decay_rows = jnp.where(row_head == h, float(np.exp(_LOG_G[h] * chunk)), decay_rows)
    avg = jnp.where(_iota((RET_W, RET_W), 0) // HEAD_DIM == _iota((RET_W, RET_W), 1) // HEAD_DIM,
                    1.0 / HEAD_DIM, 0.0).astype(BF16)
    gn = gn_ref[...]

    seqs = range(spb)
    rows = [slice(bb * chunk, (bb + 1) * chunk) for bb in seqs]
    qs = [rope(q_ref[r, :]) * SCALE for r in rows]
    kps = [_pad_rows(rope(k_ref[r, :]), ck) for r in rows]
    vps = [_pad_rows(v_ref[r, :], ck) for r in rows]
    hms = [head_of_lane == h for h in range(RET_HEADS)]
    ss = [[_dot_nt(jnp.where(hms[h], qs[bb], 0.0), kps[bb]) * dmats[h] for h in range(RET_HEADS)] for bb in seqs]
    inners = []
    for bb in seqs:
        inner = jnp.zeros((chunk, RET_W), F32)
        for h in range(RET_HEADS):
            inner = inner + jnp.where(hms[h], _dot(ss[bb][h], vps[bb]), 0.0)
        inners.append(inner)
    states = [state_ref[bb] for bb in seqs]
    os_ = [inners[bb] + _dot(qs[bb] * xi, states[bb]) for bb in seqs]
    for bb in seqs:
        kz_t = (kps[bb] * zeta).T
        new_state = decay_rows * states[bb] + jnp.where(row_head == head_of_lane, _dot(kz_t, vps[bb]), 0.0)
        state_ref[bb] = new_state
        st_ref[bb] = (new_state[:, 0:HEAD_DIM] + new_state[:, HEAD_DIM:2 * HEAD_DIM]
                      + new_state[:, 2 * HEAD_DIM:3 * HEAD_DIM] + new_state[:, 3 * HEAD_DIM:4 * HEAD_DIM])

    o = jnp.concatenate(os_, axis=0) if spb > 1 else os_[0]
    mu = _dot_exact_lhs(o, avg)
    d = o - mu
    var = _dot_exact_lhs(d * d, avg)
    o_ref[...] = d * lax.rsqrt(var + EPS) * gn


def _retention(qkvr, cos, sin, state, gn, batch, t_len, chunk, seqs_per_step=1):
    nch = t_len // chunk
    spb = seqs_per_step
    assert spb == 1 or nch == 1
    col = lambda j: pl.BlockSpec((spb * chunk, RET_W), lambda b, c: (b * nch + c, j))
    tab = pl.BlockSpec((chunk, RET_W), lambda b, c: (c, 0))
    st = pl.BlockSpec((spb, RET_W, HEAD_DIM), lambda b, c: (b, 0, 0))
    return pl.pallas_call(
        functools.partial(_retention_body, chunk=chunk),
        grid=(batch // spb, nch),
        in_specs=[col(0), col(1), col(2), tab, tab, st, pl.BlockSpec((1, RET_W), lambda b, c: (0, 0))],
        out_specs=(pl.BlockSpec((spb * chunk, RET_W), lambda b, c: (b * nch + c, 0)), st),
        out_shape=(jax.ShapeDtypeStruct((batch * t_len, RET_W), F32),
                   jax.ShapeDtypeStruct((batch, RET_W, HEAD_DIM), F32)),
        scratch_shapes=[pltpu.VMEM((spb, RET_W, RET_W), F32)],
        compiler_params=_params(("arbitrary", "arbitrary")),
        name="retention",
    )(qkvr, qkvr, qkvr, cos, sin, state, gn)


MEM_W = 4 * HEAD_DIM


def _mem_attn_body(q_ref, mkv_ref, o_ref, *, tm):
    head_of_lane = _iota((1, MEM_W), 1) // HEAD_DIM
    spb = mkv_ref.shape[0]
    q_all = q_ref[...] * SCALE
    scores = []
    for bb in range(spb):
        q = q_all[bb * tm:(bb + 1) * tm, :]
        q4 = jnp.concatenate([jnp.where(head_of_lane == h, q, 0.0) for h in range(4)], axis=0).astype(BF16)
        scores.append(_dot(q4, mkv_ref[bb, 0:MEM_W, :].astype(BF16)))
    s = jnp.concatenate(scores, axis=0)
    e = jnp.exp(s - jnp.max(s, axis=-1, keepdims=True))
    p = (e / jnp.sum(e, axis=-1, keepdims=True)).astype(BF16)
    for bb in range(spb):
        pv = _dot_nt(p[bb * 4 * tm:(bb + 1) * 4 * tm, :], mkv_ref[bb, MEM_W:2 * MEM_W, :].astype(BF16))
        o = jnp.zeros((tm, MEM_W), F32)
        for h in range(4):
            o = o + jnp.where(head_of_lane == h, pv[h * tm:(h + 1) * tm, :], 0.0)
        o_ref[bb * tm:(bb + 1) * tm, :] = o


def _mem_attn(qm, mkv_t, batch, rows_per_batch, tm, seqs_per_step=1):
    nt = rows_per_batch // tm
    spb = seqs_per_step
    assert spb == 1 or nt == 1
    return pl.pallas_call(
        functools.partial(_mem_attn_body, tm=tm),
        grid=(batch // spb, nt),
        in_specs=[pl.BlockSpec((spb * tm, MEM_W), lambda b, i: (b * nt + i, 0)),
                  pl.BlockSpec((spb, 2 * MEM_W, N_MEM), lambda b, i: (b, 0, 0))],
        out_specs=pl.BlockSpec((spb * tm, MEM_W), lambda b, i: (b * nt + i, 0)),
        out_shape=jax.ShapeDtypeStruct((batch * rows_per_batch, MEM_W), F32),
        compiler_params=_params(("arbitrary", "arbitrary")),
        name="mem_attn",
    )(qm, mkv_t)


def _finish_body(x_ref, on_ref, or_ref, om_ref, z_ref, w_ref, g_ref, y_ref):
    o = jnp.concatenate([on_ref[...], or_ref[...], om_ref[...]], axis=-1)
    z = z_ref[...]
    mix = o * (z * jax.nn.sigmoid(z))
    xo = x_ref[...] + _dot(mix.astype(BF16), w_ref[...])
    y_ref[...] = xo * lax.rsqrt(jnp.mean(xo * xo, axis=-1, keepdims=True) + EPS) * g_ref[...]


def _finish(x2d, o_n, o_r, o_m, z, w_out, ln_final):
    n = x2d.shape[0]
    tm = 512
    row = lambda w: pl.BlockSpec((tm, w), lambda i: (i, 0))
    return pl.pallas_call(
        _finish_body,
        grid=(n // tm,),
        in_specs=[row(D_MODEL), row(512), row(256), row(256), row(D_MODEL),
                  pl.BlockSpec((D_MODEL, D_MODEL), lambda i: (0, 0)),
                  pl.BlockSpec((1, D_MODEL), lambda i: (0, 0))],
        out_specs=row(D_MODEL),
        out_shape=jax.ShapeDtypeStruct((n, D_MODEL), F32),
        compiler_params=_params(("arbitrary",)),
        name="finish",
    )(x2d, o_n, o_r, o_m, z, w_out, ln_final)


def _permute_w_in(w):
    sizes = (512, 256, 256, 256, 24, 512, 256, 256, 256, 256, 256, 256)
    offs = np.concatenate([[0], np.cumsum(sizes)])
    part = lambda i: w[:, int(offs[i]):int(offs[i + 1])]
    q_n, kv_c, kv_s, kv_w, g_n, z_n, q_r, k_r, v_r, z_r, q_m, z_m = [part(i) for i in range(12)]
    g_pad = jnp.pad(g_n, ((0, 0), (0, LANES - g_n.shape[1])))
    return jnp.concatenate([q_n, kv_c, kv_s, kv_w, q_r, k_r, v_r, q_m, z_n, z_r, z_m, g_pad], axis=1).astype(BF16)


def _compress_weights(pe, w1, b1, w2):
    npair = CMP_STRIDE // 2
    pe_r = pe.reshape(2, 2, npair, 2, HEAD_DIM)
    pe_l = jnp.broadcast_to(pe_r[:, :, :, :, None, :], (2, 2, npair, 2, NSA_KV_HEADS, HEAD_DIM))
    pe_l = pe_l.reshape(2, 2, npair, 2 * LANES).transpose(0, 2, 1, 3)
    pe_l = jnp.pad(pe_l, ((0, 0), (0, 0), (0, PE_ROWS - 2), (0, 0))).astype(BF16)
    w1_r = w1.reshape(2, 2, CMP_STRIDE, HEAD_DIM, CMP_HID)
    zw = jnp.zeros_like(w1_r)
    w1_bd = jnp.concatenate([jnp.concatenate([w1_r, zw], axis=-1),
                             jnp.concatenate([zw, w1_r], axis=-1)], axis=-2)
    w1_bd = w1_bd.reshape(2, 2, npair, 2 * LANES, 2 * CMP_HID)
    w1_bd = jnp.concatenate([w1_bd[:, 0], w1_bd[:, 1]], axis=-1).astype(BF16)
    b1_l = jnp.concatenate([b1, b1], axis=-1).reshape(2, 1, 2 * CMP_HID)
    z2 = jnp.zeros_like(w2)
    w2_bd = jnp.concatenate([jnp.concatenate([w2, z2], axis=-1),
                             jnp.concatenate([z2, w2], axis=-1)], axis=-2).astype(BF16)
    return pe_l, w1_bd, b1_l, w2_bd


def _rope_tables(pos):
    half = HEAD_DIM // 2
    inv = ROPE_BASE ** (-jnp.arange(half, dtype=F32) / half)
    ang = pos.astype(F32)[:, None] * inv[None, :]
    cos, sin = jnp.cos(ang), jnp.sin(ang)
    cos_l = jnp.tile(jnp.concatenate([cos, cos], axis=-1), (1, RET_HEADS))
    sin_l = jnp.tile(jnp.concatenate([sin, sin], axis=-1), (1, RET_HEADS))
    return cos_l, sin_l


def _kv_shape(a, b, t):
    return a.reshape(1, b, t, 2, NSA_KV_HEADS, HEAD_DIM)


def _position_minor(cache, heads):
    b, l = cache.shape[:2]
    return jnp.transpose(cache, (0, 2, 3, 4, 1)).reshape(b, 2 * heads * HEAD_DIM, l)


def _from_position_minor(a, heads):
    b, _, l = a.shape
    return jnp.transpose(a.reshape(b, 2, heads, HEAD_DIM, l), (0, 4, 1, 2, 3))[None]


def kernel(x_prompt, mem_prompt, x_sample, cache_nsa_cmp, cache_nsa_slc, cache_nsa_win, state_ret, cache_mem,
           page_table, ln_mix, w_in, cmp_pe, cmp_w1, cmp_b1, cmp_w2, ret_gn, ln_mem, w_mem_kv, w_out, ln_final):
    bp, t_len = x_prompt.shape[:2]
    bs, s_len = x_sample.shape[:2]
    assert (t_len, s_len) == (SEQ, DEC_SEQ) and ln_mix.shape[0] == 1
    w_perm = _permute_w_in(w_in[0])
    comp_w = _compress_weights(cmp_pe[0], cmp_w1[0], cmp_b1[0], cmp_w2[0])
    ln_g = ln_mix[0].reshape(1, D_MODEL)
    gn = ret_gn[0].reshape(1, RET_W)
    w_out_b = w_out[0].astype(BF16)
    ln_f = ln_final.reshape(1, D_MODEL)

    xp2 = x_prompt.reshape(bp * SEQ, D_MODEL)
    (qn, kvc_t, kvs_t, kvw_t, qkvr, qm, z, gt, ksk, ksvt, kwk, kwvt) = _inproj(xp2, ln_g, w_perm, SEQ, True)
    kc, vc = _compress_prompt(kvc_t, comp_w)
    o_n = _nsa_prompt(qn, gt, ksk, ksvt, kwk, kwvt, kc, vc, bp)
    cos_p, sin_p = _rope_tables(jnp.arange(SEQ))
    o_r, st_p = _retention(qkvr, cos_p, sin_p, jnp.zeros((bp, RET_W, HEAD_DIM), F32), gn, bp, SEQ, 256)
    mkv_t = _mem_kv(mem_prompt, ln_mem[0].reshape(1, D_MODEL), w_mem_kv[0].astype(BF16))
    o_m = _mem_attn(qm, mkv_t, bp, SEQ, 512)
    y_prompt = _finish(xp2, o_n, o_r, o_m, z, w_out_b, ln_f).reshape(bp, SEQ, D_MODEL)
    new_cmp_p = _from_position_minor(kvc_t, NSA_KV_HEADS)
    new_slc_p = _from_position_minor(kvs_t, NSA_KV_HEADS)
    new_win_p = _from_position_minor(kvw_t[:, :, SEQ - WINDOW:], NSA_KV_HEADS)
    new_ret_p = st_p.reshape(1, bp, RET_HEADS, HEAD_DIM, HEAD_DIM)
    new_mem_p = _from_position_minor(mkv_t, 4)

    xs2 = x_sample.reshape(bs * DEC_SEQ, D_MODEL)
    (qn_s, kvc_s, kvs_s, kvw_s, qkvr_s, qm_s, z_s, gt_s) = _inproj(xs2, ln_g, w_perm, DEC_SEQ, False)
    pool_cmp = _position_minor(cache_nsa_cmp[0], NSA_KV_HEADS).reshape(-1, 2, LANES, PAGE_SIZE)
    pool_slc = _position_minor(cache_nsa_slc[0], NSA_KV_HEADS).reshape(-1, 2, LANES, PAGE_SIZE)
    kc_s, vc_s = _compress_sample(page_table, pool_cmp, comp_w)
    q5 = qn_s.astype(F32).reshape(bs, DEC_SEQ, NSA_KV_HEADS, NSA_GROUP, HEAD_DIM).transpose(0, 2, 3, 1, 4)
    zq = jnp.zeros_like(q5[:, 0])
    qp = jnp.stack([jnp.concatenate([q5[:, 0], zq], axis=-1), jnp.concatenate([zq, q5[:, 1]], axis=-1)], axis=1)
    qp = qp.reshape(bs, ROWS_S, LANES)
    g5 = gt_s[:3 * NSA_HEADS].reshape(3, NSA_KV_HEADS, NSA_GROUP, bs, DEC_SEQ).transpose(3, 1, 2, 4, 0)
    gates = jnp.pad(g5.reshape(bs, ROWS_S, 3), ((0, 0), (0, 0), (0, LANES - 3)))
    expand = jnp.asarray(np.repeat(np.eye(N_SLC, dtype=np.float32), SLC_BLOCK, axis=1), dtype=BF16)
    win_t = _position_minor(cache_nsa_win[0], NSA_KV_HEADS)
    wtail = jnp.pad(kvw_s.reshape(bs, DEC_SEQ, 256).transpose(0, 2, 1), ((0, 0), (0, 0), (LANES - DEC_SEQ, 0)))
    o_sn, new_win_t = _nsa_sample(page_table, pool_slc, qp, gates, kc_s, vc_s, kvs_s, kvw_s, win_t, wtail, expand)
    o6 = o_sn.reshape(bs, NSA_KV_HEADS, NSA_GROUP, DEC_SEQ, NSA_KV_HEADS, HEAD_DIM)
    o_n_s = jnp.stack([o6[:, 0, :, :, 0], o6[:, 1, :, :, 1]], axis=1)
    o_n_s = o_n_s.transpose(0, 3, 1, 2, 4).reshape(bs * DEC_SEQ, 512)
    cos_s, sin_s = _rope_tables(PAST_LEN + jnp.arange(DEC_SEQ))
    o_r_s, st_s = _retention(qkvr_s, cos_s, sin_s, state_ret[0].reshape(bs, RET_W, HEAD_DIM), gn, bs, DEC_SEQ, DEC_SEQ,
                             seqs_per_step=SAMPLE_SEQS_PER_STEP)
    o_m_s = _mem_attn(qm_s, _position_minor(cache_mem[0], 4), bs, DEC_SEQ, DEC_SEQ,
                      seqs_per_step=SAMPLE_SEQS_PER_STEP)
    y_sample = _finish(xs2, o_n_s, o_r_s, o_m_s, z_s, w_out_b, ln_f).reshape(bs, DEC_SEQ, D_MODEL)
    new_cmp_s = _kv_shape(kvc_s, bs, DEC_SEQ)
    new_slc_s = _kv_shape(kvs_s, bs, DEC_SEQ)
    new_win_s = _from_position_minor(new_win_t, NSA_KV_HEADS)
    new_ret_s = st_s.reshape(1, bs, RET_HEADS, HEAD_DIM, HEAD_DIM)

    return (y_prompt, y_sample, new_cmp_p, new_cmp_s, new_slc_p, new_slc_s, new_win_p, new_win_s,
            new_ret_p, new_ret_s, new_mem_p)
```

```python
import functools

import numpy as np
import jax
import jax.numpy as jnp
from jax import lax
from jax.experimental import pallas as pl
from jax.experimental.pallas import tpu as pltpu

D_MODEL = 1024
SEQ = 8192
DEC_SEQ = 8
PAST_LEN = 8192
PAGE_SIZE = 128
HEAD_DIM = 64
NSA_HEADS = 8
NSA_KV_HEADS = 2
NSA_GROUP = 4
CMP_LEN = 32
CMP_STRIDE = 16
CMP_HID = 128
SLC_BLOCK = 64
SLC_TOPK = 16
WINDOW = 512
RET_HEADS = 4
N_MEM = 256
Q_BLOCK = 128
ROPE_BASE = 10000.0
EPS = 1e-6
NEG_INF = -1e30
SCALE = HEAD_DIM ** -0.5

N_CMP = 512
N_SLC = 128
LANES = 128
ONES_ROWS = 16
V_ROWS = HEAD_DIM + ONES_ROWS
LOG2E = 1.4426950408889634
SAMPLE_SEQS_PER_STEP = 8
VMEM_LIMIT = 56 * 1024 * 1024

_W_Q, _W_KVC, _W_KVS, _W_KVW, _W_QKVR, _W_QM, _W_Z, _W_G = 0, 512, 768, 1024, 1280, 2048, 2304, 3328
IN_WP = 3456

F32 = jnp.float32
BF16 = jnp.bfloat16

_LOG_G = [float(np.log1p(-(2.0 ** (-5.0 - h)))) for h in range(RET_HEADS)]


def _dot(a, b):
    return jnp.dot(a, b, preferred_element_type=F32)


def _dot_nt(a, b):
    return lax.dot_general(a, b, (((1,), (1,)), ((), ())), preferred_element_type=F32)


def _split3(x):
    hi = x.astype(BF16)
    r1 = x - hi.astype(F32)
    mid = r1.astype(BF16)
    lo = (r1 - mid.astype(F32)).astype(BF16)
    return hi, mid, lo


def _dot_exact_rhs(a_bf16, x):
    hi, mid, lo = _split3(x)
    return _dot(a_bf16, hi) + _dot(a_bf16, mid) + _dot(a_bf16, lo)


def _dot_exact_lhs(x, b_bf16):
    hi, mid, lo = _split3(x)
    return _dot(hi, b_bf16) + _dot(mid, b_bf16) + _dot(lo, b_bf16)


def _iota(shape, dim):
    return lax.broadcasted_iota(jnp.int32, shape, dim)


def _params(sem):
    return pltpu.CompilerParams(dimension_semantics=sem, vmem_limit_bytes=VMEM_LIMIT)


def _inproj_body(x_ref, g_ref, w_ref, qn_ref, kvc_ref, kvs_ref, kvw_ref, qkvr_ref, qm_ref, z_ref, gt_ref,
                 *attn_refs, transposed_kv):
    x = x_ref[...]
    xn = x * lax.rsqrt(jnp.mean(x * x, axis=-1, keepdims=True) + EPS) * g_ref[...]
    xb = xn.astype(BF16)

    def proj(a, b):
        return _dot(xb, w_ref[:, a:b])

    qn_ref[...] = proj(_W_Q, _W_KVC).astype(BF16)
    kvc = proj(_W_KVC, _W_KVS)
    kvs = proj(_W_KVS, _W_KVW)
    kvw = proj(_W_KVW, _W_QKVR)
    if transposed_kv:
        ksk_ref, ksvt_ref, kwk_ref, kwvt_ref = attn_refs
        kvc_ref[0] = kvc.T
        kvs_t = kvs.T
        kvs_ref[0] = kvs_t
        tm = kvs.shape[0]
        lane = _iota((tm, LANES), 1)
        blk_in_tile = _iota((tm, LANES), 0) // SLC_BLOCK
        k2 = kvs[:, :LANES]
        ksk_ref[:, 0:LANES] = jnp.where(lane < HEAD_DIM, k2,
                                        jnp.where(lane - HEAD_DIM == blk_in_tile, 1.0, 0.0)).astype(BF16)
        ksk_ref[:, LANES:2 * LANES] = jnp.where(lane >= HEAD_DIM, k2,
                                                jnp.where(lane == blk_in_tile, 1.0, 0.0)).astype(BF16)
        ones_rows = jnp.where(_iota((ONES_ROWS, tm), 0) == 0, 1.0, 0.0)
        ksvt_ref[0] = jnp.concatenate([kvs_t[LANES:LANES + HEAD_DIM, :], ones_rows,
                                       kvs_t[LANES + HEAD_DIM:, :], ones_rows], axis=0).astype(BF16)
        kvw_t = kvw.T
        kvw_ref[0] = kvw_t
        kwk_ref[...] = kvw[:, :LANES].astype(BF16)
        kwvt_ref[0] = kvw_t[LANES:, :].astype(BF16)
    else:
        kvc_ref[...] = kvc
        kvs_ref[...] = kvs
        kvw_ref[...] = kvw
    qkvr_ref[...] = proj(_W_QKVR, _W_QM)
    qm_ref[...] = proj(_W_QM, _W_Z)
    z_ref[...] = proj(_W_Z, _W_G)
    gt_ref[...] = jax.nn.sigmoid(proj(_W_G, IN_WP)).T


def _inproj(x2d, ln_g, w_perm, rows_per_batch, transposed_kv):
    n = x2d.shape[0]
    tm = 512
    nt = n // tm
    row = lambda w: pl.BlockSpec((tm, w), lambda i: (i, 0))
    if transposed_kv:
        tpb = rows_per_batch // tm
        kv_shape = jax.ShapeDtypeStruct((n // rows_per_batch, 256, rows_per_batch), F32)
        kv_spec = pl.BlockSpec((1, 256, tm), lambda i: (i // tpb, 0, i % tpb))
    else:
        kv_shape = jax.ShapeDtypeStruct((n, 256), F32)
        kv_spec = row(256)
    out_shape = [
        jax.ShapeDtypeStruct((n, 512), BF16),
        kv_shape, kv_shape, kv_shape,
        jax.ShapeDtypeStruct((n, 768), F32),
        jax.ShapeDtypeStruct((n, 256), F32),
        jax.ShapeDtypeStruct((n, 1024), F32),
        jax.ShapeDtypeStruct((LANES, n), F32),
    ]
    out_specs = [row(512), kv_spec, kv_spec, kv_spec, row(768), row(256), row(1024),
                 pl.BlockSpec((LANES, tm), lambda i: (0, i))]
    if transposed_kv:
        tile_t = pl.BlockSpec((1, LANES, tm), lambda i: (i, 0, 0))
        tile_v = pl.BlockSpec((1, NSA_KV_HEADS * V_ROWS, tm), lambda i: (i, 0, 0))
        out_shape += [jax.ShapeDtypeStruct((n, NSA_KV_HEADS * LANES), BF16),
                      jax.ShapeDtypeStruct((nt, NSA_KV_HEADS * V_ROWS, tm), BF16),
                      jax.ShapeDtypeStruct((n, LANES), BF16),
                      jax.ShapeDtypeStruct((nt, LANES, tm), BF16)]
        out_specs += [row(NSA_KV_HEADS * LANES), tile_v, row(LANES), tile_t]
    return pl.pallas_call(
        functools.partial(_inproj_body, transposed_kv=transposed_kv),
        grid=(nt,),
        in_specs=[row(D_MODEL),
                  pl.BlockSpec((1, D_MODEL), lambda i: (0, 0)),
                  pl.BlockSpec((D_MODEL, IN_WP), lambda i: (0, 0))],
        out_specs=tuple(out_specs),
        out_shape=tuple(out_shape),
        compiler_params=_params(("arbitrary",)),
        name="inproj",
    )(x2d, ln_g, w_perm)


def _mem_kv_body(x_ref, g_ref, w_ref, o_ref):
    x = x_ref[0]
    xn = x * lax.rsqrt(jnp.mean(x * x, axis=-1, keepdims=True) + EPS) * g_ref[...]
    o_ref[0] = _dot(xn.astype(BF16), w_ref[...]).T


def _mem_kv(mem, ln_g, w_bf16):
    b = mem.shape[0]
    nw = w_bf16.shape[1]
    return pl.pallas_call(
        _mem_kv_body,
        grid=(b,),
        in_specs=[pl.BlockSpec((1, N_MEM, D_MODEL), lambda i: (i, 0, 0)),
                  pl.BlockSpec((1, D_MODEL), lambda i: (0, 0)),
                  pl.BlockSpec((D_MODEL, nw), lambda i: (0, 0))],
        out_specs=pl.BlockSpec((1, nw, N_MEM), lambda i: (i, 0, 0)),
        out_shape=jax.ShapeDtypeStruct((b, nw, N_MEM), F32),
        compiler_params=_params(("arbitrary",)),
        name="mem_kv",
    )(mem, ln_g, w_bf16)


def _compress_slab(slabs_t, x_refs, pe_ref, w1_ref, b1_ref, w2_ref, out_refs):
    rows = _iota((N_CMP, LANES), 0)
    hw = 2 * CMP_HID
    chunks_per_page = LANES // CMP_STRIDE
    for v in range(2):
        for j in range(PAST_LEN // LANES):
            xt = slabs_t[v][:, j * LANES:(j + 1) * LANES].T
            for cc in range(chunks_per_page):
                r0 = (j * chunks_per_page + cc) * X_PITCH
                x_refs[v][r0:r0 + CMP_STRIDE, :] = xt[cc * CMP_STRIDE:(cc + 1) * CMP_STRIDE, :]
        nh = N_CMP // CMP_ROW_SPLITS
        accs = []
        for hf in range(CMP_ROW_SPLITS):
            acc = jnp.zeros((nh + (PE_ROWS if hf == 0 else 0), 2 * hw), F32)
            for pp in range(CMP_STRIDE // 2):
                r0 = hf * nh * X_PITCH + 2 * pp
                xcat = jnp.concatenate([x_refs[v][pl.ds(r0, nh, stride=X_PITCH), :],
                                        x_refs[v][pl.ds(r0 + 1, nh, stride=X_PITCH), :]], axis=1).astype(BF16)
                lhs = jnp.concatenate([xcat, pe_ref[v, pp]], axis=0) if hf == 0 else xcat
                acc = acc + _dot(lhs, w1_ref[v, pp])
            accs.append(acc)
        bias = b1_ref[v] + accs[0][nh:nh + 1, 0:hw] + accs[0][nh + 1:nh + 2, hw:2 * hw]
        acc = jnp.concatenate([accs[0][0:nh]] + accs[1:], axis=0)
        nxt = pltpu.roll(acc[:, hw:2 * hw], N_CMP - 1, 0)
        hid = jax.nn.gelu(acc[:, 0:hw] + nxt + bias)
        out = _dot(hid.astype(BF16), w2_ref[v])
        out_refs[v][...] = jnp.where(rows < N_CMP - 1, out, 0.0)


def _compress_prompt_body(k_ref, v_ref, pe_ref, w1_ref, b1_ref, w2_ref, kc_ref, vc_ref, xk_ref, xv_ref):
    _compress_slab((k_ref.at[0], v_ref.at[0]), (xk_ref, xv_ref), pe_ref, w1_ref, b1_ref, w2_ref,
                   (kc_ref.at[0], vc_ref.at[0]))


PE_ROWS = 16
CMP_ROW_SPLITS = 4
X_PITCH = 24


def _x_scratch():
    return [pltpu.VMEM((N_CMP * X_PITCH, LANES), F32) for _ in range(2)]


def _comp_weight_specs():
    z = lambda *_: (0, 0, 0)
    z4 = lambda *_: (0, 0, 0, 0)
    return [pl.BlockSpec((2, CMP_STRIDE // 2, PE_ROWS, 2 * LANES), z4),
            pl.BlockSpec((2, CMP_STRIDE // 2, 2 * LANES, 4 * CMP_HID), z4),
            pl.BlockSpec((2, 1, 2 * CMP_HID), z),
            pl.BlockSpec((2, 2 * CMP_HID, LANES), z)]


def _compress_prompt(kvc_t, comp_w):
    b = kvc_t.shape[0]
    out = jax.ShapeDtypeStruct((b, N_CMP, LANES), F32)
    return pl.pallas_call(
        _compress_prompt_body,
        grid=(b,),
        in_specs=[pl.BlockSpec((1, LANES, SEQ), lambda i: (i, 0, 0)),
                  pl.BlockSpec((1, LANES, SEQ), lambda i: (i, 1, 0))] + _comp_weight_specs(),
        out_specs=(pl.BlockSpec((1, N_CMP, LANES), lambda i: (i, 0, 0)),) * 2,
        out_shape=(out, out),
        scratch_shapes=_x_scratch(),
        compiler_params=_params(("arbitrary",)),
        name="compress_prompt",
    )(kvc_t, kvc_t, *comp_w)


N_PAGES = PAST_LEN // PAGE_SIZE


def _page_copy(pt_ref, pool_ref, slab_ref, sem_ref, seq, slot, s, j):
    return pltpu.make_async_copy(pool_ref.at[pt_ref[seq, j]],
                                 slab_ref.at[slot, s, :, :, pl.ds(j * PAGE_SIZE, PAGE_SIZE)],
                                 sem_ref.at[slot])


def _start_pages(pt_ref, pool_ref, slab_ref, sem_ref, step, slot, inline=False):
    spb = slab_ref.shape[1]
    for s in range(spb):
        if inline:
            for j in range(N_PAGES):
                _page_copy(pt_ref, pool_ref, slab_ref, sem_ref, step * spb + s, slot, s, j).start()
        else:
            def body(j, c):
                _page_copy(pt_ref, pool_ref, slab_ref, sem_ref, step * spb + s, slot, s, j).start()
                return c
            lax.fori_loop(0, N_PAGES, body, 0, unroll=8)


def _wait_pages(pt_ref, pool_ref, slab_ref, sem_ref, step, slot):
    spb = slab_ref.shape[1]
    for s in range(spb):
        def body(j, c):
            _page_copy(pt_ref, pool_ref, slab_ref, sem_ref, step * spb + s, slot, s, j).wait()
            return c
        lax.fori_loop(0, N_PAGES, body, 0, unroll=8)


SLAB_SLOTS = 2


def _slab_substeps(pt_ref, pool_ref, slab_ref, sem_ref, compute):
    g = pl.program_id(0)
    n_sub = pl.num_programs(0) * SLAB_SLOTS
    args = (pt_ref, pool_ref, slab_ref, sem_ref)

    @pl.when(g == 0)
    def _():
        _start_pages(*args, 0, 0)

    for slot in range(SLAB_SLOTS):
        step = g * SLAB_SLOTS + slot
        _wait_pages(*args, step, slot)
        _start_pages(*args, jnp.minimum(step + 1, n_sub - 1), (slot + 1) % SLAB_SLOTS, inline=True)
        compute(slot)

    @pl.when(g == pl.num_programs(0) - 1)
    def _():
        _wait_pages(*args, n_sub - 1, 0)


def _stream_slab(pt_ref, pool_ref, slab_ref, sem_ref):
    b = pl.program_id(0)
    slot = b % SLAB_SLOTS
    args = (pt_ref, pool_ref, slab_ref, sem_ref)

    @pl.when(b == 0)
    def _():
        _start_pages(*args, 0, 0)

    @pl.when(b + 1 < pl.num_programs(0))
    def _():
        _start_pages(*args, b + 1, 1 - slot)

    _wait_pages(*args, b, slot)
    return slot


def _compress_sample_body(pt_ref, pool_ref, pe_ref, w1_ref, b1_ref, w2_ref, kc_ref, vc_ref, slab_ref, sem_ref,
                          xk_ref, xv_ref):
    def compute(slot):
        _compress_slab((slab_ref.at[slot, 0, 0], slab_ref.at[slot, 0, 1]), (xk_ref, xv_ref), pe_ref, w1_ref, b1_ref,
                       w2_ref, (kc_ref.at[slot], vc_ref.at[slot]))

    _slab_substeps(pt_ref, pool_ref, slab_ref, sem_ref, compute)


def _slab_scratch(seqs_per_step=1):
    return [pltpu.VMEM((2, seqs_per_step, 2, LANES, PAST_LEN), F32), pltpu.SemaphoreType.DMA((2,))]


def _compress_sample(page_table, pool, comp_w):
    nb = page_table.shape[0]
    out = jax.ShapeDtypeStruct((nb, N_CMP, LANES), F32)
    grid_spec = pltpu.PrefetchScalarGridSpec(
        num_scalar_prefetch=1,
        grid=(nb // SLAB_SLOTS,),
        in_specs=[pl.BlockSpec(memory_space=pl.ANY)] + _comp_weight_specs(),
        out_specs=(pl.BlockSpec((SLAB_SLOTS, N_CMP, LANES), lambda i, pt: (i, 0, 0)),) * 2,
        scratch_shapes=_slab_scratch() + _x_scratch(),
    )
    return pl.pallas_call(
        _compress_sample_body,
        grid_spec=grid_spec,
        out_shape=(out, out),
        compiler_params=_params(("arbitrary",)),
        name="compress_sample",
    )(page_table, pool, *comp_w)


def _top_k_rows(score, blk, k):
    cand = score > -jnp.inf
    nblk = score.shape[0]
    for _ in range(k):
        m = jnp.max(score, axis=0, keepdims=True)
        idx = jnp.min(jnp.where(score == m, blk, nblk), axis=0, keepdims=True)
        score = jnp.where(blk == idx, -jnp.inf, score)
    return cand & (score == -jnp.inf)


def _add_q_bias(s, bias):
    return jnp.concatenate([s[:, g * Q_BLOCK:(g + 1) * Q_BLOCK] + bias for g in range(NSA_GROUP)], axis=1)


def _safe_inv(l):
    return 1.0 / jnp.maximum(l, 1e-30)


KT_SLC = 512
WIN_KEYS = WINDOW + Q_BLOCK
N_FORCED = 3


def _nsa_prompt_body(qn_ref, gt_ref, ksk_ref, ksvt_ref, kwk_ref, kwvt_ref, kc_ref, vc_ref, o_ref, selb_ref, s_ref):
    i = pl.program_id(1)
    q0 = i * Q_BLOCK
    qt = (qn_ref[...].astype(F32) * (SCALE * LOG2E)).T
    t_row = q0 + _iota((1, Q_BLOCK), 1)
    kc = kc_ref[0].astype(BF16)
    vct = vc_ref[0].T.astype(BF16)
    gt = gt_ref[...]
    zeros_q = jnp.zeros((HEAD_DIM, NSA_GROUP * Q_BLOCK), F32)

    mi = _iota((N_SLC, N_CMP), 0)
    mj = _iota((N_SLC, N_CMP), 1)
    msel = ((mj >= 4 * mi - 1) & (mj <= 4 * mi + 3) & (mj < N_CMP - 1)).astype(BF16)
    blk = _iota((N_SLC, Q_BLOCK), 0)
    cur = t_row // SLC_BLOCK
    forced = (blk == 0) | (blk == cur) | (blk == cur - 1)
    free = (blk * SLC_BLOCK <= t_row) & jnp.logical_not(forced)
    cend = _iota((N_CMP, 1), 0) * CMP_STRIDE + (CMP_LEN - 1)
    bias_c = jnp.where(cend <= t_row, 0.0, NEG_INF)
    wt = jnp.maximum(i - WINDOW // Q_BLOCK, 0)
    w0 = pl.multiple_of(wt * Q_BLOCK, Q_BLOCK)
    kpw = w0 + _iota((WIN_KEYS, 1), 0)
    bias_w = jnp.where((kpw <= t_row) & (kpw > t_row - WINDOW), 0.0, NEG_INF)
    krow = _iota((SLC_BLOCK, 1), 0)

    qgs, qps, o_cs, o_ws, cand_scores = [], [], [], [], []
    vrows = [slice(kvh * HEAD_DIM, (kvh + 1) * HEAD_DIM) for kvh in range(NSA_KV_HEADS)]
    for kvh in range(NSA_KV_HEADS):
        qg = jnp.concatenate([qt[(kvh * NSA_GROUP + g) * HEAD_DIM:(kvh * NSA_GROUP + g + 1) * HEAD_DIM, :]
                              for g in range(NSA_GROUP)], axis=1)
        qp = jnp.concatenate([qg, zeros_q] if kvh == 0 else [zeros_q, qg], axis=0).astype(BF16)
        vrow = vrows[kvh]
        qgs.append(qg.astype(BF16))
        qps.append(qp)

        s = _add_q_bias(_dot(kc, qp), bias_c)
        m = jnp.max(s, axis=0, keepdims=True)
        e = jnp.exp2(s - m)
        inv = jnp.where(m > 0.5 * NEG_INF, _safe_inv(jnp.sum(e, axis=0, keepdims=True)), 0.0)
        p = e * inv
        o_c = _dot(vct[vrow, :], p.astype(BF16))
        imp = (p[:, 0:Q_BLOCK] + p[:, Q_BLOCK:2 * Q_BLOCK]
               + p[:, 2 * Q_BLOCK:3 * Q_BLOCK] + p[:, 3 * Q_BLOCK:4 * Q_BLOCK])

        cand_scores.append(jnp.where(free, _dot_exact_rhs(msel, imp), -jnp.inf))
        o_cs.append(o_c)

    blk2 = jnp.concatenate([blk] * NSA_KV_HEADS, axis=1)
    picked = _top_k_rows(jnp.concatenate(cand_scores, axis=1), blk2, SLC_TOPK - N_FORCED)
    for kvh in range(NSA_KV_HEADS):
        selb_ref[kvh] = jnp.where(forced | picked[:, kvh * Q_BLOCK:(kvh + 1) * Q_BLOCK], 0.0, NEG_INF)

    for kvh in range(NSA_KV_HEADS):
        qp, vrow = qps[kvh], vrows[kvh]
        sw = _add_q_bias(_dot(kwk_ref[pl.ds(w0, WIN_KEYS), :], qp), bias_w)
        ew = jnp.exp2(sw - jnp.max(sw, axis=0, keepdims=True))
        inv_w = _safe_inv(jnp.sum(ew, axis=0, keepdims=True))
        pw = ew.astype(BF16)
        o_w = jnp.zeros((HEAD_DIM, NSA_GROUP * Q_BLOCK), F32)
        for r in range(WIN_KEYS // Q_BLOCK):
            tix = wt + r
            vt = kwvt_ref[tix // 4, vrow, :]
            sub = tix % 4
            vpiece = jnp.where(sub == 0, vt[:, 0:128],
                               jnp.where(sub == 1, vt[:, 128:256],
                                         jnp.where(sub == 2, vt[:, 256:384], vt[:, 384:512])))
            o_w = o_w + _dot(vpiece, pw[r * Q_BLOCK:(r + 1) * Q_BLOCK, :])
        o_ws.append(o_w * inv_w)

    bias_pad = jnp.zeros((HEAD_DIM - 8, NSA_GROUP * Q_BLOCK), F32)

    def scores(j, buf):
        k0 = pl.multiple_of(j * KT_SLC, KT_SLC)
        for kvh in range(NSA_KV_HEADS):
            b8 = selb_ref[kvh, pl.ds(pl.multiple_of(j * 8, 8), 8), :]
            b_rows = jnp.concatenate([jnp.concatenate([b8] * NSA_GROUP, axis=1), bias_pad], axis=0).astype(BF16)
            q_aug = jnp.concatenate([qgs[kvh], b_rows] if kvh == 0 else [b_rows, qgs[kvh]], axis=0)
            s_ref[buf, kvh] = _dot(ksk_ref[pl.ds(k0, KT_SLC), kvh * LANES:(kvh + 1) * LANES], q_aug)

    def consume(j, buf, state, causal):
        k0 = j * KT_SLC
        out = []
        for kvh in range(NSA_KV_HEADS):
            m_run, acc = state[kvh]
            sj = s_ref[buf, kvh]
            if causal:
                parts = []
                for r in range(KT_SLC // SLC_BLOCK):
                    ok = k0 + r * SLC_BLOCK + krow <= t_row
                    parts.append(jnp.concatenate(
                        [jnp.where(ok, sj[r * SLC_BLOCK:(r + 1) * SLC_BLOCK, g * Q_BLOCK:(g + 1) * Q_BLOCK], NEG_INF)
                         for g in range(NSA_GROUP)], axis=1))
                sj = jnp.concatenate(parts, axis=0)
            m_new = jnp.maximum(m_run, jnp.max(sj, axis=0, keepdims=True))
            alpha = jnp.exp2(m_run - m_new)
            ej = jnp.exp2(sj - m_new).astype(BF16)
            acc = alpha * acc + _dot(ksvt_ref[j, kvh * V_ROWS:(kvh + 1) * V_ROWS, :], ej)
            out.append((m_new, acc))
        return tuple(out)

    init1 = (jnp.full((1, NSA_GROUP * Q_BLOCK), NEG_INF, F32),
             jnp.zeros((V_ROWS, NSA_GROUP * Q_BLOCK), F32))
    n_tiles = q0 // KT_SLC + 1
    n_pairs = (n_tiles + 1) // 2

    def pair(jj, state):
        scores(2 * jj + 1, 1)
        state = consume(2 * jj, 0, state, False)
        scores(2 * jj + 2, 0)
        return consume(2 * jj + 1, 1, state, False)

    scores(0, 0)
    state = lax.fori_loop(0, n_pairs - 1, pair, (init1,) * NSA_KV_HEADS)
    last = 2 * n_pairs - 2

    def last_two(st):
        scores(last + 1, 1)
        return consume(last + 1, 1, consume(last, 0, st, True), True)

    state = lax.cond(n_tiles % 2 == 0, last_two, lambda st: consume(last, 0, st, True), state)

    outs = []
    for kvh in range(NSA_KV_HEADS):
        acc_s = state[kvh][1]
        o_s = acc_s[0:HEAD_DIM, :] * _safe_inv(acc_s[HEAD_DIM:HEAD_DIM + 1, :])

        def gate(br):
            base = br * NSA_HEADS + kvh * NSA_GROUP
            return jnp.concatenate([gt[base + g:base + g + 1, :] for g in range(NSA_GROUP)], axis=1)

        o = gate(0) * o_cs[kvh] + gate(1) * o_s + gate(2) * o_ws[kvh]
        outs.extend([o[:, g * Q_BLOCK:(g + 1) * Q_BLOCK] for g in range(NSA_GROUP)])
    o_ref[...] = jnp.concatenate(outs, axis=0).T


def _nsa_prompt(qn, gt, ksk, ksvt, kwk, kwvt, kc, vc, batch):
    nqb = SEQ // Q_BLOCK
    ntile = SEQ // 512
    return pl.pallas_call(
        _nsa_prompt_body,
        grid=(batch, nqb),
        in_specs=[pl.BlockSpec((Q_BLOCK, 512), lambda b, i: (b * nqb + i, 0)),
                  pl.BlockSpec((LANES, Q_BLOCK), lambda b, i: (0, b * nqb + i)),
                  pl.BlockSpec((SEQ, NSA_KV_HEADS * LANES), lambda b, i: (b, 0)),
                  pl.BlockSpec((ntile, NSA_KV_HEADS * V_ROWS, 512), lambda b, i: (b, 0, 0)),
                  pl.BlockSpec((SEQ, LANES), lambda b, i: (b, 0)),
                  pl.BlockSpec((ntile, LANES, 512), lambda b, i: (b, 0, 0)),
                  pl.BlockSpec((1, N_CMP, LANES), lambda b, i: (b, 0, 0)),
                  pl.BlockSpec((1, N_CMP, LANES), lambda b, i: (b, 0, 0))],
        out_specs=pl.BlockSpec((Q_BLOCK, 512), lambda b, i: (b * nqb + i, 0)),
        out_shape=jax.ShapeDtypeStruct((batch * SEQ, 512), F32),
        scratch_shapes=[pltpu.VMEM((NSA_KV_HEADS, N_SLC, Q_BLOCK), F32),
                        pltpu.VMEM((2, NSA_KV_HEADS, KT_SLC, NSA_GROUP * Q_BLOCK), F32)],
        compiler_params=_params(("arbitrary", "arbitrary")),
        name="nsa_prompt",
    )(qn, gt, ksk, ksvt, kwk, kwvt, kc, vc)


def _softmax_parts(parts, masks):
    m = None
    for s, mk in zip(parts, masks):
        mi = jnp.max(jnp.where(mk, s, NEG_INF), axis=-1, keepdims=True)
        m = mi if m is None else jnp.maximum(m, mi)
    es = [jnp.where(mk, jnp.exp(jnp.where(mk, s, NEG_INF) - m), 0.0) for s, mk in zip(parts, masks)]
    l = None
    for e in es:
        li = jnp.sum(e, axis=-1, keepdims=True)
        l = li if l is None else l + li
    inv = 1.0 / jnp.maximum(l, 1e-30)
    return [e * inv for e in es]


KC_SLC = 2048
ROWS_S = NSA_HEADS * DEC_SEQ


def _pad_rows(x, n):
    if x.shape[0] == n:
        return x
    return jnp.concatenate([x, jnp.zeros((n - x.shape[0], x.shape[1]), x.dtype)], axis=0)


def _nsa_sample_body(pt_ref, pool_ref, qp_ref, g_ref, kc_ref, vc_ref, kvs_ref, kvw_ref, win_ref, wtail_ref,
                     expand_ref, o_ref, nwin_ref, slab_ref, sem_ref):
    slot = _stream_slab(pt_ref, pool_ref, slab_ref, sem_ref)
    spb = slab_ref.shape[1]
    seqs = range(spb)
    bi = list(seqs)
    qi = _iota((ROWS_S, 1), 0) % DEC_SEQ
    t = PAST_LEN + qi
    jn = _iota((1, LANES), 1)
    n_chunk = PAST_LEN // KC_SLC
    qs = [(qp_ref[bi[s]] * SCALE).astype(BF16) for s in seqs]
    new_rows = [slice(bi[s] * DEC_SEQ, (bi[s] + 1) * DEC_SEQ) for s in seqs]

    cend = _iota((1, N_CMP), 1) * CMP_STRIDE + (CMP_LEN - 1)
    p_cs = [_softmax_parts([_dot_nt(qs[s], kc_ref[bi[s]].astype(BF16))], [cend <= t])[0] for s in seqs]
    o_cs = [_dot(p_cs[s].astype(BF16), vc_ref[bi[s]].astype(BF16)) for s in seqs]

    half = ROWS_S // NSA_KV_HEADS
    imp = jnp.concatenate(
        [p_cs[s][k * half:k * half + 8] + p_cs[s][k * half + 8:k * half + 16]
         + p_cs[s][k * half + 16:k * half + 24] + p_cs[s][k * half + 24:k * half + 32]
         for s in seqs for k in range(NSA_KV_HEADS)], axis=0)
    mi = _iota((N_CMP, N_SLC), 0)
    mj = _iota((N_CMP, N_SLC), 1)
    msel_t = ((mi >= 4 * mj - 1) & (mi <= 4 * mj + 3) & (mi < N_CMP - 1)).astype(BF16)
    score = _dot_exact_lhs(imp, msel_t)
    score_t = _pad_rows(score, LANES).T
    blk = _iota((N_SLC, LANES), 0)
    forced = (blk == 0) | (blk == N_SLC - 1)
    picked = _top_k_rows(jnp.where(forced, -jnp.inf, score_t), blk, SLC_TOPK - 1 - 2)
    unsel = jnp.where(forced | picked, 0.0, 1.0).T
    unsel64 = [jnp.concatenate([unsel[s * 16:s * 16 + 8]] * NSA_GROUP + [unsel[s * 16 + 8:s * 16 + 16]] * NSA_GROUP,
                               axis=0).astype(BF16) for s in seqs]

    new_bias = jnp.where((jn <= qi) & (jn < DEC_SEQ), 0.0, NEG_INF)
    k_news = [_pad_rows(kvs_ref[new_rows[s], 0:LANES], LANES).astype(BF16) for s in seqs]
    v_news = [_pad_rows(kvs_ref[new_rows[s], LANES:2 * LANES], LANES).astype(BF16) for s in seqs]
    s_news = [_dot_nt(qs[s], k_news[s]) + new_bias for s in seqs]
    scs = [[_dot(qs[s], slab_ref[slot, s, 0, :, c * KC_SLC:(c + 1) * KC_SLC].astype(BF16))
            + _dot(unsel64[s], expand_ref[:, c * KC_SLC:(c + 1) * KC_SLC]) * NEG_INF
            for c in range(n_chunk)] for s in seqs]
    o_ss = []
    for s in seqs:
        m_s = jnp.max(s_news[s], axis=-1, keepdims=True)
        for sc in scs[s]:
            m_s = jnp.maximum(m_s, jnp.max(sc, axis=-1, keepdims=True))
        e_new = jnp.exp(s_news[s] - m_s)
        l_s = jnp.sum(e_new, axis=-1, keepdims=True)
        acc = _dot(e_new.astype(BF16), v_news[s])
        for c in range(n_chunk):
            ec = jnp.exp(scs[s][c] - m_s)
            l_s = l_s + jnp.sum(ec, axis=-1, keepdims=True)
            acc = acc + _dot_nt(ec.astype(BF16), slab_ref[slot, s, 1, :, c * KC_SLC:(c + 1) * KC_SLC].astype(BF16))
        o_ss.append(acc * _safe_inv(l_s))

    wb = win_ref.shape[2]
    kpos_b = PAST_LEN - wb + _iota((1, wb), 1)
    kpos_n = PAST_LEN + jn
    mask_b = (kpos_b <= t) & (kpos_b > t - WINDOW) & (kpos_b >= 0)
    mask_n = (kpos_n <= t) & (kpos_n > t - WINDOW) & (jn < DEC_SEQ)
    for s in seqs:
        win_t = win_ref[bi[s]]
        kw_new = _pad_rows(kvw_ref[new_rows[s], 0:LANES], LANES).astype(BF16)
        vw_new = _pad_rows(kvw_ref[new_rows[s], LANES:2 * LANES], LANES).astype(BF16)
        p_b, p_n = _softmax_parts([_dot(qs[s], win_t[0:LANES, :].astype(BF16)), _dot_nt(qs[s], kw_new)],
                                  [mask_b, mask_n])
        o_w = _dot_nt(p_b.astype(BF16), win_t[LANES:2 * LANES, :].astype(BF16)) + _dot(p_n.astype(BF16), vw_new)

        g = g_ref[bi[s]]
        o_ref[bi[s]] = g[:, 0:1] * o_cs[s] + g[:, 1:2] * o_ss[s] + g[:, 2:3] * o_w
        shifted = pltpu.roll(win_t, wb - DEC_SEQ, 1)
        nwin_ref[bi[s], :, 0:wb - LANES] = shifted[:, 0:wb - LANES]
        nwin_ref[bi[s], :, wb - LANES:wb] = jnp.where(jn >= LANES - DEC_SEQ, wtail_ref[bi[s]],
                                                      shifted[:, wb - LANES:wb])


NSA_SAMPLE_SEQS_PER_STEP = 2


def _nsa_sample(page_table, pool_slc, qp, gates, kc, vc, kvs, kvw, win_t, wtail, expand):
    nb = page_table.shape[0]
    wb = win_t.shape[2]
    sps = spb = NSA_SAMPLE_SEQS_PER_STEP
    per_b = lambda *shape: pl.BlockSpec((sps,) + shape, lambda i, pt: (i,) + (0,) * len(shape))
    rows8 = pl.BlockSpec((sps * DEC_SEQ, 256), lambda i, pt: (i, 0))
    grid_spec = pltpu.PrefetchScalarGridSpec(
        num_scalar_prefetch=1,
        grid=(nb // sps,),
        in_specs=[pl.BlockSpec(memory_space=pl.ANY),
                  per_b(ROWS_S, LANES), per_b(ROWS_S, LANES), per_b(N_CMP, LANES), per_b(N_CMP, LANES),
                  rows8, rows8, per_b(256, wb), per_b(256, LANES),
                  pl.BlockSpec((N_SLC, PAST_LEN), lambda i, pt: (0, 0))],
        out_specs=(per_b(ROWS_S, LANES), per_b(256, wb)),
        scratch_shapes=_slab_scratch(spb),
    )
    return pl.pallas_call(
        _nsa_sample_body,
        grid_spec=grid_spec,
        out_shape=(jax.ShapeDtypeStruct((nb, ROWS_S, LANES), F32),
                   jax.ShapeDtypeStruct((nb, 256, wb), F32)),
        compiler_params=_params(("arbitrary",)),
        name="nsa_sample",
    )(page_table, pool_slc, qp, gates, kc, vc, kvs, kvw, win_t, wtail, expand)


RET_W = RET_HEADS * HEAD_DIM


def _retention_body(q_ref, k_ref, v_ref, cos_ref, sin_ref, st0_ref, gn_ref, o_ref, st_ref, state_ref, *, chunk):
    c = pl.program_id(1)
    ck = max(chunk, LANES)
    spb = st0_ref.shape[0]

    lane = _iota((1, RET_W), 1)
    head_of_lane = lane // HEAD_DIM
    row_head = _iota((RET_W, 1), 0) // HEAD_DIM

    @pl.when(c == 0)
    def _():
        for bb in range(spb):
            st_c = st0_ref[bb]
            state_ref[bb] = jnp.concatenate([jnp.where(row_head == h, st_c, 0.0) for h in range(RET_HEADS)], axis=1)
    low_half = (lane % HEAD_DIM) < HEAD_DIM // 2
    cos = cos_ref[...]
    sin = sin_ref[...]

    def rope(x):
        rot = jnp.where(low_half, -pltpu.roll(x, RET_W - HEAD_DIM // 2, 1), pltpu.roll(x, HEAD_DIM // 2, 1))
        return x * cos + rot * sin

    logg_lane = jnp.zeros((1, RET_W), F32)
    for h in range(RET_HEADS):
        logg_lane = jnp.where(head_of_lane == h, _LOG_G[h], logg_lane)

    n = _iota((chunk, 1), 0).astype(F32)
    nk = _iota((ck, 1), 0).astype(F32)
    diff = n - _iota((1, ck), 1).astype(F32)
    in_chunk = _iota((1, ck), 1) < chunk
    dmats = [jnp.where((diff >= 0) & in_chunk, jnp.exp(_LOG_G[h] * jnp.maximum(diff, 0.0)), 0.0)
             for h in range(RET_HEADS)]
    xi = jnp.exp(logg_lane * (n + 1.0))
    zeta = jnp.where(nk < chunk, jnp.exp(logg_lane * (chunk - 1.0 - nk)), 0.0)
    decay_rows = jnp.zeros((RET_W, 1), F32)
    for h in range(RET_HEADS):
        decay_rows = jnp.where(row_head == h, float(np.exp(_LOG_G[h] * chunk)), decay_rows)
    avg = jnp.where(_iota((RET_W, RET_W), 0) // HEAD_DIM == _iota((RET_W, RET_W), 1) // HEAD_DIM,
                    1.0 / HEAD_DIM, 0.0).astype(BF16)
    gn = gn_ref[...]

    seqs = range(spb)
    rows = [slice(bb * chunk, (bb + 1) * chunk) for bb in seqs]
    qs = [rope(q_ref[r, :]) * SCALE for r in rows]
    kps = [_pad_rows(rope(k_ref[r, :]), ck) for r in rows]
    vps = [_pad_rows(v_ref[r, :], ck) for r in rows]
    hms = [head_of_lane == h for h in range(RET_HEADS)]
    ss = [[_dot_nt(jnp.where(hms[h], qs[bb], 0.0), kps[bb]) * dmats[h] for h in range(RET_HEADS)] for bb in seqs]
    inners = []
    for bb in seqs:
        inner = jnp.zeros((chunk, RET_W), F32)
        for h in range(RET_HEADS):
            inner = inner + jnp.where(hms[h], _dot(ss[bb][h], vps[bb]), 0.0)
        inners.append(inner)
    states = [state_ref[bb] for bb in seqs]
    os_ = [inners[bb] + _dot(qs[bb] * xi, states[bb]) for bb in seqs]
    for bb in seqs:
        kz_t = (kps[bb] * zeta).T
        new_state = decay_rows * states[bb] + jnp.where(row_head == head_of_lane, _dot(kz_t, vps[bb]), 0.0)
        state_ref[bb] = new_state
        st_ref[bb] = (new_state[:, 0:HEAD_DIM] + new_state[:, HEAD_DIM:2 * HEAD_DIM]
                      + new_state[:, 2 * HEAD_DIM:3 * HEAD_DIM] + new_state[:, 3 * HEAD_DIM:4 * HEAD_DIM])

    o = jnp.concatenate(os_, axis=0) if spb > 1 else os_[0]
    mu = _dot_exact_lhs(o, avg)
    d = o - mu
    var = _dot_exact_lhs(d * d, avg)
    o_ref[...] = d * lax.rsqrt(var + EPS) * gn


def _retention(qkvr, cos, sin, state, gn, batch, t_len, chunk, seqs_per_step=1):
    nch = t_len // chunk
    spb = seqs_per_step
    assert spb == 1 or nch == 1
    col = lambda j: pl.BlockSpec((spb * chunk, RET_W), lambda b, c: (b * nch + c, j))
    tab = pl.BlockSpec((chunk, RET_W), lambda b, c: (c, 0))
    st = pl.BlockSpec((spb, RET_W, HEAD_DIM), lambda b, c: (b, 0, 0))
    return pl.pallas_call(
        functools.partial(_retention_body, chunk=chunk),
        grid=(batch // spb, nch),
        in_specs=[col(0), col(1), col(2), tab, tab, st, pl.BlockSpec((1, RET_W), lambda b, c: (0, 0))],
        out_specs=(pl.BlockSpec((spb * chunk, RET_W), lambda b, c: (b * nch + c, 0)), st),
        out_shape=(jax.ShapeDtypeStruct((batch * t_len, RET_W), F32),
                   jax.ShapeDtypeStruct((batch, RET_W, HEAD_DIM), F32)),
        scratch_shapes=[pltpu.VMEM((spb, RET_W, RET_W), F32)],
        compiler_params=_params(("arbitrary", "arbitrary")),
        name="retention",
    )(qkvr, qkvr, qkvr, cos, sin, state, gn)


MEM_W = 4 * HEAD_DIM


def _mem_attn_body(q_ref, mkv_ref, o_ref, *, tm):
    head_of_lane = _iota((1, MEM_W), 1) // HEAD_DIM
    spb = mkv_ref.shape[0]
    q_all = q_ref[...] * SCALE
    scores = []
    for bb in range(spb):
        q = q_all[bb * tm:(bb + 1) * tm, :]
        q4 = jnp.concatenate([jnp.where(head_of_lane == h, q, 0.0) for h in range(4)], axis=0).astype(BF16)
        scores.append(_dot(q4, mkv_ref[bb, 0:MEM_W, :].astype(BF16)))
    s = jnp.concatenate(scores, axis=0)
    e = jnp.exp(s - jnp.max(s, axis=-1, keepdims=True))
    p = (e / jnp.sum(e, axis=-1, keepdims=True)).astype(BF16)
    for bb in range(spb):
        pv = _dot_nt(p[bb * 4 * tm:(bb + 1) * 4 * tm, :], mkv_ref[bb, MEM_W:2 * MEM_W, :].astype(BF16))
        o = jnp.zeros((tm, MEM_W), F32)
        for h in range(4):
            o = o + jnp.where(head_of_lane == h, pv[h * tm:(h + 1) * tm, :], 0.0)
        o_ref[bb * tm:(bb + 1) * tm, :] = o


def _mem_attn(qm, mkv_t, batch, rows_per_batch, tm, seqs_per_step=1):
    nt = rows_per_batch // tm
    spb = seqs_per_step
    assert spb == 1 or nt == 1
    return pl.pallas_call(
        functools.partial(_mem_attn_body, tm=tm),
        grid=(batch // spb, nt),
        in_specs=[pl.BlockSpec((spb * tm, MEM_W), lambda b, i: (b * nt + i, 0)),
                  pl.BlockSpec((spb, 2 * MEM_W, N_MEM), lambda b, i: (b, 0, 0))],
        out_specs=pl.BlockSpec((spb * tm, MEM_W), lambda b, i: (b * nt + i, 0)),
        out_shape=jax.ShapeDtypeStruct((batch * rows_per_batch, MEM_W), F32),
        compiler_params=_params(("arbitrary", "arbitrary")),
        name="mem_attn",
    )(qm, mkv_t)


def _finish_body(x_ref, on_ref, or_ref, om_ref, z_ref, w_ref, g_ref, y_ref):
    o = jnp.concatenate([on_ref[...], or_ref[...], om_ref[...]], axis=-1)
    z = z_ref[...]
    mix = o * (z * jax.nn.sigmoid(z))
    xo = x_ref[...] + _dot(mix.astype(BF16), w_ref[...])
    y_ref[...] = xo * lax.rsqrt(jnp.mean(xo * xo, axis=-1, keepdims=True) + EPS) * g_ref[...]


def _finish(x2d, o_n, o_r, o_m, z, w_out, ln_final):
    n = x2d.shape[0]
    tm = 512
    row = lambda w: pl.BlockSpec((tm, w), lambda i: (i, 0))
    return pl.pallas_call(
        _finish_body,
        grid=(n // tm,),
        in_specs=[row(D_MODEL), row(512), row(256), row(256), row(D_MODEL),
                  pl.BlockSpec((D_MODEL, D_MODEL), lambda i: (0, 0)),
                  pl.BlockSpec((1, D_MODEL), lambda i: (0, 0))],
        out_specs=row(D_MODEL),
        out_shape=jax.ShapeDtypeStruct((n, D_MODEL), F32),
        compiler_params=_params(("arbitrary",)),
        name="finish",
    )(x2d, o_n, o_r, o_m, z, w_out, ln_final)


def _permute_w_in(w):
    sizes = (512, 256, 256, 256, 24, 512, 256, 256, 256, 256, 256, 256)
    offs = np.concatenate([[0], np.cumsum(sizes)])
    part = lambda i: w[:, int(offs[i]):int(offs[i + 1])]
    q_n, kv_c, kv_s, kv_w, g_n, z_n, q_r, k_r, v_r, z_r, q_m, z_m = [part(i) for i in range(12)]
    g_pad = jnp.pad(g_n, ((0, 0), (0, LANES - g_n.shape[1])))
    return jnp.concatenate([q_n, kv_c, kv_s, kv_w, q_r, k_r, v_r, q_m, z_n, z_r, z_m, g_pad], axis=1).astype(BF16)


def _compress_weights(pe, w1, b1, w2):
    npair = CMP_STRIDE // 2
    pe_r = pe.reshape(2, 2, npair, 2, HEAD_DIM)
    pe_l = jnp.broadcast_to(pe_r[:, :, :, :, None, :], (2, 2, npair, 2, NSA_KV_HEADS, HEAD_DIM))
    pe_l = pe_l.reshape(2, 2, npair, 2 * LANES).transpose(0, 2, 1, 3)
    pe_l = jnp.pad(pe_l, ((0, 0), (0, 0), (0, PE_ROWS - 2), (0, 0))).astype(BF16)
    w1_r = w1.reshape(2, 2, CMP_STRIDE, HEAD_DIM, CMP_HID)
    zw = jnp.zeros_like(w1_r)
    w1_bd = jnp.concatenate([jnp.concatenate([w1_r, zw], axis=-1),
                             jnp.concatenate([zw, w1_r], axis=-1)], axis=-2)
    w1_bd = w1_bd.reshape(2, 2, npair, 2 * LANES, 2 * CMP_HID)
    w1_bd = jnp.concatenate([w1_bd[:, 0], w1_bd[:, 1]], axis=-1).astype(BF16)
    b1_l = jnp.concatenate([b1, b1], axis=-1).reshape(2, 1, 2 * CMP_HID)
    z2 = jnp.zeros_like(w2)
    w2_bd = jnp.concatenate([jnp.concatenate([w2, z2], axis=-1),
                             jnp.concatenate([z2, w2], axis=-1)], axis=-2).astype(BF16)
    return pe_l, w1_bd, b1_l, w2_bd


def _rope_tables(pos):
    half = HEAD_DIM // 2
    inv = ROPE_BASE ** (-jnp.arange(half, dtype=F32) / half)
    ang = pos.astype(F32)[:, None] * inv[None, :]
    cos, sin = jnp.cos(ang), jnp.sin(ang)
    cos_l = jnp.tile(jnp.concatenate([cos, cos], axis=-1), (1, RET_HEADS))
    sin_l = jnp.tile(jnp.concatenate([sin, sin], axis=-1), (1, RET_HEADS))
    return cos_l, sin_l


def _kv_shape(a, b, t):
    return a.reshape(1, b, t, 2, NSA_KV_HEADS, HEAD_DIM)


def _position_minor(cache, heads):
    b, l = cache.shape[:2]
    return jnp.transpose(cache, (0, 2, 3, 4, 1)).reshape(b, 2 * heads * HEAD_DIM, l)


def _from_position_minor(a, heads):
    b, _, l = a.shape
    return jnp.transpose(a.reshape(b, 2, heads, HEAD_DIM, l), (0, 4, 1, 2, 3))[None]


def kernel(x_prompt, mem_prompt, x_sample, cache_nsa_cmp, cache_nsa_slc, cache_nsa_win, state_ret, cache_mem,
           page_table, ln_mix, w_in, cmp_pe, cmp_w1, cmp_b1, cmp_w2, ret_gn, ln_mem, w_mem_kv, w_out, ln_final):
    bp, t_len = x_prompt.shape[:2]
    bs, s_len = x_sample.shape[:2]
    assert (t_len, s_len) == (SEQ, DEC_SEQ) and ln_mix.shape[0] == 1
    w_perm = _permute_w_in(w_in[0])
    comp_w = _compress_weights(cmp_pe[0], cmp_w1[0], cmp_b1[0], cmp_w2[0])
    ln_g = ln_mix[0].reshape(1, D_MODEL)
    gn = ret_gn[0].reshape(1, RET_W)
    w_out_b = w_out[0].astype(BF16)
    ln_f = ln_final.reshape(1, D_MODEL)

    xp2 = x_prompt.reshape(bp * SEQ, D_MODEL)
    (qn, kvc_t, kvs_t, kvw_t, qkvr, qm, z, gt, ksk, ksvt, kwk, kwvt) = _inproj(xp2, ln_g, w_perm, SEQ, True)
    kc, vc = _compress_prompt(kvc_t, comp_w)
    o_n = _nsa_prompt(qn, gt, ksk, ksvt, kwk, kwvt, kc, vc, bp)
    cos_p, sin_p = _rope_tables(jnp.arange(SEQ))
    o_r, st_p = _retention(qkvr, cos_p, sin_p, jnp.zeros((bp, RET_W, HEAD_DIM), F32), gn, bp, SEQ, 256)
    mkv_t = _mem_kv(mem_prompt, ln_mem[0].reshape(1, D_MODEL), w_mem_kv[0].astype(BF16))
    o_m = _mem_attn(qm, mkv_t, bp, SEQ, 512)
    y_prompt = _finish(xp2, o_n, o_r, o_m, z, w_out_b, ln_f).reshape(bp, SEQ, D_MODEL)
    new_cmp_p = _from_position_minor(kvc_t, NSA_KV_HEADS)
    new_slc_p = _from_position_minor(kvs_t, NSA_KV_HEADS)
    new_win_p = _from_position_minor(kvw_t[:, :, SEQ - WINDOW:], NSA_KV_HEADS)
    new_ret_p = st_p.reshape(1, bp, RET_HEADS, HEAD_DIM, HEAD_DIM)
    new_mem_p = _from_position_minor(mkv_t, 4)

    xs2 = x_sample.reshape(bs * DEC_SEQ, D_MODEL)
    (qn_s, kvc_s, kvs_s, kvw_s, qkvr_s, qm_s, z_s, gt_s) = _inproj(xs2, ln_g, w_perm, DEC_SEQ, False)
    pool_cmp = _position_minor(cache_nsa_cmp[0], NSA_KV_HEADS).reshape(-1, 2, LANES, PAGE_SIZE)
    pool_slc = _position_minor(cache_nsa_slc[0], NSA_KV_HEADS).reshape(-1, 2, LANES, PAGE_SIZE)
    kc_s, vc_s = _compress_sample(page_table, pool_cmp, comp_w)
    q5 = qn_s.astype(F32).reshape(bs, DEC_SEQ, NSA_KV_HEADS, NSA_GROUP, HEAD_DIM).transpose(0, 2, 3, 1, 4)
    zq = jnp.zeros_like(q5[:, 0])
    qp = jnp.stack([jnp.concatenate([q5[:, 0], zq], axis=-1), jnp.concatenate([zq, q5[:, 1]], axis=-1)], axis=1)
    qp = qp.reshape(bs, ROWS_S, LANES)
    g5 = gt_s[:3 * NSA_HEADS].reshape(3, NSA_KV_HEADS, NSA_GROUP, bs, DEC_SEQ).transpose(3, 1, 2, 4, 0)
    gates = jnp.pad(g5.reshape(bs, ROWS_S, 3), ((0, 0), (0, 0), (0, LANES - 3)))
    expand = jnp.asarray(np.repeat(np.eye(N_SLC, dtype=np.float32), SLC_BLOCK, axis=1), dtype=BF16)
    win_t = _position_minor(cache_nsa_win[0], NSA_KV_HEADS)
    wtail = jnp.pad(kvw_s.reshape(bs, DEC_SEQ, 256).transpose(0, 2, 1), ((0, 0), (0, 0), (LANES - DEC_SEQ, 0)))
    o_sn, new_win_t = _nsa_sample(page_table, pool_slc, qp, gates, kc_s, vc_s, kvs_s, kvw_s, win_t, wtail, expand)
    o6 = o_sn.reshape(bs, NSA_KV_HEADS, NSA_GROUP, DEC_SEQ, NSA_KV_HEADS, HEAD_DIM)
    o_n_s = jnp.stack([o6[:, 0, :, :, 0], o6[:, 1, :, :, 1]], axis=1)
    o_n_s = o_n_s.transpose(0, 3, 1, 2, 4).reshape(bs * DEC_SEQ, 512)
    cos_s, sin_s = _rope_tables(PAST_LEN + jnp.arange(DEC_SEQ))
    o_r_s, st_s = _retention(qkvr_s, cos_s, sin_s, state_ret[0].reshape(bs, RET_W, HEAD_DIM), gn, bs, DEC_SEQ, DEC_SEQ,
                             seqs_per_step=SAMPLE_SEQS_PER_STEP)
    o_m_s = _mem_attn(qm_s, _position_minor(cache_mem[0], 4), bs, DEC_SEQ, DEC_SEQ,
                      seqs_per_step=SAMPLE_SEQS_PER_STEP)
    y_sample = _finish(xs2, o_n_s, o_r_s, o_m_s, z_s, w_out_b, ln_f).reshape(bs, DEC_SEQ, D_MODEL)
    new_cmp_s = _kv_shape(kvc_s, bs, DEC_SEQ)
    new_slc_s = _kv_shape(kvs_s, bs, DEC_SEQ)
    new_win_s = _from_position_minor(new_win_t, NSA_KV_HEADS)
    new_ret_s = st_s.reshape(1, bs, RET_HEADS, HEAD_DIM, HEAD_DIM)

    return (y_prompt, y_sample, new_cmp_p, new_cmp_s, new_slc_p, new_slc_s, new_win_p, new_win_s,
            new_ret_p, new_ret_s, new_mem_p)
```

```python
import functools

import numpy as np
import jax
import jax.numpy as jnp
from jax import lax
from jax.experimental import pallas as pl
from jax.experimental.pallas import tpu as pltpu

D_MODEL = 1024
SEQ = 8192
DEC_SEQ = 8
PAST_LEN = 8192
PAGE_SIZE = 128
HEAD_DIM = 64
NSA_HEADS = 8
NSA_KV_HEADS = 2
NSA_GROUP = 4
CMP_LEN = 32
CMP_STRIDE = 16
CMP_HID = 128
SLC_BLOCK = 64
SLC_TOPK = 16
WINDOW = 512
RET_HEADS = 4
N_MEM = 256
Q_BLOCK = 128
ROPE_BASE = 10000.0
EPS = 1e-6
NEG_INF = -1e30
SCALE = HEAD_DIM ** -0.5

N_CMP = 512
N_SLC = 128
LANES = 128
ONES_ROWS = 16
V_ROWS = HEAD_DIM + ONES_ROWS
LOG2E = 1.4426950408889634
SAMPLE_SEQS_PER_STEP = 8
VMEM_LIMIT = 56 * 1024 * 1024

_W_Q, _W_KVC, _W_KVS, _W_KVW, _W_QKVR, _W_QM, _W_Z, _W_G = 0, 512, 768, 1024, 1280, 2048, 2304, 3328
IN_WP = 3456

F32 = jnp.float32
BF16 = jnp.bfloat16

_LOG_G = [float(np.log1p(-(2.0 ** (-5.0 - h)))) for h in range(RET_HEADS)]


def _dot(a, b):
    return jnp.dot(a, b, preferred_element_type=F32)


def _dot_nt(a, b):
    return lax.dot_general(a, b, (((1,), (1,)), ((), ())), preferred_element_type=F32)


def _split3(x):
    hi = x.astype(BF16)
    r1 = x - hi.astype(F32)
    mid = r1.astype(BF16)
    lo = (r1 - mid.astype(F32)).astype(BF16)
    return hi, mid, lo


def _dot_exact_rhs(a_bf16, x):
    hi, mid, lo = _split3(x)
    return _dot(a_bf16, hi) + _dot(a_bf16, mid) + _dot(a_bf16, lo)


def _dot_exact_lhs(x, b_bf16):
    hi, mid, lo = _split3(x)
    return _dot(hi, b_bf16) + _dot(mid, b_bf16) + _dot(lo, b_bf16)


def _iota(shape, dim):
    return lax.broadcasted_iota(jnp.int32, shape, dim)


def _params(sem):
    return pltpu.CompilerParams(dimension_semantics=sem, vmem_limit_bytes=VMEM_LIMIT)


def _inproj_body(x_ref, g_ref, w_ref, qn_ref, kvc_ref, kvs_ref, kvw_ref, qkvr_ref, qm_ref, z_ref, gt_ref,
                 *attn_refs, transposed_kv):
    x = x_ref[...]
    xn = x * lax.rsqrt(jnp.mean(x * x, axis=-1, keepdims=True) + EPS) * g_ref[...]
    xb = xn.astype(BF16)

    def proj(a, b):
        return _dot(xb, w_ref[:, a:b])

    qn_ref[...] = proj(_W_Q, _W_KVC).astype(BF16)
    kvc = proj(_W_KVC, _W_KVS)
    kvs = proj(_W_KVS, _W_KVW)
    kvw = proj(_W_KVW, _W_QKVR)
    if transposed_kv:
        ksk_ref, ksvt_ref, kwk_ref, kwvt_ref = attn_refs
        kvc_ref[0] = kvc.T
        kvs_t = kvs.T
        kvs_ref[0] = kvs_t
        tm = kvs.shape[0]
        lane = _iota((tm, LANES), 1)
        blk_in_tile = _iota((tm, LANES), 0) // SLC_BLOCK
        k2 = kvs[:, :LANES]
        ksk_ref[:, 0:LANES] = jnp.where(lane < HEAD_DIM, k2,
                                        jnp.where(lane - HEAD_DIM == blk_in_tile, 1.0, 0.0)).astype(BF16)
        ksk_ref[:, LANES:2 * LANES] = jnp.where(lane >= HEAD_DIM, k2,
                                                jnp.where(lane == blk_in_tile, 1.0, 0.0)).astype(BF16)
        ones_rows = jnp.where(_iota((ONES_ROWS, tm), 0) == 0, 1.0, 0.0)
        ksvt_ref[0] = jnp.concatenate([kvs_t[LANES:LANES + HEAD_DIM, :], ones_rows,
                                       kvs_t[LANES + HEAD_DIM:, :], ones_rows], axis=0).astype(BF16)
        kvw_t = kvw.T
        kvw_ref[0] = kvw_t
        kwk_ref[...] = kvw[:, :LANES].astype(BF16)
        kwvt_ref[0] = kvw_t[LANES:, :].astype(BF16)
    else:
        kvc_ref[...] = kvc
        kvs_ref[...] = kvs
        kvw_ref[...] = kvw
    qkvr_ref[...] = proj(_W_QKVR, _W_QM)
    qm_ref[...] = proj(_W_QM, _W_Z)
    z_ref[...] = proj(_W_Z, _W_G)
    gt_ref[...] = jax.nn.sigmoid(proj(_W_G, IN_WP)).T


def _inproj(x2d, ln_g, w_perm, rows_per_batch, transposed_kv):
    n = x2d.shape[0]
    tm = 512
    nt = n // tm
    row = lambda w: pl.BlockSpec((tm, w), lambda i: (i, 0))
    if transposed_kv:
        tpb = rows_per_batch // tm
        kv_shape = jax.ShapeDtypeStruct((n // rows_per_batch, 256, rows_per_batch), F32)
        kv_spec = pl.BlockSpec((1, 256, tm), lambda i: (i // tpb, 0, i % tpb))
    else:
        kv_shape = jax.ShapeDtypeStruct((n, 256), F32)
        kv_spec = row(256)
    out_shape = [
        jax.ShapeDtypeStruct((n, 512), BF16),
        kv_shape, kv_shape, kv_shape,
        jax.ShapeDtypeStruct((n, 768), F32),
        jax.ShapeDtypeStruct((n, 256), F32),
        jax.ShapeDtypeStruct((n, 1024), F32),
        jax.ShapeDtypeStruct((LANES, n), F32),
    ]
    out_specs = [row(512), kv_spec, kv_spec, kv_spec, row(768), row(256), row(1024),
                 pl.BlockSpec((LANES, tm), lambda i: (0, i))]
    if transposed_kv:
        tile_t = pl.BlockSpec((1, LANES, tm), lambda i: (i, 0, 0))
        tile_v = pl.BlockSpec((1, NSA_KV_HEADS * V_ROWS, tm), lambda i: (i, 0, 0))
        out_shape += [jax.ShapeDtypeStruct((n, NSA_KV_HEADS * LANES), BF16),
                      jax.ShapeDtypeStruct((nt, NSA_KV_HEADS * V_ROWS, tm), BF16),
                      jax.ShapeDtypeStruct((n, LANES), BF16),
                      jax.ShapeDtypeStruct((nt, LANES, tm), BF16)]
        out_specs += [row(NSA_KV_HEADS * LANES), tile_v, row(LANES), tile_t]
    return pl.pallas_call(
        functools.partial(_inproj_body, transposed_kv=transposed_kv),
        grid=(nt,),
        in_specs=[row(D_MODEL),
                  pl.BlockSpec((1, D_MODEL), lambda i: (0, 0)),
                  pl.BlockSpec((D_MODEL, IN_WP), lambda i: (0, 0))],
        out_specs=tuple(out_specs),
        out_shape=tuple(out_shape),
        compiler_params=_params(("arbitrary",)),
        name="inproj",
    )(x2d, ln_g, w_perm)


def _mem_kv_body(x_ref, g_ref, w_ref, o_ref):
    x = x_ref[0]
    xn = x * lax.rsqrt(jnp.mean(x * x, axis=-1, keepdims=True) + EPS) * g_ref[...]
    o_ref[0] = _dot(xn.astype(BF16), w_ref[...]).T


def _mem_kv(mem, ln_g, w_bf16):
    b = mem.shape[0]
    nw = w_bf16.shape[1]
    return pl.pallas_call(
        _mem_kv_body,
        grid=(b,),
        in_specs=[pl.BlockSpec((1, N_MEM, D_MODEL), lambda i: (i, 0, 0)),
                  pl.BlockSpec((1, D_MODEL), lambda i: (0, 0)),
                  pl.BlockSpec((D_MODEL, nw), lambda i: (0, 0))],
        out_specs=pl.BlockSpec((1, nw, N_MEM), lambda i: (i, 0, 0)),
        out_shape=jax.ShapeDtypeStruct((b, nw, N_MEM), F32),
        compiler_params=_params(("arbitrary",)),
        name="mem_kv",
    )(mem, ln_g, w_bf16)


def _compress_slab(slabs_t, x_refs, pe_ref, w1_ref, b1_ref, w2_ref, out_refs):
    rows = _iota((N_CMP, LANES), 0)
    hw = 2 * CMP_HID
    chunks_per_page = LANES // CMP_STRIDE
    for v in range(2):
        for j in range(PAST_LEN // LANES):
            xt = slabs_t[v][:, j * LANES:(j + 1) * LANES].T
            for cc in range(chunks_per_page):
                r0 = (j * chunks_per_page + cc) * X_PITCH
                x_refs[v][r0:r0 + CMP_STRIDE, :] = xt[cc * CMP_STRIDE:(cc + 1) * CMP_STRIDE, :]
        nh = N_CMP // CMP_ROW_SPLITS
        accs = []
        for hf in range(CMP_ROW_SPLITS):
            acc = jnp.zeros((nh + (PE_ROWS if hf == 0 else 0), 2 * hw), F32)
            for pp in range(CMP_STRIDE // 2):
                r0 = hf * nh * X_PITCH + 2 * pp
                xcat = jnp.concatenate([x_refs[v][pl.ds(r0, nh, stride=X_PITCH), :],
                                        x_refs[v][pl.ds(r0 + 1, nh, stride=X_PITCH), :]], axis=1).astype(BF16)
                lhs = jnp.concatenate([xcat, pe_ref[v, pp]], axis=0) if hf == 0 else xcat
                acc = acc + _dot(lhs, w1_ref[v, pp])
            accs.append(acc)
        bias = b1_ref[v] + accs[0][nh:nh + 1, 0:hw] + accs[0][nh + 1:nh + 2, hw:2 * hw]
        acc = jnp.concatenate([accs[0][0:nh]] + accs[1:], axis=0)
        nxt = pltpu.roll(acc[:, hw:2 * hw], N_CMP - 1, 0)
        hid = jax.nn.gelu(acc[:, 0:hw] + nxt + bias)
        out = _dot(hid.astype(BF16), w2_ref[v])
        out_refs[v][...] = jnp.where(rows < N_CMP - 1, out, 0.0)


def _compress_prompt_body(k_ref, v_ref, pe_ref, w1_ref, b1_ref, w2_ref, kc_ref, vc_ref, xk_ref, xv_ref):
    _compress_slab((k_ref.at[0], v_ref.at[0]), (xk_ref, xv_ref), pe_ref, w1_ref, b1_ref, w2_ref,
                   (kc_ref.at[0], vc_ref.at[0]))


PE_ROWS = 16
CMP_ROW_SPLITS = 4
X_PITCH = 24


def _x_scratch():
    return [pltpu.VMEM((N_CMP * X_PITCH, LANES), F32) for _ in range(2)]


def _comp_weight_specs():
    z = lambda *_: (0, 0, 0)
    z4 = lambda *_: (0, 0, 0, 0)
    return [pl.BlockSpec((2, CMP_STRIDE // 2, PE_ROWS, 2 * LANES), z4),
            pl.BlockSpec((2, CMP_STRIDE // 2, 2 * LANES, 4 * CMP_HID), z4),
            pl.BlockSpec((2, 1, 2 * CMP_HID), z),
            pl.BlockSpec((2, 2 * CMP_HID, LANES), z)]


def _compress_prompt(kvc_t, comp_w):
    b = kvc_t.shape[0]
    out = jax.ShapeDtypeStruct((b, N_CMP, LANES), F32)
    return pl.pallas_call(
        _compress_prompt_body,
        grid=(b,),
        in_specs=[pl.BlockSpec((1, LANES, SEQ), lambda i: (i, 0, 0)),
                  pl.BlockSpec((1, LANES, SEQ), lambda i: (i, 1, 0))] + _comp_weight_specs(),
        out_specs=(pl.BlockSpec((1, N_CMP, LANES), lambda i: (i, 0, 0)),) * 2,
        out_shape=(out, out),
        scratch_shapes=_x_scratch(),
        compiler_params=_params(("arbitrary",)),
        name="compress_prompt",
    )(kvc_t, kvc_t, *comp_w)


N_PAGES = PAST_LEN // PAGE_SIZE


def _page_copy(pt_ref, pool_ref, slab_ref, sem_ref, seq, slot, s, j):
    return pltpu.make_async_copy(pool_ref.at[pt_ref[seq, j]],
                                 slab_ref.at[slot, s, :, :, pl.ds(j * PAGE_SIZE, PAGE_SIZE)],
                                 sem_ref.at[slot])


def _start_pages(pt_ref, pool_ref, slab_ref, sem_ref, step, slot, inline=False):
    spb = slab_ref.shape[1]
    for s in range(spb):
        if inline:
            for j in range(N_PAGES):
                _page_copy(pt_ref, pool_ref, slab_ref, sem_ref, step * spb + s, slot, s, j).start()
        else:
            def body(j, c):
                _page_copy(pt_ref, pool_ref, slab_ref, sem_ref, step * spb + s, slot, s, j).start()
                return c
            lax.fori_loop(0, N_PAGES, body, 0, unroll=8)


def _wait_pages(pt_ref, pool_ref, slab_ref, sem_ref, step, slot):
    spb = slab_ref.shape[1]
    for s in range(spb):
        def body(j, c):
            _page_copy(pt_ref, pool_ref, slab_ref, sem_ref, step * spb + s, slot, s, j).wait()
            return c
        lax.fori_loop(0, N_PAGES, body, 0, unroll=8)


SLAB_SLOTS = 2


def _slab_substeps(pt_ref, pool_ref, slab_ref, sem_ref, compute):
    g = pl.program_id(0)
    n_sub = pl.num_programs(0) * SLAB_SLOTS
    args = (pt_ref, pool_ref, slab_ref, sem_ref)

    @pl.when(g == 0)
    def _():
        _start_pages(*args, 0, 0)

    for slot in range(SLAB_SLOTS):
        step = g * SLAB_SLOTS + slot
        _wait_pages(*args, step, slot)
        _start_pages(*args, jnp.minimum(step + 1, n_sub - 1), (slot + 1) % SLAB_SLOTS, inline=True)
        compute(slot)

    @pl.when(g == pl.num_programs(0) - 1)
    def _():
        _wait_pages(*args, n_sub - 1, 0)


def _stream_slab(pt_ref, pool_ref, slab_ref, sem_ref):
    b = pl.program_id(0)
    slot = b % SLAB_SLOTS
    args = (pt_ref, pool_ref, slab_ref, sem_ref)

    @pl.when(b == 0)
    def _():
        _start_pages(*args, 0, 0)

    @pl.when(b + 1 < pl.num_programs(0))
    def _():
        _start_pages(*args, b + 1, 1 - slot)

    _wait_pages(*args, b, slot)
    return slot


def _compress_sample_body(pt_ref, pool_ref, pe_ref, w1_ref, b1_ref, w2_ref, kc_ref, vc_ref, slab_ref, sem_ref,
                          xk_ref, xv_ref):
    def compute(slot):
        _compress_slab((slab_ref.at[slot, 0, 0], slab_ref.at[slot, 0, 1]), (xk_ref, xv_ref), pe_ref, w1_ref, b1_ref,
                       w2_ref, (kc_ref.at[slot], vc_ref.at[slot]))

    _slab_substeps(pt_ref, pool_ref, slab_ref, sem_ref, compute)


def _slab_scratch(seqs_per_step=1):
    return [pltpu.VMEM((2, seqs_per_step, 2, LANES, PAST_LEN), F32), pltpu.SemaphoreType.DMA((2,))]


def _compress_sample(page_table, pool, comp_w):
    nb = page_table.shape[0]
    out = jax.ShapeDtypeStruct((nb, N_CMP, LANES), F32)
    grid_spec = pltpu.PrefetchScalarGridSpec(
        num_scalar_prefetch=1,
        grid=(nb // SLAB_SLOTS,),
        in_specs=[pl.BlockSpec(memory_space=pl.ANY)] + _comp_weight_specs(),
        out_specs=(pl.BlockSpec((SLAB_SLOTS, N_CMP, LANES), lambda i, pt: (i, 0, 0)),) * 2,
        scratch_shapes=_slab_scratch() + _x_scratch(),
    )
    return pl.pallas_call(
        _compress_sample_body,
        grid_spec=grid_spec,
        out_shape=(out, out),
        compiler_params=_params(("arbitrary",)),
        name="compress_sample",
    )(page_table, pool, *comp_w)


def _top_k_rows(score, blk, k):
    cand = score > -jnp.inf
    nblk = score.shape[0]
    for _ in range(k):
        m = jnp.max(score, axis=0, keepdims=True)
        idx = jnp.min(jnp.where(score == m, blk, nblk), axis=0, keepdims=True)
        score = jnp.where(blk == idx, -jnp.inf, score)
    return cand & (score == -jnp.inf)


def _add_q_bias(s, bias):
    return jnp.concatenate([s[:, g * Q_BLOCK:(g + 1) * Q_BLOCK] + bias for g in range(NSA_GROUP)], axis=1)


def _safe_inv(l):
    return 1.0 / jnp.maximum(l, 1e-30)


KT_SLC = 512
WIN_KEYS = WINDOW + Q_BLOCK
N_FORCED = 3


def _nsa_prompt_body(qn_ref, gt_ref, ksk_ref, ksvt_ref, kwk_ref, kwvt_ref, kc_ref, vc_ref, o_ref, selb_ref, s_ref):
    i = pl.program_id(1)
    q0 = i * Q_BLOCK
    qt = (qn_ref[...].astype(F32) * (SCALE * LOG2E)).T
    t_row = q0 + _iota((1, Q_BLOCK), 1)
    kc = kc_ref[0].astype(BF16)
    vct = vc_ref[0].T.astype(BF16)
    gt = gt_ref[...]
    zeros_q = jnp.zeros((HEAD_DIM, NSA_GROUP * Q_BLOCK), F32)

    mi = _iota((N_SLC, N_CMP), 0)
    mj = _iota((N_SLC, N_CMP), 1)
    msel = ((mj >= 4 * mi - 1) & (mj <= 4 * mi + 3) & (mj < N_CMP - 1)).astype(BF16)
    blk = _iota((N_SLC, Q_BLOCK), 0)
    cur = t_row // SLC_BLOCK
    forced = (blk == 0) | (blk == cur) | (blk == cur - 1)
    free = (blk * SLC_BLOCK <= t_row) & jnp.logical_not(forced)
    cend = _iota((N_CMP, 1), 0) * CMP_STRIDE + (CMP_LEN - 1)
    bias_c = jnp.where(cend <= t_row, 0.0, NEG_INF)
    wt = jnp.maximum(i - WINDOW // Q_BLOCK, 0)
    w0 = pl.multiple_of(wt * Q_BLOCK, Q_BLOCK)
    kpw = w0 + _iota((WIN_KEYS, 1), 0)
    bias_w = jnp.where((kpw <= t_row) & (kpw > t_row - WINDOW), 0.0, NEG_INF)
    krow = _iota((SLC_BLOCK, 1), 0)

    qgs, qps, o_cs, o_ws, cand_scores = [], [], [], [], []
    vrows = [slice(kvh * HEAD_DIM, (kvh + 1) * HEAD_DIM) for kvh in range(NSA_KV_HEADS)]
    for kvh in range(NSA_KV_HEADS):
        qg = jnp.concatenate([qt[(kvh * NSA_GROUP + g) * HEAD_DIM:(kvh * NSA_GROUP + g + 1) * HEAD_DIM, :]
                              for g in range(NSA_GROUP)], axis=1)
        qp = jnp.concatenate([qg, zeros_q] if kvh == 0 else [zeros_q, qg], axis=0).astype(BF16)
        vrow = vrows[kvh]
        qgs.append(qg.astype(BF16))
        qps.append(qp)

        s = _add_q_bias(_dot(kc, qp), bias_c)
        m = jnp.max(s, axis=0, keepdims=True)
        e = jnp.exp2(s - m)
        inv = jnp.where(m > 0.5 * NEG_INF, _safe_inv(jnp.sum(e, axis=0, keepdims=True)), 0.0)
        p = e * inv
        o_c = _dot(vct[vrow, :], p.astype(BF16))
        imp = (p[:, 0:Q_BLOCK] + p[:, Q_BLOCK:2 * Q_BLOCK]
               + p[:, 2 * Q_BLOCK:3 * Q_BLOCK] + p[:, 3 * Q_BLOCK:4 * Q_BLOCK])

        cand_scores.append(jnp.where(free, _dot_exact_rhs(msel, imp), -jnp.inf))
        o_cs.append(o_c)

    blk2 = jnp.concatenate([blk] * NSA_KV_HEADS, axis=1)
    picked = _top_k_rows(jnp.concatenate(cand_scores, axis=1), blk2, SLC_TOPK - N_FORCED)
    for kvh in range(NSA_KV_HEADS):
        selb_ref[kvh] = jnp.where(forced | picked[:, kvh * Q_BLOCK:(kvh + 1) * Q_BLOCK], 0.0, NEG_INF)

    for kvh in range(NSA_KV_HEADS):
        qp, vrow = qps[kvh], vrows[kvh]
        sw = _add_q_bias(_dot(kwk_ref[pl.ds(w0, WIN_KEYS), :], qp), bias_w)
        ew = jnp.exp2(sw - jnp.max(sw, axis=0, keepdims=True))
        inv_w = _safe_inv(jnp.sum(ew, axis=0, keepdims=True))
        pw = ew.astype(BF16)
        o_w = jnp.zeros((HEAD_DIM, NSA_GROUP * Q_BLOCK), F32)
        for r in range(WIN_KEYS // Q_BLOCK):
            tix = wt + r
            vt = kwvt_ref[tix // 4, vrow, :]
            sub = tix % 4
            vpiece = jnp.where(sub == 0, vt[:, 0:128],
                               jnp.where(sub == 1, vt[:, 128:256],
                                         jnp.where(sub == 2, vt[:, 256:384], vt[:, 384:512])))
            o_w = o_w + _dot(vpiece, pw[r * Q_BLOCK:(r + 1) * Q_BLOCK, :])
        o_ws.append(o_w * inv_w)

    bias_pad = jnp.zeros((HEAD_DIM - 8, NSA_GROUP * Q_BLOCK), F32)

    def scores(j, buf):
        k0 = pl.multiple_of(j * KT_SLC, KT_SLC)
        for kvh in range(NSA_KV_HEADS):
            b8 = selb_ref[kvh, pl.ds(pl.multiple_of(j * 8, 8), 8), :]
            b_rows = jnp.concatenate([jnp.concatenate([b8] * NSA_GROUP, axis=1), bias_pad], axis=0).astype(BF16)
            q_aug = jnp.concatenate([qgs[kvh], b_rows] if kvh == 0 else [b_rows, qgs[kvh]], axis=0)
            s_ref[buf, kvh] = _dot(ksk_ref[pl.ds(k0, KT_SLC), kvh * LANES:(kvh + 1) * LANES], q_aug)

    def consume(j, buf, state, causal):
        k0 = j * KT_SLC
        out = []
        for kvh in range(NSA_KV_HEADS):
            m_run, acc = state[kvh]
            sj = s_ref[buf, kvh]
            if causal:
                parts = []
                for r in range(KT_SLC // SLC_BLOCK):
                    ok = k0 + r * SLC_BLOCK + krow <= t_row
                    parts.append(jnp.concatenate(
                        [jnp.where(ok, sj[r * SLC_BLOCK:(r + 1) * SLC_BLOCK, g * Q_BLOCK:(g + 1) * Q_BLOCK], NEG_INF)
                         for g in range(NSA_GROUP)], axis=1))
                sj = jnp.concatenate(parts, axis=0)
            m_new = jnp.maximum(m_run, jnp.max(sj, axis=0, keepdims=True))
            alpha = jnp.exp2(m_run - m_new)
            ej = jnp.exp2(sj - m_new).astype(BF16)
            acc = alpha * acc + _dot(ksvt_ref[j, kvh * V_ROWS:(kvh + 1) * V_ROWS, :], ej)
            out.append((m_new, acc))
        return tuple(out)

    init1 = (jnp.full((1, NSA_GROUP * Q_BLOCK), NEG_INF, F32),
             jnp.zeros((V_ROWS, NSA_GROUP * Q_BLOCK), F32))
    n_tiles = q0 // KT_SLC + 1
    n_pairs = (n_tiles + 1) // 2

    def pair(jj, state):
        scores(2 * jj + 1, 1)
        state = consume(2 * jj, 0, state, False)
        scores(2 * jj + 2, 0)
        return consume(2 * jj + 1, 1, state, False)

    scores(0, 0)
    state = lax.fori_loop(0, n_pairs - 1, pair, (init1,) * NSA_KV_HEADS)
    last = 2 * n_pairs - 2

    def last_two(st):
        scores(last + 1, 1)
        return consume(last + 1, 1, consume(last, 0, st, True), True)

    state = lax.cond(n_tiles % 2 == 0, last_two, lambda st: consume(last, 0, st, True), state)

    outs = []
    for kvh in range(NSA_KV_HEADS):
        acc_s = state[kvh][1]
        o_s = acc_s[0:HEAD_DIM, :] * _safe_inv(acc_s[HEAD_DIM:HEAD_DIM + 1, :])

        def gate(br):
            base = br * NSA_HEADS + kvh * NSA_GROUP
            return jnp.concatenate([gt[base + g:base + g + 1, :] for g in range(NSA_GROUP)], axis=1)

        o = gate(0) * o_cs[kvh] + gate(1) * o_s + gate(2) * o_ws[kvh]
        outs.extend([o[:, g * Q_BLOCK:(g + 1) * Q_BLOCK] for g in range(NSA_GROUP)])
    o_ref[...] = jnp.concatenate(outs, axis=0).T


def _nsa_prompt(qn, gt, ksk, ksvt, kwk, kwvt, kc, vc, batch):
    nqb = SEQ // Q_BLOCK
    ntile = SEQ // 512
    return pl.pallas_call(
        _nsa_prompt_body,
        grid=(batch, nqb),
        in_specs=[pl.BlockSpec((Q_BLOCK, 512), lambda b, i: (b * nqb + i, 0)),
                  pl.BlockSpec((LANES, Q_BLOCK), lambda b, i: (0, b * nqb + i)),
                  pl.BlockSpec((SEQ, NSA_KV_HEADS * LANES), lambda b, i: (b, 0)),
                  pl.BlockSpec((ntile, NSA_KV_HEADS * V_ROWS, 512), lambda b, i: (b, 0, 0)),
                  pl.BlockSpec((SEQ, LANES), lambda b, i: (b, 0)),
                  pl.BlockSpec((ntile, LANES, 512), lambda b, i: (b, 0, 0)),
                  pl.BlockSpec((1, N_CMP, LANES), lambda b, i: (b, 0, 0)),
                  pl.BlockSpec((1, N_CMP, LANES), lambda b, i: (b, 0, 0))],
        out_specs=pl.BlockSpec((Q_BLOCK, 512), lambda b, i: (b * nqb + i, 0)),
        out_shape=jax.ShapeDtypeStruct((batch * SEQ, 512), F32),
        scratch_shapes=[pltpu.VMEM((NSA_KV_HEADS, N_SLC, Q_BLOCK), F32),
                        pltpu.VMEM((2, NSA_KV_HEADS, KT_SLC, NSA_GROUP * Q_BLOCK), F32)],
        compiler_params=_params(("arbitrary", "arbitrary")),
        name="nsa_prompt",
    )(qn, gt, ksk, ksvt, kwk, kwvt, kc, vc)


def _softmax_parts(parts, masks):
    m = None
    for s, mk in zip(parts, masks):
        mi = jnp.max(jnp.where(mk, s, NEG_INF), axis=-1, keepdims=True)
        m = mi if m is None else jnp.maximum(m, mi)
    es = [jnp.where(mk, jnp.exp(jnp.where(mk, s, NEG_INF) - m), 0.0) for s, mk in zip(parts, masks)]
    l = None
    for e in es:
        li = jnp.sum(e, axis=-1, keepdims=True)
        l = li if l is None else l + li
    inv = 1.0 / jnp.maximum(l, 1e-30)
    return [e * inv for e in es]


KC_SLC = 2048
ROWS_S = NSA_HEADS * DEC_SEQ


def _pad_rows(x, n):
    if x.shape[0] == n:
        return x
    return jnp.concatenate([x, jnp.zeros((n - x.shape[0], x.shape[1]), x.dtype)], axis=0)


def _nsa_sample_body(pt_ref, pool_ref, qp_ref, g_ref, kc_ref, vc_ref, kvs_ref, kvw_ref, win_ref, wtail_ref,
                     expand_ref, o_ref, nwin_ref, slab_ref, sem_ref):
    slot = _stream_slab(pt_ref, pool_ref, slab_ref, sem_ref)
    spb = slab_ref.shape[1]
    seqs = range(spb)
    bi = list(seqs)
    qi = _iota((ROWS_S, 1), 0) % DEC_SEQ
    t = PAST_LEN + qi
    jn = _iota((1, LANES), 1)
    n_chunk = PAST_LEN // KC_SLC
    qs = [(qp_ref[bi[s]] * SCALE).astype(BF16) for s in seqs]
    new_rows = [slice(bi[s] * DEC_SEQ, (bi[s] + 1) * DEC_SEQ) for s in seqs]

    cend = _iota((1, N_CMP), 1) * CMP_STRIDE + (CMP_LEN - 1)
    p_cs = [_softmax_parts([_dot_nt(qs[s], kc_ref[bi[s]].astype(BF16))], [cend <= t])[0] for s in seqs]
    o_cs = [_dot(p_cs[s].astype(BF16), vc_ref[bi[s]].astype(BF16)) for s in seqs]

    half = ROWS_S // NSA_KV_HEADS
    imp = jnp.concatenate(
        [p_cs[s][k * half:k * half + 8] + p_cs[s][k * half + 8:k * half + 16]
         + p_cs[s][k * half + 16:k * half + 24] + p_cs[s][k * half + 24:k * half + 32]
         for s in seqs for k in range(NSA_KV_HEADS)], axis=0)
    mi = _iota((N_CMP, N_SLC), 0)
    mj = _iota((N_CMP, N_SLC), 1)
    msel_t = ((mi >= 4 * mj - 1) & (mi <= 4 * mj + 3) & (mi < N_CMP - 1)).astype(BF16)
    score = _dot_exact_lhs(imp, msel_t)
    score_t = _pad_rows(score, LANES).T
    blk = _iota((N_SLC, LANES), 0)
    forced = (blk == 0) | (blk == N_SLC - 1)
    picked = _top_k_rows(jnp.where(forced, -jnp.inf, score_t), blk, SLC_TOPK - 1 - 2)
    unsel = jnp.where(forced | picked, 0.0, 1.0).T
    unsel64 = [jnp.concatenate([unsel[s * 16:s * 16 + 8]] * NSA_GROUP + [unsel[s * 16 + 8:s * 16 + 16]] * NSA_GROUP,
                               axis=0).astype(BF16) for s in seqs]

    new_bias = jnp.where((jn <= qi) & (jn < DEC_SEQ), 0.0, NEG_INF)
    k_news = [_pad_rows(kvs_ref[new_rows[s], 0:LANES], LANES).astype(BF16) for s in seqs]
    v_news = [_pad_rows(kvs_ref[new_rows[s], LANES:2 * LANES], LANES).astype(BF16) for s in seqs]
    s_news = [_dot_nt(qs[s], k_news[s]) + new_bias for s in seqs]
    scs = [[_dot(qs[s], slab_ref[slot, s, 0, :, c * KC_SLC:(c + 1) * KC_SLC].astype(BF16))
            + _dot(unsel64[s], expand_ref[:, c * KC_SLC:(c + 1) * KC_SLC]) * NEG_INF
            for c in range(n_chunk)] for s in seqs]
    o_ss = []
    for s in seqs:
        m_s = jnp.max(s_news[s], axis=-1, keepdims=True)
        for sc in scs[s]:
            m_s = jnp.maximum(m_s, jnp.max(sc, axis=-1, keepdims=True))
        e_new = jnp.exp(s_news[s] - m_s)
        l_s = jnp.sum(e_new, axis=-1, keepdims=True)
        acc = _dot(e_new.astype(BF16), v_news[s])
        for c in range(n_chunk):
            ec = jnp.exp(scs[s][c] - m_s)
            l_s = l_s + jnp.sum(ec, axis=-1, keepdims=True)
            acc = acc + _dot_nt(ec.astype(BF16), slab_ref[slot, s, 1, :, c * KC_SLC:(c + 1) * KC_SLC].astype(BF16))
        o_ss.append(acc * _safe_inv(l_s))

    wb = win_ref.shape[2]
    kpos_b = PAST_LEN - wb + _iota((1, wb), 1)
    kpos_n = PAST_LEN + jn
    mask_b = (kpos_b <= t) & (kpos_b > t - WINDOW) & (kpos_b >= 0)
    mask_n = (kpos_n <= t) & (kpos_n > t - WINDOW) & (jn < DEC_SEQ)
    for s in seqs:
        win_t = win_ref[bi[s]]
        kw_new = _pad_rows(kvw_ref[new_rows[s], 0:LANES], LANES).astype(BF16)
        vw_new = _pad_rows(kvw_ref[new_rows[s], LANES:2 * LANES], LANES).astype(BF16)
        p_b, p_n = _softmax_parts([_dot(qs[s], win_t[0:LANES, :].astype(BF16)), _dot_nt(qs[s], kw_new)],
                                  [mask_b, mask_n])
        o_w = _dot_nt(p_b.astype(BF16), win_t[LANES:2 * LANES, :].astype(BF16)) + _dot(p_n.astype(BF16), vw_new)

        g = g_ref[bi[s]]
        o_ref[bi[s]] = g[:, 0:1] * o_cs[s] + g[:, 1:2] * o_ss[s] + g[:, 2:3] * o_w
        shifted = pltpu.roll(win_t, wb - DEC_SEQ, 1)
        nwin_ref[bi[s], :, 0:wb - LANES] = shifted[:, 0:wb - LANES]
        nwin_ref[bi[s], :, wb - LANES:wb] = jnp.where(jn >= LANES - DEC_SEQ, wtail_ref[bi[s]],
                                                      shifted[:, wb - LANES:wb])


NSA_SAMPLE_SEQS_PER_STEP = 2


def _nsa_sample(page_table, pool_slc, qp, gates, kc, vc, kvs, kvw, win_t, wtail, expand):
    nb = page_table.shape[0]
    wb = win_t.shape[2]
    sps = spb = NSA_SAMPLE_SEQS_PER_STEP
    per_b = lambda *shape: pl.BlockSpec((sps,) + shape, lambda i, pt: (i,) + (0,) * len(shape))
    rows8 = pl.BlockSpec((sps * DEC_SEQ, 256), lambda i, pt: (i, 0))
    grid_spec = pltpu.PrefetchScalarGridSpec(
        num_scalar_prefetch=1,
        grid=(nb // sps,),
        in_specs=[pl.BlockSpec(memory_space=pl.ANY),
                  per_b(ROWS_S, LANES), per_b(ROWS_S, LANES), per_b(N_CMP, LANES), per_b(N_CMP, LANES),
                  rows8, rows8, per_b(256, wb), per_b(256, LANES),
                  pl.BlockSpec((N_SLC, PAST_LEN), lambda i, pt: (0, 0))],
        out_specs=(per_b(ROWS_S, LANES), per_b(256, wb)),
        scratch_shapes=_slab_scratch(spb),
    )
    return pl.pallas_call(
        _nsa_sample_body,
        grid_spec=grid_spec,
        out_shape=(jax.ShapeDtypeStruct((nb, ROWS_S, LANES), F32),
                   jax.ShapeDtypeStruct((nb, 256, wb), F32)),
        compiler_params=_params(("arbitrary",)),
        name="nsa_sample",
    )(page_table, pool_slc, qp, gates, kc, vc, kvs, kvw, win_t, wtail, expand)


RET_W = RET_HEADS * HEAD_DIM


def _retention_body(q_ref, k_ref, v_ref, cos_ref, sin_ref, st0_ref, gn_ref, o_ref, st_ref, state_ref, *, chunk):
    c = pl.program_id(1)
    ck = max(chunk, LANES)
    spb = st0_ref.shape[0]

    lane = _iota((1, RET_W), 1)
    head_of_lane = lane // HEAD_DIM
    row_head = _iota((RET_W, 1), 0) // HEAD_DIM

    @pl.when(c == 0)
    def _():
        for bb in range(spb):
            st_c = st0_ref[bb]
            state_ref[bb] = jnp.concatenate([jnp.where(row_head == h, st_c, 0.0) for h in range(RET_HEADS)], axis=1)
    low_half = (lane % HEAD_DIM) < HEAD_DIM // 2
    cos = cos_ref[...]
    sin = sin_ref[...]

    def rope(x):
        rot = jnp.where(low_half, -pltpu.roll(x, RET_W - HEAD_DIM // 2, 1), pltpu.roll(x, HEAD_DIM // 2, 1))
        return x * cos + rot * sin

    logg_lane = jnp.zeros((1, RET_W), F32)
    for h in range(RET_HEADS):
        logg_lane = jnp.where(head_of_lane == h, _LOG_G[h], logg_lane)

    n = _iota((chunk, 1), 0).astype(F32)
    nk = _iota((ck, 1), 0).astype(F32)
    diff = n - _iota((1, ck), 1).astype(F32)
    in_chunk = _iota((1, ck), 1) < chunk
    dmats = [jnp.where((diff >= 0) & in_chunk, jnp.exp(_LOG_G[h] * jnp.maximum(diff, 0.0)), 0.0)
             for h in range(RET_HEADS)]
    xi = jnp.exp(logg_lane * (n + 1.0))
    zeta = jnp.where(nk < chunk, jnp.exp(logg_lane * (chunk - 1.0 - nk)), 0.0)
    decay_rows = jnp.zeros((RET_W, 1), F32)
    for h in range(RET_HEADS):
        decay_rows = jnp.where(row_head == h, float(np.exp(_LOG_G[h] * chunk)), decay_rows)
    avg = jnp.where(_iota((RET_W, RET_W), 0) // HEAD_DIM == _iota((RET_W, RET_W), 1) // HEAD_DIM,
                    1.0 / HEAD_DIM, 0.0).astype(BF16)
    gn = gn_ref[...]

    seqs = range(spb)
    qs = [rope(q_ref[bb]) * SCALE for bb in seqs]
    kps = [_pad_rows(rope(k_ref[bb]), ck) for bb in seqs]
    vps = [_pad_rows(v_ref[bb], ck) for bb in seqs]
    hms = [head_of_lane == h for h in range(RET_HEADS)]
    ss = [[_dot_nt(jnp.where(hms[h], qs[bb], 0.0), kps[bb]) * dmats[h] for h in range(RET_HEADS)] for bb in seqs]
    inners = []
    for bb in seqs:
        inner = jnp.zeros((chunk, RET_W), F32)
        for h in range(RET_HEADS):
            inner = inner + jnp.where(hms[h], _dot(ss[bb][h], vps[bb]), 0.0)
        inners.append(inner)
    states = [state_ref[bb] for bb in seqs]
    os_ = [inners[bb] + _dot(qs[bb] * xi, states[bb]) for bb in seqs]
    for bb in seqs:
        kz_t = (kps[bb] * zeta).T
        new_state = decay_rows * states[bb] + jnp.where(row_head == head_of_lane, _dot(kz_t, vps[bb]), 0.0)
        state_ref[bb] = new_state
        st_ref[bb] = (new_state[:, 0:HEAD_DIM] + new_state[:, HEAD_DIM:2 * HEAD_DIM]
                      + new_state[:, 2 * HEAD_DIM:3 * HEAD_DIM] + new_state[:, 3 * HEAD_DIM:4 * HEAD_DIM])

    o = jnp.concatenate(os_, axis=0) if spb > 1 else os_[0]
    mu = _dot_exact_lhs(o, avg)
    d = o - mu
    var = _dot_exact_lhs(d * d, avg)
    y = d * lax.rsqrt(var + EPS) * gn
    for bb in seqs:
        o_ref[bb] = y[bb * chunk:(bb + 1) * chunk, :]


def _retention(qkvr, cos, sin, state, gn, batch, t_len, chunk, seqs_per_step=1):
    nch = t_len // chunk
    spb = seqs_per_step
    qkvr3 = qkvr.reshape(batch, t_len, 3 * RET_W)
    col = lambda j: pl.BlockSpec((spb, chunk, RET_W), lambda b, c: (b, c, j))
    tab = pl.BlockSpec((chunk, RET_W), lambda b, c: (c, 0))
    st = pl.BlockSpec((spb, RET_W, HEAD_DIM), lambda b, c: (b, 0, 0))
    o, st_out = pl.pallas_call(
        functools.partial(_retention_body, chunk=chunk),
        grid=(batch // spb, nch),
        in_specs=[col(0), col(1), col(2), tab, tab, st, pl.BlockSpec((1, RET_W), lambda b, c: (0, 0))],
        out_specs=(col(0), st),
        out_shape=(jax.ShapeDtypeStruct((batch, t_len, RET_W), F32),
                   jax.ShapeDtypeStruct((batch, RET_W, HEAD_DIM), F32)),
        scratch_shapes=[pltpu.VMEM((spb, RET_W, RET_W), F32)],
        compiler_params=_params(("arbitrary", "arbitrary")),
        name="retention",
    )(qkvr3, qkvr3, qkvr3, cos, sin, state, gn)
    return o.reshape(batch * t_len, RET_W), st_out


MEM_W = 4 * HEAD_DIM


def _mem_attn_body(q_ref, mkv_ref, o_ref, *, tm):
    head_of_lane = _iota((1, MEM_W), 1) // HEAD_DIM
    spb = mkv_ref.shape[0]
    q_all = q_ref[...] * SCALE
    scores = []
    for bb in range(spb):
        q = q_all[bb * tm:(bb + 1) * tm, :]
        q4 = jnp.concatenate([jnp.where(head_of_lane == h, q, 0.0) for h in range(4)], axis=0).astype(BF16)
        scores.append(_dot(q4, mkv_ref[bb, 0:MEM_W, :].astype(BF16)))
    s = jnp.concatenate(scores, axis=0)
    e = jnp.exp(s - jnp.max(s, axis=-1, keepdims=True))
    p = (e / jnp.sum(e, axis=-1, keepdims=True)).astype(BF16)
    for bb in range(spb):
        pv = _dot_nt(p[bb * 4 * tm:(bb + 1) * 4 * tm, :], mkv_ref[bb, MEM_W:2 * MEM_W, :].astype(BF16))
        o = jnp.zeros((tm, MEM_W), F32)
        for h in range(4):
            o = o + jnp.where(head_of_lane == h, pv[h * tm:(h + 1) * tm, :], 0.0)
        o_ref[bb * tm:(bb + 1) * tm, :] = o


def _mem_attn(qm, mkv_t, batch, rows_per_batch, tm, seqs_per_step=1):
    nt = rows_per_batch // tm
    spb = seqs_per_step
    assert spb == 1 or nt == 1
    return pl.pallas_call(
        functools.partial(_mem_attn_body, tm=tm),
        grid=(batch // spb, nt),
        in_specs=[pl.BlockSpec((spb * tm, MEM_W), lambda b, i: (b * nt + i, 0)),
                  pl.BlockSpec((spb, 2 * MEM_W, N_MEM), lambda b, i: (b, 0, 0))],
        out_specs=pl.BlockSpec((spb * tm, MEM_W), lambda b, i: (b * nt + i, 0)),
        out_shape=jax.ShapeDtypeStruct((batch * rows_per_batch, MEM_W), F32),
        compiler_params=_params(("arbitrary", "arbitrary")),
        name="mem_attn",
    )(qm, mkv_t)


def _finish_body(x_ref, on_ref, or_ref, om_ref, z_ref, w_ref, g_ref, y_ref):
    o = jnp.concatenate([on_ref[...], or_ref[...], om_ref[...]], axis=-1)
    z = z_ref[...]
    mix = o * (z * jax.nn.sigmoid(z))
    xo = x_ref[...] + _dot(mix.astype(BF16), w_ref[...])
    y_ref[...] = xo * lax.rsqrt(jnp.mean(xo * xo, axis=-1, keepdims=True) + EPS) * g_ref[...]


def _finish(x2d, o_n, o_r, o_m, z, w_out, ln_final):
    n = x2d.shape[0]
    tm = 512
    row = lambda w: pl.BlockSpec((tm, w), lambda i: (i, 0))
    return pl.pallas_call(
        _finish_body,
        grid=(n // tm,),
        in_specs=[row(D_MODEL), row(512), row(256), row(256), row(D_MODEL),
                  pl.BlockSpec((D_MODEL, D_MODEL), lambda i: (0, 0)),
                  pl.BlockSpec((1, D_MODEL), lambda i: (0, 0))],
        out_specs=row(D_MODEL),
        out_shape=jax.ShapeDtypeStruct((n, D_MODEL), F32),
        compiler_params=_params(("arbitrary",)),
        name="finish",
    )(x2d, o_n, o_r, o_m, z, w_out, ln_final)


def _permute_w_in(w):
    sizes = (512, 256, 256, 256, 24, 512, 256, 256, 256, 256, 256, 256)
    offs = np.concatenate([[0], np.cumsum(sizes)])
    part = lambda i: w[:, int(offs[i]):int(offs[i + 1])]
    q_n, kv_c, kv_s, kv_w, g_n, z_n, q_r, k_r, v_r, z_r, q_m, z_m = [part(i) for i in range(12)]
    g_pad = jnp.pad(g_n, ((0, 0), (0, LANES - g_n.shape[1])))
    return jnp.concatenate([q_n, kv_c, kv_s, kv_w, q_r, k_r, v_r, q_m, z_n, z_r, z_m, g_pad], axis=1).astype(BF16)


def _compress_weights(pe, w1, b1, w2):
    npair = CMP_STRIDE // 2
    pe_r = pe.reshape(2, 2, npair, 2, HEAD_DIM)
    pe_l = jnp.broadcast_to(pe_r[:, :, :, :, None, :], (2, 2, npair, 2, NSA_KV_HEADS, HEAD_DIM))
    pe_l = pe_l.reshape(2, 2, npair, 2 * LANES).transpose(0, 2, 1, 3)
    pe_l = jnp.pad(pe_l, ((0, 0), (0, 0), (0, PE_ROWS - 2), (0, 0))).astype(BF16)
    w1_r = w1.reshape(2, 2, CMP_STRIDE, HEAD_DIM, CMP_HID)
    zw = jnp.zeros_like(w1_r)
    w1_bd = jnp.concatenate([jnp.concatenate([w1_r, zw], axis=-1),
                             jnp.concatenate([zw, w1_r], axis=-1)], axis=-2)
    w1_bd = w1_bd.reshape(2, 2, npair, 2 * LANES, 2 * CMP_HID)
    w1_bd = jnp.concatenate([w1_bd[:, 0], w1_bd[:, 1]], axis=-1).astype(BF16)
    b1_l = jnp.concatenate([b1, b1], axis=-1).reshape(2, 1, 2 * CMP_HID)
    z2 = jnp.zeros_like(w2)
    w2_bd = jnp.concatenate([jnp.concatenate([w2, z2], axis=-1),
                             jnp.concatenate([z2, w2], axis=-1)], axis=-2).astype(BF16)
    return pe_l, w1_bd, b1_l, w2_bd


def _rope_tables(pos):
    half = HEAD_DIM // 2
    inv = ROPE_BASE ** (-jnp.arange(half, dtype=F32) / half)
    ang = pos.astype(F32)[:, None] * inv[None, :]
    cos, sin = jnp.cos(ang), jnp.sin(ang)
    cos_l = jnp.tile(jnp.concatenate([cos, cos], axis=-1), (1, RET_HEADS))
    sin_l = jnp.tile(jnp.concatenate([sin, sin], axis=-1), (1, RET_HEADS))
    return cos_l, sin_l


def _kv_shape(a, b, t):
    return a.reshape(1, b, t, 2, NSA_KV_HEADS, HEAD_DIM)


def _position_minor(cache, heads):
    b, l = cache.shape[:2]
    return jnp.transpose(cache, (0, 2, 3, 4, 1)).reshape(b, 2 * heads * HEAD_DIM, l)


def _from_position_minor(a, heads):
    b, _, l = a.shape
    return jnp.transpose(a.reshape(b, 2, heads, HEAD_DIM, l), (0, 4, 1, 2, 3))[None]


def kernel(x_prompt, mem_prompt, x_sample, cache_nsa_cmp, cache_nsa_slc, cache_nsa_win, state_ret, cache_mem,
           page_table, ln_mix, w_in, cmp_pe, cmp_w1, cmp_b1, cmp_w2, ret_gn, ln_mem, w_mem_kv, w_out, ln_final):
    bp, t_len = x_prompt.shape[:2]
    bs, s_len = x_sample.shape[:2]
    assert (t_len, s_len) == (SEQ, DEC_SEQ) and ln_mix.shape[0] == 1
    w_perm = _permute_w_in(w_in[0])
    comp_w = _compress_weights(cmp_pe[0], cmp_w1[0], cmp_b1[0], cmp_w2[0])
    ln_g = ln_mix[0].reshape(1, D_MODEL)
    gn = ret_gn[0].reshape(1, RET_W)
    w_out_b = w_out[0].astype(BF16)
    ln_f = ln_final.reshape(1, D_MODEL)

    xp2 = x_prompt.reshape(bp * SEQ, D_MODEL)
    (qn, kvc_t, kvs_t, kvw_t, qkvr, qm, z, gt, ksk, ksvt, kwk, kwvt) = _inproj(xp2, ln_g, w_perm, SEQ, True)
    kc, vc = _compress_prompt(kvc_t, comp_w)
    o_n = _nsa_prompt(qn, gt, ksk, ksvt, kwk, kwvt, kc, vc, bp)
    cos_p, sin_p = _rope_tables(jnp.arange(SEQ))
    o_r, st_p = _retention(qkvr, cos_p, sin_p, jnp.zeros((bp, RET_W, HEAD_DIM), F32), gn, bp, SEQ, 256,
                           seqs_per_step=bp)
    mkv_t = _mem_kv(mem_prompt, ln_mem[0].reshape(1, D_MODEL), w_mem_kv[0].astype(BF16))
    o_m = _mem_attn(qm, mkv_t, bp, SEQ, 512)
    y_prompt = _finish(xp2, o_n, o_r, o_m, z, w_out_b, ln_f).reshape(bp, SEQ, D_MODEL)
    new_cmp_p = _from_position_minor(kvc_t, NSA_KV_HEADS)
    new_slc_p = _from_position_minor(kvs_t, NSA_KV_HEADS)
    new_win_p = _from_position_minor(kvw_t[:, :, SEQ - WINDOW:], NSA_KV_HEADS)
    new_ret_p = st_p.reshape(1, bp, RET_HEADS, HEAD_DIM, HEAD_DIM)
    new_mem_p = _from_position_minor(mkv_t, 4)

    xs2 = x_sample.reshape(bs * DEC_SEQ, D_MODEL)
    (qn_s, kvc_s, kvs_s, kvw_s, qkvr_s, qm_s, z_s, gt_s) = _inproj(xs2, ln_g, w_perm, DEC_SEQ, False)
    pool_cmp = _position_minor(cache_nsa_cmp[0], NSA_KV_HEADS).reshape(-1, 2, LANES, PAGE_SIZE)
    pool_slc = _position_minor(cache_nsa_slc[0], NSA_KV_HEADS).reshape(-1, 2, LANES, PAGE_SIZE)
    kc_s, vc_s = _compress_sample(page_table, pool_cmp, comp_w)
    q5 = qn_s.astype(F32).reshape(bs, DEC_SEQ, NSA_KV_HEADS, NSA_GROUP, HEAD_DIM).transpose(0, 2, 3, 1, 4)
    zq = jnp.zeros_like(q5[:, 0])
    qp = jnp.stack([jnp.concatenate([q5[:, 0], zq], axis=-1), jnp.concatenate([zq, q5[:, 1]], axis=-1)], axis=1)
    qp = qp.reshape(bs, ROWS_S, LANES)
    g5 = gt_s[:3 * NSA_HEADS].reshape(3, NSA_KV_HEADS, NSA_GROUP, bs, DEC_SEQ).transpose(3, 1, 2, 4, 0)
    gates = jnp.pad(g5.reshape(bs, ROWS_S, 3), ((0, 0), (0, 0), (0, LANES - 3)))
    expand = jnp.asarray(np.repeat(np.eye(N_SLC, dtype=np.float32), SLC_BLOCK, axis=1), dtype=BF16)
    win_t = _position_minor(cache_nsa_win[0], NSA_KV_HEADS)
    wtail = jnp.pad(kvw_s.reshape(bs, DEC_SEQ, 256).transpose(0, 2, 1), ((0, 0), (0, 0), (LANES - DEC_SEQ, 0)))
    o_sn, new_win_t = _nsa_sample(page_table, pool_slc, qp, gates, kc_s, vc_s, kvs_s, kvw_s, win_t, wtail, expand)
    o6 = o_sn.reshape(bs, NSA_KV_HEADS, NSA_GROUP, DEC_SEQ, NSA_KV_HEADS, HEAD_DIM)
    o_n_s = jnp.stack([o6[:, 0, :, :, 0], o6[:, 1, :, :, 1]], axis=1)
    o_n_s = o_n_s.transpose(0, 3, 1, 2, 4).reshape(bs * DEC_SEQ, 512)
    cos_s, sin_s = _rope_tables(PAST_LEN + jnp.arange(DEC_SEQ))
    o_r_s, st_s = _retention(qkvr_s, cos_s, sin_s, state_ret[0].reshape(bs, RET_W, HEAD_DIM), gn, bs, DEC_SEQ, DEC_SEQ,
                             seqs_per_step=SAMPLE_SEQS_PER_STEP)
    o_m_s = _mem_attn(qm_s, _position_minor(cache_mem[0], 4), bs, DEC_SEQ, DEC_SEQ,
                      seqs_per_step=SAMPLE_SEQS_PER_STEP)
    y_sample = _finish(xs2, o_n_s, o_r_s, o_m_s, z_s, w_out_b, ln_f).reshape(bs, DEC_SEQ, D_MODEL)
    new_cmp_s = _kv_shape(kvc_s, bs, DEC_SEQ)
    new_slc_s = _kv_shape(kvs_s, bs, DEC_SEQ)
    new_win_s = _from_position_minor(new_win_t, NSA_KV_HEADS)
    new_ret_s = st_s.reshape(1, bs, RET_HEADS, HEAD_DIM, HEAD_DIM)

    return (y_prompt, y_sample, new_cmp_p, new_cmp_s, new_slc_p, new_slc_s, new_win_p, new_win_s,
            new_ret_p, new_ret_s, new_mem_p)
```

```python
import functools

import numpy as np
import jax
import jax.numpy as jnp
from jax import lax
from jax.experimental import pallas as pl
from jax.experimental.pallas import tpu as pltpu

D_MODEL = 1024
SEQ = 8192
DEC_SEQ = 8
PAST_LEN = 8192
PAGE_SIZE = 128
HEAD_DIM = 64
NSA_HEADS = 8
NSA_KV_HEADS = 2
NSA_GROUP = 4
CMP_LEN = 32
CMP_STRIDE = 16
CMP_HID = 128
SLC_BLOCK = 64
SLC_TOPK = 16
WINDOW = 512
RET_HEADS = 4
N_MEM = 256
Q_BLOCK = 128
ROPE_BASE = 10000.0
EPS = 1e-6
NEG_INF = -1e30
SCALE = HEAD_DIM ** -0.5

N_CMP = 512
N_SLC = 128
LANES = 128
ONES_ROWS = 16
V_ROWS = HEAD_DIM + ONES_ROWS
LOG2E = 1.4426950408889634
SAMPLE_SEQS_PER_STEP = 16
VMEM_LIMIT = 56 * 1024 * 1024

_W_Q, _W_KVC, _W_KVS, _W_KVW, _W_QKVR, _W_QM, _W_Z, _W_G = 0, 512, 768, 1024, 1280, 2048, 2304, 3328
IN_WP = 3456

F32 = jnp.float32
BF16 = jnp.bfloat16

_LOG_G = [float(np.log1p(-(2.0 ** (-5.0 - h)))) for h in range(RET_HEADS)]


def _dot(a, b):
    return jnp.dot(a, b, preferred_element_type=F32)


def _dot_nt(a, b):
    return lax.dot_general(a, b, (((1,), (1,)), ((), ())), preferred_element_type=F32)


def _split3(x):
    hi = x.astype(BF16)
    r1 = x - hi.astype(F32)
    mid = r1.astype(BF16)
    lo = (r1 - mid.astype(F32)).astype(BF16)
    return hi, mid, lo


def _dot_exact_rhs(a_bf16, x):
    hi, mid, lo = _split3(x)
    return _dot(a_bf16, hi) + _dot(a_bf16, mid) + _dot(a_bf16, lo)


def _dot_exact_lhs(x, b_bf16):
    hi, mid, lo = _split3(x)
    return _dot(hi, b_bf16) + _dot(mid, b_bf16) + _dot(lo, b_bf16)


def _iota(shape, dim):
    return lax.broadcasted_iota(jnp.int32, shape, dim)


def _params(sem):
    return pltpu.CompilerParams(dimension_semantics=sem, vmem_limit_bytes=VMEM_LIMIT)


def _inproj_body(x_ref, g_ref, w_ref, qn_ref, kvc_ref, kvs_ref, kvw_ref, qkvr_ref, qm_ref, z_ref, gt_ref,
                 *attn_refs, transposed_kv):
    x = x_ref[...]
    xn = x * lax.rsqrt(jnp.mean(x * x, axis=-1, keepdims=True) + EPS) * g_ref[...]
    xb = xn.astype(BF16)

    def proj(a, b):
        return _dot(xb, w_ref[:, a:b])

    qn_ref[...] = proj(_W_Q, _W_KVC).astype(BF16)
    kvc = proj(_W_KVC, _W_KVS)
    kvs = proj(_W_KVS, _W_KVW)
    kvw = proj(_W_KVW, _W_QKVR)
    if transposed_kv:
        ksk_ref, ksvt_ref, kwk_ref, kwvt_ref = attn_refs
        kvc_ref[0] = kvc.T
        kvs_t = kvs.T
        kvs_ref[0] = kvs_t
        tm = kvs.shape[0]
        lane = _iota((tm, LANES), 1)
        blk_in_tile = _iota((tm, LANES), 0) // SLC_BLOCK
        k2 = kvs[:, :LANES]
        ksk_ref[:, 0:LANES] = jnp.where(lane < HEAD_DIM, k2,
                                        jnp.where(lane - HEAD_DIM == blk_in_tile, 1.0, 0.0)).astype(BF16)
        ksk_ref[:, LANES:2 * LANES] = jnp.where(lane >= HEAD_DIM, k2,
                                                jnp.where(lane == blk_in_tile, 1.0, 0.0)).astype(BF16)
        ones_rows = jnp.where(_iota((ONES_ROWS, tm), 0) == 0, 1.0, 0.0)
        ksvt_ref[0] = jnp.concatenate([kvs_t[LANES:LANES + HEAD_DIM, :], ones_rows,
                                       kvs_t[LANES + HEAD_DIM:, :], ones_rows], axis=0).astype(BF16)
        kvw_t = kvw.T
        kvw_ref[0] = kvw_t
        kwk_ref[...] = kvw[:, :LANES].astype(BF16)
        kwvt_ref[0] = kvw_t[LANES:, :].astype(BF16)
    else:
        kvc_ref[...] = kvc
        kvs_ref[...] = kvs
        kvw_ref[...] = kvw
    qkvr_ref[...] = proj(_W_QKVR, _W_QM)
    qm_ref[...] = proj(_W_QM, _W_Z)
    z_ref[...] = proj(_W_Z, _W_G)
    gt_ref[...] = jax.nn.sigmoid(proj(_W_G, IN_WP)).T


def _inproj(x2d, ln_g, w_perm, rows_per_batch, transposed_kv):
    n = x2d.shape[0]
    tm = 512
    nt = n // tm
    row = lambda w: pl.BlockSpec((tm, w), lambda i: (i, 0))
    if transposed_kv:
        tpb = rows_per_batch // tm
        kv_shape = jax.ShapeDtypeStruct((n // rows_per_batch, 256, rows_per_batch), F32)
        kv_spec = pl.BlockSpec((1, 256, tm), lambda i: (i // tpb, 0, i % tpb))
    else:
        kv_shape = jax.ShapeDtypeStruct((n, 256), F32)
        kv_spec = row(256)
    out_shape = [
        jax.ShapeDtypeStruct((n, 512), BF16),
        kv_shape, kv_shape, kv_shape,
        jax.ShapeDtypeStruct((n, 768), F32),
        jax.ShapeDtypeStruct((n, 256), F32),
        jax.ShapeDtypeStruct((n, 1024), F32),
        jax.ShapeDtypeStruct((LANES, n), F32),
    ]
    out_specs = [row(512), kv_spec, kv_spec, kv_spec, row(768), row(256), row(1024),
                 pl.BlockSpec((LANES, tm), lambda i: (0, i))]
    if transposed_kv:
        tile_t = pl.BlockSpec((1, LANES, tm), lambda i: (i, 0, 0))
        tile_v = pl.BlockSpec((1, NSA_KV_HEADS * V_ROWS, tm), lambda i: (i, 0, 0))
        out_shape += [jax.ShapeDtypeStruct((n, NSA_KV_HEADS * LANES), BF16),
                      jax.ShapeDtypeStruct((nt, NSA_KV_HEADS * V_ROWS, tm), BF16),
                      jax.ShapeDtypeStruct((n, LANES), BF16),
                      jax.ShapeDtypeStruct((nt, LANES, tm), BF16)]
        out_specs += [row(NSA_KV_HEADS * LANES), tile_v, row(LANES), tile_t]
    return pl.pallas_call(
        functools.partial(_inproj_body, transposed_kv=transposed_kv),
        grid=(nt,),
        in_specs=[row(D_MODEL),
                  pl.BlockSpec((1, D_MODEL), lambda i: (0, 0)),
                  pl.BlockSpec((D_MODEL, IN_WP), lambda i: (0, 0))],
        out_specs=tuple(out_specs),
        out_shape=tuple(out_shape),
        compiler_params=_params(("arbitrary",)),
        name="inproj",
    )(x2d, ln_g, w_perm)


def _mem_kv_body(x_ref, g_ref, w_ref, o_ref):
    x = x_ref[0]
    xn = x * lax.rsqrt(jnp.mean(x * x, axis=-1, keepdims=True) + EPS) * g_ref[...]
    o_ref[0] = _dot(xn.astype(BF16), w_ref[...]).T


def _mem_kv(mem, ln_g, w_bf16):
    b = mem.shape[0]
    nw = w_bf16.shape[1]
    return pl.pallas_call(
        _mem_kv_body,
        grid=(b,),
        in_specs=[pl.BlockSpec((1, N_MEM, D_MODEL), lambda i: (i, 0, 0)),
                  pl.BlockSpec((1, D_MODEL), lambda i: (0, 0)),
                  pl.BlockSpec((D_MODEL, nw), lambda i: (0, 0))],
        out_specs=pl.BlockSpec((1, nw, N_MEM), lambda i: (i, 0, 0)),
        out_shape=jax.ShapeDtypeStruct((b, nw, N_MEM), F32),
        compiler_params=_params(("arbitrary",)),
        name="mem_kv",
    )(mem, ln_g, w_bf16)


def _compress_slab(slabs_t, x_refs, pe_ref, w1_ref, b1_ref, w2_ref, out_refs):
    rows = _iota((N_CMP, LANES), 0)
    hw = 2 * CMP_HID
    chunks_per_page = LANES // CMP_STRIDE
    for v in range(2):
        for j in range(PAST_LEN // LANES):
            xt = slabs_t[v][:, j * LANES:(j + 1) * LANES].T
            for cc in range(chunks_per_page):
                r0 = (j * chunks_per_page + cc) * X_PITCH
                x_refs[v][r0:r0 + CMP_STRIDE, :] = xt[cc * CMP_STRIDE:(cc + 1) * CMP_STRIDE, :]
        nh = N_CMP // CMP_ROW_SPLITS
        accs = []
        for hf in range(CMP_ROW_SPLITS):
            acc = jnp.zeros((nh + (PE_ROWS if hf == 0 else 0), 2 * hw), F32)
            for pp in range(CMP_STRIDE // 2):
                r0 = hf * nh * X_PITCH + 2 * pp
                xcat = jnp.concatenate([x_refs[v][pl.ds(r0, nh, stride=X_PITCH), :],
                                        x_refs[v][pl.ds(r0 + 1, nh, stride=X_PITCH), :]], axis=1).astype(BF16)
                lhs = jnp.concatenate([xcat, pe_ref[v, pp]], axis=0) if hf == 0 else xcat
                acc = acc + _dot(lhs, w1_ref[v, pp])
            accs.append(acc)
        bias = b1_ref[v] + accs[0][nh:nh + 1, 0:hw] + accs[0][nh + 1:nh + 2, hw:2 * hw]
        acc = jnp.concatenate([accs[0][0:nh]] + accs[1:], axis=0)
        nxt = pltpu.roll(acc[:, hw:2 * hw], N_CMP - 1, 0)
        hid = jax.nn.gelu(acc[:, 0:hw] + nxt + bias)
        out = _dot(hid.astype(BF16), w2_ref[v])
        out_refs[v][...] = jnp.where(rows < N_CMP - 1, out, 0.0)


def _compress_prompt_body(k_ref, v_ref, pe_ref, w1_ref, b1_ref, w2_ref, kc_ref, vc_ref, xk_ref, xv_ref):
    _compress_slab((k_ref.at[0], v_ref.at[0]), (xk_ref, xv_ref), pe_ref, w1_ref, b1_ref, w2_ref,
                   (kc_ref.at[0], vc_ref.at[0]))


PE_ROWS = 16
CMP_ROW_SPLITS = 4
X_PITCH = 24


def _x_scratch():
    return [pltpu.VMEM((N_CMP * X_PITCH, LANES), F32) for _ in range(2)]


def _comp_weight_specs():
    z = lambda *_: (0, 0, 0)
    z4 = lambda *_: (0, 0, 0, 0)
    return [pl.BlockSpec((2, CMP_STRIDE // 2, PE_ROWS, 2 * LANES), z4),
            pl.BlockSpec((2, CMP_STRIDE // 2, 2 * LANES, 4 * CMP_HID), z4),
            pl.BlockSpec((2, 1, 2 * CMP_HID), z),
            pl.BlockSpec((2, 2 * CMP_HID, LANES), z)]


def _compress_prompt(kvc_t, comp_w):
    b = kvc_t.shape[0]
    out = jax.ShapeDtypeStruct((b, N_CMP, LANES), F32)
    return pl.pallas_call(
        _compress_prompt_body,
        grid=(b,),
        in_specs=[pl.BlockSpec((1, LANES, SEQ), lambda i: (i, 0, 0)),
                  pl.BlockSpec((1, LANES, SEQ), lambda i: (i, 1, 0))] + _comp_weight_specs(),
        out_specs=(pl.BlockSpec((1, N_CMP, LANES), lambda i: (i, 0, 0)),) * 2,
        out_shape=(out, out),
        scratch_shapes=_x_scratch(),
        compiler_params=_params(("arbitrary",)),
        name="compress_prompt",
    )(kvc_t, kvc_t, *comp_w)


N_PAGES = PAST_LEN // PAGE_SIZE


def _page_copy(pt_ref, pool_ref, slab_ref, sem_ref, seq, slot, s, j):
    return pltpu.make_async_copy(pool_ref.at[pt_ref[seq, j]],
                                 slab_ref.at[slot, s, :, :, pl.ds(j * PAGE_SIZE, PAGE_SIZE)],
                                 sem_ref.at[slot])


def _start_pages(pt_ref, pool_ref, slab_ref, sem_ref, step, slot, inline=False):
    spb = slab_ref.shape[1]
    for s in range(spb):
        if inline:
            for j in range(N_PAGES):
                _page_copy(pt_ref, pool_ref, slab_ref, sem_ref, step * spb + s, slot, s, j).start()
        else:
            def body(j, c):
                _page_copy(pt_ref, pool_ref, slab_ref, sem_ref, step * spb + s, slot, s, j).start()
                return c
            lax.fori_loop(0, N_PAGES, body, 0, unroll=8)


def _wait_pages(pt_ref, pool_ref, slab_ref, sem_ref, step, slot):
    spb = slab_ref.shape[1]
    for s in range(spb):
        def body(j, c):
            _page_copy(pt_ref, pool_ref, slab_ref, sem_ref, step * spb + s, slot, s, j).wait()
            return c
        lax.fori_loop(0, N_PAGES, body, 0, unroll=8)


SLAB_SLOTS = 2


def _slab_substeps(pt_ref, pool_ref, slab_ref, sem_ref, compute):
    g = pl.program_id(0)
    n_sub = pl.num_programs(0) * SLAB_SLOTS
    args = (pt_ref, pool_ref, slab_ref, sem_ref)

    @pl.when(g == 0)
    def _():
        _start_pages(*args, 0, 0)

    for slot in range(SLAB_SLOTS):
        step = g * SLAB_SLOTS + slot
        _wait_pages(*args, step, slot)
        _start_pages(*args, jnp.minimum(step + 1, n_sub - 1), (slot + 1) % SLAB_SLOTS, inline=True)
        compute(slot)

    @pl.when(g == pl.num_programs(0) - 1)
    def _():
        _wait_pages(*args, n_sub - 1, 0)


def _stream_slab(pt_ref, pool_ref, slab_ref, sem_ref):
    b = pl.program_id(0)
    slot = b % SLAB_SLOTS
    args = (pt_ref, pool_ref, slab_ref, sem_ref)

    @pl.when(b == 0)
    def _():
        _start_pages(*args, 0, 0)

    @pl.when(b + 1 < pl.num_programs(0))
    def _():
        _start_pages(*args, b + 1, 1 - slot)

    _wait_pages(*args, b, slot)
    return slot


def _compress_sample_body(pt_ref, pool_ref, pe_ref, w1_ref, b1_ref, w2_ref, kc_ref, vc_ref, slab_ref, sem_ref,
                          xk_ref, xv_ref):
    def compute(slot):
        _compress_slab((slab_ref.at[slot, 0, 0], slab_ref.at[slot, 0, 1]), (xk_ref, xv_ref), pe_ref, w1_ref, b1_ref,
                       w2_ref, (kc_ref.at[slot], vc_ref.at[slot]))

    _slab_substeps(pt_ref, pool_ref, slab_ref, sem_ref, compute)


def _slab_scratch(seqs_per_step=1):
    return [pltpu.VMEM((2, seqs_per_step, 2, LANES, PAST_LEN), F32), pltpu.SemaphoreType.DMA((2,))]


def _compress_sample(page_table, pool, comp_w):
    nb = page_table.shape[0]
    out = jax.ShapeDtypeStruct((nb, N_CMP, LANES), F32)
    grid_spec = pltpu.PrefetchScalarGridSpec(
        num_scalar_prefetch=1,
        grid=(nb // SLAB_SLOTS,),
        in_specs=[pl.BlockSpec(memory_space=pl.ANY)] + _comp_weight_specs(),
        out_specs=(pl.BlockSpec((SLAB_SLOTS, N_CMP, LANES), lambda i, pt: (i, 0, 0)),) * 2,
        scratch_shapes=_slab_scratch() + _x_scratch(),
    )
    return pl.pallas_call(
        _compress_sample_body,
        grid_spec=grid_spec,
        out_shape=(out, out),
        compiler_params=_params(("arbitrary",)),
        name="compress_sample",
    )(page_table, pool, *comp_w)


def _top_k_rows(score, blk, k):
    cand = score > -jnp.inf
    nblk = score.shape[0]
    for _ in range(k):
        m = jnp.max(score, axis=0, keepdims=True)
        idx = jnp.min(jnp.where(score == m, blk, nblk), axis=0, keepdims=True)
        score = jnp.where(blk == idx, -jnp.inf, score)
    return cand & (score == -jnp.inf)


def _add_q_bias(s, bias):
    return jnp.concatenate([s[:, g * Q_BLOCK:(g + 1) * Q_BLOCK] + bias for g in range(NSA_GROUP)], axis=1)


def _safe_inv(l):
    return 1.0 / jnp.maximum(l, 1e-30)


KT_SLC = 512
WIN_KEYS = WINDOW + Q_BLOCK
N_FORCED = 3


def _nsa_prompt_body(qn_ref, gt_ref, ksk_ref, ksvt_ref, kwk_ref, kwvt_ref, kc_ref, vc_ref, o_ref, selb_ref, s_ref):
    i = pl.program_id(1)
    q0 = i * Q_BLOCK
    qt = (qn_ref[...].astype(F32) * (SCALE * LOG2E)).T
    t_row = q0 + _iota((1, Q_BLOCK), 1)
    kc = kc_ref[0].astype(BF16)
    vct = vc_ref[0].T.astype(BF16)
    gt = gt_ref[...]
    zeros_q = jnp.zeros((HEAD_DIM, NSA_GROUP * Q_BLOCK), F32)

    mi = _iota((N_SLC, N_CMP), 0)
    mj = _iota((N_SLC, N_CMP), 1)
    msel = ((mj >= 4 * mi - 1) & (mj <= 4 * mi + 3) & (mj < N_CMP - 1)).astype(BF16)
    blk = _iota((N_SLC, Q_BLOCK), 0)
    cur = t_row // SLC_BLOCK
    forced = (blk == 0) | (blk == cur) | (blk == cur - 1)
    free = (blk * SLC_BLOCK <= t_row) & jnp.logical_not(forced)
    cend = _iota((N_CMP, 1), 0) * CMP_STRIDE + (CMP_LEN - 1)
    bias_c = jnp.where(cend <= t_row, 0.0, NEG_INF)
    wt = jnp.maximum(i - WINDOW // Q_BLOCK, 0)
    w0 = pl.multiple_of(wt * Q_BLOCK, Q_BLOCK)
    kpw = w0 + _iota((WIN_KEYS, 1), 0)
    bias_w = jnp.where((kpw <= t_row) & (kpw > t_row - WINDOW), 0.0, NEG_INF)
    krow = _iota((SLC_BLOCK, 1), 0)

    qgs, qps, o_cs, o_ws, cand_scores = [], [], [], [], []
    vrows = [slice(kvh * HEAD_DIM, (kvh + 1) * HEAD_DIM) for kvh in range(NSA_KV_HEADS)]
    for kvh in range(NSA_KV_HEADS):
        qg = jnp.concatenate([qt[(kvh * NSA_GROUP + g) * HEAD_DIM:(kvh * NSA_GROUP + g + 1) * HEAD_DIM, :]
                              for g in range(NSA_GROUP)], axis=1)
        qp = jnp.concatenate([qg, zeros_q] if kvh == 0 else [zeros_q, qg], axis=0).astype(BF16)
        vrow = vrows[kvh]
        qgs.append(qg.astype(BF16))
        qps.append(qp)

        s = _add_q_bias(_dot(kc, qp), bias_c)
        m = jnp.max(s, axis=0, keepdims=True)
        e = jnp.exp2(s - m)
        inv = jnp.where(m > 0.5 * NEG_INF, _safe_inv(jnp.sum(e, axis=0, keepdims=True)), 0.0)
        p = e * inv
        o_c = _dot(vct[vrow, :], p.astype(BF16))
        imp = (p[:, 0:Q_BLOCK] + p[:, Q_BLOCK:2 * Q_BLOCK]
               + p[:, 2 * Q_BLOCK:3 * Q_BLOCK] + p[:, 3 * Q_BLOCK:4 * Q_BLOCK])

        cand_scores.append(jnp.where(free, _dot_exact_rhs(msel, imp), -jnp.inf))
        o_cs.append(o_c)

    blk2 = jnp.concatenate([blk] * NSA_KV_HEADS, axis=1)
    picked = _top_k_rows(jnp.concatenate(cand_scores, axis=1), blk2, SLC_TOPK - N_FORCED)
    for kvh in range(NSA_KV_HEADS):
        selb_ref[kvh] = jnp.where(forced | picked[:, kvh * Q_BLOCK:(kvh + 1) * Q_BLOCK], 0.0, NEG_INF)

    for kvh in range(NSA_KV_HEADS):
        qp, vrow = qps[kvh], vrows[kvh]
        sw = _add_q_bias(_dot(kwk_ref[pl.ds(w0, WIN_KEYS), :], qp), bias_w)
        ew = jnp.exp2(sw - jnp.max(sw, axis=0, keepdims=True))
        inv_w = _safe_inv(jnp.sum(ew, axis=0, keepdims=True))
        pw = ew.astype(BF16)
        o_w = jnp.zeros((HEAD_DIM, NSA_GROUP * Q_BLOCK), F32)
        for r in range(WIN_KEYS // Q_BLOCK):
            tix = wt + r
            vt = kwvt_ref[tix // 4, vrow, :]
            sub = tix % 4
            vpiece = jnp.where(sub == 0, vt[:, 0:128],
                               jnp.where(sub == 1, vt[:, 128:256],
                                         jnp.where(sub == 2, vt[:, 256:384], vt[:, 384:512])))
            o_w = o_w + _dot(vpiece, pw[r * Q_BLOCK:(r + 1) * Q_BLOCK, :])
        o_ws.append(o_w * inv_w)

    bias_pad = jnp.zeros((HEAD_DIM - 8, NSA_GROUP * Q_BLOCK), F32)

    def scores(j, buf):
        k0 = pl.multiple_of(j * KT_SLC, KT_SLC)
        for kvh in range(NSA_KV_HEADS):
            b8 = selb_ref[kvh, pl.ds(pl.multiple_of(j * 8, 8), 8), :]
            b_rows = jnp.concatenate([jnp.concatenate([b8] * NSA_GROUP, axis=1), bias_pad], axis=0).astype(BF16)
            q_aug = jnp.concatenate([qgs[kvh], b_rows] if kvh == 0 else [b_rows, qgs[kvh]], axis=0)
            s_ref[buf, kvh] = _dot(ksk_ref[pl.ds(k0, KT_SLC), kvh * LANES:(kvh + 1) * LANES], q_aug)

    def consume(j, buf, state, causal):
        k0 = j * KT_SLC
        out = []
        for kvh in range(NSA_KV_HEADS):
            m_run, acc = state[kvh]
            sj = s_ref[buf, kvh]
            if causal:
                parts = []
                for r in range(KT_SLC // SLC_BLOCK):
                    ok = k0 + r * SLC_BLOCK + krow <= t_row
                    parts.append(jnp.concatenate(
                        [jnp.where(ok, sj[r * SLC_BLOCK:(r + 1) * SLC_BLOCK, g * Q_BLOCK:(g + 1) * Q_BLOCK], NEG_INF)
                         for g in range(NSA_GROUP)], axis=1))
                sj = jnp.concatenate(parts, axis=0)
            m_new = jnp.maximum(m_run, jnp.max(sj, axis=0, keepdims=True))
            alpha = jnp.exp2(m_run - m_new)
            ej = jnp.exp2(sj - m_new).astype(BF16)
            acc = alpha * acc + _dot(ksvt_ref[j, kvh * V_ROWS:(kvh + 1) * V_ROWS, :], ej)
            out.append((m_new, acc))
        return tuple(out)

    init1 = (jnp.full((1, NSA_GROUP * Q_BLOCK), NEG_INF, F32),
             jnp.zeros((V_ROWS, NSA_GROUP * Q_BLOCK), F32))
    n_tiles = q0 // KT_SLC + 1
    n_pairs = (n_tiles + 1) // 2

    def pair(jj, state):
        scores(2 * jj + 1, 1)
        state = consume(2 * jj, 0, state, False)
        scores(2 * jj + 2, 0)
        return consume(2 * jj + 1, 1, state, False)

    scores(0, 0)
    state = lax.fori_loop(0, n_pairs - 1, pair, (init1,) * NSA_KV_HEADS)
    last = 2 * n_pairs - 2

    def last_two(st):
        scores(last + 1, 1)
        return consume(last + 1, 1, consume(last, 0, st, True), True)

    state = lax.cond(n_tiles % 2 == 0, last_two, lambda st: consume(last, 0, st, True), state)

    outs = []
    for kvh in range(NSA_KV_HEADS):
        acc_s = state[kvh][1]
        o_s = acc_s[0:HEAD_DIM, :] * _safe_inv(acc_s[HEAD_DIM:HEAD_DIM + 1, :])

        def gate(br):
            base = br * NSA_HEADS + kvh * NSA_GROUP
            return jnp.concatenate([gt[base + g:base + g + 1, :] for g in range(NSA_GROUP)], axis=1)

        o = gate(0) * o_cs[kvh] + gate(1) * o_s + gate(2) * o_ws[kvh]
        outs.extend([o[:, g * Q_BLOCK:(g + 1) * Q_BLOCK] for g in range(NSA_GROUP)])
    o_ref[...] = jnp.concatenate(outs, axis=0).T


def _nsa_prompt(qn, gt, ksk, ksvt, kwk, kwvt, kc, vc, batch):
    nqb = SEQ // Q_BLOCK
    ntile = SEQ // 512
    return pl.pallas_call(
        _nsa_prompt_body,
        grid=(batch, nqb),
        in_specs=[pl.BlockSpec((Q_BLOCK, 512), lambda b, i: (b * nqb + i, 0)),
                  pl.BlockSpec((LANES, Q_BLOCK), lambda b, i: (0, b * nqb + i)),
                  pl.BlockSpec((SEQ, NSA_KV_HEADS * LANES), lambda b, i: (b, 0)),
                  pl.BlockSpec((ntile, NSA_KV_HEADS * V_ROWS, 512), lambda b, i: (b, 0, 0)),
                  pl.BlockSpec((SEQ, LANES), lambda b, i: (b, 0)),
                  pl.BlockSpec((ntile, LANES, 512), lambda b, i: (b, 0, 0)),
                  pl.BlockSpec((1, N_CMP, LANES), lambda b, i: (b, 0, 0)),
                  pl.BlockSpec((1, N_CMP, LANES), lambda b, i: (b, 0, 0))],
        out_specs=pl.BlockSpec((Q_BLOCK, 512), lambda b, i: (b * nqb + i, 0)),
        out_shape=jax.ShapeDtypeStruct((batch * SEQ, 512), F32),
        scratch_shapes=[pltpu.VMEM((NSA_KV_HEADS, N_SLC, Q_BLOCK), F32),
                        pltpu.VMEM((2, NSA_KV_HEADS, KT_SLC, NSA_GROUP * Q_BLOCK), F32)],
        compiler_params=_params(("arbitrary", "arbitrary")),
        name="nsa_prompt",
    )(qn, gt, ksk, ksvt, kwk, kwvt, kc, vc)


def _softmax_parts(parts, masks):
    m = None
    for s, mk in zip(parts, masks):
        mi = jnp.max(jnp.where(mk, s, NEG_INF), axis=-1, keepdims=True)
        m = mi if m is None else jnp.maximum(m, mi)
    es = [jnp.where(mk, jnp.exp(jnp.where(mk, s, NEG_INF) - m), 0.0) for s, mk in zip(parts, masks)]
    l = None
    for e in es:
        li = jnp.sum(e, axis=-1, keepdims=True)
        l = li if l is None else l + li
    inv = 1.0 / jnp.maximum(l, 1e-30)
    return [e * inv for e in es]


KC_SLC = 2048
ROWS_S = NSA_HEADS * DEC_SEQ


def _pad_rows(x, n):
    if x.shape[0] == n:
        return x
    return jnp.concatenate([x, jnp.zeros((n - x.shape[0], x.shape[1]), x.dtype)], axis=0)


def _nsa_sample_body(pt_ref, pool_ref, qp_ref, g_ref, kc_ref, vc_ref, kvs_ref, kvw_ref, win_ref, wtail_ref,
                     expand_ref, o_ref, nwin_ref, slab_ref, sem_ref):
    slot = _stream_slab(pt_ref, pool_ref, slab_ref, sem_ref)
    spb = slab_ref.shape[1]
    seqs = range(spb)
    bi = list(seqs)
    qi = _iota((ROWS_S, 1), 0) % DEC_SEQ
    t = PAST_LEN + qi
    jn = _iota((1, LANES), 1)
    n_chunk = PAST_LEN // KC_SLC
    qs = [(qp_ref[bi[s]] * SCALE).astype(BF16) for s in seqs]
    new_rows = [slice(bi[s] * DEC_SEQ, (bi[s] + 1) * DEC_SEQ) for s in seqs]

    cend = _iota((1, N_CMP), 1) * CMP_STRIDE + (CMP_LEN - 1)
    p_cs = [_softmax_parts([_dot_nt(qs[s], kc_ref[bi[s]].astype(BF16))], [cend <= t])[0] for s in seqs]
    o_cs = [_dot(p_cs[s].astype(BF16), vc_ref[bi[s]].astype(BF16)) for s in seqs]

    half = ROWS_S // NSA_KV_HEADS
    imp = jnp.concatenate(
        [p_cs[s][k * half:k * half + 8] + p_cs[s][k * half + 8:k * half + 16]
         + p_cs[s][k * half + 16:k * half + 24] + p_cs[s][k * half + 24:k * half + 32]
         for s in seqs for k in range(NSA_KV_HEADS)], axis=0)
    mi = _iota((N_CMP, N_SLC), 0)
    mj = _iota((N_CMP, N_SLC), 1)
    msel_t = ((mi >= 4 * mj - 1) & (mi <= 4 * mj + 3) & (mi < N_CMP - 1)).astype(BF16)
    score = _dot_exact_lhs(imp, msel_t)
    score_t = _pad_rows(score, LANES).T
    blk = _iota((N_SLC, LANES), 0)
    forced = (blk == 0) | (blk == N_SLC - 1)
    picked = _top_k_rows(jnp.where(forced, -jnp.inf, score_t), blk, SLC_TOPK - 1 - 2)
    unsel = jnp.where(forced | picked, 0.0, 1.0).T
    unsel64 = [jnp.concatenate([unsel[s * 16:s * 16 + 8]] * NSA_GROUP + [unsel[s * 16 + 8:s * 16 + 16]] * NSA_GROUP,
                               axis=0).astype(BF16) for s in seqs]

    new_bias = jnp.where((jn <= qi) & (jn < DEC_SEQ), 0.0, NEG_INF)
    k_news = [_pad_rows(kvs_ref[new_rows[s], 0:LANES], LANES).astype(BF16) for s in seqs]
    v_news = [_pad_rows(kvs_ref[new_rows[s], LANES:2 * LANES], LANES).astype(BF16) for s in seqs]
    s_news = [_dot_nt(qs[s], k_news[s]) + new_bias for s in seqs]
    scs = [[_dot(qs[s], slab_ref[slot, s, 0, :, c * KC_SLC:(c + 1) * KC_SLC].astype(BF16))
            + _dot(unsel64[s], expand_ref[:, c * KC_SLC:(c + 1) * KC_SLC]) * NEG_INF
            for c in range(n_chunk)] for s in seqs]
    o_ss = []
    for s in seqs:
        m_s = jnp.max(s_news[s], axis=-1, keepdims=True)
        for sc in scs[s]:
            m_s = jnp.maximum(m_s, jnp.max(sc, axis=-1, keepdims=True))
        e_new = jnp.exp(s_news[s] - m_s)
        l_s = jnp.sum(e_new, axis=-1, keepdims=True)
        acc = _dot(e_new.astype(BF16), v_news[s])
        for c in range(n_chunk):
            ec = jnp.exp(scs[s][c] - m_s)
            l_s = l_s + jnp.sum(ec, axis=-1, keepdims=True)
            acc = acc + _dot_nt(ec.astype(BF16), slab_ref[slot, s, 1, :, c * KC_SLC:(c + 1) * KC_SLC].astype(BF16))
        o_ss.append(acc * _safe_inv(l_s))

    wb = win_ref.shape[2]
    kpos_b = PAST_LEN - wb + _iota((1, wb), 1)
    kpos_n = PAST_LEN + jn
    mask_b = (kpos_b <= t) & (kpos_b > t - WINDOW) & (kpos_b >= 0)
    mask_n = (kpos_n <= t) & (kpos_n > t - WINDOW) & (jn < DEC_SEQ)
    for s in seqs:
        win_t = win_ref[bi[s]]
        kw_new = _pad_rows(kvw_ref[new_rows[s], 0:LANES], LANES).astype(BF16)
        vw_new = _pad_rows(kvw_ref[new_rows[s], LANES:2 * LANES], LANES).astype(BF16)
        p_b, p_n = _softmax_parts([_dot(qs[s], win_t[0:LANES, :].astype(BF16)), _dot_nt(qs[s], kw_new)],
                                  [mask_b, mask_n])
        o_w = _dot_nt(p_b.astype(BF16), win_t[LANES:2 * LANES, :].astype(BF16)) + _dot(p_n.astype(BF16), vw_new)

        g = g_ref[bi[s]]
        o_ref[bi[s]] = g[:, 0:1] * o_cs[s] + g[:, 1:2] * o_ss[s] + g[:, 2:3] * o_w
        shifted = pltpu.roll(win_t, wb - DEC_SEQ, 1)
        nwin_ref[bi[s], :, 0:wb - LANES] = shifted[:, 0:wb - LANES]
        nwin_ref[bi[s], :, wb - LANES:wb] = jnp.where(jn >= LANES - DEC_SEQ, wtail_ref[bi[s]],
                                                      shifted[:, wb - LANES:wb])


NSA_SAMPLE_SEQS_PER_STEP = 2


def _nsa_sample(page_table, pool_slc, qp, gates, kc, vc, kvs, kvw, win_t, wtail, expand):
    nb = page_table.shape[0]
    wb = win_t.shape[2]
    sps = spb = NSA_SAMPLE_SEQS_PER_STEP
    per_b = lambda *shape: pl.BlockSpec((sps,) + shape, lambda i, pt: (i,) + (0,) * len(shape))
    rows8 = pl.BlockSpec((sps * DEC_SEQ, 256), lambda i, pt: (i, 0))
    grid_spec = pltpu.PrefetchScalarGridSpec(
        num_scalar_prefetch=1,
        grid=(nb // sps,),
        in_specs=[pl.BlockSpec(memory_space=pl.ANY),
                  per_b(ROWS_S, LANES), per_b(ROWS_S, LANES), per_b(N_CMP, LANES), per_b(N_CMP, LANES),
                  rows8, rows8, per_b(256, wb), per_b(256, LANES),
                  pl.BlockSpec((N_SLC, PAST_LEN), lambda i, pt: (0, 0))],
        out_specs=(per_b(ROWS_S, LANES), per_b(256, wb)),
        scratch_shapes=_slab_scratch(spb),
    )
    return pl.pallas_call(
        _nsa_sample_body,
        grid_spec=grid_spec,
        out_shape=(jax.ShapeDtypeStruct((nb, ROWS_S, LANES), F32),
                   jax.ShapeDtypeStruct((nb, 256, wb), F32)),
        compiler_params=_params(("arbitrary",)),
        name="nsa_sample",
    )(page_table, pool_slc, qp, gates, kc, vc, kvs, kvw, win_t, wtail, expand)


RET_W = RET_HEADS * HEAD_DIM


def _retention_body(q_ref, k_ref, v_ref, cos_ref, sin_ref, st0_ref, gn_ref, o_ref, st_ref, state_ref, *, chunk):
    c = pl.program_id(1)
    ck = max(chunk, LANES)
    spb = st0_ref.shape[0]

    lane = _iota((1, RET_W), 1)
    head_of_lane = lane // HEAD_DIM
    row_head = _iota((RET_W, 1), 0) // HEAD_DIM

    @pl.when(c == 0)
    def _():
        for bb in range(spb):
            st_c = st0_ref[bb]
            state_ref[bb] = jnp.concatenate([jnp.where(row_head == h, st_c, 0.0) for h in range(RET_HEADS)], axis=1)
    low_half = (lane % HEAD_DIM) < HEAD_DIM // 2
    cos = cos_ref[...]
    sin = sin_ref[...]

    def rope(x):
        rot = jnp.where(low_half, -pltpu.roll(x, RET_W - HEAD_DIM // 2, 1), pltpu.roll(x, HEAD_DIM // 2, 1))
        return x * cos + rot * sin

    logg_lane = jnp.zeros((1, RET_W), F32)
    for h in range(RET_HEADS):
        logg_lane = jnp.where(head_of_lane == h, _LOG_G[h], logg_lane)

    n = _iota((chunk, 1), 0).astype(F32)
    nk = _iota((ck, 1), 0).astype(F32)
    diff = n - _iota((1, ck), 1).astype(F32)
    in_chunk = _iota((1, ck), 1) < chunk
    dmats = [jnp.where((diff >= 0) & in_chunk, jnp.exp(_LOG_G[h] * jnp.maximum(diff, 0.0)), 0.0)
             for h in range(RET_HEADS)]
    xi = jnp.exp(logg_lane * (n + 1.0))
    zeta = jnp.where(nk < chunk, jnp.exp(logg_lane * (chunk - 1.0 - nk)), 0.0)
    decay_rows = jnp.zeros((RET_W, 1), F32)
    for h in range(RET_HEADS):
        decay_rows = jnp.where(row_head == h, float(np.exp(_LOG_G[h] * chunk)), decay_rows)
    avg = jnp.where(_iota((RET_W, RET_W), 0) // HEAD_DIM == _iota((RET_W, RET_W), 1) // HEAD_DIM,
                    1.0 / HEAD_DIM, 0.0).astype(BF16)
    gn = gn_ref[...]

    seqs = range(spb)
    qs = [rope(q_ref[bb]) * SCALE for bb in seqs]
    kps = [_pad_rows(rope(k_ref[bb]), ck) for bb in seqs]
    vps = [_pad_rows(v_ref[bb], ck) for bb in seqs]
    hms = [head_of_lane == h for h in range(RET_HEADS)]
    ss = [[_dot_nt(jnp.where(hms[h], qs[bb], 0.0), kps[bb]) * dmats[h] for h in range(RET_HEADS)] for bb in seqs]
    inners = []
    for bb in seqs:
        inner = jnp.zeros((chunk, RET_W), F32)
        for h in range(RET_HEADS):
            inner = inner + jnp.where(hms[h], _dot(ss[bb][h], vps[bb]), 0.0)
        inners.append(inner)
    states = [state_ref[bb] for bb in seqs]
    os_ = [inners[bb] + _dot(qs[bb] * xi, states[bb]) for bb in seqs]
    for bb in seqs:
        kz_t = (kps[bb] * zeta).T
        new_state = decay_rows * states[bb] + jnp.where(row_head == head_of_lane, _dot(kz_t, vps[bb]), 0.0)
        state_ref[bb] = new_state
        st_ref[bb] = (new_state[:, 0:HEAD_DIM] + new_state[:, HEAD_DIM:2 * HEAD_DIM]
                      + new_state[:, 2 * HEAD_DIM:3 * HEAD_DIM] + new_state[:, 3 * HEAD_DIM:4 * HEAD_DIM])

    o = jnp.concatenate(os_, axis=0) if spb > 1 else os_[0]
    mu = _dot_exact_lhs(o, avg)
    d = o - mu
    var = _dot_exact_lhs(d * d, avg)
    y = d * lax.rsqrt(var + EPS) * gn
    for bb in seqs:
        o_ref[bb] = y[bb * chunk:(bb + 1) * chunk, :]


def _retention(qkvr, cos, sin, state, gn, batch, t_len, chunk, seqs_per_step=1):
    nch = t_len // chunk
    spb = seqs_per_step
    qkvr3 = qkvr.reshape(batch, t_len, 3 * RET_W)
    col = lambda j: pl.BlockSpec((spb, chunk, RET_W), lambda b, c: (b, c, j))
    tab = pl.BlockSpec((chunk, RET_W), lambda b, c: (c, 0))
    st = pl.BlockSpec((spb, RET_W, HEAD_DIM), lambda b, c: (b, 0, 0))
    o, st_out = pl.pallas_call(
        functools.partial(_retention_body, chunk=chunk),
        grid=(batch // spb, nch),
        in_specs=[col(0), col(1), col(2), tab, tab, st, pl.BlockSpec((1, RET_W), lambda b, c: (0, 0))],
        out_specs=(col(0), st),
        out_shape=(jax.ShapeDtypeStruct((batch, t_len, RET_W), F32),
                   jax.ShapeDtypeStruct((batch, RET_W, HEAD_DIM), F32)),
        scratch_shapes=[pltpu.VMEM((spb, RET_W, RET_W), F32)],
        compiler_params=_params(("arbitrary", "arbitrary")),
        name="retention",
    )(qkvr3, qkvr3, qkvr3, cos, sin, state, gn)
    return o.reshape(batch * t_len, RET_W), st_out


MEM_W = 4 * HEAD_DIM


def _mem_attn_body(q_ref, mkv_ref, o_ref, *, tm):
    head_of_lane = _iota((1, MEM_W), 1) // HEAD_DIM
    spb = mkv_ref.shape[0]
    q_all = q_ref[...] * SCALE
    scores = []
    for bb in range(spb):
        q = q_all[bb * tm:(bb + 1) * tm, :]
        q4 = jnp.concatenate([jnp.where(head_of_lane == h, q, 0.0) for h in range(4)], axis=0).astype(BF16)
        scores.append(_dot(q4, mkv_ref[bb, 0:MEM_W, :].astype(BF16)))
    s = jnp.concatenate(scores, axis=0)
    e = jnp.exp(s - jnp.max(s, axis=-1, keepdims=True))
    p = (e / jnp.sum(e, axis=-1, keepdims=True)).astype(BF16)
    for bb in range(spb):
        pv = _dot_nt(p[bb * 4 * tm:(bb + 1) * 4 * tm, :], mkv_ref[bb, MEM_W:2 * MEM_W, :].astype(BF16))
        o = jnp.zeros((tm, MEM_W), F32)
        for h in range(4):
            o = o + jnp.where(head_of_lane == h, pv[h * tm:(h + 1) * tm, :], 0.0)
        o_ref[bb * tm:(bb + 1) * tm, :] = o


def _mem_attn(qm, mkv_t, batch, rows_per_batch, tm, seqs_per_step=1):
    nt = rows_per_batch // tm
    spb = seqs_per_step
    assert spb == 1 or nt == 1
    return pl.pallas_call(
        functools.partial(_mem_attn_body, tm=tm),
        grid=(batch // spb, nt),
        in_specs=[pl.BlockSpec((spb * tm, MEM_W), lambda b, i: (b * nt + i, 0)),
                  pl.BlockSpec((spb, 2 * MEM_W, N_MEM), lambda b, i: (b, 0, 0))],
        out_specs=pl.BlockSpec((spb * tm, MEM_W), lambda b, i: (b * nt + i, 0)),
        out_shape=jax.ShapeDtypeStruct((batch * rows_per_batch, MEM_W), F32),
        compiler_params=_params(("arbitrary", "arbitrary")),
        name="mem_attn",
    )(qm, mkv_t)


def _finish_body(x_ref, on_ref, or_ref, om_ref, z_ref, w_ref, g_ref, y_ref):
    o = jnp.concatenate([on_ref[...], or_ref[...], om_ref[...]], axis=-1)
    z = z_ref[...]
    mix = o * (z * jax.nn.sigmoid(z))
    xo = x_ref[...] + _dot(mix.astype(BF16), w_ref[...])
    y_ref[...] = xo * lax.rsqrt(jnp.mean(xo * xo, axis=-1, keepdims=True) + EPS) * g_ref[...]


def _finish(x2d, o_n, o_r, o_m, z, w_out, ln_final):
    n = x2d.shape[0]
    tm = 512
    row = lambda w: pl.BlockSpec((tm, w), lambda i: (i, 0))
    return pl.pallas_call(
        _finish_body,
        grid=(n // tm,),
        in_specs=[row(D_MODEL), row(512), row(256), row(256), row(D_MODEL),
                  pl.BlockSpec((D_MODEL, D_MODEL), lambda i: (0, 0)),
                  pl.BlockSpec((1, D_MODEL), lambda i: (0, 0))],
        out_specs=row(D_MODEL),
        out_shape=jax.ShapeDtypeStruct((n, D_MODEL), F32),
        compiler_params=_params(("arbitrary",)),
        name="finish",
    )(x2d, o_n, o_r, o_m, z, w_out, ln_final)


def _permute_w_in(w):
    sizes = (512, 256, 256, 256, 24, 512, 256, 256, 256, 256, 256, 256)
    offs = np.concatenate([[0], np.cumsum(sizes)])
    part = lambda i: w[:, int(offs[i]):int(offs[i + 1])]
    q_n, kv_c, kv_s, kv_w, g_n, z_n, q_r, k_r, v_r, z_r, q_m, z_m = [part(i) for i in range(12)]
    g_pad = jnp.pad(g_n, ((0, 0), (0, LANES - g_n.shape[1])))
    return jnp.concatenate([q_n, kv_c, kv_s, kv_w, q_r, k_r, v_r, q_m, z_n, z_r, z_m, g_pad], axis=1).astype(BF16)


def _compress_weights(pe, w1, b1, w2):
    npair = CMP_STRIDE // 2
    pe_r = pe.reshape(2, 2, npair, 2, HEAD_DIM)
    pe_l = jnp.broadcast_to(pe_r[:, :, :, :, None, :], (2, 2, npair, 2, NSA_KV_HEADS, HEAD_DIM))
    pe_l = pe_l.reshape(2, 2, npair, 2 * LANES).transpose(0, 2, 1, 3)
    pe_l = jnp.pad(pe_l, ((0, 0), (0, 0), (0, PE_ROWS - 2), (0, 0))).astype(BF16)
    w1_r = w1.reshape(2, 2, CMP_STRIDE, HEAD_DIM, CMP_HID)
    zw = jnp.zeros_like(w1_r)
    w1_bd = jnp.concatenate([jnp.concatenate([w1_r, zw], axis=-1),
                             jnp.concatenate([zw, w1_r], axis=-1)], axis=-2)
    w1_bd = w1_bd.reshape(2, 2, npair, 2 * LANES, 2 * CMP_HID)
    w1_bd = jnp.concatenate([w1_bd[:, 0], w1_bd[:, 1]], axis=-1).astype(BF16)
    b1_l = jnp.concatenate([b1, b1], axis=-1).reshape(2, 1, 2 * CMP_HID)
    z2 = jnp.zeros_like(w2)
    w2_bd = jnp.concatenate([jnp.concatenate([w2, z2], axis=-1),
                             jnp.concatenate([z2, w2], axis=-1)], axis=-2).astype(BF16)
    return pe_l, w1_bd, b1_l, w2_bd


def _rope_tables(pos):
    half = HEAD_DIM // 2
    inv = ROPE_BASE ** (-jnp.arange(half, dtype=F32) / half)
    ang = pos.astype(F32)[:, None] * inv[None, :]
    cos, sin = jnp.cos(ang), jnp.sin(ang)
    cos_l = jnp.tile(jnp.concatenate([cos, cos], axis=-1), (1, RET_HEADS))
    sin_l = jnp.tile(jnp.concatenate([sin, sin], axis=-1), (1, RET_HEADS))
    return cos_l, sin_l


def _kv_shape(a, b, t):
    return a.reshape(1, b, t, 2, NSA_KV_HEADS, HEAD_DIM)


def _position_minor(cache, heads):
    b, l = cache.shape[:2]
    return jnp.transpose(cache, (0, 2, 3, 4, 1)).reshape(b, 2 * heads * HEAD_DIM, l)


def _from_position_minor(a, heads):
    b, _, l = a.shape
    return jnp.transpose(a.reshape(b, 2, heads, HEAD_DIM, l), (0, 4, 1, 2, 3))[None]


def kernel(x_prompt, mem_prompt, x_sample, cache_nsa_cmp, cache_nsa_slc, cache_nsa_win, state_ret, cache_mem,
           page_table, ln_mix, w_in, cmp_pe, cmp_w1, cmp_b1, cmp_w2, ret_gn, ln_mem, w_mem_kv, w_out, ln_final):
    bp, t_len = x_prompt.shape[:2]
    bs, s_len = x_sample.shape[:2]
    assert (t_len, s_len) == (SEQ, DEC_SEQ) and ln_mix.shape[0] == 1
    w_perm = _permute_w_in(w_in[0])
    comp_w = _compress_weights(cmp_pe[0], cmp_w1[0], cmp_b1[0], cmp_w2[0])
    ln_g = ln_mix[0].reshape(1, D_MODEL)
    gn = ret_gn[0].reshape(1, RET_W)
    w_out_b = w_out[0].astype(BF16)
    ln_f = ln_final.reshape(1, D_MODEL)

    xp2 = x_prompt.reshape(bp * SEQ, D_MODEL)
    (qn, kvc_t, kvs_t, kvw_t, qkvr, qm, z, gt, ksk, ksvt, kwk, kwvt) = _inproj(xp2, ln_g, w_perm, SEQ, True)
    kc, vc = _compress_prompt(kvc_t, comp_w)
    o_n = _nsa_prompt(qn, gt, ksk, ksvt, kwk, kwvt, kc, vc, bp)
    cos_p, sin_p = _rope_tables(jnp.arange(SEQ))
    o_r, st_p = _retention(qkvr, cos_p, sin_p, jnp.zeros((bp, RET_W, HEAD_DIM), F32), gn, bp, SEQ, 256,
                           seqs_per_step=bp)
    mkv_t = _mem_kv(mem_prompt, ln_mem[0].reshape(1, D_MODEL), w_mem_kv[0].astype(BF16))
    o_m = _mem_attn(qm, mkv_t, bp, SEQ, 1024)
    y_prompt = _finish(xp2, o_n, o_r, o_m, z, w_out_b, ln_f).reshape(bp, SEQ, D_MODEL)
    new_cmp_p = _from_position_minor(kvc_t, NSA_KV_HEADS)
    new_slc_p = _from_position_minor(kvs_t, NSA_KV_HEADS)
    new_win_p = _from_position_minor(kvw_t[:, :, SEQ - WINDOW:], NSA_KV_HEADS)
    new_ret_p = st_p.reshape(1, bp, RET_HEADS, HEAD_DIM, HEAD_DIM)
    new_mem_p = _from_position_minor(mkv_t, 4)

    xs2 = x_sample.reshape(bs * DEC_SEQ, D_MODEL)
    (qn_s, kvc_s, kvs_s, kvw_s, qkvr_s, qm_s, z_s, gt_s) = _inproj(xs2, ln_g, w_perm, DEC_SEQ, False)
    pool_cmp = _position_minor(cache_nsa_cmp[0], NSA_KV_HEADS).reshape(-1, 2, LANES, PAGE_SIZE)
    pool_slc = _position_minor(cache_nsa_slc[0], NSA_KV_HEADS).reshape(-1, 2, LANES, PAGE_SIZE)
    kc_s, vc_s = _compress_sample(page_table, pool_cmp, comp_w)
    q5 = qn_s.astype(F32).reshape(bs, DEC_SEQ, NSA_KV_HEADS, NSA_GROUP, HEAD_DIM).transpose(0, 2, 3, 1, 4)
    zq = jnp.zeros_like(q5[:, 0])
    qp = jnp.stack([jnp.concatenate([q5[:, 0], zq], axis=-1), jnp.concatenate([zq, q5[:, 1]], axis=-1)], axis=1)
    qp = qp.reshape(bs, ROWS_S, LANES)
    g5 = gt_s[:3 * NSA_HEADS].reshape(3, NSA_KV_HEADS, NSA_GROUP, bs, DEC_SEQ).transpose(3, 1, 2, 4, 0)
    gates = jnp.pad(g5.reshape(bs, ROWS_S, 3), ((0, 0), (0, 0), (0, LANES - 3)))
    expand = jnp.asarray(np.repeat(np.eye(N_SLC, dtype=np.float32), SLC_BLOCK, axis=1), dtype=BF16)
    win_t = _position_minor(cache_nsa_win[0], NSA_KV_HEADS)
    wtail = jnp.pad(kvw_s.reshape(bs, DEC_SEQ, 256).transpose(0, 2, 1), ((0, 0), (0, 0), (LANES - DEC_SEQ, 0)))
    o_sn, new_win_t = _nsa_sample(page_table, pool_slc, qp, gates, kc_s, vc_s, kvs_s, kvw_s, win_t, wtail, expand)
    o6 = o_sn.reshape(bs, NSA_KV_HEADS, NSA_GROUP, DEC_SEQ, NSA_KV_HEADS, HEAD_DIM)
    o_n_s = jnp.stack([o6[:, 0, :, :, 0], o6[:, 1, :, :, 1]], axis=1)
    o_n_s = o_n_s.transpose(0, 3, 1, 2, 4).reshape(bs * DEC_SEQ, 512)
    cos_s, sin_s = _rope_tables(PAST_LEN + jnp.arange(DEC_SEQ))
    o_r_s, st_s = _retention(qkvr_s, cos_s, sin_s, state_ret[0].reshape(bs, RET_W, HEAD_DIM), gn, bs, DEC_SEQ, DEC_SEQ,
                             seqs_per_step=SAMPLE_SEQS_PER_STEP)
    o_m_s = _mem_attn(qm_s, _position_minor(cache_mem[0], 4), bs, DEC_SEQ, DEC_SEQ,
                      seqs_per_step=SAMPLE_SEQS_PER_STEP)
    y_sample = _finish(xs2, o_n_s, o_r_s, o_m_s, z_s, w_out_b, ln_f).reshape(bs, DEC_SEQ, D_MODEL)
    new_cmp_s = _kv_shape(kvc_s, bs, DEC_SEQ)
    new_slc_s = _kv_shape(kvs_s, bs, DEC_SEQ)
    new_win_s = _from_position_minor(new_win_t, NSA_KV_HEADS)
    new_ret_s = st_s.reshape(1, bs, RET_HEADS, HEAD_DIM, HEAD_DIM)

    return (y_prompt, y_sample, new_cmp_p, new_cmp_s, new_slc_p, new_slc_s, new_win_p, new_win_s,
            new_ret_p, new_ret_s, new_mem_p)
```
